```python
import jax, jax.numpy as jnp
from jax import lax
import numpy as np

D_MODEL = 2048
BATCH = 8
SEQ = 8192
DEPTH = 4

N_MIXERS = 2
N_ATTN = (DEPTH + 1) // 2
N_REC = DEPTH // 2
HEAD_DIM = 64
N_Q_HEADS = D_MODEL // HEAD_DIM
N_KV_HEADS = 8
GROUP = N_Q_HEADS // N_KV_HEADS
WINDOW = 128
BLOCK = 128
QKV_DIM = (N_Q_HEADS + 2 * N_KV_HEADS) * HEAD_DIM
LRU_WIDTH = D_MODEL
LRU_BLOCKS = 8
LRU_BLOCK_DIM = LRU_WIDTH // LRU_BLOCKS
LRU_CONV = 4
LRU_C = 8.0
D_FF = 3 * D_MODEL
FFN_CONV = 3
EPS = 1e-6

kernel_name = "hybrid_swa_sink_rglru_convffn"


def rmsnorm(x, g):
    xf = x.astype(jnp.float32)
    y = xf * lax.rsqrt(jnp.mean(xf * xf, axis=-1, keepdims=True) + EPS)
    return (y * g.astype(jnp.float32)).astype(x.dtype)


def causal_dwconv(x, w, b):
    K = w.shape[0]
    T = x.shape[1]
    xp = jnp.pad(x, ((0, 0), (K - 1, 0), (0, 0)))
    y = b
    for k in range(K):
        y = y + xp[:, k:k + T] * w[k]
    return y


def sliding_window_attention(h, w_qkv, q_gain, k_gain, sinks, w_o):
    B, T, _ = h.shape
    nb = T // BLOCK
    qkv = h @ w_qkv
    q, k, v = jnp.split(qkv, [N_Q_HEADS * HEAD_DIM, (N_Q_HEADS + N_KV_HEADS) * HEAD_DIM], axis=-1)
    q = rmsnorm(q.reshape(B, T, N_KV_HEADS, GROUP, HEAD_DIM), q_gain)
    k = rmsnorm(k.reshape(B, T, N_KV_HEADS, HEAD_DIM), k_gain)
    v = v.reshape(B, T, N_KV_HEADS, HEAD_DIM)
    q = q.reshape(B, nb, BLOCK, N_KV_HEADS, GROUP, HEAD_DIM)
    kb = k.reshape(B, nb, BLOCK, N_KV_HEADS, HEAD_DIM)
    vb = v.reshape(B, nb, BLOCK, N_KV_HEADS, HEAD_DIM)
    pad = ((0, 0), (1, 0), (0, 0), (0, 0), (0, 0))
    kw = jnp.concatenate([jnp.pad(kb, pad)[:, :-1], kb], axis=2)
    vw = jnp.concatenate([jnp.pad(vb, pad)[:, :-1], vb], axis=2)
    scores = jnp.einsum('bnqhgd,bnkhd->bnhgqk', q, kw).astype(jnp.float32) * (HEAD_DIM ** -0.5)
    qpos = jnp.arange(BLOCK)[:, None] + BLOCK
    kpos = jnp.arange(2 * BLOCK)[None, :]
    rel = qpos - kpos
    band = (rel >= 0) & (rel < WINDOW)
    real_key = (jnp.arange(nb)[:, None] > 0) | (jnp.arange(2 * BLOCK)[None, :] >= BLOCK)
    mask = band[None, :, :] & real_key[:, None, :]
    scores = jnp.where(mask[None, :, None, None], scores, jnp.finfo(jnp.float32).min)
    sink = jnp.broadcast_to(sinks.astype(jnp.float32).reshape(1, 1, N_KV_HEADS, GROUP, 1, 1),
                            scores.shape[:-1] + (1,))
    probs = jax.nn.softmax(jnp.concatenate([scores, sink], axis=-1), axis=-1)[..., :-1]
    out = jnp.einsum('bnhgqk,bnkhd->bnqhgd', probs.astype(vw.dtype), vw)
    return out.reshape(B, T, N_Q_HEADS * HEAD_DIM) @ w_o


def rglru_block(h, w_in, conv_w, conv_b, w_a, b_a, w_i, b_i, lam, w_out):
    B, T, _ = h.shape
    xb, yb = jnp.split(h @ w_in, 2, axis=-1)
    gate = jax.nn.gelu(yb, approximate=True)
    xb = causal_dwconv(xb, conv_w, conv_b)
    xh = xb.reshape(B, T, LRU_BLOCKS, LRU_BLOCK_DIM)
    r = jax.nn.sigmoid(jnp.einsum('bthi,hij->bthj', xh, w_a) + b_a).reshape(B, T, LRU_WIDTH)
    i = jax.nn.sigmoid(jnp.einsum('bthi,hij->bthj', xh, w_i) + b_i).reshape(B, T, LRU_WIDTH)
    log_a = -LRU_C * r.astype(jnp.float32) * jax.nn.softplus(-lam.astype(jnp.float32))
    a = jnp.exp(log_a)
    u = jnp.sqrt(-jnp.expm1(2.0 * log_a)) * (i * xb).astype(jnp.float32)

    def combine(left, right):
        a1, b1 = left
        a2, b2 = right
        return a1 * a2, a2 * b1 + b2

    _, hs = lax.associative_scan(combine, (a, u), axis=1)
    return (hs.astype(h.dtype) * gate) @ w_out


def conv_ffn(h, w_up, conv_w, conv_b, w_down):
    u = causal_dwconv(h @ w_up, conv_w, conv_b)
    g, v = jnp.split(u, 2, axis=-1)
    return (jax.nn.gelu(g, approximate=True) * v) @ w_down


def _fwd_setup_inputs(seed: int = 0) -> dict:
    key = jax.random.key(seed)
    ks = iter(jax.random.split(key, 32))
    f32 = jnp.float32

    def nrm(shape, scale):
        return jax.random.normal(next(ks), shape, f32) * scale

    def gain(shape):
        return 1.0 + 0.02 * jax.random.normal(next(ks), shape, f32)

    a0 = jax.random.uniform(next(ks), (N_REC, LRU_WIDTH), f32, 0.9, 0.999)
    return {
        "x": jax.random.normal(next(ks), (BATCH, SEQ, D_MODEL), f32),
        "mix_norm": gain((DEPTH, D_MODEL)),
        "ffn_norm": gain((DEPTH, D_MODEL)),
        "attn_w_qkv": nrm((N_ATTN, D_MODEL, QKV_DIM), D_MODEL ** -0.5),
        "attn_q_gain": gain((N_ATTN, HEAD_DIM)),
        "attn_k_gain": gain((N_ATTN, HEAD_DIM)),
        "attn_sinks": nrm((N_ATTN, N_Q_HEADS), 0.5),
        "attn_w_o": nrm((N_ATTN, N_Q_HEADS * HEAD_DIM, D_MODEL), (N_Q_HEADS * HEAD_DIM) ** -0.5),
        "rec_w_in": nrm((N_REC, D_MODEL, 2 * LRU_WIDTH), D_MODEL ** -0.5),
        "rec_conv_w": nrm((N_REC, LRU_CONV, LRU_WIDTH), LRU_CONV ** -0.5),
        "rec_conv_b": nrm((N_REC, LRU_WIDTH), 0.01),
        "rec_w_a": nrm((N_REC, LRU_BLOCKS, LRU_BLOCK_DIM, LRU_BLOCK_DIM), LRU_BLOCK_DIM ** -0.5),
        "rec_b_a": nrm((N_REC, LRU_BLOCKS, LRU_BLOCK_DIM), 0.01),
        "rec_w_i": nrm((N_REC, LRU_BLOCKS, LRU_BLOCK_DIM, LRU_BLOCK_DIM), LRU_BLOCK_DIM ** -0.5),
        "rec_b_i": nrm((N_REC, LRU_BLOCKS, LRU_BLOCK_DIM), 0.01),
        "rec_lambda": jnp.log(a0) - jnp.log1p(-a0),
        "rec_w_out": nrm((N_REC, LRU_WIDTH, D_MODEL), LRU_WIDTH ** -0.5),
        "ffn_w_up": nrm((DEPTH, D_MODEL, 2 * D_FF), D_MODEL ** -0.5),
        "ffn_conv_w": nrm((DEPTH, FFN_CONV, 2 * D_FF), FFN_CONV ** -0.5),
        "ffn_conv_b": nrm((DEPTH, 2 * D_FF), 0.01),
        "ffn_w_down": nrm((DEPTH, D_FF, D_MODEL), D_FF ** -0.5),
    }


def _fwd_reference(x, mix_norm, ffn_norm, attn_w_qkv, attn_q_gain, attn_k_gain, attn_sinks, attn_w_o,
              rec_w_in, rec_conv_w, rec_conv_b, rec_w_a, rec_b_a, rec_w_i, rec_b_i, rec_lambda,
              rec_w_out, ffn_w_up, ffn_conv_w, ffn_conv_b, ffn_w_down):
    for layer in range(DEPTH):
        h = rmsnorm(x, mix_norm[layer])
        j = layer // N_MIXERS
        if layer % N_MIXERS == 0:
            x = x + sliding_window_attention(h, attn_w_qkv[j], attn_q_gain[j], attn_k_gain[j],
                                             attn_sinks[j], attn_w_o[j])
        else:
            x = x + rglru_block(h, rec_w_in[j], rec_conv_w[j], rec_conv_b[j], rec_w_a[j], rec_b_a[j],
                                rec_w_i[j], rec_b_i[j], rec_lambda[j], rec_w_out[j])
        h = rmsnorm(x, ffn_norm[layer])
        x = x + conv_ffn(h, ffn_w_up[layer], ffn_conv_w[layer], ffn_conv_b[layer], ffn_w_down[layer])
    return x


import jax as _jax
import jax.numpy as _jnp

TWIN_FORMAT = 'train_step'
FWD_PARAMS = ['x', 'mix_norm', 'ffn_norm', 'attn_w_qkv', 'attn_q_gain', 'attn_k_gain', 'attn_sinks', 'attn_w_o', 'rec_w_in', 'rec_conv_w', 'rec_conv_b', 'rec_w_a', 'rec_b_a', 'rec_w_i', 'rec_b_i', 'rec_lambda', 'rec_w_out', 'ffn_w_up', 'ffn_conv_w', 'ffn_conv_b', 'ffn_w_down']
TWIN_WEIGHTS = ['mix_norm', 'ffn_norm', 'attn_w_qkv', 'attn_q_gain', 'attn_k_gain', 'attn_sinks', 'attn_w_o', 'rec_w_in', 'rec_conv_w', 'rec_conv_b', 'rec_w_a', 'rec_b_a', 'rec_w_i', 'rec_b_i', 'rec_lambda', 'rec_w_out', 'ffn_w_up', 'ffn_conv_w', 'ffn_conv_b', 'ffn_w_down']
TWIN_DIFF_INPUT = 'x'
TWIN_INPUTS = ['x', 'mix_norm', 'ffn_norm', 'attn_w_qkv', 'attn_q_gain', 'attn_k_gain', 'attn_sinks', 'attn_w_o', 'rec_w_in', 'rec_conv_w', 'rec_conv_b', 'rec_w_a', 'rec_b_a', 'rec_w_i', 'rec_b_i', 'rec_lambda', 'rec_w_out', 'ffn_w_up', 'ffn_conv_w', 'ffn_conv_b', 'ffn_w_down', 'loss_target', 'm_mix_norm', 'm_ffn_norm', 'm_attn_w_qkv', 'm_attn_q_gain', 'm_attn_k_gain', 'm_attn_sinks', 'm_attn_w_o', 'm_rec_w_in', 'm_rec_conv_w', 'm_rec_conv_b', 'm_rec_w_a', 'm_rec_b_a', 'm_rec_w_i', 'm_rec_b_i', 'm_rec_lambda', 'm_rec_w_out', 'm_ffn_w_up', 'm_ffn_conv_w', 'm_ffn_conv_b', 'm_ffn_w_down', 'v_mix_norm', 'v_ffn_norm', 'v_attn_w_qkv', 'v_attn_q_gain', 'v_attn_k_gain', 'v_attn_sinks', 'v_attn_w_o', 'v_rec_w_in', 'v_rec_conv_w', 'v_rec_conv_b', 'v_rec_w_a', 'v_rec_b_a', 'v_rec_w_i', 'v_rec_b_i', 'v_rec_lambda', 'v_rec_w_out', 'v_ffn_w_up', 'v_ffn_conv_w', 'v_ffn_conv_b', 'v_ffn_w_down']
TWIN_OUTPUTS = ['loss', 'grad_x', 'grad_mix_norm', 'grad_ffn_norm', 'grad_attn_w_qkv', 'grad_attn_q_gain', 'grad_attn_k_gain', 'grad_attn_sinks', 'grad_attn_w_o', 'grad_rec_w_in', 'grad_rec_conv_w', 'grad_rec_conv_b', 'grad_rec_w_a', 'grad_rec_b_a', 'grad_rec_w_i', 'grad_rec_b_i', 'grad_rec_lambda', 'grad_rec_w_out', 'grad_ffn_w_up', 'grad_ffn_conv_w', 'grad_ffn_conv_b', 'grad_ffn_w_down', 'delta_mix_norm', 'delta_ffn_norm', 'delta_attn_w_qkv', 'delta_attn_q_gain', 'delta_attn_k_gain', 'delta_attn_sinks', 'delta_attn_w_o', 'delta_rec_w_in', 'delta_rec_conv_w', 'delta_rec_conv_b', 'delta_rec_w_a', 'delta_rec_b_a', 'delta_rec_w_i', 'delta_rec_b_i', 'delta_rec_lambda', 'delta_rec_w_out', 'delta_ffn_w_up', 'delta_ffn_conv_w', 'delta_ffn_conv_b', 'delta_ffn_w_down', 'new_m_mix_norm', 'new_m_ffn_norm', 'new_m_attn_w_qkv', 'new_m_attn_q_gain', 'new_m_attn_k_gain', 'new_m_attn_sinks', 'new_m_attn_w_o', 'new_m_rec_w_in', 'new_m_rec_conv_w', 'new_m_rec_conv_b', 'new_m_rec_w_a', 'new_m_rec_b_a', 'new_m_rec_w_i', 'new_m_rec_b_i', 'new_m_rec_lambda', 'new_m_rec_w_out', 'new_m_ffn_w_up', 'new_m_ffn_conv_w', 'new_m_ffn_conv_b', 'new_m_ffn_w_down', 'new_v_mix_norm', 'new_v_ffn_norm', 'new_v_attn_w_qkv', 'new_v_attn_q_gain', 'new_v_attn_k_gain', 'new_v_attn_sinks', 'new_v_attn_w_o', 'new_v_rec_w_in', 'new_v_rec_conv_w', 'new_v_rec_conv_b', 'new_v_rec_w_a', 'new_v_rec_b_a', 'new_v_rec_w_i', 'new_v_rec_b_i', 'new_v_rec_lambda', 'new_v_rec_w_out', 'new_v_ffn_w_up', 'new_v_ffn_conv_w', 'new_v_ffn_conv_b', 'new_v_ffn_w_down']
TWIN_LEAF_KINDS = {'loss': 'loss', 'grad_x': 'grad_x', 'grad_mix_norm': 'grad_w', 'grad_ffn_norm': 'grad_w', 'grad_attn_w_qkv': 'grad_w', 'grad_attn_q_gain': 'grad_w', 'grad_attn_k_gain': 'grad_w', 'grad_attn_sinks': 'grad_w', 'grad_attn_w_o': 'grad_w', 'grad_rec_w_in': 'grad_w', 'grad_rec_conv_w': 'grad_w', 'grad_rec_conv_b': 'grad_w', 'grad_rec_w_a': 'grad_w', 'grad_rec_b_a': 'grad_w', 'grad_rec_w_i': 'grad_w', 'grad_rec_b_i': 'grad_w', 'grad_rec_lambda': 'grad_w', 'grad_rec_w_out': 'grad_w', 'grad_ffn_w_up': 'grad_w', 'grad_ffn_conv_w': 'grad_w', 'grad_ffn_conv_b': 'grad_w', 'grad_ffn_w_down': 'grad_w', 'delta_mix_norm': 'delta_w', 'delta_ffn_norm': 'delta_w', 'delta_attn_w_qkv': 'delta_w', 'delta_attn_q_gain': 'delta_w', 'delta_attn_k_gain': 'delta_w', 'delta_attn_sinks': 'delta_w', 'delta_attn_w_o': 'delta_w', 'delta_rec_w_in': 'delta_w', 'delta_rec_conv_w': 'delta_w', 'delta_rec_conv_b': 'delta_w', 'delta_rec_w_a': 'delta_w', 'delta_rec_b_a': 'delta_w', 'delta_rec_w_i': 'delta_w', 'delta_rec_b_i': 'delta_w', 'delta_rec_lambda': 'delta_w', 'delta_rec_w_out': 'delta_w', 'delta_ffn_w_up': 'delta_w', 'delta_ffn_conv_w': 'delta_w', 'delta_ffn_conv_b': 'delta_w', 'delta_ffn_w_down': 'delta_w', 'new_m_mix_norm': 'new_m', 'new_m_ffn_norm': 'new_m', 'new_m_attn_w_qkv': 'new_m', 'new_m_attn_q_gain': 'new_m', 'new_m_attn_k_gain': 'new_m', 'new_m_attn_sinks': 'new_m', 'new_m_attn_w_o': 'new_m', 'new_m_rec_w_in': 'new_m', 'new_m_rec_conv_w': 'new_m', 'new_m_rec_conv_b': 'new_m', 'new_m_rec_w_a': 'new_m', 'new_m_rec_b_a': 'new_m', 'new_m_rec_w_i': 'new_m', 'new_m_rec_b_i': 'new_m', 'new_m_rec_lambda': 'new_m', 'new_m_rec_w_out': 'new_m', 'new_m_ffn_w_up': 'new_m', 'new_m_ffn_conv_w': 'new_m', 'new_m_ffn_conv_b': 'new_m', 'new_m_ffn_w_down': 'new_m', 'new_v_mix_norm': 'new_v', 'new_v_ffn_norm': 'new_v', 'new_v_attn_w_qkv': 'new_v', 'new_v_attn_q_gain': 'new_v', 'new_v_attn_k_gain': 'new_v', 'new_v_attn_sinks': 'new_v', 'new_v_attn_w_o': 'new_v', 'new_v_rec_w_in': 'new_v', 'new_v_rec_conv_w': 'new_v', 'new_v_rec_conv_b': 'new_v', 'new_v_rec_w_a': 'new_v', 'new_v_rec_b_a': 'new_v', 'new_v_rec_w_i': 'new_v', 'new_v_rec_b_i': 'new_v', 'new_v_rec_lambda': 'new_v', 'new_v_rec_w_out': 'new_v', 'new_v_ffn_w_up': 'new_v', 'new_v_ffn_conv_w': 'new_v', 'new_v_ffn_conv_b': 'new_v', 'new_v_ffn_w_down': 'new_v'}


def _forward(args):
    return _fwd_reference(*[args[k] for k in FWD_PARAMS])


def _output_shape():
    def fwd():
        inp = _fwd_setup_inputs(0)
        return _fwd_reference(*[inp[k] for k in FWD_PARAMS])
    out = _jax.eval_shape(fwd)
    return out.shape, out.dtype

N_MICROBATCH = 1
ADAM_LR = 0.001
ADAM_B1 = 0.9
ADAM_B2 = 0.999
ADAM_EPS = 1e-08
ADAM_WD = 0.01
ADAM_STEP = 10
PER_EXAMPLE_BATCH_AXIS = {'x': 0, 'loss_target': 0}
SHARED_INPUTS = []
_WEIGHT_DTYPES = {'mix_norm': _jnp.float32, 'ffn_norm': _jnp.float32, 'attn_w_qkv': _jnp.float32, 'attn_q_gain': _jnp.float32, 'attn_k_gain': _jnp.float32, 'attn_sinks': _jnp.float32, 'attn_w_o': _jnp.float32, 'rec_w_in': _jnp.float32, 'rec_conv_w': _jnp.float32, 'rec_conv_b': _jnp.float32, 'rec_w_a': _jnp.float32, 'rec_b_a': _jnp.float32, 'rec_w_i': _jnp.float32, 'rec_b_i': _jnp.float32, 'rec_lambda': _jnp.float32, 'rec_w_out': _jnp.float32, 'ffn_w_up': _jnp.float32, 'ffn_conv_w': _jnp.float32, 'ffn_conv_b': _jnp.float32, 'ffn_w_down': _jnp.float32}
MOMENT_SCALE = {'mix_norm': 7.055422e+00, 'ffn_norm': 2.882887e+01, 'attn_w_qkv': 3.227683e-01, 'attn_q_gain': 1.168520e+01, 'attn_k_gain': 1.169656e+01, 'attn_sinks': 1.251201e+00, 'attn_w_o': 3.134888e-01, 'rec_w_in': 2.720553e-01, 'rec_conv_w': 3.537305e+00, 'rec_conv_b': 1.139879e+01, 'rec_w_a': 2.032960e-01, 'rec_b_a': 3.091170e-01, 'rec_w_i': 3.779337e-01, 'rec_b_i': 2.351549e+00, 'rec_lambda': 7.645107e-01, 'rec_w_out': 3.097771e-01, 'ffn_w_up': 2.357233e-01, 'ffn_conv_w': 3.570845e+00, 'ffn_conv_b': 3.841991e+00, 'ffn_w_down': 3.771368e-01}


def _to_microbatches(a, axis):
    t = _jnp.moveaxis(a, axis, 0)
    t = t.reshape((N_MICROBATCH, t.shape[0] // N_MICROBATCH) + t.shape[1:])
    return _jnp.moveaxis(t, 1, axis + 1)


def setup_inputs(seed: int = 0) -> dict:
    inp = _fwd_setup_inputs(seed)
    key = _jax.random.fold_in(_jax.random.key(seed), 7919)
    shape, _ = _output_shape()
    out = dict(inp)
    out["loss_target"] = _jax.random.normal(_jax.random.fold_in(key, 0), shape, _jnp.float32)
    for i, name in enumerate(TWIN_WEIGHTS):
        w = inp[name].astype(_jnp.float32)
        if MOMENT_SCALE is None:
            s = _jnp.sqrt(_jnp.mean(_jnp.square(w)) + 1e-30)
        else:
            s = MOMENT_SCALE[name]
        km, kv = _jax.random.split(_jax.random.fold_in(key, i + 1))
        out[name] = w
        out["m_" + name] = s * _jax.random.normal(km, w.shape, _jnp.float32)
        out["v_" + name] = (s * s) * _jax.random.uniform(kv, w.shape, _jnp.float32, 0.5, 1.5)
    if N_MICROBATCH > 1:
        for name, axis in PER_EXAMPLE_BATCH_AXIS.items():
            out[name] = _to_microbatches(out[name], axis)
    return {'x': out['x'], 'mix_norm': out['mix_norm'], 'ffn_norm': out['ffn_norm'], 'attn_w_qkv': out['attn_w_qkv'], 'attn_q_gain': out['attn_q_gain'], 'attn_k_gain': out['attn_k_gain'], 'attn_sinks': out['attn_sinks'], 'attn_w_o': out['attn_w_o'], 'rec_w_in': out['rec_w_in'], 'rec_conv_w': out['rec_conv_w'], 'rec_conv_b': out['rec_conv_b'], 'rec_w_a': out['rec_w_a'], 'rec_b_a': out['rec_b_a'], 'rec_w_i': out['rec_w_i'], 'rec_b_i': out['rec_b_i'], 'rec_lambda': out['rec_lambda'], 'rec_w_out': out['rec_w_out'], 'ffn_w_up': out['ffn_w_up'], 'ffn_conv_w': out['ffn_conv_w'], 'ffn_conv_b': out['ffn_conv_b'], 'ffn_w_down': out['ffn_w_down'], 'loss_target': out['loss_target'], 'm_mix_norm': out['m_mix_norm'], 'm_ffn_norm': out['m_ffn_norm'], 'm_attn_w_qkv': out['m_attn_w_qkv'], 'm_attn_q_gain': out['m_attn_q_gain'], 'm_attn_k_gain': out['m_attn_k_gain'], 'm_attn_sinks': out['m_attn_sinks'], 'm_attn_w_o': out['m_attn_w_o'], 'm_rec_w_in': out['m_rec_w_in'], 'm_rec_conv_w': out['m_rec_conv_w'], 'm_rec_conv_b': out['m_rec_conv_b'], 'm_rec_w_a': out['m_rec_w_a'], 'm_rec_b_a': out['m_rec_b_a'], 'm_rec_w_i': out['m_rec_w_i'], 'm_rec_b_i': out['m_rec_b_i'], 'm_rec_lambda': out['m_rec_lambda'], 'm_rec_w_out': out['m_rec_w_out'], 'm_ffn_w_up': out['m_ffn_w_up'], 'm_ffn_conv_w': out['m_ffn_conv_w'], 'm_ffn_conv_b': out['m_ffn_conv_b'], 'm_ffn_w_down': out['m_ffn_w_down'], 'v_mix_norm': out['v_mix_norm'], 'v_ffn_norm': out['v_ffn_norm'], 'v_attn_w_qkv': out['v_attn_w_qkv'], 'v_attn_q_gain': out['v_attn_q_gain'], 'v_attn_k_gain': out['v_attn_k_gain'], 'v_attn_sinks': out['v_attn_sinks'], 'v_attn_w_o': out['v_attn_w_o'], 'v_rec_w_in': out['v_rec_w_in'], 'v_rec_conv_w': out['v_rec_conv_w'], 'v_rec_conv_b': out['v_rec_conv_b'], 'v_rec_w_a': out['v_rec_w_a'], 'v_rec_b_a': out['v_rec_b_a'], 'v_rec_w_i': out['v_rec_w_i'], 'v_rec_b_i': out['v_rec_b_i'], 'v_rec_lambda': out['v_rec_lambda'], 'v_rec_w_out': out['v_rec_w_out'], 'v_ffn_w_up': out['v_ffn_w_up'], 'v_ffn_conv_w': out['v_ffn_conv_w'], 'v_ffn_conv_b': out['v_ffn_conv_b'], 'v_ffn_w_down': out['v_ffn_w_down']}


def _loss(weights, diff, rest, loss_target):
    with _jax.named_scope("forward"):
        args = {**rest, TWIN_DIFF_INPUT: diff, **{k: w.astype(_WEIGHT_DTYPES[k]) for k, w in weights.items()}}
        y = _forward(args)
    with _jax.named_scope("loss_head"):
        err = _jnp.square(y.astype(_jnp.float32) - loss_target)
        return 0.5 * _jnp.sum(_jnp.mean(err, axis=-1)) if err.ndim else 0.5 * err


def _adamw(w, g, m, v):
    m = ADAM_B1 * m + (1.0 - ADAM_B1) * g
    v = ADAM_B2 * v + (1.0 - ADAM_B2) * _jnp.square(g)
    m_hat = m / (1.0 - ADAM_B1 ** ADAM_STEP)
    v_hat = v / (1.0 - ADAM_B2 ** ADAM_STEP)
    delta = -ADAM_LR * (m_hat / (_jnp.sqrt(v_hat) + ADAM_EPS) + ADAM_WD * w)
    return delta, m, v


def reference(x, mix_norm, ffn_norm, attn_w_qkv, attn_q_gain, attn_k_gain, attn_sinks, attn_w_o, rec_w_in, rec_conv_w, rec_conv_b, rec_w_a, rec_b_a, rec_w_i, rec_b_i, rec_lambda, rec_w_out, ffn_w_up, ffn_conv_w, ffn_conv_b, ffn_w_down, loss_target, m_mix_norm, m_ffn_norm, m_attn_w_qkv, m_attn_q_gain, m_attn_k_gain, m_attn_sinks, m_attn_w_o, m_rec_w_in, m_rec_conv_w, m_rec_conv_b, m_rec_w_a, m_rec_b_a, m_rec_w_i, m_rec_b_i, m_rec_lambda, m_rec_w_out, m_ffn_w_up, m_ffn_conv_w, m_ffn_conv_b, m_ffn_w_down, v_mix_norm, v_ffn_norm, v_attn_w_qkv, v_attn_q_gain, v_attn_k_gain, v_attn_sinks, v_attn_w_o, v_rec_w_in, v_rec_conv_w, v_rec_conv_b, v_rec_w_a, v_rec_b_a, v_rec_w_i, v_rec_b_i, v_rec_lambda, v_rec_w_out, v_ffn_w_up, v_ffn_conv_w, v_ffn_conv_b, v_ffn_w_down):
    given = dict(x=x, mix_norm=mix_norm, ffn_norm=ffn_norm, attn_w_qkv=attn_w_qkv, attn_q_gain=attn_q_gain, attn_k_gain=attn_k_gain, attn_sinks=attn_sinks, attn_w_o=attn_w_o, rec_w_in=rec_w_in, rec_conv_w=rec_conv_w, rec_conv_b=rec_conv_b, rec_w_a=rec_w_a, rec_b_a=rec_b_a, rec_w_i=rec_w_i, rec_b_i=rec_b_i, rec_lambda=rec_lambda, rec_w_out=rec_w_out, ffn_w_up=ffn_w_up, ffn_conv_w=ffn_conv_w, ffn_conv_b=ffn_conv_b, ffn_w_down=ffn_w_down, loss_target=loss_target, m_mix_norm=m_mix_norm, m_ffn_norm=m_ffn_norm, m_attn_w_qkv=m_attn_w_qkv, m_attn_q_gain=m_attn_q_gain, m_attn_k_gain=m_attn_k_gain, m_attn_sinks=m_attn_sinks, m_attn_w_o=m_attn_w_o, m_rec_w_in=m_rec_w_in, m_rec_conv_w=m_rec_conv_w, m_rec_conv_b=m_rec_conv_b, m_rec_w_a=m_rec_w_a, m_rec_b_a=m_rec_b_a, m_rec_w_i=m_rec_w_i, m_rec_b_i=m_rec_b_i, m_rec_lambda=m_rec_lambda, m_rec_w_out=m_rec_w_out, m_ffn_w_up=m_ffn_w_up, m_ffn_conv_w=m_ffn_conv_w, m_ffn_conv_b=m_ffn_conv_b, m_ffn_w_down=m_ffn_w_down, v_mix_norm=v_mix_norm, v_ffn_norm=v_ffn_norm, v_attn_w_qkv=v_attn_w_qkv, v_attn_q_gain=v_attn_q_gain, v_attn_k_gain=v_attn_k_gain, v_attn_sinks=v_attn_sinks, v_attn_w_o=v_attn_w_o, v_rec_w_in=v_rec_w_in, v_rec_conv_w=v_rec_conv_w, v_rec_conv_b=v_rec_conv_b, v_rec_w_a=v_rec_w_a, v_rec_b_a=v_rec_b_a, v_rec_w_i=v_rec_w_i, v_rec_b_i=v_rec_b_i, v_rec_lambda=v_rec_lambda, v_rec_w_out=v_rec_w_out, v_ffn_w_up=v_ffn_w_up, v_ffn_conv_w=v_ffn_conv_w, v_ffn_conv_b=v_ffn_conv_b, v_ffn_w_down=v_ffn_w_down)
    weights = {n: given[n] for n in TWIN_WEIGHTS}
    shared = {n: given[n] for n in SHARED_INPUTS}
    per_example = {n: given[n] for n in ['x']}
    grad_fn = _jax.value_and_grad(_loss, argnums=(0, 1))

    def one_microbatch(ex, loss_target):
        ex = dict(ex)
        diff = ex.pop(TWIN_DIFF_INPUT)
        return grad_fn(weights, diff, {**shared, **ex}, loss_target)

    if N_MICROBATCH == 1:
        loss, (grad_w, grad_x) = one_microbatch(per_example, given["loss_target"])
    else:
        def body(carry, xs):
            loss_sum, grad_sum = carry
            l_k, (gw_k, gx_k) = one_microbatch(xs[0], xs[1])
            with _jax.named_scope("update"):
                return (loss_sum + l_k, _jax.tree.map(_jnp.add, grad_sum, gw_k)), gx_k

        init = (_jnp.zeros((), _jnp.float32), _jax.tree.map(_jnp.zeros_like, weights))
        (loss, grad_w), grad_x = _jax.lax.scan(body, init, (per_example, given["loss_target"]))
    with _jax.named_scope("update"):
        delta_w, new_m, new_v = {}, {}, {}
        for n in TWIN_WEIGHTS:
            delta_w[n], new_m[n], new_v[n] = _adamw(weights[n], grad_w[n], given["m_" + n], given["v_" + n])
    return (loss, grad_x, *[grad_w[n] for n in TWIN_WEIGHTS], *[delta_w[n] for n in TWIN_WEIGHTS],
            *[new_m[n] for n in TWIN_WEIGHTS], *[new_v[n] for n in TWIN_WEIGHTS])
```

```python
import functools
import math

import jax
import jax.numpy as jnp
import numpy as np
from jax import lax
from jax.experimental import pallas as pl
from jax.experimental.pallas import tpu as pltpu

F32 = jnp.float32
BF16 = jnp.bfloat16

N_DEV = 8
HEAD_DIM = 64
GROUP = 4
BLOCK = 128
LRU_C = 8.0
EPS = 1e-6
HALO = 16
ADAM_LR, ADAM_B1, ADAM_B2, ADAM_EPS, ADAM_WD, ADAM_STEP = 0.001, 0.9, 0.999, 1e-08, 0.01, 10
VMEM_LIMIT = 56 * 1024 * 1024
MESH = pl.DeviceIdType.MESH
GELU_C = math.sqrt(2.0 / math.pi)


def _cp(*sem, vmem=VMEM_LIMIT):
    return pltpu.CompilerParams(dimension_semantics=tuple(sem), vmem_limit_bytes=vmem)


def _tile(dim, pref, mult=128):
    if dim <= pref:
        return dim
    t = (pref // mult) * mult
    while t >= mult:
        if dim % t == 0:
            return t
        t -= mult
    return dim


def _gelu(x):
    th = jnp.tanh(GELU_C * (x + 0.044715 * x * x * x))
    return 0.5 * x * (1.0 + th)


def _gelu_and_grad(x):
    x2 = x * x
    th = jnp.tanh(GELU_C * (x + 0.044715 * x2 * x))
    g = 0.5 * x * (1.0 + th)
    dg = 0.5 * (1.0 + th) + 0.5 * x * (1.0 - th * th) * GELU_C * (1.0 + 3.0 * 0.044715 * x2)
    return g, dg


def _dot(a, b, dims):
    return lax.dot_general(a.astype(BF16), b.astype(BF16), (dims, ((), ())), preferred_element_type=F32)


NN = ((1,), (0,))
NT = ((1,), (1,))
TN = ((0,), (0,))


def _matmul(a, b, *, dims, grid, a_spec, b_spec, o_spec, out_shape, acc_shape, res=None, res_spec=None, name):
    nk = grid[2]

    def body(*refs):
        if res is None:
            a_ref, b_ref, o_ref, acc_ref = refs
            r_ref = None
        else:
            a_ref, b_ref, r_ref, o_ref, acc_ref = refs
        k = pl.program_id(2)

        @pl.when(k == 0)
        def _():
            acc_ref[...] = jnp.zeros_like(acc_ref)

        acc_ref[...] += _dot(a_ref[...], b_ref[...], dims)

        @pl.when(k == nk - 1)
        def _():
            acc = acc_ref[...]
            if r_ref is not None:
                acc = acc + r_ref[...].astype(F32)
            o_ref[...] = acc.astype(o_ref.dtype)

    in_specs = [a_spec, b_spec] + ([res_spec] if res is not None else [])
    args = (a, b) + ((res,) if res is not None else ())
    return pl.pallas_call(
        body, grid=grid, in_specs=in_specs, out_specs=o_spec, out_shape=out_shape,
        scratch_shapes=[pltpu.VMEM(acc_shape, F32)],
        compiler_params=_cp("parallel", "parallel", "arbitrary"), name=name,
    )(*args)


def mm_nn(a, b, *, out_dtype, res=None, name):
    M, K = a.shape
    N = b.shape[1]
    tm, tn, tk = _tile(M, 1024), _tile(N, 1024), _tile(K, 512)
    return _matmul(
        a, b, dims=NN, grid=(M // tm, N // tn, K // tk),
        a_spec=pl.BlockSpec((tm, tk), lambda i, j, k: (i, k)),
        b_spec=pl.BlockSpec((tk, tn), lambda i, j, k: (k, j)),
        o_spec=pl.BlockSpec((tm, tn), lambda i, j, k: (i, j)),
        out_shape=jax.ShapeDtypeStruct((M, N), out_dtype), acc_shape=(tm, tn),
        res=res, res_spec=pl.BlockSpec((tm, tn), lambda i, j, k: (i, j)), name=name)


def mm_nt(a, b, *, out_dtype, res=None, name):
    M, N = a.shape
    K = b.shape[0]
    tm, tn, tk = _tile(M, 1024), _tile(K, 1024), _tile(N, 512)
    return _matmul(
        a, b, dims=NT, grid=(M // tm, K // tn, N // tk),
        a_spec=pl.BlockSpec((tm, tk), lambda i, j, k: (i, k)),
        b_spec=pl.BlockSpec((tn, tk), lambda i, j, k: (j, k)),
        o_spec=pl.BlockSpec((tm, tn), lambda i, j, k: (i, j)),
        out_shape=jax.ShapeDtypeStruct((M, K), out_dtype), acc_shape=(tm, tn),
        res=res, res_spec=pl.BlockSpec((tm, tn), lambda i, j, k: (i, j)), name=name)


def mm_tn(a, b, *, out_dtype, col_shards=None, shard_of_block=None, name):
    T, K = a.shape
    N = b.shape[1]
    tt = _tile(T, 1024)
    tm = _tile(K, 512)
    if col_shards is None:
        tn = _tile(N, 1024)
        o_spec = pl.BlockSpec((tm, tn), lambda i, j, k: (i, j))
        out_shape = jax.ShapeDtypeStruct((K, N), out_dtype)
    else:
        n = N // col_shards
        tn = _tile(n, 1536)
        per = n // tn
        sob = shard_of_block if shard_of_block is not None else (lambda s: s)
        o_spec = pl.BlockSpec((None, tm, tn), lambda i, j, k: (sob(j // per), i, j % per))
        out_shape = jax.ShapeDtypeStruct((col_shards, K, n), out_dtype)
    return _matmul(
        a, b, dims=TN, grid=(K // tm, N // tn, T // tt),
        a_spec=pl.BlockSpec((tt, tm), lambda i, j, k: (k, i)),
        b_spec=pl.BlockSpec((tt, tn), lambda i, j, k: (k, j)),
        o_spec=o_spec, out_shape=out_shape, acc_shape=(tm, tn), name=name)


def rmsnorm_fwd(x, g, *, name):
    T, D = x.shape
    tm = _tile(T, 512, 8)

    def body(x_ref, g_ref, o_ref):
        xv = x_ref[...]
        r = lax.rsqrt(jnp.mean(xv * xv, axis=-1, keepdims=True) + EPS)
        o_ref[...] = (xv * r * g_ref[...]).astype(o_ref.dtype)

    return pl.pallas_call(
        body, grid=(T // tm,),
        in_specs=[pl.BlockSpec((tm, D), lambda i: (i, 0)), pl.BlockSpec((1, D), lambda i: (0, 0))],
        out_specs=pl.BlockSpec((tm, D), lambda i: (i, 0)),
        out_shape=jax.ShapeDtypeStruct((T, D), BF16), compiler_params=_cp("parallel"), name=name)(x, g)


def rmsnorm_bwd(x, g, dh, dres, *, name):
    T, D = x.shape
    tm = _tile(T, 512, 8)

    def body(x_ref, g_ref, dh_ref, dres_ref, dx_ref, dg_ref):
        @pl.when(pl.program_id(0) == 0)
        def _():
            dg_ref[...] = jnp.zeros_like(dg_ref)

        xv = x_ref[...]
        dh_v = dh_ref[...].astype(F32)
        r = lax.rsqrt(jnp.mean(xv * xv, axis=-1, keepdims=True) + EPS)
        u = dh_v * g_ref[...]
        dot = jnp.mean(u * xv, axis=-1, keepdims=True)
        dx_ref[...] = dres_ref[...] + r * u - xv * (r * r * r * dot)
        dg_ref[...] += jnp.sum(dh_v * xv * r, axis=0, keepdims=True)

    row = pl.BlockSpec((tm, D), lambda i: (i, 0))
    vec = pl.BlockSpec((1, D), lambda i: (0, 0))
    return pl.pallas_call(
        body, grid=(T // tm,), in_specs=[row, vec, row, row], out_specs=[row, vec],
        out_shape=[jax.ShapeDtypeStruct((T, D), F32), jax.ShapeDtypeStruct((1, D), F32)],
        compiler_params=_cp("arbitrary"), name=name)(x, g, dh, dres)


def loss_head(y, target, *, name):
    T, D = y.shape
    tm = _tile(T, 512, 8)

    def body(y_ref, t_ref, l_ref, dy_ref):
        @pl.when(pl.program_id(0) == 0)
        def _():
            l_ref[...] = jnp.zeros_like(l_ref)

        e = y_ref[...] - t_ref[...]
        dy_ref[...] = e * (1.0 / D)
        l_ref[...] += jnp.sum(e * e, axis=0, keepdims=True) * (0.5 / D)

    row = pl.BlockSpec((tm, D), lambda i: (i, 0))
    vec = pl.BlockSpec((1, D), lambda i: (0, 0))
    return pl.pallas_call(
        body, grid=(T // tm,), in_specs=[row, row], out_specs=[vec, row],
        out_shape=[jax.ShapeDtypeStruct((1, D), F32), jax.ShapeDtypeStruct((T, D), F32)],
        compiler_params=_cp("arbitrary"), name=name)(y, target)


PAIR = 2
QW = PAIR * GROUP * HEAD_DIM
KW = PAIR * HEAD_DIM
NEG = -1e30


def _headnorm(x):
    r = lax.rsqrt(jnp.mean(x * x, axis=-1, keepdims=True) + EPS)
    return x * r, r


def _attn_mask(n):
    qi = lax.broadcasted_iota(jnp.int32, (BLOCK, 2 * BLOCK), 0)
    kj = lax.broadcasted_iota(jnp.int32, (BLOCK, 2 * BLOCK), 1)
    rel = qi + BLOCK - kj
    m = (rel >= 0) & (rel < BLOCK) & ((kj >= BLOCK) | (n > 0))
    return jnp.concatenate([m] * GROUP, axis=0)


def _attn_probs(q_ref, kp_ref, kc_ref, qg, kg, sink_ref, mask4, p, hh):
    lo = hh * HEAD_DIM
    k_raw = jnp.concatenate([kp_ref[:, lo:lo + HEAD_DIM], kc_ref[:, lo:lo + HEAD_DIM]], axis=0)
    kn, kr = _headnorm(k_raw)
    khat = kn * kg
    q_raw, qn, qr = [], [], []
    for g in range(GROUP):
        c0 = (hh * GROUP + g) * HEAD_DIM
        x = q_ref[:, c0:c0 + HEAD_DIM]
        xn, r = _headnorm(x)
        q_raw.append(x)
        qn.append(xn)
        qr.append(r)
    q4 = jnp.concatenate([xn * qg for xn in qn], axis=0)
    s = _dot(q4, khat, NT) * (HEAD_DIM ** -0.5)
    sink = jnp.concatenate(
        [jnp.full((BLOCK, 1), sink_ref[0, p * PAIR * GROUP + hh * GROUP + g], F32) for g in range(GROUP)], axis=0)
    m = jnp.maximum(jnp.max(jnp.where(mask4, s, NEG), axis=-1, keepdims=True), sink)
    e = jnp.where(mask4, jnp.exp(s - m), 0.0)
    es = jnp.exp(sink - m)
    inv = 1.0 / (jnp.sum(e, axis=-1, keepdims=True) + es)
    return e * inv, es * inv, q4, khat, (q_raw, qn, qr), (k_raw, kn, kr)


def attn_fwd(qkv, q_gain, k_gain, sinks, *, name):
    T, W = qkv.shape
    hq = W // HEAD_DIM * GROUP // (GROUP + 2)
    dq = hq * HEAD_DIM
    npair = hq // (GROUP * PAIR)
    nb = T // BLOCK
    k0 = dq // KW
    v0 = k0 + npair

    def body(q_ref, kp_ref, kc_ref, vp_ref, vc_ref, qg_ref, kg_ref, sink_ref, o_ref):
        p, n = pl.program_id(0), pl.program_id(1)
        mask4 = _attn_mask(n)
        outs = []
        for hh in range(PAIR):
            lo = hh * HEAD_DIM
            pr, _, _, _, _, _ = _attn_probs(q_ref, kp_ref, kc_ref, qg_ref[...], kg_ref[...], sink_ref, mask4, p, hh)
            vwin = jnp.concatenate([vp_ref[:, lo:lo + HEAD_DIM], vc_ref[:, lo:lo + HEAD_DIM]], axis=0)
            o4 = _dot(pr, vwin, NN)
            outs += [o4[g * BLOCK:(g + 1) * BLOCK] for g in range(GROUP)]
        o_ref[...] = jnp.concatenate(outs, axis=1).astype(o_ref.dtype)

    prev = lambda n: jnp.maximum(n - 1, 0)
    vec = pl.BlockSpec((1, HEAD_DIM), lambda p, n: (0, 0))
    return pl.pallas_call(
        body, grid=(npair, nb),
        in_specs=[pl.BlockSpec((BLOCK, QW), lambda p, n: (n, p)),
                  pl.BlockSpec((BLOCK, KW), lambda p, n: (prev(n), k0 + p)),
                  pl.BlockSpec((BLOCK, KW), lambda p, n: (n, k0 + p)),
                  pl.BlockSpec((BLOCK, KW), lambda p, n: (prev(n), v0 + p)),
                  pl.BlockSpec((BLOCK, KW), lambda p, n: (n, v0 + p)),
                  vec, vec, pl.BlockSpec(memory_space=pltpu.SMEM)],
        out_specs=pl.BlockSpec((BLOCK, QW), lambda p, n: (n, p)),
        out_shape=jax.ShapeDtypeStruct((T, dq), BF16),
        compiler_params=_cp("parallel", "parallel"), name=name)(qkv, qkv, qkv, qkv, qkv, q_gain, k_gain, sinks)


def attn_bwd(qkv, d_out, q_gain, k_gain, sinks, *, name):
    T, W = qkv.shape
    hq = W // HEAD_DIM * GROUP // (GROUP + 2)
    dq_w = hq * HEAD_DIM
    npair = hq // (GROUP * PAIR)
    nb = T // BLOCK
    k0 = dq_w // KW
    v0 = k0 + npair

    def body(q_ref, kp_ref, kc_ref, vp_ref, vc_ref, do_ref, qg_ref, kg_ref, sink_ref,
             dq_ref, dk_ref, dv_ref, dqg_ref, dkg_ref, dsink_ref, dk_carry, dv_carry):
        p, i = pl.program_id(0), pl.program_id(1)
        n = nb - 1 - i

        @pl.when(i == 0)
        def _():
            dk_carry[...] = jnp.zeros_like(dk_carry)
            dv_carry[...] = jnp.zeros_like(dv_carry)
            dqg_ref[...] = jnp.zeros_like(dqg_ref)
            dkg_ref[...] = jnp.zeros_like(dkg_ref)
            dsink_ref[...] = jnp.zeros_like(dsink_ref)

        qg, kg = qg_ref[...], kg_ref[...]
        mask4 = _attn_mask(n)
        dq_parts, dk_parts, dv_parts, dsink_rows = [], [], [], []
        dqg_acc = jnp.zeros((1, HEAD_DIM), F32)
        dkg_acc = jnp.zeros((1, HEAD_DIM), F32)
        for hh in range(PAIR):
            lo = hh * HEAD_DIM
            pr, psink, q4, khat, (q_raw, qn, qr), (k_raw, kn, kr) = _attn_probs(
                q_ref, kp_ref, kc_ref, qg, kg, sink_ref, mask4, p, hh)
            vwin = jnp.concatenate([vp_ref[:, lo:lo + HEAD_DIM], vc_ref[:, lo:lo + HEAD_DIM]], axis=0)
            do4 = jnp.concatenate(
                [do_ref[:, (hh * GROUP + g) * HEAD_DIM:(hh * GROUP + g + 1) * HEAD_DIM] for g in range(GROUP)],
                axis=0).astype(F32)
            dp = _dot(do4, vwin, NT)
            delta = jnp.sum(pr * dp, axis=-1, keepdims=True)
            ds = pr * (dp - delta) * (HEAD_DIM ** -0.5)
            dsk = -psink * delta
            for g in range(GROUP):
                tot = jnp.sum(dsk[g * BLOCK:(g + 1) * BLOCK], axis=0, keepdims=True)
                dsink_rows.append(jnp.broadcast_to(tot, (1, 128)))
            dq4 = _dot(ds, khat, NN)
            dkhat_win = _dot(ds, q4, TN)
            dv_win = _dot(pr, do4, TN)
            for g in range(GROUP):
                dqh = dq4[g * BLOCK:(g + 1) * BLOCK]
                dqg_acc += jnp.sum(dqh * qn[g], axis=0, keepdims=True)
                dqn = dqh * qg
                r = qr[g]
                dq_parts.append(r * dqn - q_raw[g] * (r * r * r * jnp.mean(dqn * q_raw[g], axis=-1, keepdims=True)))
            dkh = dkhat_win[BLOCK:] + dk_carry[hh]
            dk_carry[hh] = dkhat_win[:BLOCK]
            dkg_acc += jnp.sum(dkh * kn[BLOCK:], axis=0, keepdims=True)
            dkn = dkh * kg
            r = kr[BLOCK:]
            kc_raw = k_raw[BLOCK:]
            dk_parts.append(r * dkn - kc_raw * (r * r * r * jnp.mean(dkn * kc_raw, axis=-1, keepdims=True)))
            dv_parts.append(dv_win[BLOCK:] + dv_carry[hh])
            dv_carry[hh] = dv_win[:BLOCK]
        dq_ref[...] = jnp.concatenate(dq_parts, axis=1).astype(dq_ref.dtype)
        dk_ref[...] = jnp.concatenate(dk_parts, axis=1).astype(dk_ref.dtype)
        dv_ref[...] = jnp.concatenate(dv_parts, axis=1).astype(dv_ref.dtype)
        dqg_ref[...] += dqg_acc
        dkg_ref[...] += dkg_acc
        dsink_ref[...] += jnp.concatenate(dsink_rows, axis=0)

    rev = lambda i: nb - 1 - i
    prev = lambda i: jnp.maximum(nb - 2 - i, 0)
    vec = pl.BlockSpec((1, HEAD_DIM), lambda p, i: (0, 0))
    acc64 = pl.BlockSpec((None, 1, HEAD_DIM), lambda p, i: (p, 0, 0))
    dq, dk, dv, dqg, dkg, dsink = pl.pallas_call(
        body, grid=(npair, nb),
        in_specs=[pl.BlockSpec((BLOCK, QW), lambda p, i: (rev(i), p)),
                  pl.BlockSpec((BLOCK, KW), lambda p, i: (prev(i), k0 + p)),
                  pl.BlockSpec((BLOCK, KW), lambda p, i: (rev(i), k0 + p)),
                  pl.BlockSpec((BLOCK, KW), lambda p, i: (prev(i), v0 + p)),
                  pl.BlockSpec((BLOCK, KW), lambda p, i: (rev(i), v0 + p)),
                  pl.BlockSpec((BLOCK, QW), lambda p, i: (rev(i), p)),
                  vec, vec, pl.BlockSpec(memory_space=pltpu.SMEM)],
        out_specs=[pl.BlockSpec((BLOCK, QW), lambda p, i: (rev(i), p)),
                   pl.BlockSpec((BLOCK, KW), lambda p, i: (rev(i), p)),
                   pl.BlockSpec((BLOCK, KW), lambda p, i: (rev(i), p)),
                   acc64, acc64,
                   pl.BlockSpec((None, PAIR * GROUP, 128), lambda p, i: (p, 0, 0))],
        out_shape=[jax.ShapeDtypeStruct((T, dq_w), BF16),
                   jax.ShapeDtypeStruct((T, npair * KW), BF16),
                   jax.ShapeDtypeStruct((T, npair * KW), BF16),
                   jax.ShapeDtypeStruct((npair, 1, HEAD_DIM), F32),
                   jax.ShapeDtypeStruct((npair, 1, HEAD_DIM), F32),
                   jax.ShapeDtypeStruct((npair, PAIR * GROUP, 128), F32)],
        scratch_shapes=[pltpu.VMEM((PAIR, BLOCK, HEAD_DIM), F32), pltpu.VMEM((PAIR, BLOCK, HEAD_DIM), F32)],
        compiler_params=_cp("parallel", "arbitrary"), name=name,
    )(qkv, qkv, qkv, qkv, qkv, d_out, q_gain, k_gain, sinks)
    return dq, dk, dv, jnp.sum(dqg, axis=0), jnp.sum(dkg, axis=0), dsink[:, :, 0].reshape(-1)


def _softplus_neg(lam):
    return jnp.maximum(-lam, 0.0) + jnp.log1p(jnp.exp(-jnp.abs(lam)))


def _conv_taps(xc, cw_ref, tt, lead):
    K = cw_ref.shape[0]
    acc = None
    for k in range(K):
        off = lead - (K - 1 - k)
        term = cw_ref[k:k + 1, :] * xc[off:off + tt]
        acc = term if acc is None else acc + term
    return acc


def _bcast_row(x, row):
    return jnp.broadcast_to(x[row:row + 1, :], x.shape)


def rec_fwd(z, cw, cb, wa, ba, wi, bi, lam, *, name):
    T, C2 = z.shape
    C = C2 // 2
    nblk, bd, _ = wa.shape
    tt = _tile(T, 128, HALO)
    ng = tt // 8

    def body(x_ref, y_ref, halo_ref, cw_ref, cb_ref, wa_ref, ba_ref, wi_ref, bi_ref, lam_ref,
             xb_ref, r_ref, i_ref, a_ref, h_ref, hp_ref, hg_ref, carry, u_scr):
        step = pl.program_id(0)

        @pl.when(step == 0)
        def _():
            carry[...] = jnp.zeros_like(carry)

        halo = jnp.where(step > 0, halo_ref[...], 0.0)
        xc = jnp.concatenate([halo, x_ref[...]], axis=0)
        xb = cb_ref[...] + _conv_taps(xc, cw_ref, tt, HALO)
        xb_ref[...] = xb
        pa, pi = [], []
        for b in range(nblk):
            xs = xb[:, b * bd:(b + 1) * bd]
            pa.append(_dot(xs, wa_ref[b], NN))
            pi.append(_dot(xs, wi_ref[b], NN))
        r = jax.nn.sigmoid(jnp.concatenate(pa, axis=1) + ba_ref[...])
        ig = jax.nn.sigmoid(jnp.concatenate(pi, axis=1) + bi_ref[...])
        r_ref[...] = r
        i_ref[...] = ig
        nl = LRU_C * r * _softplus_neg(lam_ref[...])
        a_ref[...] = jnp.exp(-nl)
        th = jnp.tanh(nl)
        u_scr[...] = jnp.sqrt(2.0 * th / (1.0 + th)) * (ig * xb)

        rowid = lax.broadcasted_iota(jnp.int32, (8, C), 0)

        def group(gi, hc):
            r0 = pl.multiple_of(gi * 8, 8)
            a8 = a_ref[pl.ds(r0, 8), :]
            u8 = u_scr[pl.ds(r0, 8), :]
            for d in (1, 2, 4):
                a_sh = jnp.where(rowid >= d, pltpu.roll(a8, d, 0), 1.0)
                u_sh = jnp.where(rowid >= d, pltpu.roll(u8, d, 0), 0.0)
                u8 = a8 * u_sh + u8
                a8 = a8 * a_sh
            h8 = u8 + a8 * hc
            h_ref[pl.ds(r0, 8), :] = h8
            hp_ref[pl.ds(r0, 8), :] = jnp.where(rowid >= 1, pltpu.roll(h8, 1, 0), hc)
            return _bcast_row(h8, 7)

        carry[...] = lax.fori_loop(0, ng, group, carry[...])
        hg_ref[...] = (h_ref[...] * _gelu(y_ref[...])).astype(hg_ref.dtype)

    row = lambda c: pl.BlockSpec((tt, C), lambda i, c=c: (i, c))
    vec = pl.BlockSpec((1, C), lambda i: (0, 0))
    full = lambda shp: pl.BlockSpec(shp, lambda i, n=len(shp): (0,) * n)
    per = tt // HALO
    outs = pl.pallas_call(
        body, grid=(T // tt,),
        in_specs=[row(0), row(1), pl.BlockSpec((HALO, C), lambda i: (jnp.maximum(i * per - 1, 0), 0)),
                  full(cw.shape), vec, full(wa.shape), vec, full(wi.shape), vec, vec],
        out_specs=[row(0)] * 7,
        out_shape=[jax.ShapeDtypeStruct((T, C), F32)] * 6 + [jax.ShapeDtypeStruct((T, C), BF16)],
        scratch_shapes=[pltpu.VMEM((8, C), F32), pltpu.VMEM((tt, C), F32)],
        compiler_params=_cp("arbitrary"), name=name,
    )(z, z, z, cw, cb, wa, ba, wi, bi, lam)
    return outs


def rec_bwd_scan(dhg, h, a, z, *, name):
    T, C = h.shape
    tt = _tile(T, 256, HALO)
    ng = tt // 8
    nb = T // tt

    def body(dhg_ref, h_ref, a_ref, y_ref, dy_ref, yb_ref, ycarry, acarry, g_scr):
        step = pl.program_id(0)

        @pl.when(step == 0)
        def _():
            ycarry[...] = jnp.zeros_like(ycarry)
            acarry[...] = jnp.zeros_like(acarry)

        gate, dgate = _gelu_and_grad(y_ref[...])
        dhg_v = dhg_ref[...].astype(F32)
        dy_ref[...] = (dhg_v * h_ref[...] * dgate).astype(dy_ref.dtype)
        g_scr[...] = dhg_v * gate
        rowid = lax.broadcasted_iota(jnp.int32, (8, C), 0)

        def group(j, c):
            yc, ac = c
            r0 = pl.multiple_of((ng - 1 - j) * 8, 8)
            a8 = a_ref[pl.ds(r0, 8), :]
            y8 = g_scr[pl.ds(r0, 8), :]
            b8 = jnp.where(rowid < 7, pltpu.roll(a8, 7, 0), ac)
            for d in (1, 2, 4):
                y_sh = jnp.where(rowid < 8 - d, pltpu.roll(y8, 8 - d, 0), 0.0)
                b_sh = jnp.where(rowid < 8 - d, pltpu.roll(b8, 8 - d, 0), 1.0)
                y8 = y8 + b8 * y_sh
                b8 = b8 * b_sh
            y8 = y8 + b8 * yc
            yb_ref[pl.ds(r0, 8), :] = y8
            return _bcast_row(y8, 0), _bcast_row(a8, 0)

        yc, ac = lax.fori_loop(0, ng, group, (ycarry[...], acarry[...]))
        ycarry[...] = yc
        acarry[...] = ac

    rev = lambda c: pl.BlockSpec((tt, C), lambda i, c=c: (nb - 1 - i, c))
    return pl.pallas_call(
        body, grid=(nb,), in_specs=[rev(0), rev(0), rev(0), rev(1)], out_specs=[rev(0), rev(0)],
        out_shape=[jax.ShapeDtypeStruct((T, C), BF16), jax.ShapeDtypeStruct((T, C), F32)],
        scratch_shapes=[pltpu.VMEM((8, C), F32), pltpu.VMEM((8, C), F32), pltpu.VMEM((tt, C), F32)],
        compiler_params=_cp("arbitrary"), name=name,
    )(dhg, h, a, z)


def rec_bwd_gates(ybar, hprev, a, r, ig, xb, lam, wa, wi, *, name):
    T, C = xb.shape
    nblk, bd, _ = wa.shape
    tt = _tile(T, 256, 8)
    nb = T // tt

    def body(y_ref, hp_ref, a_ref, r_ref, i_ref, xb_ref, lam_ref, wa_ref, wi_ref,
             dxb_ref, dwa_ref, dwi_ref, dba_ref, dbi_ref, dlam_ref):
        step = pl.program_id(0)

        @pl.when(step == 0)
        def _():
            for ref in (dwa_ref, dwi_ref, dba_ref, dbi_ref, dlam_ref):
                ref[...] = jnp.zeros_like(ref)

        y, av, rv, iv, xv = y_ref[...], a_ref[...], r_ref[...], i_ref[...], xb_ref[...]
        sp = _softplus_neg(lam_ref[...])
        th = jnp.tanh(LRU_C * rv * sp)
        s = jnp.sqrt(2.0 * th / (1.0 + th))
        d_nl = -(y * hp_ref[...] * av) + (y * iv * xv) * (av * av) / s
        dlam_ref[...] += jnp.sum(d_nl * rv, axis=0, keepdims=True) * LRU_C
        dr = d_nl * (LRU_C * sp)
        di = y * s * xv
        dpa = dr * rv * (1.0 - rv)
        dpi = di * iv * (1.0 - iv)
        dba_ref[...] += jnp.sum(dpa, axis=0, keepdims=True)
        dbi_ref[...] += jnp.sum(dpi, axis=0, keepdims=True)
        parts = []
        for b in range(nblk):
            sl = slice(b * bd, (b + 1) * bd)
            xs, da_b, di_b = xv[:, sl], dpa[:, sl], dpi[:, sl]
            dwa_ref[b] += _dot(xs, da_b, TN)
            dwi_ref[b] += _dot(xs, di_b, TN)
            parts.append(_dot(da_b, wa_ref[b], NT) + _dot(di_b, wi_ref[b], NT))
        dxb_ref[...] = y * s * iv + jnp.concatenate(parts, axis=1)

        @pl.when(step == nb - 1)
        def _():
            dlam_ref[...] = dlam_ref[...] * (-jax.nn.sigmoid(-lam_ref[...]))

    row = pl.BlockSpec((tt, C), lambda i: (i, 0))
    vec = pl.BlockSpec((1, C), lambda i: (0, 0))
    wsp = pl.BlockSpec(wa.shape, lambda i: (0, 0, 0))
    return pl.pallas_call(
        body, grid=(nb,), in_specs=[row] * 6 + [vec, wsp, wsp],
        out_specs=[row, wsp, wsp, vec, vec, vec],
        out_shape=[jax.ShapeDtypeStruct((T, C), F32), jax.ShapeDtypeStruct(wa.shape, F32),
                   jax.ShapeDtypeStruct(wa.shape, F32)] + [jax.ShapeDtypeStruct((1, C), F32)] * 3,
        compiler_params=_cp("arbitrary"), name=name,
    )(ybar, hprev, a, r, ig, xb, lam, wa, wi)


def conv_bwd(d, x0, cw, *, name):
    T, C = d.shape
    K = cw.shape[0]
    tt = _tile(T, 256, HALO)
    per = tt // HALO
    nb = T // tt

    def body(d_ref, dn_ref, x_ref, xp_ref, cw_ref, dx_ref, dcw_ref, dcb_ref):
        step = pl.program_id(0)

        @pl.when(step == 0)
        def _():
            dcw_ref[...] = jnp.zeros_like(dcw_ref)
            dcb_ref[...] = jnp.zeros_like(dcb_ref)

        dv = d_ref[...].astype(F32)
        dc = jnp.concatenate([dv, jnp.where(step < nb - 1, dn_ref[...].astype(F32), 0.0)], axis=0)
        xc = jnp.concatenate([jnp.where(step > 0, xp_ref[...].astype(F32), 0.0), x_ref[...].astype(F32)], axis=0)
        acc = None
        rows = []
        for k in range(K):
            sh = K - 1 - k
            term = cw_ref[k:k + 1, :] * dc[sh:sh + tt]
            acc = term if acc is None else acc + term
            rows.append(jnp.sum(dv * xc[HALO - sh:HALO - sh + tt], axis=0, keepdims=True))
        dx_ref[...] = acc.astype(dx_ref.dtype)
        dcw_ref[...] += jnp.concatenate(rows, axis=0)
        dcb_ref[...] += jnp.sum(dv, axis=0, keepdims=True)

    row = pl.BlockSpec((tt, C), lambda i: (i, 0))
    return pl.pallas_call(
        body, grid=(nb,),
        in_specs=[row, pl.BlockSpec((HALO, C), lambda i: (jnp.minimum((i + 1) * per, T // HALO - 1), 0)),
                  row, pl.BlockSpec((HALO, C), lambda i: (jnp.maximum(i * per - 1, 0), 0)),
                  pl.BlockSpec((K, C), lambda i: (0, 0))],
        out_specs=[row, pl.BlockSpec((K, C), lambda i: (0, 0)), pl.BlockSpec((1, C), lambda i: (0, 0))],
        out_shape=[jax.ShapeDtypeStruct((T, C), BF16), jax.ShapeDtypeStruct((K, C), F32),
                   jax.ShapeDtypeStruct((1, C), F32)],
        compiler_params=_cp("arbitrary"), name=name,
    )(d, d, x0, x0, cw)


def ffn_act_fwd(u0, cw, cb, *, n, name):
    T, W = u0.shape
    G = W // (2 * n)
    tt = _tile(T, 256, HALO)
    per = tt // HALO

    def body(u_ref, up_ref, cw_ref, cb_ref, a_ref):
        step = pl.program_id(1)
        xc = jnp.concatenate([jnp.where(step > 0, up_ref[...].astype(F32), 0.0), u_ref[...].astype(F32)], axis=0)
        u = cb_ref[...] + _conv_taps(xc, cw_ref, tt, HALO)
        a_ref[...] = (_gelu(u[:, :n]) * u[:, n:]).astype(a_ref.dtype)

    return pl.pallas_call(
        body, grid=(G, T // tt),
        in_specs=[pl.BlockSpec((tt, 2 * n), lambda j, i: (i, j)),
                  pl.BlockSpec((HALO, 2 * n), lambda j, i: (jnp.maximum(i * per - 1, 0), j)),
                  pl.BlockSpec((cw.shape[0], 2 * n), lambda j, i: (0, j)),
                  pl.BlockSpec((1, 2 * n), lambda j, i: (0, j))],
        out_specs=pl.BlockSpec((tt, n), lambda j, i: (i, j)),
        out_shape=jax.ShapeDtypeStruct((T, G * n), BF16),
        compiler_params=_cp("parallel", "parallel"), name=name)(u0, u0, cw, cb)


def ffn_act_bwd(u0, da, cw, cb, *, n, name):
    T, W = u0.shape
    G = W // (2 * n)
    K = cw.shape[0]
    tt = _tile(T, 128, HALO)
    per = tt // HALO
    nb = T // tt
    ext = tt + HALO

    def body(u_ref, up_ref, un_ref, da_ref, dan_ref, cw_ref, cb_ref, du_ref, dcw_ref, dcb_ref):
        step = pl.program_id(1)

        @pl.when(step == 0)
        def _():
            dcw_ref[...] = jnp.zeros_like(dcw_ref)
            dcb_ref[...] = jnp.zeros_like(dcb_ref)

        xc = jnp.concatenate([jnp.where(step > 0, up_ref[...].astype(F32), 0.0), u_ref[...].astype(F32),
                              un_ref[...].astype(F32)], axis=0)
        u = cb_ref[...] + _conv_taps(xc, cw_ref, ext, HALO)
        dae = jnp.concatenate([da_ref[...].astype(F32), jnp.where(step < nb - 1, dan_ref[...].astype(F32), 0.0)],
                              axis=0)
        gl, dgl = _gelu_and_grad(u[:, :n])
        du = jnp.concatenate([dae * u[:, n:] * dgl, dae * gl], axis=1)
        acc = None
        rows = []
        dut = du[:tt]
        for k in range(K):
            sh = K - 1 - k
            term = cw_ref[k:k + 1, :] * du[sh:sh + tt]
            acc = term if acc is None else acc + term
            rows.append(jnp.sum(dut * xc[HALO - sh:HALO - sh + tt], axis=0, keepdims=True))
        du_ref[...] = acc.astype(du_ref.dtype)
        dcw_ref[...] += jnp.concatenate(rows, axis=0)
        dcb_ref[...] += jnp.sum(dut, axis=0, keepdims=True)

    last = T // HALO - 1
    return pl.pallas_call(
        body, grid=(G, nb),
        in_specs=[pl.BlockSpec((tt, 2 * n), lambda j, i: (i, j)),
                  pl.BlockSpec((HALO, 2 * n), lambda j, i: (jnp.maximum(i * per - 1, 0), j)),
                  pl.BlockSpec((HALO, 2 * n), lambda j, i: (jnp.minimum((i + 1) * per, last), j)),
                  pl.BlockSpec((tt, n), lambda j, i: (i, j)),
                  pl.BlockSpec((HALO, n), lambda j, i: (jnp.minimum((i + 1) * per, last), j)),
                  pl.BlockSpec((K, 2 * n), lambda j, i: (0, j)),
                  pl.BlockSpec((1, 2 * n), lambda j, i: (0, j))],
        out_specs=[pl.BlockSpec((tt, 2 * n), lambda j, i: (i, j)),
                   pl.BlockSpec((K, 2 * n), lambda j, i: (0, j)),
                   pl.BlockSpec((1, 2 * n), lambda j, i: (0, j))],
        out_shape=[jax.ShapeDtypeStruct((T, W), BF16), jax.ShapeDtypeStruct((K, W), F32),
                   jax.ShapeDtypeStruct((1, W), F32)],
        compiler_params=_cp("parallel", "arbitrary"), name=name,
    )(u0, u0, u0, da, da, cw, cb)


def _ffn_shard_of_block(q):
    return (q % 2) * (N_DEV // 2) + q // 2


def _ffn_block_of_shard(s):
    return (s % (N_DEV // 2)) * 2 + s // (N_DEV // 2)


def _group_cols(v):
    lead = v.shape[:-1]
    n = v.shape[-1] // N_DEV
    order = np.array([_ffn_shard_of_block(q) for q in range(N_DEV)])
    return v.reshape(lead + (N_DEV, n))[..., order, :].reshape(v.shape)


def _ungroup_cols(v):
    lead = v.shape[:-1]
    n = v.shape[-1] // N_DEV
    order = np.array([_ffn_block_of_shard(s) for s in range(N_DEV)])
    return v.reshape(lead + (N_DEV, n))[..., order, :].reshape(v.shape)


def local_step(x, target, w):
    depth = w["mix_norm"].shape[0]
    D = x.shape[1]
    dff2 = w["ffn_w_up"][0].shape[1]
    n_up = dff2 // N_DEV
    saved = []
    for l in range(depth):
        j = l // 2
        h = rmsnorm_fwd(x, w["mix_norm"][l:l + 1], name="mix_norm_fwd")
        if l % 2 == 0:
            qkv = mm_nn(h, w["attn_w_qkv"][j], out_dtype=F32, name="qkv_proj")
            ao = attn_fwd(qkv, w["attn_q_gain"][j:j + 1], w["attn_k_gain"][j:j + 1], w["attn_sinks"][j:j + 1],
                          name="attn_fwd")
            x1 = mm_nn(ao, w["attn_w_o"][j], out_dtype=F32, res=x, name="attn_out_proj")
            mix = (qkv, ao)
        else:
            z = mm_nn(h, w["rec_w_in"][j], out_dtype=F32, name="rec_in_proj")
            xb, r, ig, a, hs, hprev, hg = rec_fwd(
                z, w["rec_conv_w"][j], w["rec_conv_b"][j:j + 1], w["rec_w_a"][j], w["rec_b_a"][j:j + 1],
                w["rec_w_i"][j], w["rec_b_i"][j:j + 1], w["rec_lambda"][j:j + 1], name="rec_fwd")
            x1 = mm_nn(hg, w["rec_w_out"][j], out_dtype=F32, res=x, name="rec_out_proj")
            mix = (z, xb, r, ig, a, hs, hprev, hg)
        h2 = rmsnorm_fwd(x1, w["ffn_norm"][l:l + 1], name="ffn_norm_fwd")
        u0 = mm_nn(h2, w["ffn_w_up"][l], out_dtype=BF16, name="ffn_up_proj")
        act = ffn_act_fwd(u0, w["ffn_conv_w"][l], w["ffn_conv_b"][l:l + 1], n=n_up, name="ffn_act_fwd")
        x2 = mm_nn(act, w["ffn_w_down"][l], out_dtype=F32, res=x1, name="ffn_down_proj")
        saved.append((x, h, mix, x1, h2, u0, act))
        x = x2

    loss_vec, dx = loss_head(x, target, name="loss_head")

    g = {k: [None] * w[k].shape[0] if hasattr(w[k], "shape") else [None] * len(w[k]) for k in w}
    for l in reversed(range(depth)):
        j = l // 2
        x0, h, mix, x1, h2, u0, act = saved[l]
        dact = mm_nt(dx, w["ffn_w_down"][l], out_dtype=BF16, name="ffn_down_dx")
        g["ffn_w_down"][l] = mm_tn(act, dx, out_dtype=BF16, name="ffn_down_dw")
        du0, dcw, dcb = ffn_act_bwd(u0, dact, w["ffn_conv_w"][l], w["ffn_conv_b"][l:l + 1], n=n_up, name="ffn_act_bwd")
        g["ffn_conv_w"][l], g["ffn_conv_b"][l] = _ungroup_cols(dcw), _ungroup_cols(dcb)[0]
        g["ffn_w_up"][l] = mm_tn(h2, du0, out_dtype=BF16, col_shards=N_DEV, shard_of_block=_ffn_shard_of_block,
                                 name="ffn_up_dw")
        dh2 = mm_nt(du0, w["ffn_w_up"][l], out_dtype=F32, name="ffn_up_dx")
        dx1, dgf = rmsnorm_bwd(x1, w["ffn_norm"][l:l + 1], dh2, dx, name="ffn_norm_bwd")
        g["ffn_norm"][l] = dgf[0]
        if l % 2 == 0:
            qkv, ao = mix
            dao = mm_nt(dx1, w["attn_w_o"][j], out_dtype=BF16, name="attn_out_dx")
            g["attn_w_o"][j] = mm_tn(ao, dx1, out_dtype=BF16, name="attn_out_dw")
            dq, dk, dv, dqg, dkg, dsk = attn_bwd(qkv, dao, w["attn_q_gain"][j:j + 1], w["attn_k_gain"][j:j + 1],
                                                 w["attn_sinks"][j:j + 1], name="attn_bwd")
            g["attn_q_gain"][j], g["attn_k_gain"][j], g["attn_sinks"][j] = dqg[0], dkg[0], dsk
            dqkv = jnp.concatenate([dq, dk, dv], axis=1)
            g["attn_w_qkv"][j] = mm_tn(h, dqkv, out_dtype=BF16, col_shards=N_DEV, name="qkv_dw")
            dh = mm_nt(dqkv, w["attn_w_qkv"][j], out_dtype=F32, name="qkv_dx")
        else:
            z, xb, r, ig, a, hs, hprev, hg = mix
            dhg = mm_nt(dx1, w["rec_w_out"][j], out_dtype=BF16, name="rec_out_dx")
            g["rec_w_out"][j] = mm_tn(hg, dx1, out_dtype=BF16, name="rec_out_dw")
            dyb, ybar = rec_bwd_scan(dhg, hs, a, z, name="rec_bwd_scan")
            dxb, dwa, dwi, dba, dbi, dlam = rec_bwd_gates(
                ybar, hprev, a, r, ig, xb, w["rec_lambda"][j:j + 1], w["rec_w_a"][j], w["rec_w_i"][j],
                name="rec_bwd_gates")
            dxb0, dcw, dcb = conv_bwd(dxb, z, w["rec_conv_w"][j], name="rec_conv_bwd")
            g["rec_w_a"][j], g["rec_w_i"][j] = dwa, dwi
            g["rec_b_a"][j], g["rec_b_i"][j], g["rec_lambda"][j] = dba[0], dbi[0], dlam[0]
            g["rec_conv_w"][j], g["rec_conv_b"][j] = dcw, dcb[0]
            dz = jnp.concatenate([dxb0, dyb], axis=1)
            g["rec_w_in"][j] = mm_tn(h, dz, out_dtype=BF16, col_shards=N_DEV, name="rec_in_dw")
            dh = mm_nt(dz, w["rec_w_in"][j], out_dtype=F32, name="rec_in_dx")
        dx, dgm = rmsnorm_bwd(x0, w["mix_norm"][l:l + 1], dh, dx1, name="mix_norm_bwd")
        g["mix_norm"][l] = dgm[0]
    return loss_vec, dx, g


HBM = pl.BlockSpec(memory_space=pltpu.HBM)
N_PEER = N_DEV - 1


def _here():
    return lax.axis_index("x"), lax.axis_index("y"), lax.axis_index("c")


def _sid(dev):
    return 4 * dev[0] + 2 * dev[1] + dev[2]


def all_gather(shards, out_structs, windows, *, name):
    n = len(shards)

    def body(*refs):
        ins, outs = refs[:n], refs[n:2 * n]
        send_sems, recv_sems, local_sems = refs[2 * n:]
        x, y, c = _here()
        me, sib = (x, y, c), (x, y, 1 - c)
        chips = [(1 - x, y), (x, 1 - y), (1 - x, 1 - y)]

        def copy(i, k, block, to, src=None):
            dst = windows[i](outs[i], _sid(block))
            return pltpu.make_async_remote_copy(
                src_ref=dst if src is None else src, dst_ref=dst,
                send_sem=send_sems.at[i * N_PEER + k], recv_sem=recv_sems.at[i * N_PEER + k],
                device_id=to, device_id_type=MESH)

        local = [pltpu.make_async_copy(ins[i], windows[i](outs[i], _sid(me)), local_sems.at[i]) for i in range(n)]
        for cp in local:
            cp.start()
        first = []
        for i in range(n):
            first.append(copy(i, 0, me, sib, src=ins[i]))
            first += [copy(i, 1 + j, me, (*chip, c), src=ins[i]) for j, chip in enumerate(chips)]
        for cp in first:
            cp.start()
        passed = []
        for i in range(n):
            for j, chip in enumerate(chips):
                copy(i, 1 + j, (*chip, c), me).wait_recv()
                fwd = copy(i, 4 + j, (*chip, c), sib)
                fwd.start()
                passed.append(fwd)
        for i in range(n):
            copy(i, 0, sib, me).wait_recv()
            for j, chip in enumerate(chips):
                copy(i, 4 + j, (*chip, 1 - c), me).wait_recv()
        for cp in first + passed:
            cp.wait_send()
        for cp in local:
            cp.wait()

    return pl.pallas_call(
        body, in_specs=[HBM] * n, out_specs=[HBM] * n, out_shape=list(out_structs),
        scratch_shapes=[pltpu.SemaphoreType.DMA((n * N_PEER,)), pltpu.SemaphoreType.DMA((n * N_PEER,)),
                        pltpu.SemaphoreType.DMA((n,))],
        name=name)(*shards)


def reduce_scatter_exchange(grads, *, name):
    n = len(grads)

    def body(*refs):
        ins, outs = refs[:n], refs[n:2 * n]
        send_sems, recv_sems, local_sems = refs[2 * n:]
        x, y, c = _here()
        me = (x, y, c)
        peers = []
        for k in range(1, N_DEV):
            kx, ky, kc = (k >> 2) & 1, (k >> 1) & 1, k & 1
            peers.append((1 - x if kx else x, 1 - y if ky else y, 1 - c if kc else c))

        def copy(i, k):
            return pltpu.make_async_remote_copy(
                src_ref=ins[i].at[_sid(peers[k])], dst_ref=outs[i].at[_sid(me)],
                send_sem=send_sems.at[i * N_PEER + k], recv_sem=recv_sems.at[i * N_PEER + k],
                device_id=peers[k], device_id_type=MESH)

        def arrival(i, k):
            return pltpu.make_async_remote_copy(
                src_ref=ins[i].at[_sid(me)], dst_ref=outs[i].at[_sid(peers[k])],
                send_sem=send_sems.at[i * N_PEER + k], recv_sem=recv_sems.at[i * N_PEER + k],
                device_id=peers[k], device_id_type=MESH)

        local = [pltpu.make_async_copy(ins[i].at[_sid(me)], outs[i].at[_sid(me)], local_sems.at[i]) for i in range(n)]
        for cp in local:
            cp.start()
        sends = [copy(i, k) for i in range(n) for k in range(N_PEER)]
        for cp in sends:
            cp.start()
        for i in range(n):
            for k in range(N_PEER):
                arrival(i, k).wait_recv()
        for cp in sends:
            cp.wait_send()
        for cp in local:
            cp.wait()

    return pl.pallas_call(
        body, in_specs=[HBM] * n, out_specs=[HBM] * n,
        out_shape=[jax.ShapeDtypeStruct(g.shape, g.dtype) for g in grads],
        scratch_shapes=[pltpu.SemaphoreType.DMA((n * N_PEER,)), pltpu.SemaphoreType.DMA((n * N_PEER,)),
                        pltpu.SemaphoreType.DMA((n,))],
        name=name)(*grads)


def adamw_family(contribs, w, m, v, *, name):
    L, R, C = w.shape
    S = contribs[0].shape[0]
    tr = _tile(R, max(8, (1 << 20) // (C * S)), 8)
    nr = R // tr
    c1 = 1.0 / (1.0 - ADAM_B1 ** ADAM_STEP)
    c2 = 1.0 / (1.0 - ADAM_B2 ** ADAM_STEP)

    def body(*refs):
        c_refs = refs[:L]
        w_ref, m_ref, v_ref, g_ref, d_ref, nm_ref, nv_ref = refs[L:]
        layer = pl.program_id(0)
        for l in range(L):
            @pl.when(layer == l)
            def _(l=l):
                g = c_refs[l][0].astype(F32)
                for s in range(1, S):
                    g = g + c_refs[l][s].astype(F32)
                mm = ADAM_B1 * m_ref[...] + (1.0 - ADAM_B1) * g
                vv = ADAM_B2 * v_ref[...] + (1.0 - ADAM_B2) * (g * g)
                g_ref[...] = g
                nm_ref[...] = mm
                nv_ref[...] = vv
                d_ref[...] = -ADAM_LR * ((mm * c1) / (jnp.sqrt(vv * c2) + ADAM_EPS) + ADAM_WD * w_ref[...])

    def cspec(l):
        return pl.BlockSpec((S, tr, C), lambda ll, i, l=l: (0, jnp.where(ll == l, i, 0), 0))

    lay = pl.BlockSpec((None, tr, C), lambda ll, i: (ll, i, 0))
    return pl.pallas_call(
        body, grid=(L, nr), in_specs=[cspec(l) for l in range(L)] + [lay] * 3, out_specs=[lay] * 4,
        out_shape=[jax.ShapeDtypeStruct((L, R, C), F32)] * 4,
        compiler_params=_cp("arbitrary", "arbitrary"), name=name)(*contribs, w, m, v)


def sum_slots(a, *, name):
    S, R, C = a.shape
    tr = _tile(R, 256, 8)

    def body(a_ref, o_ref):
        t = a_ref[0]
        for s in range(1, S):
            t = t + a_ref[s]
        o_ref[...] = t

    return pl.pallas_call(
        body, grid=(R // tr,), in_specs=[pl.BlockSpec((S, tr, C), lambda i: (0, i, 0))],
        out_specs=pl.BlockSpec((tr, C), lambda i: (i, 0)), out_shape=jax.ShapeDtypeStruct((R, C), F32),
        compiler_params=_cp("parallel"), name=name)(a)


LANES = 128


def _pack(arrs):
    flat = jnp.concatenate([a.reshape(-1).astype(F32) for a in arrs])
    rows = -(-flat.shape[0] // LANES)
    rows = -(-rows // 8) * 8
    return jnp.pad(flat, (0, rows * LANES - flat.shape[0])).reshape(rows, LANES)


def _unpack(buf, shapes):
    flat = buf.reshape(-1)
    out, off = [], 0
    for shp in shapes:
        size = int(np.prod(shp))
        out.append(flat[off:off + size].reshape(shp))
        off += size
    return out


def _gather_last(g):
    t = jnp.moveaxis(g, 0, -2)
    return t.reshape(t.shape[:-2] + (t.shape[-2] * t.shape[-1],))


def _own_last(full, s):
    n = full.shape[-1] // N_DEV
    t = full.reshape(full.shape[:-1] + (N_DEV, n))
    return lax.dynamic_index_in_dim(t, s, axis=t.ndim - 2, keepdims=False)


BIG = ["attn_w_qkv", "attn_w_o", "rec_w_in", "rec_w_out", "ffn_w_up", "ffn_w_down", "rec_w_a", "rec_w_i"]
SMALL_REPLICATED = ["mix_norm", "ffn_norm", "attn_q_gain", "attn_k_gain", "attn_sinks", "ffn_conv_b"]
SMALL_SHARDED = ["rec_conv_w", "rec_conv_b", "rec_b_a", "rec_b_i", "rec_lambda", "ffn_conv_w"]
SMALL = SMALL_REPLICATED + SMALL_SHARDED
WEIGHTS = ["mix_norm", "ffn_norm", "attn_w_qkv", "attn_q_gain", "attn_k_gain", "attn_sinks", "attn_w_o", "rec_w_in",
           "rec_conv_w", "rec_conv_b", "rec_w_a", "rec_b_a", "rec_w_i", "rec_b_i", "rec_lambda", "rec_w_out",
           "ffn_w_up", "ffn_conv_w", "ffn_conv_b", "ffn_w_down"]


def _col_window(n, block_of_shard=None):
    def win(ref, s):
        q = s if block_of_shard is None else block_of_shard(s)
        return ref.at[:, pl.ds(pl.multiple_of(q * n, 128), n)]
    return win


def _row_window(r):
    return lambda ref, s: ref.at[pl.ds(pl.multiple_of(s * r, 16), r), :]


def _gate_window(r):
    return lambda ref, s: ref.at[:, pl.ds(pl.multiple_of(s * r, 16), r), :]


def _slot_window(ref, s):
    return ref.at[s]


def _gather_layer(p, names, j, tag):
    shards, structs, wins = [], [], []
    for nme in names:
        sh = p[nme][j].astype(BF16)
        if nme in ("attn_w_qkv", "rec_w_in", "ffn_w_up"):
            K, n = sh.shape
            structs.append(jax.ShapeDtypeStruct((K, n * N_DEV), BF16))
            wins.append(_col_window(n, _ffn_block_of_shard if nme == "ffn_w_up" else None))
        elif nme in ("rec_w_a", "rec_w_i"):
            nblk, r, bd = sh.shape
            structs.append(jax.ShapeDtypeStruct((nblk, r * N_DEV, bd), BF16))
            wins.append(_gate_window(r))
        else:
            r, N = sh.shape
            structs.append(jax.ShapeDtypeStruct((r * N_DEV, N), BF16))
            wins.append(_row_window(r))
        shards.append(sh)
    outs = all_gather(shards, structs, wins, name="all_gather_" + tag)
    return dict(zip(names, outs))


ATTN_MATS = ["attn_w_qkv", "attn_w_o", "ffn_w_up", "ffn_w_down"]
REC_MATS = ["rec_w_in", "rec_w_out", "rec_w_a", "rec_w_i", "ffn_w_up", "ffn_w_down"]


def _train_step(p, x, target, mom, vel):
    depth = p["mix_norm"].shape[0]
    s_me = _sid(_here())

    w = {k: [None] * (depth if k.startswith("ffn") else depth // 2) for k in BIG}
    for l in range(depth):
        names = ATTN_MATS if l % 2 == 0 else REC_MATS
        pl_ = {k: (p[k][l:l + 1] if k.startswith("ffn") else p[k][l // 2:l // 2 + 1]) for k in names}
        got = _gather_layer(pl_, names, 0, "attn" if l % 2 == 0 else "rec")
        for k in names:
            w[k][l if k.startswith("ffn") else l // 2] = got[k]

    local_small = [p[k] for k in SMALL_SHARDED]
    packed = _pack(local_small)
    gathered, = all_gather([packed], [jax.ShapeDtypeStruct((N_DEV,) + packed.shape, F32)], [_slot_window],
                           name="all_gather_small")
    per_dev = [_unpack(gathered[s], [a.shape for a in local_small]) for s in range(N_DEV)]
    for i, k in enumerate(SMALL_SHARDED):
        w[k] = _gather_last(jnp.stack([per_dev[s][i] for s in range(N_DEV)]))
    for k in SMALL_REPLICATED:
        w[k] = p[k]
    nrec = w["rec_b_a"].shape[0]
    w["rec_b_a"] = w["rec_b_a"].reshape(nrec, -1)
    w["rec_b_i"] = w["rec_b_i"].reshape(nrec, -1)
    w["ffn_conv_w"] = _group_cols(w["ffn_conv_w"])
    w["ffn_conv_b"] = _group_cols(w["ffn_conv_b"])

    loss_vec, dx, g = local_step(x[0], target[0], w)
    loss = lax.psum(jnp.sum(loss_vec), ("x", "y", "c"))

    contribs = {k: [None] * len(w[k]) for k in BIG}
    for l in range(depth):
        names = ATTN_MATS if l % 2 == 0 else REC_MATS
        send = []
        for k in names:
            t = g[k][l if k.startswith("ffn") else l // 2]
            if k in ("rec_w_a", "rec_w_i"):
                nblk, bd, _ = t.shape
                t = jnp.transpose(t.reshape(nblk, N_DEV, bd // N_DEV, bd), (1, 0, 2, 3)).astype(BF16)
            elif k in ("attn_w_o", "rec_w_out", "ffn_w_down"):
                t = t.reshape((N_DEV, t.shape[0] // N_DEV) + t.shape[1:])
            send.append(t)
        recv = reduce_scatter_exchange(send, name="grad_exchange_" + ("attn" if l % 2 == 0 else "rec"))
        for k, r in zip(names, recv):
            contribs[k][l if k.startswith("ffn") else l // 2] = r

    out = {}
    for k in BIG:
        shp = p[k].shape
        L = shp[0]
        C = shp[-1]
        R = int(np.prod(shp[1:-1]))
        cs = [c.reshape(N_DEV, R, C) for c in contribs[k]]
        res = adamw_family(cs, p[k].reshape(L, R, C), mom[k].reshape(L, R, C), vel[k].reshape(L, R, C),
                           name="adamw_" + k)
        out[k] = [t.reshape(shp) for t in res]

    gsmall = [jnp.stack(g[k]) for k in SMALL]
    gp = _pack(gsmall)
    gall, = all_gather([gp], [jax.ShapeDtypeStruct((N_DEV,) + gp.shape, F32)], [_slot_window],
                       name="all_gather_small_grads")
    gsum = _unpack(sum_slots(gall, name="sum_small_grads"), [a.shape for a in gsmall])
    glocal = []
    for k, t in zip(SMALL, gsum):
        if k in SMALL_SHARDED:
            t = _own_last(t.reshape(p[k].shape[:-1] + (p[k].shape[-1] * N_DEV,)), s_me)
        glocal.append(t.reshape(p[k].shape))
    wp, mp, vp, gpk = (_pack([d[k] for k in SMALL]) for d in (p, mom, vel, dict(zip(SMALL, glocal))))
    res = adamw_family([gpk[None]], wp[None], mp[None], vp[None], name="adamw_small")
    shapes = [p[k].shape for k in SMALL]
    unp = [_unpack(t[0], shapes) for t in res]
    for i, k in enumerate(SMALL):
        out[k] = [glocal[i], unp[1][i], unp[2][i], unp[3][i]]

    return (loss, dx[None]) + tuple(out[k][q] for q in range(4) for k in WEIGHTS)


def kernel(x, mix_norm, ffn_norm, attn_w_qkv, attn_q_gain, attn_k_gain, attn_sinks, attn_w_o, rec_w_in, rec_conv_w, rec_conv_b, rec_w_a, rec_b_a, rec_w_i, rec_b_i, rec_lambda, rec_w_out, ffn_w_up, ffn_conv_w, ffn_conv_b, ffn_w_down, loss_target, m_mix_norm, m_ffn_norm, m_attn_w_qkv, m_attn_q_gain, m_attn_k_gain, m_attn_sinks, m_attn_w_o, m_rec_w_in, m_rec_conv_w, m_rec_conv_b, m_rec_w_a, m_rec_b_a, m_rec_w_i, m_rec_b_i, m_rec_lambda, m_rec_w_out, m_ffn_w_up, m_ffn_conv_w, m_ffn_conv_b, m_ffn_w_down, v_mix_norm, v_ffn_norm, v_attn_w_qkv, v_attn_q_gain, v_attn_k_gain, v_attn_sinks, v_attn_w_o, v_rec_w_in, v_rec_conv_w, v_rec_conv_b, v_rec_w_a, v_rec_b_a, v_rec_w_i, v_rec_b_i, v_rec_lambda, v_rec_w_out, v_ffn_w_up, v_ffn_conv_w, v_ffn_conv_b, v_ffn_w_down):
    p = dict(zip(WEIGHTS, (mix_norm, ffn_norm, attn_w_qkv, attn_q_gain, attn_k_gain, attn_sinks, attn_w_o, rec_w_in,
                           rec_conv_w, rec_conv_b, rec_w_a, rec_b_a, rec_w_i, rec_b_i, rec_lambda, rec_w_out,
                           ffn_w_up, ffn_conv_w, ffn_conv_b, ffn_w_down)))
    mom = dict(zip(WEIGHTS, (m_mix_norm, m_ffn_norm, m_attn_w_qkv, m_attn_q_gain, m_attn_k_gain, m_attn_sinks,
                             m_attn_w_o, m_rec_w_in, m_rec_conv_w, m_rec_conv_b, m_rec_w_a, m_rec_b_a, m_rec_w_i,
                             m_rec_b_i, m_rec_lambda, m_rec_w_out, m_ffn_w_up, m_ffn_conv_w, m_ffn_conv_b,
                             m_ffn_w_down)))
    vel = dict(zip(WEIGHTS, (v_mix_norm, v_ffn_norm, v_attn_w_qkv, v_attn_q_gain, v_attn_k_gain, v_attn_sinks,
                             v_attn_w_o, v_rec_w_in, v_rec_conv_w, v_rec_conv_b, v_rec_w_a, v_rec_b_a, v_rec_w_i,
                             v_rec_b_i, v_rec_lambda, v_rec_w_out, v_ffn_w_up, v_ffn_conv_w, v_ffn_conv_b,
                             v_ffn_w_down)))
    return _train_step(p, x, loss_target, mom, vel)
```

```python
import functools
import math

import jax
import jax.numpy as jnp
import numpy as np
from jax import lax
from jax.experimental import pallas as pl
from jax.experimental.pallas import tpu as pltpu

F32 = jnp.float32
BF16 = jnp.bfloat16

N_DEV = 8
HEAD_DIM = 64
GROUP = 4
BLOCK = 128
LRU_C = 8.0
EPS = 1e-6
HALO = 16
ADAM_LR, ADAM_B1, ADAM_B2, ADAM_EPS, ADAM_WD, ADAM_STEP = 0.001, 0.9, 0.999, 1e-08, 0.01, 10
VMEM_LIMIT = 56 * 1024 * 1024
MESH = pl.DeviceIdType.MESH
GELU_C = math.sqrt(2.0 / math.pi)


def _cp(*sem, vmem=VMEM_LIMIT):
    return pltpu.CompilerParams(dimension_semantics=tuple(sem), vmem_limit_bytes=vmem)


def _tile(dim, pref, mult=128):
    if dim <= pref:
        return dim
    t = (pref // mult) * mult
    while t >= mult:
        if dim % t == 0:
            return t
        t -= mult
    return dim


def _gelu(x):
    th = jnp.tanh(GELU_C * (x + 0.044715 * x * x * x))
    return 0.5 * x * (1.0 + th)


def _gelu_and_grad(x):
    x2 = x * x
    th = jnp.tanh(GELU_C * (x + 0.044715 * x2 * x))
    g = 0.5 * x * (1.0 + th)
    dg = 0.5 * (1.0 + th) + 0.5 * x * (1.0 - th * th) * GELU_C * (1.0 + 3.0 * 0.044715 * x2)
    return g, dg


def _dot(a, b, dims):
    return lax.dot_general(a.astype(BF16), b.astype(BF16), (dims, ((), ())), preferred_element_type=F32)


NN = ((1,), (0,))
NT = ((1,), (1,))
TN = ((0,), (0,))


def _matmul(a, b, *, dims, grid, a_spec, b_spec, o_spec, out_shape, acc_shape, res=None, res_spec=None,
            carry=None, name):
    ni, nj, nk = grid
    nres = 0 if res is None else 1
    ncar = 0 if carry is None else len(carry["arrays"])

    def body(*refs):
        a_ref, b_ref = refs[0], refs[1]
        r_ref = refs[2] if nres else None
        car_in = refs[2 + nres:2 + nres + ncar]
        o_ref = refs[2 + nres + ncar]
        car_out = refs[3 + nres + ncar:3 + nres + 2 * ncar]
        scratch = refs[3 + nres + 2 * ncar:]
        i, j, k = pl.program_id(0), pl.program_id(1), pl.program_id(2)

        if carry is not None:
            @pl.when((i == 0) & (j == 0) & (k == 0))
            def _():
                carry["start"](car_in, car_out, scratch[1:])

        def finish(acc):
            if r_ref is not None:
                acc = acc + r_ref[...].astype(F32)
            o_ref[...] = acc.astype(o_ref.dtype)

        if nk == 1:
            finish(_dot(a_ref[...], b_ref[...], dims))
        else:
            acc_ref = scratch[0]

            @pl.when(k == 0)
            def _():
                acc_ref[...] = _dot(a_ref[...], b_ref[...], dims)

            if nk > 2:
                @pl.when((k > 0) & (k < nk - 1))
                def _():
                    acc_ref[...] += _dot(a_ref[...], b_ref[...], dims)

            @pl.when(k == nk - 1)
            def _():
                finish(acc_ref[...] + _dot(a_ref[...], b_ref[...], dims))

        if carry is not None:
            @pl.when((i == ni - 1) & (j == nj - 1) & (k == nk - 1))
            def _():
                carry["finish"](car_in, car_out, scratch[1:])

    in_specs = [a_spec, b_spec] + ([res_spec] if nres else [])
    args = (a, b) + ((res,) if nres else ())
    out_specs, out_shapes = [o_spec], [out_shape]
    scratch_shapes = [pltpu.VMEM(acc_shape if nk > 1 else (8, 128), F32)]
    sem = ("parallel", "parallel", "arbitrary")
    if carry is not None:
        in_specs += [HBM] * ncar
        args += tuple(carry["arrays"])
        out_specs += [HBM] * ncar
        out_shapes += list(carry["out_structs"])
        scratch_shapes += carry["sems"]
        sem = ("arbitrary", "arbitrary", "arbitrary")
    outs = pl.pallas_call(
        body, grid=grid, in_specs=in_specs, out_specs=out_specs, out_shape=out_shapes,
        scratch_shapes=scratch_shapes, compiler_params=_cp(*sem), name=name,
    )(*args)
    return outs[0] if carry is None else (outs[0], list(outs[1:]))


def mm_nn(a, b, *, out_dtype, res=None, carry=None, name):
    M, K = a.shape
    N = b.shape[1]
    tm, tn, tk = _tile(M, 1024), _tile(N, 1024), _tile(K, 2048)
    return _matmul(
        a, b, dims=NN, grid=(M // tm, N // tn, K // tk),
        a_spec=pl.BlockSpec((tm, tk), lambda i, j, k: (i, k)),
        b_spec=pl.BlockSpec((tk, tn), lambda i, j, k: (k, j)),
        o_spec=pl.BlockSpec((tm, tn), lambda i, j, k: (i, j)),
        out_shape=jax.ShapeDtypeStruct((M, N), out_dtype), acc_shape=(tm, tn),
        res=res, res_spec=pl.BlockSpec((tm, tn), lambda i, j, k: (i, j)), carry=carry, name=name)


def mm_nt(a, b, *, out_dtype, res=None, carry=None, name):
    M, N = a.shape
    K = b.shape[0]
    tm, tn, tk = _tile(M, 1024), _tile(K, 1024), _tile(N, 2048)
    return _matmul(
        a, b, dims=NT, grid=(M // tm, K // tn, N // tk),
        a_spec=pl.BlockSpec((tm, tk), lambda i, j, k: (i, k)),
        b_spec=pl.BlockSpec((tn, tk), lambda i, j, k: (j, k)),
        o_spec=pl.BlockSpec((tm, tn), lambda i, j, k: (i, j)),
        out_shape=jax.ShapeDtypeStruct((M, K), out_dtype), acc_shape=(tm, tn),
        res=res, res_spec=pl.BlockSpec((tm, tn), lambda i, j, k: (i, j)), carry=carry, name=name)


def mm_tn(a, b, *, out_dtype, col_shards=None, shard_of_block=None, carry=None, name):
    T, K = a.shape
    N = b.shape[1]
    tt = _tile(T, 2048)
    tm = _tile(K, 1024)
    if col_shards is None:
        tn = _tile(N, 1024)
        o_spec = pl.BlockSpec((tm, tn), lambda i, j, k: (i, j))
        out_shape = jax.ShapeDtypeStruct((K, N), out_dtype)
    else:
        n = N // col_shards
        tn = _tile(n, 1536)
        per = n // tn
        sob = shard_of_block if shard_of_block is not None else (lambda s: s)
        o_spec = pl.BlockSpec((None, tm, tn), lambda i, j, k: (sob(j // per), i, j % per))
        out_shape = jax.ShapeDtypeStruct((col_shards, K, n), out_dtype)
    return _matmul(
        a, b, dims=TN, grid=(K // tm, N // tn, T // tt),
        a_spec=pl.BlockSpec((tt, tm), lambda i, j, k: (k, i)),
        b_spec=pl.BlockSpec((tt, tn), lambda i, j, k: (k, j)),
        o_spec=o_spec, out_shape=out_shape, acc_shape=(tm, tn), carry=carry, name=name)


def rmsnorm_fwd(x, g, *, name):
    T, D = x.shape
    tm = _tile(T, 512, 8)

    def body(x_ref, g_ref, o_ref):
        xv = x_ref[...]
        r = lax.rsqrt(jnp.mean(xv * xv, axis=-1, keepdims=True) + EPS)
        o_ref[...] = (xv * r * g_ref[...]).astype(o_ref.dtype)

    return pl.pallas_call(
        body, grid=(T // tm,),
        in_specs=[pl.BlockSpec((tm, D), lambda i: (i, 0)), pl.BlockSpec((1, D), lambda i: (0, 0))],
        out_specs=pl.BlockSpec((tm, D), lambda i: (i, 0)),
        out_shape=jax.ShapeDtypeStruct((T, D), BF16), compiler_params=_cp("parallel"), name=name)(x, g)


def rmsnorm_bwd(x, g, dh, dres, *, name):
    T, D = x.shape
    tm = _tile(T, 512, 8)

    def body(x_ref, g_ref, dh_ref, dres_ref, dx_ref, dg_ref):
        @pl.when(pl.program_id(0) == 0)
        def _():
            dg_ref[...] = jnp.zeros_like(dg_ref)

        xv = x_ref[...]
        dh_v = dh_ref[...].astype(F32)
        r = lax.rsqrt(jnp.mean(xv * xv, axis=-1, keepdims=True) + EPS)
        u = dh_v * g_ref[...]
        dot = jnp.mean(u * xv, axis=-1, keepdims=True)
        dx_ref[...] = dres_ref[...] + r * u - xv * (r * r * r * dot)
        dg_ref[...] += jnp.sum(dh_v * xv * r, axis=0, keepdims=True)

    row = pl.BlockSpec((tm, D), lambda i: (i, 0))
    vec = pl.BlockSpec((1, D), lambda i: (0, 0))
    return pl.pallas_call(
        body, grid=(T // tm,), in_specs=[row, vec, row, row], out_specs=[row, vec],
        out_shape=[jax.ShapeDtypeStruct((T, D), F32), jax.ShapeDtypeStruct((1, D), F32)],
        compiler_params=_cp("arbitrary"), name=name)(x, g, dh, dres)


def loss_head(y, target, *, name):
    T, D = y.shape
    tm = _tile(T, 512, 8)

    def body(y_ref, t_ref, l_ref, dy_ref):
        @pl.when(pl.program_id(0) == 0)
        def _():
            l_ref[...] = jnp.zeros_like(l_ref)

        e = y_ref[...] - t_ref[...]
        dy_ref[...] = e * (1.0 / D)
        l_ref[...] += jnp.sum(e * e, axis=0, keepdims=True) * (0.5 / D)

    row = pl.BlockSpec((tm, D), lambda i: (i, 0))
    vec = pl.BlockSpec((1, D), lambda i: (0, 0))
    return pl.pallas_call(
        body, grid=(T // tm,), in_specs=[row, row], out_specs=[vec, row],
        out_shape=[jax.ShapeDtypeStruct((1, D), F32), jax.ShapeDtypeStruct((T, D), F32)],
        compiler_params=_cp("arbitrary"), name=name)(y, target)


PAIR = 2
QW = PAIR * GROUP * HEAD_DIM
KW = PAIR * HEAD_DIM
NEG = -1e30


def _headnorm(x):
    r = lax.rsqrt(jnp.mean(x * x, axis=-1, keepdims=True) + EPS)
    return x * r, r


def _attn_mask(n):
    qi = lax.broadcasted_iota(jnp.int32, (BLOCK, 2 * BLOCK), 0)
    kj = lax.broadcasted_iota(jnp.int32, (BLOCK, 2 * BLOCK), 1)
    rel = qi + BLOCK - kj
    m = (rel >= 0) & (rel < BLOCK) & ((kj >= BLOCK) | (n > 0))
    return jnp.concatenate([m] * GROUP, axis=0)


def _attn_probs(q_ref, kp_ref, kc_ref, qg, kg, sink_ref, mask4, p, hh):
    lo = hh * HEAD_DIM
    k_raw = jnp.concatenate([kp_ref[:, lo:lo + HEAD_DIM], kc_ref[:, lo:lo + HEAD_DIM]], axis=0)
    kn, kr = _headnorm(k_raw)
    khat = kn * kg
    q_raw, qn, qr = [], [], []
    for g in range(GROUP):
        c0 = (hh * GROUP + g) * HEAD_DIM
        x = q_ref[:, c0:c0 + HEAD_DIM]
        xn, r = _headnorm(x)
        q_raw.append(x)
        qn.append(xn)
        qr.append(r)
    q4 = jnp.concatenate([xn * qg for xn in qn], axis=0)
    s = _dot(q4, khat, NT) * (HEAD_DIM ** -0.5)
    sink = jnp.concatenate(
        [jnp.full((BLOCK, 1), sink_ref[0, p * PAIR * GROUP + hh * GROUP + g], F32) for g in range(GROUP)], axis=0)
    m = jnp.maximum(jnp.max(jnp.where(mask4, s, NEG), axis=-1, keepdims=True), sink)
    e = jnp.where(mask4, jnp.exp(s - m), 0.0)
    es = jnp.exp(sink - m)
    inv = 1.0 / (jnp.sum(e, axis=-1, keepdims=True) + es)
    return e * inv, es * inv, q4, khat, (q_raw, qn, qr), (k_raw, kn, kr)


def attn_fwd(qkv, q_gain, k_gain, sinks, *, name):
    T, W = qkv.shape
    hq = W // HEAD_DIM * GROUP // (GROUP + 2)
    dq = hq * HEAD_DIM
    npair = hq // (GROUP * PAIR)
    nb = T // BLOCK
    k0 = dq // KW
    v0 = k0 + npair

    def body(q_ref, kp_ref, kc_ref, vp_ref, vc_ref, qg_ref, kg_ref, sink_ref, o_ref):
        p, n = pl.program_id(0), pl.program_id(1)
        mask4 = _attn_mask(n)
        outs = []
        for hh in range(PAIR):
            lo = hh * HEAD_DIM
            pr, _, _, _, _, _ = _attn_probs(q_ref, kp_ref, kc_ref, qg_ref[...], kg_ref[...], sink_ref, mask4, p, hh)
            vwin = jnp.concatenate([vp_ref[:, lo:lo + HEAD_DIM], vc_ref[:, lo:lo + HEAD_DIM]], axis=0)
            o4 = _dot(pr, vwin, NN)
            outs += [o4[g * BLOCK:(g + 1) * BLOCK] for g in range(GROUP)]
        o_ref[...] = jnp.concatenate(outs, axis=1).astype(o_ref.dtype)

    prev = lambda n: jnp.maximum(n - 1, 0)
    vec = pl.BlockSpec((1, HEAD_DIM), lambda p, n: (0, 0))
    return pl.pallas_call(
        body, grid=(npair, nb),
        in_specs=[pl.BlockSpec((BLOCK, QW), lambda p, n: (n, p)),
                  pl.BlockSpec((BLOCK, KW), lambda p, n: (prev(n), k0 + p)),
                  pl.BlockSpec((BLOCK, KW), lambda p, n: (n, k0 + p)),
                  pl.BlockSpec((BLOCK, KW), lambda p, n: (prev(n), v0 + p)),
                  pl.BlockSpec((BLOCK, KW), lambda p, n: (n, v0 + p)),
                  vec, vec, pl.BlockSpec(memory_space=pltpu.SMEM)],
        out_specs=pl.BlockSpec((BLOCK, QW), lambda p, n: (n, p)),
        out_shape=jax.ShapeDtypeStruct((T, dq), BF16),
        compiler_params=_cp("parallel", "parallel"), name=name)(qkv, qkv, qkv, qkv, qkv, q_gain, k_gain, sinks)


def attn_bwd(qkv, d_out, q_gain, k_gain, sinks, *, name):
    T, W = qkv.shape
    hq = W // HEAD_DIM * GROUP // (GROUP + 2)
    dq_w = hq * HEAD_DIM
    npair = hq // (GROUP * PAIR)
    nb = T // BLOCK
    k0 = dq_w // KW
    v0 = k0 + npair

    def body(q_ref, kp_ref, kc_ref, vp_ref, vc_ref, do_ref, qg_ref, kg_ref, sink_ref,
             dq_ref, dk_ref, dv_ref, dqg_ref, dkg_ref, dsink_ref, dk_carry, dv_carry):
        p, i = pl.program_id(0), pl.program_id(1)
        n = nb - 1 - i

        @pl.when(i == 0)
        def _():
            dk_carry[...] = jnp.zeros_like(dk_carry)
            dv_carry[...] = jnp.zeros_like(dv_carry)
            dqg_ref[...] = jnp.zeros_like(dqg_ref)
            dkg_ref[...] = jnp.zeros_like(dkg_ref)
            dsink_ref[...] = jnp.zeros_like(dsink_ref)

        qg, kg = qg_ref[...], kg_ref[...]
        mask4 = _attn_mask(n)
        dq_parts, dk_parts, dv_parts, dsink_rows = [], [], [], []
        dqg_acc = jnp.zeros((1, HEAD_DIM), F32)
        dkg_acc = jnp.zeros((1, HEAD_DIM), F32)
        for hh in range(PAIR):
            lo = hh * HEAD_DIM
            pr, psink, q4, khat, (q_raw, qn, qr), (k_raw, kn, kr) = _attn_probs(
                q_ref, kp_ref, kc_ref, qg, kg, sink_ref, mask4, p, hh)
            vwin = jnp.concatenate([vp_ref[:, lo:lo + HEAD_DIM], vc_ref[:, lo:lo + HEAD_DIM]], axis=0)
            do4 = jnp.concatenate(
                [do_ref[:, (hh * GROUP + g) * HEAD_DIM:(hh * GROUP + g + 1) * HEAD_DIM] for g in range(GROUP)],
                axis=0).astype(F32)
            dp = _dot(do4, vwin, NT)
            delta = jnp.sum(pr * dp, axis=-1, keepdims=True)
            ds = pr * (dp - delta) * (HEAD_DIM ** -0.5)
            dsk = -psink * delta
            for g in range(GROUP):
                tot = jnp.sum(dsk[g * BLOCK:(g + 1) * BLOCK], axis=0, keepdims=True)
                dsink_rows.append(jnp.broadcast_to(tot, (1, 128)))
            dq4 = _dot(ds, khat, NN)
            dkhat_win = _dot(ds, q4, TN)
            dv_win = _dot(pr, do4, TN)
            for g in range(GROUP):
                dqh = dq4[g * BLOCK:(g + 1) * BLOCK]
                dqg_acc += jnp.sum(dqh * qn[g], axis=0, keepdims=True)
                dqn = dqh * qg
                r = qr[g]
                dq_parts.append(r * dqn - q_raw[g] * (r * r * r * jnp.mean(dqn * q_raw[g], axis=-1, keepdims=True)))
            dkh = dkhat_win[BLOCK:] + dk_carry[hh]
            dk_carry[hh] = dkhat_win[:BLOCK]
            dkg_acc += jnp.sum(dkh * kn[BLOCK:], axis=0, keepdims=True)
            dkn = dkh * kg
            r = kr[BLOCK:]
            kc_raw = k_raw[BLOCK:]
            dk_parts.append(r * dkn - kc_raw * (r * r * r * jnp.mean(dkn * kc_raw, axis=-1, keepdims=True)))
            dv_parts.append(dv_win[BLOCK:] + dv_carry[hh])
            dv_carry[hh] = dv_win[:BLOCK]
        dq_ref[...] = jnp.concatenate(dq_parts, axis=1).astype(dq_ref.dtype)
        dk_ref[...] = jnp.concatenate(dk_parts, axis=1).astype(dk_ref.dtype)
        dv_ref[...] = jnp.concatenate(dv_parts, axis=1).astype(dv_ref.dtype)
        dqg_ref[...] += dqg_acc
        dkg_ref[...] += dkg_acc
        dsink_ref[...] += jnp.concatenate(dsink_rows, axis=0)

    rev = lambda i: nb - 1 - i
    prev = lambda i: jnp.maximum(nb - 2 - i, 0)
    vec = pl.BlockSpec((1, HEAD_DIM), lambda p, i: (0, 0))
    acc64 = pl.BlockSpec((None, 1, HEAD_DIM), lambda p, i: (p, 0, 0))
    dq, dk, dv, dqg, dkg, dsink = pl.pallas_call(
        body, grid=(npair, nb),
        in_specs=[pl.BlockSpec((BLOCK, QW), lambda p, i: (rev(i), p)),
                  pl.BlockSpec((BLOCK, KW), lambda p, i: (prev(i), k0 + p)),
                  pl.BlockSpec((BLOCK, KW), lambda p, i: (rev(i), k0 + p)),
                  pl.BlockSpec((BLOCK, KW), lambda p, i: (prev(i), v0 + p)),
                  pl.BlockSpec((BLOCK, KW), lambda p, i: (rev(i), v0 + p)),
                  pl.BlockSpec((BLOCK, QW), lambda p, i: (rev(i), p)),
                  vec, vec, pl.BlockSpec(memory_space=pltpu.SMEM)],
        out_specs=[pl.BlockSpec((BLOCK, QW), lambda p, i: (rev(i), p)),
                   pl.BlockSpec((BLOCK, KW), lambda p, i: (rev(i), p)),
                   pl.BlockSpec((BLOCK, KW), lambda p, i: (rev(i), p)),
                   acc64, acc64,
                   pl.BlockSpec((None, PAIR * GROUP, 128), lambda p, i: (p, 0, 0))],
        out_shape=[jax.ShapeDtypeStruct((T, dq_w), BF16),
                   jax.ShapeDtypeStruct((T, npair * KW), BF16),
                   jax.ShapeDtypeStruct((T, npair * KW), BF16),
                   jax.ShapeDtypeStruct((npair, 1, HEAD_DIM), F32),
                   jax.ShapeDtypeStruct((npair, 1, HEAD_DIM), F32),
                   jax.ShapeDtypeStruct((npair, PAIR * GROUP, 128), F32)],
        scratch_shapes=[pltpu.VMEM((PAIR, BLOCK, HEAD_DIM), F32), pltpu.VMEM((PAIR, BLOCK, HEAD_DIM), F32)],
        compiler_params=_cp("parallel", "arbitrary"), name=name,
    )(qkv, qkv, qkv, qkv, qkv, d_out, q_gain, k_gain, sinks)
    return dq, dk, dv, jnp.sum(dqg, axis=0), jnp.sum(dkg, axis=0), dsink[:, :, 0].reshape(-1)


def _softplus_neg(lam):
    return jnp.maximum(-lam, 0.0) + jnp.log1p(jnp.exp(-jnp.abs(lam)))


def _conv_taps(xc, cw_ref, tt, lead):
    K = cw_ref.shape[0]
    acc = None
    for k in range(K):
        off = lead - (K - 1 - k)
        term = cw_ref[k:k + 1, :] * xc[off:off + tt]
        acc = term if acc is None else acc + term
    return acc


def _bcast_row(x, row):
    return jnp.broadcast_to(x[row:row + 1, :], x.shape)


def rec_fwd(z, cw, cb, wa, ba, wi, bi, lam, *, name):
    T, C2 = z.shape
    C = C2 // 2
    nblk, bd, _ = wa.shape
    tt = _tile(T, 128, HALO)
    ng = tt // 8

    def body(x_ref, y_ref, halo_ref, cw_ref, cb_ref, wa_ref, ba_ref, wi_ref, bi_ref, lam_ref,
             xb_ref, r_ref, i_ref, a_ref, h_ref, hp_ref, hg_ref, carry, u_scr):
        step = pl.program_id(0)

        @pl.when(step == 0)
        def _():
            carry[...] = jnp.zeros_like(carry)

        halo = jnp.where(step > 0, halo_ref[...], 0.0)
        xc = jnp.concatenate([halo, x_ref[...]], axis=0)
        xb = cb_ref[...] + _conv_taps(xc, cw_ref, tt, HALO)
        xb_ref[...] = xb
        pa, pi = [], []
        for b in range(nblk):
            xs = xb[:, b * bd:(b + 1) * bd]
            pa.append(_dot(xs, wa_ref[b], NN))
            pi.append(_dot(xs, wi_ref[b], NN))
        r = jax.nn.sigmoid(jnp.concatenate(pa, axis=1) + ba_ref[...])
        ig = jax.nn.sigmoid(jnp.concatenate(pi, axis=1) + bi_ref[...])
        r_ref[...] = r
        i_ref[...] = ig
        nl = LRU_C * r * _softplus_neg(lam_ref[...])
        a_ref[...] = jnp.exp(-nl)
        th = jnp.tanh(nl)
        u_scr[...] = jnp.sqrt(2.0 * th / (1.0 + th)) * (ig * xb)

        rowid = lax.broadcasted_iota(jnp.int32, (8, C), 0)

        def group(gi, hc):
            r0 = pl.multiple_of(gi * 8, 8)
            a8 = a_ref[pl.ds(r0, 8), :]
            u8 = u_scr[pl.ds(r0, 8), :]
            for d in (1, 2, 4):
                a_sh = jnp.where(rowid >= d, pltpu.roll(a8, d, 0), 1.0)
                u_sh = jnp.where(rowid >= d, pltpu.roll(u8, d, 0), 0.0)
                u8 = a8 * u_sh + u8
                a8 = a8 * a_sh
            h8 = u8 + a8 * hc
            h_ref[pl.ds(r0, 8), :] = h8
            hp_ref[pl.ds(r0, 8), :] = jnp.where(rowid >= 1, pltpu.roll(h8, 1, 0), hc)
            return _bcast_row(h8, 7)

        carry[...] = lax.fori_loop(0, ng, group, carry[...])
        hg_ref[...] = (h_ref[...] * _gelu(y_ref[...])).astype(hg_ref.dtype)

    row = lambda c: pl.BlockSpec((tt, C), lambda i, c=c: (i, c))
    vec = pl.BlockSpec((1, C), lambda i: (0, 0))
    full = lambda shp: pl.BlockSpec(shp, lambda i, n=len(shp): (0,) * n)
    per = tt // HALO
    outs = pl.pallas_call(
        body, grid=(T // tt,),
        in_specs=[row(0), row(1), pl.BlockSpec((HALO, C), lambda i: (jnp.maximum(i * per - 1, 0), 0)),
                  full(cw.shape), vec, full(wa.shape), vec, full(wi.shape), vec, vec],
        out_specs=[row(0)] * 7,
        out_shape=[jax.ShapeDtypeStruct((T, C), F32)] * 6 + [jax.ShapeDtypeStruct((T, C), BF16)],
        scratch_shapes=[pltpu.VMEM((8, C), F32), pltpu.VMEM((tt, C), F32)],
        compiler_params=_cp("arbitrary"), name=name,
    )(z, z, z, cw, cb, wa, ba, wi, bi, lam)
    return outs


def rec_bwd_scan(dhg, h, a, z, *, name):
    T, C = h.shape
    tt = _tile(T, 256, HALO)
    ng = tt // 8
    nb = T // tt

    def body(dhg_ref, h_ref, a_ref, y_ref, dy_ref, yb_ref, ycarry, acarry, g_scr):
        step = pl.program_id(0)

        @pl.when(step == 0)
        def _():
            ycarry[...] = jnp.zeros_like(ycarry)
            acarry[...] = jnp.zeros_like(acarry)

        gate, dgate = _gelu_and_grad(y_ref[...])
        dhg_v = dhg_ref[...].astype(F32)
        dy_ref[...] = (dhg_v * h_ref[...] * dgate).astype(dy_ref.dtype)
        g_scr[...] = dhg_v * gate
        rowid = lax.broadcasted_iota(jnp.int32, (8, C), 0)

        def group(j, c):
            yc, ac = c
            r0 = pl.multiple_of((ng - 1 - j) * 8, 8)
            a8 = a_ref[pl.ds(r0, 8), :]
            y8 = g_scr[pl.ds(r0, 8), :]
            b8 = jnp.where(rowid < 7, pltpu.roll(a8, 7, 0), ac)
            for d in (1, 2, 4):
                y_sh = jnp.where(rowid < 8 - d, pltpu.roll(y8, 8 - d, 0), 0.0)
                b_sh = jnp.where(rowid < 8 - d, pltpu.roll(b8, 8 - d, 0), 1.0)
                y8 = y8 + b8 * y_sh
                b8 = b8 * b_sh
            y8 = y8 + b8 * yc
            yb_ref[pl.ds(r0, 8), :] = y8
            return _bcast_row(y8, 0), _bcast_row(a8, 0)

        yc, ac = lax.fori_loop(0, ng, group, (ycarry[...], acarry[...]))
        ycarry[...] = yc
        acarry[...] = ac

    rev = lambda c: pl.BlockSpec((tt, C), lambda i, c=c: (nb - 1 - i, c))
    return pl.pallas_call(
        body, grid=(nb,), in_specs=[rev(0), rev(0), rev(0), rev(1)], out_specs=[rev(0), rev(0)],
        out_shape=[jax.ShapeDtypeStruct((T, C), BF16), jax.ShapeDtypeStruct((T, C), F32)],
        scratch_shapes=[pltpu.VMEM((8, C), F32), pltpu.VMEM((8, C), F32), pltpu.VMEM((tt, C), F32)],
        compiler_params=_cp("arbitrary"), name=name,
    )(dhg, h, a, z)


def rec_bwd_gates(ybar, hprev, a, r, ig, xb, lam, wa, wi, *, name):
    T, C = xb.shape
    nblk, bd, _ = wa.shape
    tt = _tile(T, 256, 8)
    nb = T // tt

    def body(y_ref, hp_ref, a_ref, r_ref, i_ref, xb_ref, lam_ref, wa_ref, wi_ref,
             dxb_ref, dwa_ref, dwi_ref, dba_ref, dbi_ref, dlam_ref):
        step = pl.program_id(0)

        @pl.when(step == 0)
        def _():
            for ref in (dwa_ref, dwi_ref, dba_ref, dbi_ref, dlam_ref):
                ref[...] = jnp.zeros_like(ref)

        y, av, rv, iv, xv = y_ref[...], a_ref[...], r_ref[...], i_ref[...], xb_ref[...]
        sp = _softplus_neg(lam_ref[...])
        th = jnp.tanh(LRU_C * rv * sp)
        s = jnp.sqrt(2.0 * th / (1.0 + th))
        d_nl = -(y * hp_ref[...] * av) + (y * iv * xv) * (av * av) / s
        dlam_ref[...] += jnp.sum(d_nl * rv, axis=0, keepdims=True) * LRU_C
        dr = d_nl * (LRU_C * sp)
        di = y * s * xv
        dpa = dr * rv * (1.0 - rv)
        dpi = di * iv * (1.0 - iv)
        dba_ref[...] += jnp.sum(dpa, axis=0, keepdims=True)
        dbi_ref[...] += jnp.sum(dpi, axis=0, keepdims=True)
        parts = []
        for b in range(nblk):
            sl = slice(b * bd, (b + 1) * bd)
            xs, da_b, di_b = xv[:, sl], dpa[:, sl], dpi[:, sl]
            dwa_ref[b] += _dot(xs, da_b, TN)
            dwi_ref[b] += _dot(xs, di_b, TN)
            parts.append(_dot(da_b, wa_ref[b], NT) + _dot(di_b, wi_ref[b], NT))
        dxb_ref[...] = y * s * iv + jnp.concatenate(parts, axis=1)

        @pl.when(step == nb - 1)
        def _():
            dlam_ref[...] = dlam_ref[...] * (-jax.nn.sigmoid(-lam_ref[...]))

    row = pl.BlockSpec((tt, C), lambda i: (i, 0))
    vec = pl.BlockSpec((1, C), lambda i: (0, 0))
    wsp = pl.BlockSpec(wa.shape, lambda i: (0, 0, 0))
    return pl.pallas_call(
        body, grid=(nb,), in_specs=[row] * 6 + [vec, wsp, wsp],
        out_specs=[row, wsp, wsp, vec, vec, vec],
        out_shape=[jax.ShapeDtypeStruct((T, C), F32), jax.ShapeDtypeStruct(wa.shape, F32),
                   jax.ShapeDtypeStruct(wa.shape, F32)] + [jax.ShapeDtypeStruct((1, C), F32)] * 3,
        compiler_params=_cp("arbitrary"), name=name,
    )(ybar, hprev, a, r, ig, xb, lam, wa, wi)


def conv_bwd(d, x0, cw, *, name):
    T, C = d.shape
    K = cw.shape[0]
    tt = _tile(T, 256, HALO)
    per = tt // HALO
    nb = T // tt

    def body(d_ref, dn_ref, x_ref, xp_ref, cw_ref, dx_ref, dcw_ref, dcb_ref):
        step = pl.program_id(0)

        @pl.when(step == 0)
        def _():
            dcw_ref[...] = jnp.zeros_like(dcw_ref)
            dcb_ref[...] = jnp.zeros_like(dcb_ref)

        dv = d_ref[...].astype(F32)
        dc = jnp.concatenate([dv, jnp.where(step < nb - 1, dn_ref[...].astype(F32), 0.0)], axis=0)
        xc = jnp.concatenate([jnp.where(step > 0, xp_ref[...].astype(F32), 0.0), x_ref[...].astype(F32)], axis=0)
        acc = None
        rows = []
        for k in range(K):
            sh = K - 1 - k
            term = cw_ref[k:k + 1, :] * dc[sh:sh + tt]
            acc = term if acc is None else acc + term
            rows.append(jnp.sum(dv * xc[HALO - sh:HALO - sh + tt], axis=0, keepdims=True))
        dx_ref[...] = acc.astype(dx_ref.dtype)
        dcw_ref[...] += jnp.concatenate(rows, axis=0)
        dcb_ref[...] += jnp.sum(dv, axis=0, keepdims=True)

    row = pl.BlockSpec((tt, C), lambda i: (i, 0))
    return pl.pallas_call(
        body, grid=(nb,),
        in_specs=[row, pl.BlockSpec((HALO, C), lambda i: (jnp.minimum((i + 1) * per, T // HALO - 1), 0)),
                  row, pl.BlockSpec((HALO, C), lambda i: (jnp.maximum(i * per - 1, 0), 0)),
                  pl.BlockSpec((K, C), lambda i: (0, 0))],
        out_specs=[row, pl.BlockSpec((K, C), lambda i: (0, 0)), pl.BlockSpec((1, C), lambda i: (0, 0))],
        out_shape=[jax.ShapeDtypeStruct((T, C), BF16), jax.ShapeDtypeStruct((K, C), F32),
                   jax.ShapeDtypeStruct((1, C), F32)],
        compiler_params=_cp("arbitrary"), name=name,
    )(d, d, x0, x0, cw)


def ffn_act_fwd(u0, cw, cb, *, n, name):
    T, W = u0.shape
    G = W // (2 * n)
    tt = _tile(T, 256, HALO)
    per = tt // HALO

    def body(u_ref, up_ref, cw_ref, cb_ref, a_ref):
        step = pl.program_id(1)
        xc = jnp.concatenate([jnp.where(step > 0, up_ref[...].astype(F32), 0.0), u_ref[...].astype(F32)], axis=0)
        u = cb_ref[...] + _conv_taps(xc, cw_ref, tt, HALO)
        a_ref[...] = (_gelu(u[:, :n]) * u[:, n:]).astype(a_ref.dtype)

    return pl.pallas_call(
        body, grid=(G, T // tt),
        in_specs=[pl.BlockSpec((tt, 2 * n), lambda j, i: (i, j)),
                  pl.BlockSpec((HALO, 2 * n), lambda j, i: (jnp.maximum(i * per - 1, 0), j)),
                  pl.BlockSpec((cw.shape[0], 2 * n), lambda j, i: (0, j)),
                  pl.BlockSpec((1, 2 * n), lambda j, i: (0, j))],
        out_specs=pl.BlockSpec((tt, n), lambda j, i: (i, j)),
        out_shape=jax.ShapeDtypeStruct((T, G * n), BF16),
        compiler_params=_cp("parallel", "parallel"), name=name)(u0, u0, cw, cb)


def ffn_act_bwd(u0, da, cw, cb, *, n, name):
    T, W = u0.shape
    G = W // (2 * n)
    K = cw.shape[0]
    tt = _tile(T, 128, HALO)
    per = tt // HALO
    nb = T // tt
    ext = tt + HALO

    def body(u_ref, up_ref, un_ref, da_ref, dan_ref, cw_ref, cb_ref, du_ref, dcw_ref, dcb_ref):
        step = pl.program_id(1)

        @pl.when(step == 0)
        def _():
            dcw_ref[...] = jnp.zeros_like(dcw_ref)
            dcb_ref[...] = jnp.zeros_like(dcb_ref)

        xc = jnp.concatenate([jnp.where(step > 0, up_ref[...].astype(F32), 0.0), u_ref[...].astype(F32),
                              un_ref[...].astype(F32)], axis=0)
        u = cb_ref[...] + _conv_taps(xc, cw_ref, ext, HALO)
        dae = jnp.concatenate([da_ref[...].astype(F32), jnp.where(step < nb - 1, dan_ref[...].astype(F32), 0.0)],
                              axis=0)
        gl, dgl = _gelu_and_grad(u[:, :n])
        du = jnp.concatenate([dae * u[:, n:] * dgl, dae * gl], axis=1)
        acc = None
        rows = []
        dut = du[:tt]
        for k in range(K):
            sh = K - 1 - k
            term = cw_ref[k:k + 1, :] * du[sh:sh + tt]
            acc = term if acc is None else acc + term
            rows.append(jnp.sum(dut * xc[HALO - sh:HALO - sh + tt], axis=0, keepdims=True))
        du_ref[...] = acc.astype(du_ref.dtype)
        dcw_ref[...] += jnp.concatenate(rows, axis=0)
        dcb_ref[...] += jnp.sum(dut, axis=0, keepdims=True)

    last = T // HALO - 1
    return pl.pallas_call(
        body, grid=(G, nb),
        in_specs=[pl.BlockSpec((tt, 2 * n), lambda j, i: (i, j)),
                  pl.BlockSpec((HALO, 2 * n), lambda j, i: (jnp.maximum(i * per - 1, 0), j)),
                  pl.BlockSpec((HALO, 2 * n), lambda j, i: (jnp.minimum((i + 1) * per, last), j)),
                  pl.BlockSpec((tt, n), lambda j, i: (i, j)),
                  pl.BlockSpec((HALO, n), lambda j, i: (jnp.minimum((i + 1) * per, last), j)),
                  pl.BlockSpec((K, 2 * n), lambda j, i: (0, j)),
                  pl.BlockSpec((1, 2 * n), lambda j, i: (0, j))],
        out_specs=[pl.BlockSpec((tt, 2 * n), lambda j, i: (i, j)),
                   pl.BlockSpec((K, 2 * n), lambda j, i: (0, j)),
                   pl.BlockSpec((1, 2 * n), lambda j, i: (0, j))],
        out_shape=[jax.ShapeDtypeStruct((T, W), BF16), jax.ShapeDtypeStruct((K, W), F32),
                   jax.ShapeDtypeStruct((1, W), F32)],
        compiler_params=_cp("parallel", "arbitrary"), name=name,
    )(u0, u0, u0, da, da, cw, cb)


def _ffn_shard_of_block(q):
    return (q % 2) * (N_DEV // 2) + q // 2


def _ffn_block_of_shard(s):
    return (s % (N_DEV // 2)) * 2 + s // (N_DEV // 2)


def _group_cols(v):
    lead = v.shape[:-1]
    n = v.shape[-1] // N_DEV
    return jnp.swapaxes(v.reshape(lead + (2, N_DEV // 2, n)), -3, -2).reshape(v.shape)


def _ungroup_cols(v):
    lead = v.shape[:-1]
    n = v.shape[-1] // N_DEV
    return jnp.swapaxes(v.reshape(lead + (N_DEV // 2, 2, n)), -3, -2).reshape(v.shape)


def local_step(x, target, w, plan=None):
    depth = w["mix_norm"].shape[0]
    D = x.shape[1]
    dff2 = w["ffn_w_up"][0].shape[1]
    n_up = dff2 // N_DEV
    g = {k: [None] * (w[k].shape[0] if hasattr(w[k], "shape") else len(w[k])) for k in w}

    def run(fn, stage, l, slot, *args, **kw):
        carry = None if plan is None else plan.carry(stage, l, slot, g)
        if carry is None:
            return fn(*args, **kw)
        out, extra = fn(*args, carry=carry, **kw)
        plan.done(stage, l, slot, extra)
        return out

    saved = []
    for l in range(depth):
        j = l // 2
        h = rmsnorm_fwd(x, w["mix_norm"][l:l + 1], name="mix_norm_fwd")
        if l % 2 == 0:
            qkv = run(mm_nn, "fwd", l, "in", h, w["attn_w_qkv"][j], out_dtype=F32, name="qkv_proj")
            ao = attn_fwd(qkv, w["attn_q_gain"][j:j + 1], w["attn_k_gain"][j:j + 1], w["attn_sinks"][j:j + 1],
                          name="attn_fwd")
            x1 = run(mm_nn, "fwd", l, "out", ao, w["attn_w_o"][j], out_dtype=F32, res=x, name="attn_out_proj")
            mix = (qkv, ao)
        else:
            z = run(mm_nn, "fwd", l, "in", h, w["rec_w_in"][j], out_dtype=F32, name="rec_in_proj")
            xb, r, ig, a, hs, hprev, hg = rec_fwd(
                z, w["rec_conv_w"][j], w["rec_conv_b"][j:j + 1], w["rec_w_a"][j], w["rec_b_a"][j:j + 1],
                w["rec_w_i"][j], w["rec_b_i"][j:j + 1], w["rec_lambda"][j:j + 1], name="rec_fwd")
            x1 = run(mm_nn, "fwd", l, "out", hg, w["rec_w_out"][j], out_dtype=F32, res=x, name="rec_out_proj")
            mix = (z, xb, r, ig, a, hs, hprev, hg)
        h2 = rmsnorm_fwd(x1, w["ffn_norm"][l:l + 1], name="ffn_norm_fwd")
        u0 = run(mm_nn, "fwd", l, "up", h2, w["ffn_w_up"][l], out_dtype=BF16, name="ffn_up_proj")
        act = ffn_act_fwd(u0, w["ffn_conv_w"][l], w["ffn_conv_b"][l:l + 1], n=n_up, name="ffn_act_fwd")
        x2 = run(mm_nn, "fwd", l, "down", act, w["ffn_w_down"][l], out_dtype=F32, res=x1, name="ffn_down_proj")
        saved.append((x, h, mix, x1, h2, u0, act))
        x = x2

    loss_vec, dx = loss_head(x, target, name="loss_head")

    for l in reversed(range(depth)):
        j = l // 2
        x0, h, mix, x1, h2, u0, act = saved[l]
        dact = mm_nt(dx, w["ffn_w_down"][l], out_dtype=BF16, name="ffn_down_dx")
        g["ffn_w_down"][l] = mm_tn(act, dx, out_dtype=BF16, name="ffn_down_dw")
        du0, dcw, dcb = ffn_act_bwd(u0, dact, w["ffn_conv_w"][l], w["ffn_conv_b"][l:l + 1], n=n_up, name="ffn_act_bwd")
        g["ffn_conv_w"][l], g["ffn_conv_b"][l] = _ungroup_cols(dcw), _ungroup_cols(dcb)[0]
        g["ffn_w_up"][l] = run(mm_tn, "bwd", l, "up_dw", h2, du0, out_dtype=BF16, col_shards=N_DEV, shard_of_block=_ffn_shard_of_block,
                                 name="ffn_up_dw")
        dh2 = run(mm_nt, "bwd", l, "up_dx", du0, w["ffn_w_up"][l], out_dtype=F32, name="ffn_up_dx")
        dx1, dgf = rmsnorm_bwd(x1, w["ffn_norm"][l:l + 1], dh2, dx, name="ffn_norm_bwd")
        g["ffn_norm"][l] = dgf[0]
        if l % 2 == 0:
            qkv, ao = mix
            dao = mm_nt(dx1, w["attn_w_o"][j], out_dtype=BF16, name="attn_out_dx")
            g["attn_w_o"][j] = mm_tn(ao, dx1, out_dtype=BF16, name="attn_out_dw")
            dq, dk, dv, dqg, dkg, dsk = attn_bwd(qkv, dao, w["attn_q_gain"][j:j + 1], w["attn_k_gain"][j:j + 1],
                                                 w["attn_sinks"][j:j + 1], name="attn_bwd")
            g["attn_q_gain"][j], g["attn_k_gain"][j], g["attn_sinks"][j] = dqg[0], dkg[0], dsk
            dqkv = jnp.concatenate([dq, dk, dv], axis=1)
            g["attn_w_qkv"][j] = run(mm_tn, "bwd", l, "in_dw", h, dqkv, out_dtype=BF16, col_shards=N_DEV, name="qkv_dw")
            dh = run(mm_nt, "bwd", l, "in_dx", dqkv, w["attn_w_qkv"][j], out_dtype=F32, name="qkv_dx")
        else:
            z, xb, r, ig, a, hs, hprev, hg = mix
            dhg = mm_nt(dx1, w["rec_w_out"][j], out_dtype=BF16, name="rec_out_dx")
            g["rec_w_out"][j] = mm_tn(hg, dx1, out_dtype=BF16, name="rec_out_dw")
            dyb, ybar = rec_bwd_scan(dhg, hs, a, z, name="rec_bwd_scan")
            dxb, dwa, dwi, dba, dbi, dlam = rec_bwd_gates(
                ybar, hprev, a, r, ig, xb, w["rec_lambda"][j:j + 1], w["rec_w_a"][j], w["rec_w_i"][j],
                name="rec_bwd_gates")
            dxb0, dcw, dcb = conv_bwd(dxb, z, w["rec_conv_w"][j], name="rec_conv_bwd")
            g["rec_w_a"][j], g["rec_w_i"][j] = dwa, dwi
            g["rec_b_a"][j], g["rec_b_i"][j], g["rec_lambda"][j] = dba[0], dbi[0], dlam[0]
            g["rec_conv_w"][j], g["rec_conv_b"][j] = dcw, dcb[0]
            dz = jnp.concatenate([dxb0, dyb], axis=1)
            g["rec_w_in"][j] = run(mm_tn, "bwd", l, "in_dw", h, dz, out_dtype=BF16, col_shards=N_DEV, name="rec_in_dw")
            dh = run(mm_nt, "bwd", l, "in_dx", dz, w["rec_w_in"][j], out_dtype=F32, name="rec_in_dx")
        dx, dgm = rmsnorm_bwd(x0, w["mix_norm"][l:l + 1], dh, dx1, name="mix_norm_bwd")
        g["mix_norm"][l] = dgm[0]
    return loss_vec, dx, g


HBM = pl.BlockSpec(memory_space=pltpu.HBM)
N_PEER = N_DEV - 1


def _here():
    return lax.axis_index("x"), lax.axis_index("y"), lax.axis_index("c")


def _sid(dev):
    return 4 * dev[0] + 2 * dev[1] + dev[2]


def _exchange_sems(n):
    return [pltpu.SemaphoreType.DMA((n * N_PEER,)), pltpu.SemaphoreType.DMA((n * N_PEER,)),
            pltpu.SemaphoreType.DMA((n,))]


def gather_exchange(shards, out_structs, windows):
    n = len(shards)

    def parts(outs, sems):
        send_sems, recv_sems, _ = sems
        x, y, c = _here()
        me, sib = (x, y, c), (x, y, 1 - c)
        chips = [(1 - x, y), (x, 1 - y), (1 - x, 1 - y)]

        def copy(i, k, block, to, src=None):
            dst = windows[i](outs[i], _sid(block))
            return pltpu.make_async_remote_copy(
                src_ref=dst if src is None else src, dst_ref=dst,
                send_sem=send_sems.at[i * N_PEER + k], recv_sem=recv_sems.at[i * N_PEER + k],
                device_id=to, device_id_type=MESH)

        return me, sib, chips, c, copy

    def own_copies(ins, outs, sems):
        me, sib, chips, c, copy = parts(outs, sems)
        local = [pltpu.make_async_copy(ins[i], windows[i](outs[i], _sid(me)), sems[2].at[i]) for i in range(n)]
        first = []
        for i in range(n):
            first.append(copy(i, 0, me, sib, src=ins[i]))
            first += [copy(i, 1 + j, me, (*chip, c), src=ins[i]) for j, chip in enumerate(chips)]
        return local, first

    def start(ins, outs, sems):
        local, first = own_copies(ins, outs, sems)
        for cp in local + first:
            cp.start()

    def finish(ins, outs, sems):
        me, sib, chips, c, copy = parts(outs, sems)
        local, first = own_copies(ins, outs, sems)
        passed = []
        for i in range(n):
            for j, chip in enumerate(chips):
                copy(i, 1 + j, (*chip, c), me).wait_recv()
                fwd = copy(i, 4 + j, (*chip, c), sib)
                fwd.start()
                passed.append(fwd)
        for i in range(n):
            copy(i, 0, sib, me).wait_recv()
            for j, chip in enumerate(chips):
                copy(i, 4 + j, (*chip, 1 - c), me).wait_recv()
        for cp in first + passed:
            cp.wait_send()
        for cp in local:
            cp.wait()

    return dict(arrays=list(shards), out_structs=list(out_structs), sems=_exchange_sems(n), start=start,
                finish=finish)


def run_exchange(ex, *, name):
    n = len(ex["arrays"])

    def body(*refs):
        ins, outs, sems = refs[:n], refs[n:2 * n], refs[2 * n:]
        ex["start"](ins, outs, sems)
        ex["finish"](ins, outs, sems)

    return pl.pallas_call(
        body, in_specs=[HBM] * n, out_specs=[HBM] * n, out_shape=ex["out_structs"], scratch_shapes=ex["sems"],
        name=name)(*ex["arrays"])


def scatter_exchange(grads):
    n = len(grads)

    def parts(ins, outs, sems):
        send_sems, recv_sems, local_sems = sems
        x, y, c = _here()
        me = (x, y, c)
        peers = []
        for k in range(1, N_DEV):
            kx, ky, kc = (k >> 2) & 1, (k >> 1) & 1, k & 1
            peers.append((1 - x if kx else x, 1 - y if ky else y, 1 - c if kc else c))

        def copy(i, k):
            return pltpu.make_async_remote_copy(
                src_ref=ins[i].at[_sid(peers[k])], dst_ref=outs[i].at[_sid(me)],
                send_sem=send_sems.at[i * N_PEER + k], recv_sem=recv_sems.at[i * N_PEER + k],
                device_id=peers[k], device_id_type=MESH)

        def arrival(i, k):
            return pltpu.make_async_remote_copy(
                src_ref=ins[i].at[_sid(me)], dst_ref=outs[i].at[_sid(peers[k])],
                send_sem=send_sems.at[i * N_PEER + k], recv_sem=recv_sems.at[i * N_PEER + k],
                device_id=peers[k], device_id_type=MESH)

        local = [pltpu.make_async_copy(ins[i].at[_sid(me)], outs[i].at[_sid(me)], local_sems.at[i]) for i in range(n)]
        sends = [copy(i, k) for i in range(n) for k in range(N_PEER)]
        return local, sends, arrival

    def start(ins, outs, sems):
        local, sends, _ = parts(ins, outs, sems)
        for cp in local + sends:
            cp.start()

    def finish(ins, outs, sems):
        local, sends, arrival = parts(ins, outs, sems)
        for i in range(n):
            for k in range(N_PEER):
                arrival(i, k).wait_recv()
        for cp in sends:
            cp.wait_send()
        for cp in local:
            cp.wait()

    return dict(arrays=list(grads), out_structs=[jax.ShapeDtypeStruct(g.shape, g.dtype) for g in grads],
                sems=_exchange_sems(n), start=start, finish=finish)


def adamw_family(contribs, w, m, v, *, name):
    L, R, C = w.shape
    S = contribs[0].shape[0]
    tr = _tile(R, max(8, (1 << 20) // (C * S)), 8)
    nr = R // tr
    c1 = 1.0 / (1.0 - ADAM_B1 ** ADAM_STEP)
    c2 = 1.0 / (1.0 - ADAM_B2 ** ADAM_STEP)

    def body(*refs):
        c_refs = refs[:L]
        w_ref, m_ref, v_ref, g_ref, d_ref, nm_ref, nv_ref = refs[L:]
        layer = pl.program_id(0)
        for l in range(L):
            @pl.when(layer == l)
            def _(l=l):
                g = c_refs[l][0].astype(F32)
                for s in range(1, S):
                    g = g + c_refs[l][s].astype(F32)
                mm = ADAM_B1 * m_ref[...] + (1.0 - ADAM_B1) * g
                vv = ADAM_B2 * v_ref[...] + (1.0 - ADAM_B2) * (g * g)
                g_ref[...] = g
                nm_ref[...] = mm
                nv_ref[...] = vv
                d_ref[...] = -ADAM_LR * ((mm * c1) / (jnp.sqrt(vv * c2) + ADAM_EPS) + ADAM_WD * w_ref[...])

    def cspec(l):
        return pl.BlockSpec((S, tr, C), lambda ll, i, l=l: (0, jnp.where(ll == l, i, 0), 0))

    lay = pl.BlockSpec((None, tr, C), lambda ll, i: (ll, i, 0))
    return pl.pallas_call(
        body, grid=(L, nr), in_specs=[cspec(l) for l in range(L)] + [lay] * 3, out_specs=[lay] * 4,
        out_shape=[jax.ShapeDtypeStruct((L, R, C), F32)] * 4,
        compiler_params=_cp("arbitrary", "arbitrary"), name=name)(*contribs, w, m, v)


def sum_slots(a, *, name):
    S, R, C = a.shape
    tr = _tile(R, 256, 8)

    def body(a_ref, o_ref):
        t = a_ref[0]
        for s in range(1, S):
            t = t + a_ref[s]
        o_ref[...] = t

    return pl.pallas_call(
        body, grid=(R // tr,), in_specs=[pl.BlockSpec((S, tr, C), lambda i: (0, i, 0))],
        out_specs=pl.BlockSpec((tr, C), lambda i: (i, 0)), out_shape=jax.ShapeDtypeStruct((R, C), F32),
        compiler_params=_cp("parallel"), name=name)(a)


LANES = 128


def _pack(arrs):
    flat = jnp.concatenate([a.reshape(-1).astype(F32) for a in arrs])
    rows = -(-flat.shape[0] // LANES)
    rows = -(-rows // 8) * 8
    return jnp.pad(flat, (0, rows * LANES - flat.shape[0])).reshape(rows, LANES)


def _unpack(buf, shapes):
    flat = buf.reshape(-1)
    out, off = [], 0
    for shp in shapes:
        size = int(np.prod(shp))
        out.append(flat[off:off + size].reshape(shp))
        off += size
    return out


def _gather_last(g):
    t = jnp.moveaxis(g, 0, -2)
    return t.reshape(t.shape[:-2] + (t.shape[-2] * t.shape[-1],))


def _own_last(full, s):
    n = full.shape[-1] // N_DEV
    t = full.reshape(full.shape[:-1] + (N_DEV, n))
    return lax.dynamic_index_in_dim(t, s, axis=t.ndim - 2, keepdims=False)


BIG = ["attn_w_qkv", "attn_w_o", "rec_w_in", "rec_w_out", "ffn_w_up", "ffn_w_down", "rec_w_a", "rec_w_i"]
SMALL_REPLICATED = ["mix_norm", "ffn_norm", "attn_q_gain", "attn_k_gain", "attn_sinks", "ffn_conv_b"]
SMALL_SHARDED = ["rec_conv_w", "rec_conv_b", "rec_b_a", "rec_b_i", "rec_lambda", "ffn_conv_w"]
SMALL = SMALL_REPLICATED + SMALL_SHARDED
WEIGHTS = ["mix_norm", "ffn_norm", "attn_w_qkv", "attn_q_gain", "attn_k_gain", "attn_sinks", "attn_w_o", "rec_w_in",
           "rec_conv_w", "rec_conv_b", "rec_w_a", "rec_b_a", "rec_w_i", "rec_b_i", "rec_lambda", "rec_w_out",
           "ffn_w_up", "ffn_conv_w", "ffn_conv_b", "ffn_w_down"]


def _col_window(n, block_of_shard=None):
    def win(ref, s):
        q = s if block_of_shard is None else block_of_shard(s)
        return ref.at[:, pl.ds(pl.multiple_of(q * n, 128), n)]
    return win


def _row_window(r):
    return lambda ref, s: ref.at[pl.ds(pl.multiple_of(s * r, 16), r), :]


def _gate_window(r):
    return lambda ref, s: ref.at[:, pl.ds(pl.multiple_of(s * r, 16), r), :]


def _slot_window(ref, s):
    return ref.at[s]


def _idx(name, layer):
    return layer if name.startswith("ffn") else layer // 2


def _slot_names(layer, slot):
    attn = layer % 2 == 0
    return {"in": ["attn_w_qkv"] if attn else ["rec_w_in"],
            "out": ["attn_w_o"] if attn else ["rec_w_out", "rec_w_a", "rec_w_i"],
            "up": ["ffn_w_up"], "down": ["ffn_w_down"]}[slot]


def _gather_for(p, names, layer):
    shards, structs, wins = [], [], []
    for nme in names:
        sh = p[nme][_idx(nme, layer)].astype(BF16)
        if nme in ("attn_w_qkv", "rec_w_in", "ffn_w_up"):
            K, n = sh.shape
            structs.append(jax.ShapeDtypeStruct((K, n * N_DEV), BF16))
            wins.append(_col_window(n, _ffn_block_of_shard if nme == "ffn_w_up" else None))
        elif nme in ("rec_w_a", "rec_w_i"):
            nblk, r, bd = sh.shape
            structs.append(jax.ShapeDtypeStruct((nblk, r * N_DEV, bd), BF16))
            wins.append(_gate_window(r))
        else:
            r, N = sh.shape
            structs.append(jax.ShapeDtypeStruct((r * N_DEV, N), BF16))
            wins.append(_row_window(r))
        shards.append(sh)
    return gather_exchange(shards, structs, wins)


def _send_layout(name, t):
    if name in ("rec_w_a", "rec_w_i"):
        nblk, bd, _ = t.shape
        return jnp.transpose(t.reshape(nblk, N_DEV, bd // N_DEV, bd), (1, 0, 2, 3)).astype(BF16)
    if name in ("attn_w_o", "rec_w_out", "ffn_w_down"):
        return t.reshape((N_DEV, t.shape[0] // N_DEV) + t.shape[1:])
    return t


class _Plan:
    BWD_SLOT = {"up_dw": "down", "up_dx": "up", "in_dw": "out", "in_dx": "in"}

    def __init__(self, p, w, contribs, depth):
        self.p, self.w, self.contribs, self.depth = p, w, contribs, depth
        self.pending = None

    def carry(self, stage, l, slot, g):
        if stage == "fwd":
            if l + 1 >= self.depth:
                return None
            names = _slot_names(l + 1, slot)
            self.pending = (names, l + 1)
            return _gather_for(self.p, names, l + 1)
        names = _slot_names(l, self.BWD_SLOT[slot])
        self.pending = (names, l)
        return scatter_exchange([_send_layout(k, g[k][_idx(k, l)]) for k in names])

    def done(self, stage, l, slot, outs):
        names, layer = self.pending
        dst = self.w if stage == "fwd" else self.contribs
        for k, o in zip(names, outs):
            dst[k][_idx(k, layer)] = o


def _train_step(p, x, target, mom, vel):
    depth = p["mix_norm"].shape[0]
    s_me = _sid(_here())

    w = {k: [None] * p[k].shape[0] for k in BIG}
    first = [k for slot in ("in", "out", "up", "down") for k in _slot_names(0, slot)]
    for k, t in zip(first, run_exchange(_gather_for(p, first, 0), name="all_gather_layer0")):
        w[k][0] = t

    local_small = [p[k] for k in SMALL_SHARDED]
    packed = _pack(local_small)
    gathered, = run_exchange(
        gather_exchange([packed], [jax.ShapeDtypeStruct((N_DEV,) + packed.shape, F32)], [_slot_window]),
        name="all_gather_small")
    per_dev = [_unpack(gathered[s], [a.shape for a in local_small]) for s in range(N_DEV)]
    for i, k in enumerate(SMALL_SHARDED):
        w[k] = _gather_last(jnp.stack([per_dev[s][i] for s in range(N_DEV)]))
    for k in SMALL_REPLICATED:
        w[k] = p[k]
    nrec = w["rec_b_a"].shape[0]
    w["rec_b_a"] = w["rec_b_a"].reshape(nrec, -1)
    w["rec_b_i"] = w["rec_b_i"].reshape(nrec, -1)
    w["ffn_conv_w"] = _group_cols(w["ffn_conv_w"])
    w["ffn_conv_b"] = _group_cols(w["ffn_conv_b"])

    contribs = {k: [None] * len(w[k]) for k in BIG}
    loss_vec, dx, g = local_step(x[0], target[0], w, _Plan(p, w, contribs, depth))
    loss = lax.psum(jnp.sum(loss_vec), ("x", "y", "c"))

    out = {}
    for k in BIG:
        shp = p[k].shape
        L = shp[0]
        C = shp[-1]
        R = int(np.prod(shp[1:-1]))
        cs = [c.reshape(N_DEV, R, C) for c in contribs[k]]
        res = adamw_family(cs, p[k].reshape(L, R, C), mom[k].reshape(L, R, C), vel[k].reshape(L, R, C),
                           name="adamw_" + k)
        out[k] = [t.reshape(shp) for t in res]

    gsmall = [jnp.stack(g[k]) for k in SMALL]
    gp = _pack(gsmall)
    gall, = run_exchange(
        gather_exchange([gp], [jax.ShapeDtypeStruct((N_DEV,) + gp.shape, F32)], [_slot_window]),
        name="all_gather_small_grads")
    gsum = _unpack(sum_slots(gall, name="sum_small_grads"), [a.shape for a in gsmall])
    glocal = []
    for k, t in zip(SMALL, gsum):
        if k in SMALL_SHARDED:
            t = _own_last(t.reshape(p[k].shape[:-1] + (p[k].shape[-1] * N_DEV,)), s_me)
        glocal.append(t.reshape(p[k].shape))
    wp, mp, vp, gpk = (_pack([d[k] for k in SMALL]) for d in (p, mom, vel, dict(zip(SMALL, glocal))))
    res = adamw_family([gpk[None]], wp[None], mp[None], vp[None], name="adamw_small")
    shapes = [p[k].shape for k in SMALL]
    unp = [_unpack(t[0], shapes) for t in res]
    for i, k in enumerate(SMALL):
        out[k] = [glocal[i], unp[1][i], unp[2][i], unp[3][i]]

    return (loss, dx[None]) + tuple(out[k][q] for q in range(4) for k in WEIGHTS)


def kernel(x, mix_norm, ffn_norm, attn_w_qkv, attn_q_gain, attn_k_gain, attn_sinks, attn_w_o, rec_w_in, rec_conv_w, rec_conv_b, rec_w_a, rec_b_a, rec_w_i, rec_b_i, rec_lambda, rec_w_out, ffn_w_up, ffn_conv_w, ffn_conv_b, ffn_w_down, loss_target, m_mix_norm, m_ffn_norm, m_attn_w_qkv, m_attn_q_gain, m_attn_k_gain, m_attn_sinks, m_attn_w_o, m_rec_w_in, m_rec_conv_w, m_rec_conv_b, m_rec_w_a, m_rec_b_a, m_rec_w_i, m_rec_b_i, m_rec_lambda, m_rec_w_out, m_ffn_w_up, m_ffn_conv_w, m_ffn_conv_b, m_ffn_w_down, v_mix_norm, v_ffn_norm, v_attn_w_qkv, v_attn_q_gain, v_attn_k_gain, v_attn_sinks, v_attn_w_o, v_rec_w_in, v_rec_conv_w, v_rec_conv_b, v_rec_w_a, v_rec_b_a, v_rec_w_i, v_rec_b_i, v_rec_lambda, v_rec_w_out, v_ffn_w_up, v_ffn_conv_w, v_ffn_conv_b, v_ffn_w_down):
    p = dict(zip(WEIGHTS, (mix_norm, ffn_norm, attn_w_qkv, attn_q_gain, attn_k_gain, attn_sinks, attn_w_o, rec_w_in,
                           rec_conv_w, rec_conv_b, rec_w_a, rec_b_a, rec_w_i, rec_b_i, rec_lambda, rec_w_out,
                           ffn_w_up, ffn_conv_w, ffn_conv_b, ffn_w_down)))
    mom = dict(zip(WEIGHTS, (m_mix_norm, m_ffn_norm, m_attn_w_qkv, m_attn_q_gain, m_attn_k_gain, m_attn_sinks,
                             m_attn_w_o, m_rec_w_in, m_rec_conv_w, m_rec_conv_b, m_rec_w_a, m_rec_b_a, m_rec_w_i,
                             m_rec_b_i, m_rec_lambda, m_rec_w_out, m_ffn_w_up, m_ffn_conv_w, m_ffn_conv_b,
                             m_ffn_w_down)))
    vel = dict(zip(WEIGHTS, (v_mix_norm, v_ffn_norm, v_attn_w_qkv, v_attn_q_gain, v_attn_k_gain, v_attn_sinks,
                             v_attn_w_o, v_rec_w_in, v_rec_conv_w, v_rec_conv_b, v_rec_w_a, v_rec_b_a, v_rec_w_i,
                             v_rec_b_i, v_rec_lambda, v_rec_w_out, v_ffn_w_up, v_ffn_conv_w, v_ffn_conv_b,
                             v_ffn_w_down)))
    return _train_step(p, x, loss_target, mom, vel)
```

```python
import functools
import math

import jax
import jax.numpy as jnp
import numpy as np
from jax import lax
from jax.experimental import pallas as pl
from jax.experimental.pallas import tpu as pltpu

F32 = jnp.float32
BF16 = jnp.bfloat16

N_DEV = 8
HEAD_DIM = 64
GROUP = 4
BLOCK = 128
LRU_C = 8.0
EPS = 1e-6
HALO = 16
ADAM_LR, ADAM_B1, ADAM_B2, ADAM_EPS, ADAM_WD, ADAM_STEP = 0.001, 0.9, 0.999, 1e-08, 0.01, 10
VMEM_LIMIT = 56 * 1024 * 1024
MESH = pl.DeviceIdType.MESH
GELU_C = math.sqrt(2.0 / math.pi)


def _cp(*sem, vmem=VMEM_LIMIT):
    return pltpu.CompilerParams(dimension_semantics=tuple(sem), vmem_limit_bytes=vmem)


def _tile(dim, pref, mult=128):
    if dim <= pref:
        return dim
    t = (pref // mult) * mult
    while t >= mult:
        if dim % t == 0:
            return t
        t -= mult
    return dim


def _gelu(x):
    th = jnp.tanh(GELU_C * (x + 0.044715 * x * x * x))
    return 0.5 * x * (1.0 + th)


def _gelu_and_grad(x):
    x2 = x * x
    th = jnp.tanh(GELU_C * (x + 0.044715 * x2 * x))
    g = 0.5 * x * (1.0 + th)
    dg = 0.5 * (1.0 + th) + 0.5 * x * (1.0 - th * th) * GELU_C * (1.0 + 3.0 * 0.044715 * x2)
    return g, dg


def _dot(a, b, dims):
    return lax.dot_general(a.astype(BF16), b.astype(BF16), (dims, ((), ())), preferred_element_type=F32)


NN = ((1,), (0,))
NT = ((1,), (1,))
TN = ((0,), (0,))


def _matmul(a, b, *, dims, grid, a_spec, b_spec, o_spec, out_shape, acc_shape, res=None, res_spec=None,
            carry=None, name):
    ni, nj, nk = grid
    nres = 0 if res is None else 1
    ncar = 0 if carry is None else len(carry["arrays"])

    def body(*refs):
        a_ref, b_ref = refs[0], refs[1]
        r_ref = refs[2] if nres else None
        car_in = refs[2 + nres:2 + nres + ncar]
        o_ref = refs[2 + nres + ncar]
        car_out = refs[3 + nres + ncar:3 + nres + 2 * ncar]
        scratch = refs[3 + nres + 2 * ncar:]
        i, j, k = pl.program_id(0), pl.program_id(1), pl.program_id(2)

        if carry is not None:
            @pl.when((i == 0) & (j == 0) & (k == 0))
            def _():
                carry["start"](car_in, car_out, scratch[1:])

        def finish(acc):
            if r_ref is not None:
                acc = acc + r_ref[...].astype(F32)
            o_ref[...] = acc.astype(o_ref.dtype)

        if nk == 1:
            finish(_dot(a_ref[...], b_ref[...], dims))
        else:
            acc_ref = scratch[0]

            @pl.when(k == 0)
            def _():
                acc_ref[...] = _dot(a_ref[...], b_ref[...], dims)

            if nk > 2:
                @pl.when((k > 0) & (k < nk - 1))
                def _():
                    acc_ref[...] += _dot(a_ref[...], b_ref[...], dims)

            @pl.when(k == nk - 1)
            def _():
                finish(acc_ref[...] + _dot(a_ref[...], b_ref[...], dims))

        if carry is not None:
            @pl.when((i == ni - 1) & (j == nj - 1) & (k == nk - 1))
            def _():
                carry["finish"](car_in, car_out, scratch[1:])

    in_specs = [a_spec, b_spec] + ([res_spec] if nres else [])
    args = (a, b) + ((res,) if nres else ())
    out_specs, out_shapes = [o_spec], [out_shape]
    scratch_shapes = [pltpu.VMEM(acc_shape if nk > 1 else (8, 128), F32)]
    sem = ("parallel", "parallel", "arbitrary")
    if carry is not None:
        in_specs += [HBM] * ncar
        args += tuple(carry["arrays"])
        out_specs += [HBM] * ncar
        out_shapes += list(carry["out_structs"])
        scratch_shapes += carry["sems"]
        sem = ("arbitrary", "arbitrary", "arbitrary")
    outs = pl.pallas_call(
        body, grid=grid, in_specs=in_specs, out_specs=out_specs, out_shape=out_shapes,
        scratch_shapes=scratch_shapes, compiler_params=_cp(*sem), name=name,
    )(*args)
    return outs[0] if carry is None else (outs[0], list(outs[1:]))


def mm_nn(a, b, *, out_dtype, res=None, carry=None, name):
    M, K = a.shape
    N = b.shape[1]
    tm, tn, tk = _tile(M, 1024), _tile(N, 1024), _tile(K, 2048)
    return _matmul(
        a, b, dims=NN, grid=(M // tm, N // tn, K // tk),
        a_spec=pl.BlockSpec((tm, tk), lambda i, j, k: (i, k)),
        b_spec=pl.BlockSpec((tk, tn), lambda i, j, k: (k, j)),
        o_spec=pl.BlockSpec((tm, tn), lambda i, j, k: (i, j)),
        out_shape=jax.ShapeDtypeStruct((M, N), out_dtype), acc_shape=(tm, tn),
        res=res, res_spec=pl.BlockSpec((tm, tn), lambda i, j, k: (i, j)), carry=carry, name=name)


def mm_nt(a, b, *, out_dtype, res=None, carry=None, name):
    M, N = a.shape
    K = b.shape[0]
    tm, tn, tk = _tile(M, 1024), _tile(K, 1024), _tile(N, 2048)
    return _matmul(
        a, b, dims=NT, grid=(M // tm, K // tn, N // tk),
        a_spec=pl.BlockSpec((tm, tk), lambda i, j, k: (i, k)),
        b_spec=pl.BlockSpec((tn, tk), lambda i, j, k: (j, k)),
        o_spec=pl.BlockSpec((tm, tn), lambda i, j, k: (i, j)),
        out_shape=jax.ShapeDtypeStruct((M, K), out_dtype), acc_shape=(tm, tn),
        res=res, res_spec=pl.BlockSpec((tm, tn), lambda i, j, k: (i, j)), carry=carry, name=name)


def mm_tn(a, b, *, out_dtype, col_shards=None, shard_of_block=None, carry=None, name):
    T, K = a.shape
    N = b.shape[1]
    tt = _tile(T, 2048)
    tm = _tile(K, 1024)
    if col_shards is None:
        tn = _tile(N, 1024)
        o_spec = pl.BlockSpec((tm, tn), lambda i, j, k: (i, j))
        out_shape = jax.ShapeDtypeStruct((K, N), out_dtype)
    else:
        n = N // col_shards
        tn = _tile(n, 1536)
        per = n // tn
        sob = shard_of_block if shard_of_block is not None else (lambda s: s)
        o_spec = pl.BlockSpec((None, tm, tn), lambda i, j, k: (sob(j // per), i, j % per))
        out_shape = jax.ShapeDtypeStruct((col_shards, K, n), out_dtype)
    return _matmul(
        a, b, dims=TN, grid=(K // tm, N // tn, T // tt),
        a_spec=pl.BlockSpec((tt, tm), lambda i, j, k: (k, i)),
        b_spec=pl.BlockSpec((tt, tn), lambda i, j, k: (k, j)),
        o_spec=o_spec, out_shape=out_shape, acc_shape=(tm, tn), carry=carry, name=name)


def rmsnorm_fwd(x, g, *, name):
    T, D = x.shape
    tm = _tile(T, 512, 8)

    def body(x_ref, g_ref, o_ref):
        xv = x_ref[...]
        r = lax.rsqrt(jnp.mean(xv * xv, axis=-1, keepdims=True) + EPS)
        o_ref[...] = (xv * r * g_ref[...]).astype(o_ref.dtype)

    return pl.pallas_call(
        body, grid=(T // tm,),
        in_specs=[pl.BlockSpec((tm, D), lambda i: (i, 0)), pl.BlockSpec((1, D), lambda i: (0, 0))],
        out_specs=pl.BlockSpec((tm, D), lambda i: (i, 0)),
        out_shape=jax.ShapeDtypeStruct((T, D), BF16), compiler_params=_cp("parallel"), name=name)(x, g)


def rmsnorm_bwd(x, g, dh, dres, *, name):
    T, D = x.shape
    tm = _tile(T, 512, 8)

    def body(x_ref, g_ref, dh_ref, dres_ref, dx_ref, dg_ref):
        @pl.when(pl.program_id(0) == 0)
        def _():
            dg_ref[...] = jnp.zeros_like(dg_ref)

        xv = x_ref[...]
        dh_v = dh_ref[...].astype(F32)
        r = lax.rsqrt(jnp.mean(xv * xv, axis=-1, keepdims=True) + EPS)
        u = dh_v * g_ref[...]
        dot = jnp.mean(u * xv, axis=-1, keepdims=True)
        dx_ref[...] = dres_ref[...] + r * u - xv * (r * r * r * dot)
        dg_ref[...] += jnp.sum(dh_v * xv * r, axis=0, keepdims=True)

    row = pl.BlockSpec((tm, D), lambda i: (i, 0))
    vec = pl.BlockSpec((1, D), lambda i: (0, 0))
    return pl.pallas_call(
        body, grid=(T // tm,), in_specs=[row, vec, row, row], out_specs=[row, vec],
        out_shape=[jax.ShapeDtypeStruct((T, D), F32), jax.ShapeDtypeStruct((1, D), F32)],
        compiler_params=_cp("arbitrary"), name=name)(x, g, dh, dres)


def loss_head(y, target, *, name):
    T, D = y.shape
    tm = _tile(T, 512, 8)

    def body(y_ref, t_ref, l_ref, dy_ref):
        @pl.when(pl.program_id(0) == 0)
        def _():
            l_ref[...] = jnp.zeros_like(l_ref)

        e = y_ref[...] - t_ref[...]
        dy_ref[...] = e * (1.0 / D)
        l_ref[...] += jnp.sum(e * e, axis=0, keepdims=True) * (0.5 / D)

    row = pl.BlockSpec((tm, D), lambda i: (i, 0))
    vec = pl.BlockSpec((1, D), lambda i: (0, 0))
    return pl.pallas_call(
        body, grid=(T // tm,), in_specs=[row, row], out_specs=[vec, row],
        out_shape=[jax.ShapeDtypeStruct((1, D), F32), jax.ShapeDtypeStruct((T, D), F32)],
        compiler_params=_cp("arbitrary"), name=name)(y, target)


NEG = -1e30


def _kv_heads_per_step(hkv):
    return 4 if hkv % 4 == 0 else 2


def _headnorm(x):
    r = lax.rsqrt(jnp.mean(x * x, axis=-1, keepdims=True) + EPS)
    return x * r, r


def _attn_mask(n):
    qi = lax.broadcasted_iota(jnp.int32, (BLOCK, 2 * BLOCK), 0)
    kj = lax.broadcasted_iota(jnp.int32, (BLOCK, 2 * BLOCK), 1)
    rel = qi + BLOCK - kj
    m = (rel >= 0) & (rel < BLOCK) & ((kj >= BLOCK) | (n > 0))
    return jnp.concatenate([m] * GROUP, axis=0)


def _attn_prep(q_ref, kp_ref, kc_ref, qg, kg, hh):
    lo = hh * HEAD_DIM
    k_raw = jnp.concatenate([kp_ref[:, lo:lo + HEAD_DIM], kc_ref[:, lo:lo + HEAD_DIM]], axis=0)
    kn, kr = _headnorm(k_raw)
    khat = kn * kg
    q_raw, qn, qr = [], [], []
    for g in range(GROUP):
        c0 = (hh * GROUP + g) * HEAD_DIM
        x = q_ref[:, c0:c0 + HEAD_DIM]
        xn, r = _headnorm(x)
        q_raw.append(x)
        qn.append(xn)
        qr.append(r)
    q4 = jnp.concatenate([xn * qg for xn in qn], axis=0)
    return q4, khat, (q_raw, qn, qr), (k_raw, kn, kr)


def _attn_softmax(s, sink_ref, mask4, head0):
    s = s * (HEAD_DIM ** -0.5)
    sink = jnp.concatenate([jnp.full((BLOCK, 1), sink_ref[0, head0 + g], F32) for g in range(GROUP)], axis=0)
    m = jnp.maximum(jnp.max(jnp.where(mask4, s, NEG), axis=-1, keepdims=True), sink)
    e = jnp.where(mask4, jnp.exp(s - m), 0.0)
    es = jnp.exp(sink - m)
    inv = 1.0 / (jnp.sum(e, axis=-1, keepdims=True) + es)
    return e * inv, es * inv


def _window(p_ref, c_ref, hh):
    lo = hh * HEAD_DIM
    return jnp.concatenate([p_ref[:, lo:lo + HEAD_DIM], c_ref[:, lo:lo + HEAD_DIM]], axis=0)


def attn_fwd(qkv, q_gain, k_gain, sinks, *, carry=None, name):
    ncar = 0 if carry is None else len(carry["arrays"])
    T, W = qkv.shape
    hq = W // HEAD_DIM * GROUP // (GROUP + 2)
    dq = hq * HEAD_DIM
    PAIR = _kv_heads_per_step(hq // GROUP)
    QW, KW = PAIR * GROUP * HEAD_DIM, PAIR * HEAD_DIM
    npair = hq // (GROUP * PAIR)
    nb = T // BLOCK
    k0 = dq // KW
    v0 = k0 + npair

    def body(q_ref, kp_ref, kc_ref, vp_ref, vc_ref, qg_ref, kg_ref, sink_ref, *rest):
        car_in, o_ref, car_out, sems = rest[:ncar], rest[ncar], rest[ncar + 1:2 * ncar + 1], rest[2 * ncar + 1:]
        p, n = pl.program_id(0), pl.program_id(1)
        if carry is not None:
            @pl.when((p == 0) & (n == 0))
            def _():
                carry["start"](car_in, car_out, sems)

            @pl.when((p == npair - 1) & (n == nb - 1))
            def _():
                carry["finish"](car_in, car_out, sems)

        mask4 = _attn_mask(n)
        prep = [_attn_prep(q_ref, kp_ref, kc_ref, qg_ref[...], kg_ref[...], hh) for hh in range(PAIR)]
        scores = [_dot(q4, khat, NT) for q4, khat, _, _ in prep]
        probs = [_attn_softmax(s, sink_ref, mask4, (p * PAIR + hh) * GROUP)[0] for hh, s in enumerate(scores)]
        outs = []
        for hh in range(PAIR):
            o4 = _dot(probs[hh], _window(vp_ref, vc_ref, hh), NN)
            outs += [o4[g * BLOCK:(g + 1) * BLOCK] for g in range(GROUP)]
        o_ref[...] = jnp.concatenate(outs, axis=1).astype(o_ref.dtype)

    prev = lambda n: jnp.maximum(n - 1, 0)
    vec = pl.BlockSpec((1, HEAD_DIM), lambda p, n: (0, 0))
    car = carry if carry is not None else dict(arrays=[], out_structs=[], sems=[])
    outs = pl.pallas_call(
        body, grid=(npair, nb),
        in_specs=[pl.BlockSpec((BLOCK, QW), lambda p, n: (n, p)),
                  pl.BlockSpec((BLOCK, KW), lambda p, n: (prev(n), k0 + p)),
                  pl.BlockSpec((BLOCK, KW), lambda p, n: (n, k0 + p)),
                  pl.BlockSpec((BLOCK, KW), lambda p, n: (prev(n), v0 + p)),
                  pl.BlockSpec((BLOCK, KW), lambda p, n: (n, v0 + p)),
                  vec, vec, pl.BlockSpec(memory_space=pltpu.SMEM)] + [HBM] * ncar,
        out_specs=[pl.BlockSpec((BLOCK, QW), lambda p, n: (n, p))] + [HBM] * ncar,
        out_shape=[jax.ShapeDtypeStruct((T, dq), BF16)] + list(car["out_structs"]),
        scratch_shapes=list(car["sems"]),
        compiler_params=_cp(*(("parallel", "parallel") if carry is None else ("arbitrary", "arbitrary"))),
        name=name)(qkv, qkv, qkv, qkv, qkv, q_gain, k_gain, sinks, *car["arrays"])
    return outs[0] if carry is None else (outs[0], list(outs[1:]))


def attn_bwd(qkv, d_out, q_gain, k_gain, sinks, *, name):
    T, W = qkv.shape
    hq = W // HEAD_DIM * GROUP // (GROUP + 2)
    dq_w = hq * HEAD_DIM
    PAIR = _kv_heads_per_step(hq // GROUP)
    QW, KW = PAIR * GROUP * HEAD_DIM, PAIR * HEAD_DIM
    npair = hq // (GROUP * PAIR)
    nb = T // BLOCK
    k0 = dq_w // KW
    v0 = k0 + npair

    def body(q_ref, kp_ref, kc_ref, vp_ref, vc_ref, do_ref, qg_ref, kg_ref, sink_ref,
             dq_ref, dk_ref, dv_ref, dqg_ref, dkg_ref, dsink_ref, dk_carry, dv_carry):
        p, i = pl.program_id(0), pl.program_id(1)
        n = nb - 1 - i

        @pl.when(i == 0)
        def _():
            dk_carry[...] = jnp.zeros_like(dk_carry)
            dv_carry[...] = jnp.zeros_like(dv_carry)
            dqg_ref[...] = jnp.zeros_like(dqg_ref)
            dkg_ref[...] = jnp.zeros_like(dkg_ref)
            dsink_ref[...] = jnp.zeros_like(dsink_ref)

        qg, kg = qg_ref[...], kg_ref[...]
        mask4 = _attn_mask(n)
        dq_parts, dk_parts, dv_parts, dsink_rows = [], [], [], []
        dqg_acc = jnp.zeros((1, HEAD_DIM), F32)
        dkg_acc = jnp.zeros((1, HEAD_DIM), F32)
        prep = [_attn_prep(q_ref, kp_ref, kc_ref, qg, kg, hh) for hh in range(PAIR)]
        do4s = [jnp.concatenate(
            [do_ref[:, (hh * GROUP + g) * HEAD_DIM:(hh * GROUP + g + 1) * HEAD_DIM] for g in range(GROUP)],
            axis=0) for hh in range(PAIR)]
        scores = [_dot(q4, khat, NT) for q4, khat, _, _ in prep]
        dps = [_dot(do4s[hh], _window(vp_ref, vc_ref, hh), NT) for hh in range(PAIR)]
        soft = [_attn_softmax(s, sink_ref, mask4, (p * PAIR + hh) * GROUP) for hh, s in enumerate(scores)]
        dss = []
        for hh in range(PAIR):
            pr, psink = soft[hh]
            delta = jnp.sum(pr * dps[hh], axis=-1, keepdims=True)
            dss.append(pr * (dps[hh] - delta) * (HEAD_DIM ** -0.5))
            dsk = -psink * delta
            for g in range(GROUP):
                tot = jnp.sum(dsk[g * BLOCK:(g + 1) * BLOCK], axis=0, keepdims=True)
                dsink_rows.append(jnp.broadcast_to(tot, (1, 128)))
        dq4s = [_dot(dss[hh], prep[hh][1], NN) for hh in range(PAIR)]
        dkhats = [_dot(dss[hh], prep[hh][0], TN) for hh in range(PAIR)]
        dvs = [_dot(soft[hh][0], do4s[hh], TN) for hh in range(PAIR)]
        for hh in range(PAIR):
            _, _, (q_raw, qn, qr), (k_raw, kn, kr) = prep[hh]
            dq4, dkhat_win, dv_win = dq4s[hh], dkhats[hh], dvs[hh]
            for g in range(GROUP):
                dqh = dq4[g * BLOCK:(g + 1) * BLOCK]
                dqg_acc += jnp.sum(dqh * qn[g], axis=0, keepdims=True)
                dqn = dqh * qg
                r = qr[g]
                dq_parts.append(r * dqn - q_raw[g] * (r * r * r * jnp.mean(dqn * q_raw[g], axis=-1, keepdims=True)))
            dkh = dkhat_win[BLOCK:] + dk_carry[hh]
            dk_carry[hh] = dkhat_win[:BLOCK]
            dkg_acc += jnp.sum(dkh * kn[BLOCK:], axis=0, keepdims=True)
            dkn = dkh * kg
            r = kr[BLOCK:]
            kc_raw = k_raw[BLOCK:]
            dk_parts.append(r * dkn - kc_raw * (r * r * r * jnp.mean(dkn * kc_raw, axis=-1, keepdims=True)))
            dv_parts.append(dv_win[BLOCK:] + dv_carry[hh])
            dv_carry[hh] = dv_win[:BLOCK]
        dq_ref[...] = jnp.concatenate(dq_parts, axis=1).astype(dq_ref.dtype)
        dk_ref[...] = jnp.concatenate(dk_parts, axis=1).astype(dk_ref.dtype)
        dv_ref[...] = jnp.concatenate(dv_parts, axis=1).astype(dv_ref.dtype)
        dqg_ref[...] += dqg_acc
        dkg_ref[...] += dkg_acc
        dsink_ref[...] += jnp.concatenate(dsink_rows, axis=0)

    rev = lambda i: nb - 1 - i
    prev = lambda i: jnp.maximum(nb - 2 - i, 0)
    vec = pl.BlockSpec((1, HEAD_DIM), lambda p, i: (0, 0))
    acc64 = pl.BlockSpec((None, 1, HEAD_DIM), lambda p, i: (p, 0, 0))
    dq, dk, dv, dqg, dkg, dsink = pl.pallas_call(
        body, grid=(npair, nb),
        in_specs=[pl.BlockSpec((BLOCK, QW), lambda p, i: (rev(i), p)),
                  pl.BlockSpec((BLOCK, KW), lambda p, i: (prev(i), k0 + p)),
                  pl.BlockSpec((BLOCK, KW), lambda p, i: (rev(i), k0 + p)),
                  pl.BlockSpec((BLOCK, KW), lambda p, i: (prev(i), v0 + p)),
                  pl.BlockSpec((BLOCK, KW), lambda p, i: (rev(i), v0 + p)),
                  pl.BlockSpec((BLOCK, QW), lambda p, i: (rev(i), p)),
                  vec, vec, pl.BlockSpec(memory_space=pltpu.SMEM)],
        out_specs=[pl.BlockSpec((BLOCK, QW), lambda p, i: (rev(i), p)),
                   pl.BlockSpec((BLOCK, KW), lambda p, i: (rev(i), p)),
                   pl.BlockSpec((BLOCK, KW), lambda p, i: (rev(i), p)),
                   acc64, acc64,
                   pl.BlockSpec((None, PAIR * GROUP, 128), lambda p, i: (p, 0, 0))],
        out_shape=[jax.ShapeDtypeStruct((T, dq_w), BF16),
                   jax.ShapeDtypeStruct((T, npair * KW), BF16),
                   jax.ShapeDtypeStruct((T, npair * KW), BF16),
                   jax.ShapeDtypeStruct((npair, 1, HEAD_DIM), F32),
                   jax.ShapeDtypeStruct((npair, 1, HEAD_DIM), F32),
                   jax.ShapeDtypeStruct((npair, PAIR * GROUP, 128), F32)],
        scratch_shapes=[pltpu.VMEM((PAIR, BLOCK, HEAD_DIM), F32), pltpu.VMEM((PAIR, BLOCK, HEAD_DIM), F32)],
        compiler_params=_cp("parallel", "arbitrary"), name=name,
    )(qkv, qkv, qkv, qkv, qkv, d_out, q_gain, k_gain, sinks)
    return dq, dk, dv, jnp.sum(dqg, axis=0), jnp.sum(dkg, axis=0), dsink[:, :, 0].reshape(-1)


def _softplus_neg(lam):
    return jnp.maximum(-lam, 0.0) + jnp.log1p(jnp.exp(-jnp.abs(lam)))


def _conv_taps(xc, cw_ref, tt, lead):
    K = cw_ref.shape[0]
    acc = None
    for k in range(K):
        off = lead - (K - 1 - k)
        term = cw_ref[k:k + 1, :] * xc[off:off + tt]
        acc = term if acc is None else acc + term
    return acc


def _bcast_row(x, row):
    return jnp.broadcast_to(x[row:row + 1, :], x.shape)


def rec_fwd(z, cw, cb, wa, ba, wi, bi, lam, *, name):
    T, C2 = z.shape
    C = C2 // 2
    nblk, bd, _ = wa.shape
    tt = _tile(T, 128, HALO)
    ng = tt // 8

    def body(x_ref, y_ref, halo_ref, cw_ref, cb_ref, wa_ref, ba_ref, wi_ref, bi_ref, lam_ref,
             xb_ref, r_ref, i_ref, a_ref, h_ref, hp_ref, hg_ref, carry, u_scr):
        step = pl.program_id(0)

        @pl.when(step == 0)
        def _():
            carry[...] = jnp.zeros_like(carry)

        halo = jnp.where(step > 0, halo_ref[...], 0.0)
        xc = jnp.concatenate([halo, x_ref[...]], axis=0)
        xb = cb_ref[...] + _conv_taps(xc, cw_ref, tt, HALO)
        xb_ref[...] = xb
        pa, pi = [], []
        for b in range(nblk):
            xs = xb[:, b * bd:(b + 1) * bd]
            pa.append(_dot(xs, wa_ref[b], NN))
            pi.append(_dot(xs, wi_ref[b], NN))
        r = jax.nn.sigmoid(jnp.concatenate(pa, axis=1) + ba_ref[...])
        ig = jax.nn.sigmoid(jnp.concatenate(pi, axis=1) + bi_ref[...])
        r_ref[...] = r
        i_ref[...] = ig
        nl = LRU_C * r * _softplus_neg(lam_ref[...])
        a_ref[...] = jnp.exp(-nl)
        th = jnp.tanh(nl)
        u_scr[...] = jnp.sqrt(2.0 * th / (1.0 + th)) * (ig * xb)

        rowid = lax.broadcasted_iota(jnp.int32, (8, C), 0)

        def group(gi, hc):
            r0 = pl.multiple_of(gi * 8, 8)
            a8 = a_ref[pl.ds(r0, 8), :]
            u8 = u_scr[pl.ds(r0, 8), :]
            for d in (1, 2, 4):
                a_sh = jnp.where(rowid >= d, pltpu.roll(a8, d, 0), 1.0)
                u_sh = jnp.where(rowid >= d, pltpu.roll(u8, d, 0), 0.0)
                u8 = a8 * u_sh + u8
                a8 = a8 * a_sh
            h8 = u8 + a8 * hc
            h_ref[pl.ds(r0, 8), :] = h8
            hp_ref[pl.ds(r0, 8), :] = jnp.where(rowid >= 1, pltpu.roll(h8, 1, 0), hc)
            return _bcast_row(h8, 7)

        carry[...] = lax.fori_loop(0, ng, group, carry[...])
        hg_ref[...] = (h_ref[...] * _gelu(y_ref[...])).astype(hg_ref.dtype)

    row = lambda c: pl.BlockSpec((tt, C), lambda i, c=c: (i, c))
    vec = pl.BlockSpec((1, C), lambda i: (0, 0))
    full = lambda shp: pl.BlockSpec(shp, lambda i, n=len(shp): (0,) * n)
    per = tt // HALO
    outs = pl.pallas_call(
        body, grid=(T // tt,),
        in_specs=[row(0), row(1), pl.BlockSpec((HALO, C), lambda i: (jnp.maximum(i * per - 1, 0), 0)),
                  full(cw.shape), vec, full(wa.shape), vec, full(wi.shape), vec, vec],
        out_specs=[row(0)] * 7,
        out_shape=[jax.ShapeDtypeStruct((T, C), F32)] * 6 + [jax.ShapeDtypeStruct((T, C), BF16)],
        scratch_shapes=[pltpu.VMEM((8, C), F32), pltpu.VMEM((tt, C), F32)],
        compiler_params=_cp("arbitrary"), name=name,
    )(z, z, z, cw, cb, wa, ba, wi, bi, lam)
    return outs


def rec_bwd_scan(dhg, h, a, z, *, name):
    T, C = h.shape
    tt = _tile(T, 256, HALO)
    ng = tt // 8
    nb = T // tt

    def body(dhg_ref, h_ref, a_ref, y_ref, dy_ref, yb_ref, ycarry, acarry, g_scr):
        step = pl.program_id(0)

        @pl.when(step == 0)
        def _():
            ycarry[...] = jnp.zeros_like(ycarry)
            acarry[...] = jnp.zeros_like(acarry)

        gate, dgate = _gelu_and_grad(y_ref[...])
        dhg_v = dhg_ref[...].astype(F32)
        dy_ref[...] = (dhg_v * h_ref[...] * dgate).astype(dy_ref.dtype)
        g_scr[...] = dhg_v * gate
        rowid = lax.broadcasted_iota(jnp.int32, (8, C), 0)

        def group(j, c):
            yc, ac = c
            r0 = pl.multiple_of((ng - 1 - j) * 8, 8)
            a8 = a_ref[pl.ds(r0, 8), :]
            y8 = g_scr[pl.ds(r0, 8), :]
            b8 = jnp.where(rowid < 7, pltpu.roll(a8, 7, 0), ac)
            for d in (1, 2, 4):
                y_sh = jnp.where(rowid < 8 - d, pltpu.roll(y8, 8 - d, 0), 0.0)
                b_sh = jnp.where(rowid < 8 - d, pltpu.roll(b8, 8 - d, 0), 1.0)
                y8 = y8 + b8 * y_sh
                b8 = b8 * b_sh
            y8 = y8 + b8 * yc
            yb_ref[pl.ds(r0, 8), :] = y8
            return _bcast_row(y8, 0), _bcast_row(a8, 0)

        yc, ac = lax.fori_loop(0, ng, group, (ycarry[...], acarry[...]))
        ycarry[...] = yc
        acarry[...] = ac

    rev = lambda c: pl.BlockSpec((tt, C), lambda i, c=c: (nb - 1 - i, c))
    return pl.pallas_call(
        body, grid=(nb,), in_specs=[rev(0), rev(0), rev(0), rev(1)], out_specs=[rev(0), rev(0)],
        out_shape=[jax.ShapeDtypeStruct((T, C), BF16), jax.ShapeDtypeStruct((T, C), F32)],
        scratch_shapes=[pltpu.VMEM((8, C), F32), pltpu.VMEM((8, C), F32), pltpu.VMEM((tt, C), F32)],
        compiler_params=_cp("arbitrary"), name=name,
    )(dhg, h, a, z)


def rec_bwd_gates(ybar, hprev, a, r, ig, xb, lam, wa, wi, *, name):
    T, C = xb.shape
    nblk, bd, _ = wa.shape
    tt = _tile(T, 256, 8)
    nb = T // tt

    def body(y_ref, hp_ref, a_ref, r_ref, i_ref, xb_ref, lam_ref, wa_ref, wi_ref,
             dxb_ref, dwa_ref, dwi_ref, dba_ref, dbi_ref, dlam_ref):
        step = pl.program_id(0)

        @pl.when(step == 0)
        def _():
            for ref in (dwa_ref, dwi_ref, dba_ref, dbi_ref, dlam_ref):
                ref[...] = jnp.zeros_like(ref)

        y, av, rv, iv, xv = y_ref[...], a_ref[...], r_ref[...], i_ref[...], xb_ref[...]
        sp = _softplus_neg(lam_ref[...])
        th = jnp.tanh(LRU_C * rv * sp)
        s = jnp.sqrt(2.0 * th / (1.0 + th))
        d_nl = -(y * hp_ref[...] * av) + (y * iv * xv) * (av * av) / s
        dlam_ref[...] += jnp.sum(d_nl * rv, axis=0, keepdims=True) * LRU_C
        dr = d_nl * (LRU_C * sp)
        di = y * s * xv
        dpa = dr * rv * (1.0 - rv)
        dpi = di * iv * (1.0 - iv)
        dba_ref[...] += jnp.sum(dpa, axis=0, keepdims=True)
        dbi_ref[...] += jnp.sum(dpi, axis=0, keepdims=True)
        parts = []
        for b in range(nblk):
            sl = slice(b * bd, (b + 1) * bd)
            xs, da_b, di_b = xv[:, sl], dpa[:, sl], dpi[:, sl]
            dwa_ref[b] += _dot(xs, da_b, TN)
            dwi_ref[b] += _dot(xs, di_b, TN)
            parts.append(_dot(da_b, wa_ref[b], NT) + _dot(di_b, wi_ref[b], NT))
        dxb_ref[...] = y * s * iv + jnp.concatenate(parts, axis=1)

        @pl.when(step == nb - 1)
        def _():
            dlam_ref[...] = dlam_ref[...] * (-jax.nn.sigmoid(-lam_ref[...]))

    row = pl.BlockSpec((tt, C), lambda i: (i, 0))
    vec = pl.BlockSpec((1, C), lambda i: (0, 0))
    wsp = pl.BlockSpec(wa.shape, lambda i: (0, 0, 0))
    return pl.pallas_call(
        body, grid=(nb,), in_specs=[row] * 6 + [vec, wsp, wsp],
        out_specs=[row, wsp, wsp, vec, vec, vec],
        out_shape=[jax.ShapeDtypeStruct((T, C), F32), jax.ShapeDtypeStruct(wa.shape, F32),
                   jax.ShapeDtypeStruct(wa.shape, F32)] + [jax.ShapeDtypeStruct((1, C), F32)] * 3,
        compiler_params=_cp("arbitrary"), name=name,
    )(ybar, hprev, a, r, ig, xb, lam, wa, wi)


def conv_bwd(d, x0, cw, *, name):
    T, C = d.shape
    K = cw.shape[0]
    tt = _tile(T, 256, HALO)
    per = tt // HALO
    nb = T // tt

    def body(d_ref, dn_ref, x_ref, xp_ref, cw_ref, dx_ref, dcw_ref, dcb_ref):
        step = pl.program_id(0)

        @pl.when(step == 0)
        def _():
            dcw_ref[...] = jnp.zeros_like(dcw_ref)
            dcb_ref[...] = jnp.zeros_like(dcb_ref)

        dv = d_ref[...].astype(F32)
        dc = jnp.concatenate([dv, jnp.where(step < nb - 1, dn_ref[...].astype(F32), 0.0)], axis=0)
        xc = jnp.concatenate([jnp.where(step > 0, xp_ref[...].astype(F32), 0.0), x_ref[...].astype(F32)], axis=0)
        acc = None
        rows = []
        for k in range(K):
            sh = K - 1 - k
            term = cw_ref[k:k + 1, :] * dc[sh:sh + tt]
            acc = term if acc is None else acc + term
            rows.append(jnp.sum(dv * xc[HALO - sh:HALO - sh + tt], axis=0, keepdims=True))
        dx_ref[...] = acc.astype(dx_ref.dtype)
        dcw_ref[...] += jnp.concatenate(rows, axis=0)
        dcb_ref[...] += jnp.sum(dv, axis=0, keepdims=True)

    row = pl.BlockSpec((tt, C), lambda i: (i, 0))
    return pl.pallas_call(
        body, grid=(nb,),
        in_specs=[row, pl.BlockSpec((HALO, C), lambda i: (jnp.minimum((i + 1) * per, T // HALO - 1), 0)),
                  row, pl.BlockSpec((HALO, C), lambda i: (jnp.maximum(i * per - 1, 0), 0)),
                  pl.BlockSpec((K, C), lambda i: (0, 0))],
        out_specs=[row, pl.BlockSpec((K, C), lambda i: (0, 0)), pl.BlockSpec((1, C), lambda i: (0, 0))],
        out_shape=[jax.ShapeDtypeStruct((T, C), BF16), jax.ShapeDtypeStruct((K, C), F32),
                   jax.ShapeDtypeStruct((1, C), F32)],
        compiler_params=_cp("arbitrary"), name=name,
    )(d, d, x0, x0, cw)


def ffn_act_fwd(u0, cw, cb, *, n, name):
    T, W = u0.shape
    G = W // (2 * n)
    tt = _tile(T, 256, HALO)
    per = tt // HALO

    def body(u_ref, up_ref, cw_ref, cb_ref, a_ref, uo_ref):
        step = pl.program_id(1)
        xc = jnp.concatenate([jnp.where(step > 0, up_ref[...].astype(F32), 0.0), u_ref[...].astype(F32)], axis=0)
        u = cb_ref[...] + _conv_taps(xc, cw_ref, tt, HALO)
        uo_ref[...] = u.astype(uo_ref.dtype)
        a_ref[...] = (_gelu(u[:, :n]) * u[:, n:]).astype(a_ref.dtype)

    return pl.pallas_call(
        body, grid=(G, T // tt),
        in_specs=[pl.BlockSpec((tt, 2 * n), lambda j, i: (i, j)),
                  pl.BlockSpec((HALO, 2 * n), lambda j, i: (jnp.maximum(i * per - 1, 0), j)),
                  pl.BlockSpec((cw.shape[0], 2 * n), lambda j, i: (0, j)),
                  pl.BlockSpec((1, 2 * n), lambda j, i: (0, j))],
        out_specs=[pl.BlockSpec((tt, n), lambda j, i: (i, j)), pl.BlockSpec((tt, 2 * n), lambda j, i: (i, j))],
        out_shape=[jax.ShapeDtypeStruct((T, G * n), BF16), jax.ShapeDtypeStruct((T, W), BF16)],
        compiler_params=_cp("parallel", "parallel"), name=name)(u0, u0, cw, cb)


def ffn_act_bwd(u0, u, da, cw, *, n, name):
    T, W = u0.shape
    G = W // (2 * n)
    K = cw.shape[0]
    tt = _tile(T, 128, HALO)
    per = tt // HALO
    nb = T // tt

    def body(x_ref, u_ref, un_ref, da_ref, dan_ref, cw_ref, du_ref, dcw_ref, dcb_ref):
        step = pl.program_id(1)

        @pl.when(step == 0)
        def _():
            dcw_ref[...] = jnp.zeros_like(dcw_ref)
            dcb_ref[...] = jnp.zeros_like(dcb_ref)

        ue = jnp.concatenate([u_ref[...].astype(F32), un_ref[...].astype(F32)], axis=0)
        dae = jnp.concatenate([da_ref[...].astype(F32), jnp.where(step < nb - 1, dan_ref[...].astype(F32), 0.0)],
                              axis=0)
        gl, dgl = _gelu_and_grad(ue[:, :n])
        du = jnp.concatenate([dae * ue[:, n:] * dgl, dae * gl], axis=1)
        xt = x_ref[...].astype(F32)
        acc = None
        rows = []
        for k in range(K):
            sh = K - 1 - k
            dsh = du[sh:sh + tt]
            term = cw_ref[k:k + 1, :] * dsh
            acc = term if acc is None else acc + term
            rows.append(jnp.sum(dsh * xt, axis=0, keepdims=True))
        du_ref[...] = acc.astype(du_ref.dtype)
        dcw_ref[...] += jnp.concatenate(rows, axis=0)
        dcb_ref[...] += jnp.sum(du[:tt], axis=0, keepdims=True)

    last = T // HALO - 1
    nxt = lambda j, i: (jnp.minimum((i + 1) * per, last), j)
    return pl.pallas_call(
        body, grid=(G, nb),
        in_specs=[pl.BlockSpec((tt, 2 * n), lambda j, i: (i, j)),
                  pl.BlockSpec((tt, 2 * n), lambda j, i: (i, j)),
                  pl.BlockSpec((HALO, 2 * n), nxt),
                  pl.BlockSpec((tt, n), lambda j, i: (i, j)),
                  pl.BlockSpec((HALO, n), nxt),
                  pl.BlockSpec((K, 2 * n), lambda j, i: (0, j))],
        out_specs=[pl.BlockSpec((tt, 2 * n), lambda j, i: (i, j)),
                   pl.BlockSpec((K, 2 * n), lambda j, i: (0, j)),
                   pl.BlockSpec((1, 2 * n), lambda j, i: (0, j))],
        out_shape=[jax.ShapeDtypeStruct((T, W), BF16), jax.ShapeDtypeStruct((K, W), F32),
                   jax.ShapeDtypeStruct((1, W), F32)],
        compiler_params=_cp("parallel", "arbitrary"), name=name,
    )(u0, u, u, da, da, cw)


def _ffn_shard_of_block(q):
    return (q % 2) * (N_DEV // 2) + q // 2


def _ffn_block_of_shard(s):
    return (s % (N_DEV // 2)) * 2 + s // (N_DEV // 2)


def _group_cols(v):
    lead = v.shape[:-1]
    n = v.shape[-1] // N_DEV
    return jnp.swapaxes(v.reshape(lead + (2, N_DEV // 2, n)), -3, -2).reshape(v.shape)


def _ungroup_cols(v):
    lead = v.shape[:-1]
    n = v.shape[-1] // N_DEV
    return jnp.swapaxes(v.reshape(lead + (N_DEV // 2, 2, n)), -3, -2).reshape(v.shape)


def local_step(x, target, w, plan=None):
    depth = w["mix_norm"].shape[0]
    n_up = w["ffn_conv_w"].shape[-1] // N_DEV
    g = {k: [None] * (w[k].shape[0] if hasattr(w[k], "shape") else len(w[k])) for k in w}

    def run(fn, stage, l, slot, *args, **kw):
        carry = None if plan is None else plan.carry(stage, l, slot, g)
        if carry is None:
            return fn(*args, **kw)
        out, extra = fn(*args, carry=carry, **kw)
        plan.done(stage, l, slot, extra)
        return out

    saved = []
    for l in range(depth):
        j = l // 2
        h = rmsnorm_fwd(x, w["mix_norm"][l:l + 1], name="mix_norm_fwd")
        if l % 2 == 0:
            qkv = run(mm_nn, "fwd", l, "in", h, w["attn_w_qkv"][j], out_dtype=F32, name="qkv_proj")
            ao = run(attn_fwd, "fwd", l, "attn", qkv, w["attn_q_gain"][j:j + 1], w["attn_k_gain"][j:j + 1],
                     w["attn_sinks"][j:j + 1], name="attn_fwd")
            x1 = run(mm_nn, "fwd", l, "out", ao, w["attn_w_o"][j], out_dtype=F32, res=x, name="attn_out_proj")
            mix = (qkv, ao)
        else:
            z = run(mm_nn, "fwd", l, "in", h, w["rec_w_in"][j], out_dtype=F32, name="rec_in_proj")
            xb, r, ig, a, hs, hprev, hg = rec_fwd(
                z, w["rec_conv_w"][j], w["rec_conv_b"][j:j + 1], w["rec_w_a"][j], w["rec_b_a"][j:j + 1],
                w["rec_w_i"][j], w["rec_b_i"][j:j + 1], w["rec_lambda"][j:j + 1], name="rec_fwd")
            x1 = run(mm_nn, "fwd", l, "out", hg, w["rec_w_out"][j], out_dtype=F32, res=x, name="rec_out_proj")
            mix = (z, xb, r, ig, a, hs, hprev, hg)
        h2 = rmsnorm_fwd(x1, w["ffn_norm"][l:l + 1], name="ffn_norm_fwd")
        u0 = run(mm_nn, "fwd", l, "up", h2, w["ffn_w_up"][l], out_dtype=BF16, name="ffn_up_proj")
        act, u = ffn_act_fwd(u0, w["ffn_conv_w"][l], w["ffn_conv_b"][l:l + 1], n=n_up, name="ffn_act_fwd")
        x2 = run(mm_nn, "fwd", l, "down", act, w["ffn_w_down"][l], out_dtype=F32, res=x1, name="ffn_down_proj")
        saved.append((x, h, mix, x1, h2, u0, u, act))
        x = x2

    loss_vec, dx = loss_head(x, target, name="loss_head")

    for l in reversed(range(depth)):
        j = l // 2
        x0, h, mix, x1, h2, u0, u, act = saved[l]
        dact = mm_nt(dx, w["ffn_w_down"][l], out_dtype=BF16, name="ffn_down_dx")
        g["ffn_w_down"][l] = mm_tn(act, dx, out_dtype=BF16, name="ffn_down_dw")
        du0, dcw, dcb = ffn_act_bwd(u0, u, dact, w["ffn_conv_w"][l], n=n_up, name="ffn_act_bwd")
        g["ffn_conv_w"][l], g["ffn_conv_b"][l] = _ungroup_cols(dcw), _ungroup_cols(dcb)[0]
        g["ffn_w_up"][l] = run(mm_tn, "bwd", l, "up_dw", h2, du0, out_dtype=BF16, col_shards=N_DEV, shard_of_block=_ffn_shard_of_block,
                                 name="ffn_up_dw")
        dh2 = run(mm_nt, "bwd", l, "up_dx", du0, w["ffn_w_up"][l], out_dtype=F32, name="ffn_up_dx")
        dx1, dgf = rmsnorm_bwd(x1, w["ffn_norm"][l:l + 1], dh2, dx, name="ffn_norm_bwd")
        g["ffn_norm"][l] = dgf[0]
        if l % 2 == 0:
            qkv, ao = mix
            dao = mm_nt(dx1, w["attn_w_o"][j], out_dtype=BF16, name="attn_out_dx")
            g["attn_w_o"][j] = mm_tn(ao, dx1, out_dtype=BF16, name="attn_out_dw")
            dq, dk, dv, dqg, dkg, dsk = attn_bwd(qkv, dao, w["attn_q_gain"][j:j + 1], w["attn_k_gain"][j:j + 1],
                                                 w["attn_sinks"][j:j + 1], name="attn_bwd")
            g["attn_q_gain"][j], g["attn_k_gain"][j], g["attn_sinks"][j] = dqg[0], dkg[0], dsk
            dqkv = jnp.concatenate([dq, dk, dv], axis=1)
            g["attn_w_qkv"][j] = run(mm_tn, "bwd", l, "in_dw", h, dqkv, out_dtype=BF16, col_shards=N_DEV, name="qkv_dw")
            dh = run(mm_nt, "bwd", l, "in_dx", dqkv, w["attn_w_qkv"][j], out_dtype=F32, name="qkv_dx")
        else:
            z, xb, r, ig, a, hs, hprev, hg = mix
            dhg = mm_nt(dx1, w["rec_w_out"][j], out_dtype=BF16, name="rec_out_dx")
            g["rec_w_out"][j] = mm_tn(hg, dx1, out_dtype=BF16, name="rec_out_dw")
            dyb, ybar = rec_bwd_scan(dhg, hs, a, z, name="rec_bwd_scan")
            dxb, dwa, dwi, dba, dbi, dlam = rec_bwd_gates(
                ybar, hprev, a, r, ig, xb, w["rec_lambda"][j:j + 1], w["rec_w_a"][j], w["rec_w_i"][j],
                name="rec_bwd_gates")
            dxb0, dcw, dcb = conv_bwd(dxb, z, w["rec_conv_w"][j], name="rec_conv_bwd")
            g["rec_w_a"][j], g["rec_w_i"][j] = dwa, dwi
            g["rec_b_a"][j], g["rec_b_i"][j], g["rec_lambda"][j] = dba[0], dbi[0], dlam[0]
            g["rec_conv_w"][j], g["rec_conv_b"][j] = dcw, dcb[0]
            dz = jnp.concatenate([dxb0, dyb], axis=1)
            g["rec_w_in"][j] = run(mm_tn, "bwd", l, "in_dw", h, dz, out_dtype=BF16, col_shards=N_DEV, name="rec_in_dw")
            dh = run(mm_nt, "bwd", l, "in_dx", dz, w["rec_w_in"][j], out_dtype=F32, name="rec_in_dx")
        dx, dgm = rmsnorm_bwd(x0, w["mix_norm"][l:l + 1], dh, dx1, name="mix_norm_bwd")
        g["mix_norm"][l] = dgm[0]
    return loss_vec, dx, g


HBM = pl.BlockSpec(memory_space=pltpu.HBM)
N_PEER = N_DEV - 1


def _here():
    return lax.axis_index("x"), lax.axis_index("y"), lax.axis_index("c")


def _sid(dev):
    return 4 * dev[0] + 2 * dev[1] + dev[2]


def _exchange_sems(n):
    return [pltpu.SemaphoreType.DMA((n * N_PEER,)), pltpu.SemaphoreType.DMA((n * N_PEER,)),
            pltpu.SemaphoreType.DMA((n,))]


def gather_exchange(shards, out_structs, windows):
    n = len(shards)

    def parts(outs, sems):
        send_sems, recv_sems, _ = sems
        x, y, c = _here()
        me, sib = (x, y, c), (x, y, 1 - c)
        chips = [(1 - x, y), (x, 1 - y), (1 - x, 1 - y)]

        def copy(i, k, block, to, src=None):
            dst = windows[i](outs[i], _sid(block))
            return pltpu.make_async_remote_copy(
                src_ref=dst if src is None else src, dst_ref=dst,
                send_sem=send_sems.at[i * N_PEER + k], recv_sem=recv_sems.at[i * N_PEER + k],
                device_id=to, device_id_type=MESH)

        return me, sib, chips, c, copy

    def own_copies(ins, outs, sems):
        me, sib, chips, c, copy = parts(outs, sems)
        local = [pltpu.make_async_copy(ins[i], windows[i](outs[i], _sid(me)), sems[2].at[i]) for i in range(n)]
        first = []
        for i in range(n):
            first.append(copy(i, 0, me, sib, src=ins[i]))
            first += [copy(i, 1 + j, me, (*chip, c), src=ins[i]) for j, chip in enumerate(chips)]
        return local, first

    def start(ins, outs, sems):
        local, first = own_copies(ins, outs, sems)
        for cp in local + first:
            cp.start()

    def finish(ins, outs, sems):
        me, sib, chips, c, copy = parts(outs, sems)
        local, first = own_copies(ins, outs, sems)
        passed = []
        for i in range(n):
            for j, chip in enumerate(chips):
                copy(i, 1 + j, (*chip, c), me).wait_recv()
                fwd = copy(i, 4 + j, (*chip, c), sib)
                fwd.start()
                passed.append(fwd)
        for i in range(n):
            copy(i, 0, sib, me).wait_recv()
            for j, chip in enumerate(chips):
                copy(i, 4 + j, (*chip, 1 - c), me).wait_recv()
        for cp in first + passed:
            cp.wait_send()
        for cp in local:
            cp.wait()

    return dict(arrays=list(shards), out_structs=list(out_structs), sems=_exchange_sems(n), start=start,
                finish=finish)


def run_exchange(ex, *, name):
    n = len(ex["arrays"])

    def body(*refs):
        ins, outs, sems = refs[:n], refs[n:2 * n], refs[2 * n:]
        ex["start"](ins, outs, sems)
        ex["finish"](ins, outs, sems)

    return pl.pallas_call(
        body, in_specs=[HBM] * n, out_specs=[HBM] * n, out_shape=ex["out_structs"], scratch_shapes=ex["sems"],
        name=name)(*ex["arrays"])


def scatter_exchange(grads):
    n = len(grads)

    def parts(ins, outs, sems):
        send_sems, recv_sems, local_sems = sems
        x, y, c = _here()
        me = (x, y, c)
        peers = []
        for k in range(1, N_DEV):
            kx, ky, kc = (k >> 2) & 1, (k >> 1) & 1, k & 1
            peers.append((1 - x if kx else x, 1 - y if ky else y, 1 - c if kc else c))

        def copy(i, k):
            return pltpu.make_async_remote_copy(
                src_ref=ins[i].at[_sid(peers[k])], dst_ref=outs[i].at[_sid(me)],
                send_sem=send_sems.at[i * N_PEER + k], recv_sem=recv_sems.at[i * N_PEER + k],
                device_id=peers[k], device_id_type=MESH)

        def arrival(i, k):
            return pltpu.make_async_remote_copy(
                src_ref=ins[i].at[_sid(me)], dst_ref=outs[i].at[_sid(peers[k])],
                send_sem=send_sems.at[i * N_PEER + k], recv_sem=recv_sems.at[i * N_PEER + k],
                device_id=peers[k], device_id_type=MESH)

        local = [pltpu.make_async_copy(ins[i].at[_sid(me)], outs[i].at[_sid(me)], local_sems.at[i]) for i in range(n)]
        sends = [copy(i, k) for i in range(n) for k in range(N_PEER)]
        return local, sends, arrival

    def start(ins, outs, sems):
        local, sends, _ = parts(ins, outs, sems)
        for cp in local + sends:
            cp.start()

    def finish(ins, outs, sems):
        local, sends, arrival = parts(ins, outs, sems)
        for i in range(n):
            for k in range(N_PEER):
                arrival(i, k).wait_recv()
        for cp in sends:
            cp.wait_send()
        for cp in local:
            cp.wait()

    return dict(arrays=list(grads), out_structs=[jax.ShapeDtypeStruct(g.shape, g.dtype) for g in grads],
                sems=_exchange_sems(n), start=start, finish=finish)


def adamw_family(contribs, w, m, v, *, name):
    L, R, C = w.shape
    S = contribs[0].shape[0]
    tr = _tile(R, max(8, (1 << 20) // (C * S)), 8)
    nr = R // tr
    c1 = 1.0 / (1.0 - ADAM_B1 ** ADAM_STEP)
    c2 = 1.0 / (1.0 - ADAM_B2 ** ADAM_STEP)

    def body(*refs):
        c_refs = refs[:L]
        w_ref, m_ref, v_ref, g_ref, d_ref, nm_ref, nv_ref = refs[L:]
        layer = pl.program_id(0)
        for l in range(L):
            @pl.when(layer == l)
            def _(l=l):
                g = c_refs[l][0].astype(F32)
                for s in range(1, S):
                    g = g + c_refs[l][s].astype(F32)
                mm = ADAM_B1 * m_ref[...] + (1.0 - ADAM_B1) * g
                vv = ADAM_B2 * v_ref[...] + (1.0 - ADAM_B2) * (g * g)
                g_ref[...] = g
                nm_ref[...] = mm
                nv_ref[...] = vv
                d_ref[...] = -ADAM_LR * ((mm * c1) / (jnp.sqrt(vv * c2) + ADAM_EPS) + ADAM_WD * w_ref[...])

    def cspec(l):
        return pl.BlockSpec((S, tr, C), lambda ll, i, l=l: (0, jnp.where(ll == l, i, 0), 0))

    lay = pl.BlockSpec((None, tr, C), lambda ll, i: (ll, i, 0))
    return pl.pallas_call(
        body, grid=(L, nr), in_specs=[cspec(l) for l in range(L)] + [lay] * 3, out_specs=[lay] * 4,
        out_shape=[jax.ShapeDtypeStruct((L, R, C), F32)] * 4,
        compiler_params=_cp("arbitrary", "arbitrary"), name=name)(*contribs, w, m, v)


def sum_slots(a, *, name):
    S, R, C = a.shape
    tr = _tile(R, 256, 8)

    def body(a_ref, o_ref):
        t = a_ref[0]
        for s in range(1, S):
            t = t + a_ref[s]
        o_ref[...] = t

    return pl.pallas_call(
        body, grid=(R // tr,), in_specs=[pl.BlockSpec((S, tr, C), lambda i: (0, i, 0))],
        out_specs=pl.BlockSpec((tr, C), lambda i: (i, 0)), out_shape=jax.ShapeDtypeStruct((R, C), F32),
        compiler_params=_cp("parallel"), name=name)(a)


LANES = 128


def _pack(arrs):
    flat = jnp.concatenate([a.reshape(-1).astype(F32) for a in arrs])
    rows = -(-flat.shape[0] // LANES)
    rows = -(-rows // 8) * 8
    return jnp.pad(flat, (0, rows * LANES - flat.shape[0])).reshape(rows, LANES)


def _unpack(buf, shapes):
    flat = buf.reshape(-1)
    out, off = [], 0
    for shp in shapes:
        size = int(np.prod(shp))
        out.append(flat[off:off + size].reshape(shp))
        off += size
    return out


def _gather_last(g):
    t = jnp.moveaxis(g, 0, -2)
    return t.reshape(t.shape[:-2] + (t.shape[-2] * t.shape[-1],))


def _own_last(full, s):
    n = full.shape[-1] // N_DEV
    t = full.reshape(full.shape[:-1] + (N_DEV, n))
    return lax.dynamic_index_in_dim(t, s, axis=t.ndim - 2, keepdims=False)


BIG = ["attn_w_qkv", "attn_w_o", "rec_w_in", "rec_w_out", "ffn_w_up", "ffn_w_down", "rec_w_a", "rec_w_i"]
SMALL_REPLICATED = ["mix_norm", "ffn_norm", "attn_q_gain", "attn_k_gain", "attn_sinks", "ffn_conv_b"]
SMALL_SHARDED = ["rec_conv_w", "rec_conv_b", "rec_b_a", "rec_b_i", "rec_lambda", "ffn_conv_w"]
SMALL = SMALL_REPLICATED + SMALL_SHARDED
WEIGHTS = ["mix_norm", "ffn_norm", "attn_w_qkv", "attn_q_gain", "attn_k_gain", "attn_sinks", "attn_w_o", "rec_w_in",
           "rec_conv_w", "rec_conv_b", "rec_w_a", "rec_b_a", "rec_w_i", "rec_b_i", "rec_lambda", "rec_w_out",
           "ffn_w_up", "ffn_conv_w", "ffn_conv_b", "ffn_w_down"]


def _col_window(n, block_of_shard=None):
    def win(ref, s):
        q = s if block_of_shard is None else block_of_shard(s)
        return ref.at[:, pl.ds(pl.multiple_of(q * n, 128), n)]
    return win


def _row_window(r):
    return lambda ref, s: ref.at[pl.ds(pl.multiple_of(s * r, 16), r), :]


def _gate_window(r):
    return lambda ref, s: ref.at[:, pl.ds(pl.multiple_of(s * r, 16), r), :]


def _slot_window(ref, s):
    return ref.at[s]


def _idx(name, layer):
    return layer if name.startswith("ffn") else layer // 2


def _slot_names(layer, slot):
    attn = layer % 2 == 0
    return {"in": ["attn_w_qkv"] if attn else ["rec_w_in"],
            "out": ["attn_w_o"] if attn else ["rec_w_out", "rec_w_a", "rec_w_i"],
            "up": ["ffn_w_up"], "down": ["ffn_w_down"]}[slot]


def _gather_for(p, items):
    shards, structs, wins = [], [], []
    for nme, layer in items:
        sh = p[nme][_idx(nme, layer)].astype(BF16)
        if nme in ("attn_w_qkv", "rec_w_in", "ffn_w_up"):
            K, n = sh.shape
            structs.append(jax.ShapeDtypeStruct((K, n * N_DEV), BF16))
            wins.append(_col_window(n, _ffn_block_of_shard if nme == "ffn_w_up" else None))
        elif nme in ("rec_w_a", "rec_w_i"):
            nblk, r, bd = sh.shape
            structs.append(jax.ShapeDtypeStruct((nblk, r * N_DEV, bd), BF16))
            wins.append(_gate_window(r))
        else:
            r, N = sh.shape
            structs.append(jax.ShapeDtypeStruct((r * N_DEV, N), BF16))
            wins.append(_row_window(r))
        shards.append(sh)
    return gather_exchange(shards, structs, wins)


def _send_layout(name, t):
    if name in ("rec_w_a", "rec_w_i"):
        nblk, bd, _ = t.shape
        return jnp.transpose(t.reshape(nblk, N_DEV, bd // N_DEV, bd), (1, 0, 2, 3)).astype(BF16)
    if name in ("attn_w_o", "rec_w_out", "ffn_w_down"):
        return t.reshape((N_DEV, t.shape[0] // N_DEV) + t.shape[1:])
    return t


class _Plan:
    BWD_SLOT = {"up_dw": "down", "up_dx": "up", "in_dw": "out", "in_dx": "in"}

    def __init__(self, p, w, contribs, depth):
        self.p, self.w, self.contribs, self.depth = p, w, contribs, depth
        self.pending = None

    def carry(self, stage, l, slot, g):
        if stage == "fwd":
            items = []
            if l == 0 and slot == "in":
                items += [(k, 0) for k in _slot_names(0, "out")]
            if l == 0 and slot == "attn":
                items += [(k, 0) for s in ("up", "down") for k in _slot_names(0, s)]
            if slot != "attn" and l + 1 < self.depth:
                items += [(k, l + 1) for k in _slot_names(l + 1, slot)]
            if not items:
                return None
            self.pending = items
            return _gather_for(self.p, items)
        self.pending = [(k, l) for k in _slot_names(l, self.BWD_SLOT[slot])]
        return scatter_exchange([_send_layout(k, g[k][_idx(k, layer)]) for k, layer in self.pending])

    def done(self, stage, l, slot, outs):
        dst = self.w if stage == "fwd" else self.contribs
        for (k, layer), o in zip(self.pending, outs):
            dst[k][_idx(k, layer)] = o


def _train_step(p, x, target, mom, vel):
    depth = p["mix_norm"].shape[0]
    s_me = _sid(_here())

    w = {k: [None] * p[k].shape[0] for k in BIG}
    first = [(k, 0) for k in _slot_names(0, "in")]
    for (k, _), t in zip(first, run_exchange(_gather_for(p, first), name="all_gather_first")):
        w[k][0] = t

    local_small = [p[k] for k in SMALL_SHARDED]
    packed = _pack(local_small)
    gathered, = run_exchange(
        gather_exchange([packed], [jax.ShapeDtypeStruct((N_DEV,) + packed.shape, F32)], [_slot_window]),
        name="all_gather_small")
    per_dev = [_unpack(gathered[s], [a.shape for a in local_small]) for s in range(N_DEV)]
    for i, k in enumerate(SMALL_SHARDED):
        w[k] = _gather_last(jnp.stack([per_dev[s][i] for s in range(N_DEV)]))
    for k in SMALL_REPLICATED:
        w[k] = p[k]
    nrec = w["rec_b_a"].shape[0]
    w["rec_b_a"] = w["rec_b_a"].reshape(nrec, -1)
    w["rec_b_i"] = w["rec_b_i"].reshape(nrec, -1)
    w["ffn_conv_w"] = _group_cols(w["ffn_conv_w"])
    w["ffn_conv_b"] = _group_cols(w["ffn_conv_b"])

    contribs = {k: [None] * len(w[k]) for k in BIG}
    loss_vec, dx, g = local_step(x[0], target[0], w, _Plan(p, w, contribs, depth))
    loss = lax.psum(jnp.sum(loss_vec), ("x", "y", "c"))

    out = {}
    for k in BIG:
        shp = p[k].shape
        L = shp[0]
        C = shp[-1]
        R = int(np.prod(shp[1:-1]))
        cs = [c.reshape(N_DEV, R, C) for c in contribs[k]]
        res = adamw_family(cs, p[k].reshape(L, R, C), mom[k].reshape(L, R, C), vel[k].reshape(L, R, C),
                           name="adamw_" + k)
        out[k] = [t.reshape(shp) for t in res]

    gsmall = [jnp.stack(g[k]) for k in SMALL]
    gp = _pack(gsmall)
    gall, = run_exchange(
        gather_exchange([gp], [jax.ShapeDtypeStruct((N_DEV,) + gp.shape, F32)], [_slot_window]),
        name="all_gather_small_grads")
    gsum = _unpack(sum_slots(gall, name="sum_small_grads"), [a.shape for a in gsmall])
    glocal = []
    for k, t in zip(SMALL, gsum):
        if k in SMALL_SHARDED:
            t = _own_last(t.reshape(p[k].shape[:-1] + (p[k].shape[-1] * N_DEV,)), s_me)
        glocal.append(t.reshape(p[k].shape))
    wp, mp, vp, gpk = (_pack([d[k] for k in SMALL]) for d in (p, mom, vel, dict(zip(SMALL, glocal))))
    res = adamw_family([gpk[None]], wp[None], mp[None], vp[None], name="adamw_small")
    shapes = [p[k].shape for k in SMALL]
    unp = [_unpack(t[0], shapes) for t in res]
    for i, k in enumerate(SMALL):
        out[k] = [glocal[i], unp[1][i], unp[2][i], unp[3][i]]

    return (loss, dx[None]) + tuple(out[k][q] for q in range(4) for k in WEIGHTS)


def kernel(x, mix_norm, ffn_norm, attn_w_qkv, attn_q_gain, attn_k_gain, attn_sinks, attn_w_o, rec_w_in, rec_conv_w, rec_conv_b, rec_w_a, rec_b_a, rec_w_i, rec_b_i, rec_lambda, rec_w_out, ffn_w_up, ffn_conv_w, ffn_conv_b, ffn_w_down, loss_target, m_mix_norm, m_ffn_norm, m_attn_w_qkv, m_attn_q_gain, m_attn_k_gain, m_attn_sinks, m_attn_w_o, m_rec_w_in, m_rec_conv_w, m_rec_conv_b, m_rec_w_a, m_rec_b_a, m_rec_w_i, m_rec_b_i, m_rec_lambda, m_rec_w_out, m_ffn_w_up, m_ffn_conv_w, m_ffn_conv_b, m_ffn_w_down, v_mix_norm, v_ffn_norm, v_attn_w_qkv, v_attn_q_gain, v_attn_k_gain, v_attn_sinks, v_attn_w_o, v_rec_w_in, v_rec_conv_w, v_rec_conv_b, v_rec_w_a, v_rec_b_a, v_rec_w_i, v_rec_b_i, v_rec_lambda, v_rec_w_out, v_ffn_w_up, v_ffn_conv_w, v_ffn_conv_b, v_ffn_w_down):
    p = dict(zip(WEIGHTS, (mix_norm, ffn_norm, attn_w_qkv, attn_q_gain, attn_k_gain, attn_sinks, attn_w_o, rec_w_in,
                           rec_conv_w, rec_conv_b, rec_w_a, rec_b_a, rec_w_i, rec_b_i, rec_lambda, rec_w_out,
                           ffn_w_up, ffn_conv_w, ffn_conv_b, ffn_w_down)))
    mom = dict(zip(WEIGHTS, (m_mix_norm, m_ffn_norm, m_attn_w_qkv, m_attn_q_gain, m_attn_k_gain, m_attn_sinks,
                             m_attn_w_o, m_rec_w_in, m_rec_conv_w, m_rec_conv_b, m_rec_w_a, m_rec_b_a, m_rec_w_i,
                             m_rec_b_i, m_rec_lambda, m_rec_w_out, m_ffn_w_up, m_ffn_conv_w, m_ffn_conv_b,
                             m_ffn_w_down)))
    vel = dict(zip(WEIGHTS, (v_mix_norm, v_ffn_norm, v_attn_w_qkv, v_attn_q_gain, v_attn_k_gain, v_attn_sinks,
                             v_attn_w_o, v_rec_w_in, v_rec_conv_w, v_rec_conv_b, v_rec_w_a, v_rec_b_a, v_rec_w_i,
                             v_rec_b_i, v_rec_lambda, v_rec_w_out, v_ffn_w_up, v_ffn_conv_w, v_ffn_conv_b,
                             v_ffn_w_down)))
    return _train_step(p, x, loss_target, mom, vel)
```

```python
import functools
import math

import jax
import jax.numpy as jnp
import numpy as np
from jax import lax
from jax.experimental import pallas as pl
from jax.experimental.pallas import tpu as pltpu

F32 = jnp.float32
BF16 = jnp.bfloat16

N_DEV = 8
HEAD_DIM = 64
GROUP = 4
BLOCK = 128
LRU_C = 8.0
EPS = 1e-6
HALO = 16
ADAM_LR, ADAM_B1, ADAM_B2, ADAM_EPS, ADAM_WD, ADAM_STEP = 0.001, 0.9, 0.999, 1e-08, 0.01, 10
VMEM_LIMIT = 56 * 1024 * 1024
MESH = pl.DeviceIdType.MESH
GELU_C = math.sqrt(2.0 / math.pi)


def _cp(*sem, vmem=VMEM_LIMIT):
    return pltpu.CompilerParams(dimension_semantics=tuple(sem), vmem_limit_bytes=vmem)


def _tile(dim, pref, mult=128):
    if dim <= pref:
        return dim
    t = (pref // mult) * mult
    while t >= mult:
        if dim % t == 0:
            return t
        t -= mult
    return dim


def _gelu(x):
    th = jnp.tanh(GELU_C * (x + 0.044715 * x * x * x))
    return 0.5 * x * (1.0 + th)


def _gelu_and_grad(x):
    x2 = x * x
    th = jnp.tanh(GELU_C * (x + 0.044715 * x2 * x))
    g = 0.5 * x * (1.0 + th)
    dg = 0.5 * (1.0 + th) + 0.5 * x * (1.0 - th * th) * GELU_C * (1.0 + 3.0 * 0.044715 * x2)
    return g, dg


def _dot(a, b, dims):
    return lax.dot_general(a.astype(BF16), b.astype(BF16), (dims, ((), ())), preferred_element_type=F32)


NN = ((1,), (0,))
NT = ((1,), (1,))
TN = ((0,), (0,))


def _matmul(a, b, *, dims, grid, a_spec, b_spec, o_spec, out_shape, acc_shape, res=None, res_spec=None,
            carry=None, name):
    ni, nj, nk = grid
    nres = 0 if res is None else 1
    ncar = 0 if carry is None else len(carry["arrays"])

    def body(*refs):
        a_ref, b_ref = refs[0], refs[1]
        r_ref = refs[2] if nres else None
        car_in = refs[2 + nres:2 + nres + ncar]
        o_ref = refs[2 + nres + ncar]
        car_out = refs[3 + nres + ncar:3 + nres + 2 * ncar]
        scratch = refs[3 + nres + 2 * ncar:]
        i, j, k = pl.program_id(0), pl.program_id(1), pl.program_id(2)

        if carry is not None:
            @pl.when((i == 0) & (j == 0) & (k == 0))
            def _():
                carry["start"](car_in, car_out, scratch[1:])

        def finish(acc):
            if r_ref is not None:
                acc = acc + r_ref[...].astype(F32)
            o_ref[...] = acc.astype(o_ref.dtype)

        if nk == 1:
            finish(_dot(a_ref[...], b_ref[...], dims))
        else:
            acc_ref = scratch[0]

            @pl.when(k == 0)
            def _():
                acc_ref[...] = _dot(a_ref[...], b_ref[...], dims)

            if nk > 2:
                @pl.when((k > 0) & (k < nk - 1))
                def _():
                    acc_ref[...] += _dot(a_ref[...], b_ref[...], dims)

            @pl.when(k == nk - 1)
            def _():
                finish(acc_ref[...] + _dot(a_ref[...], b_ref[...], dims))

        if carry is not None:
            @pl.when((i == ni - 1) & (j == nj - 1) & (k == nk - 1))
            def _():
                carry["finish"](car_in, car_out, scratch[1:])

    in_specs = [a_spec, b_spec] + ([res_spec] if nres else [])
    args = (a, b) + ((res,) if nres else ())
    out_specs, out_shapes = [o_spec], [out_shape]
    scratch_shapes = [pltpu.VMEM(acc_shape if nk > 1 else (8, 128), F32)]
    sem = ("parallel", "parallel", "arbitrary")
    if carry is not None:
        in_specs += [HBM] * ncar
        args += tuple(carry["arrays"])
        out_specs += [HBM] * ncar
        out_shapes += list(carry["out_structs"])
        scratch_shapes += carry["sems"]
        sem = ("arbitrary", "arbitrary", "arbitrary")
    outs = pl.pallas_call(
        body, grid=grid, in_specs=in_specs, out_specs=out_specs, out_shape=out_shapes,
        scratch_shapes=scratch_shapes, compiler_params=_cp(*sem), name=name,
    )(*args)
    return outs[0] if carry is None else (outs[0], list(outs[1:]))


def mm_nn(a, b, *, out_dtype, res=None, carry=None, name):
    M, K = a.shape
    N = b.shape[1]
    tm, tn, tk = _tile(M, 1024), _tile(N, 1024), _tile(K, 2048)
    return _matmul(
        a, b, dims=NN, grid=(M // tm, N // tn, K // tk),
        a_spec=pl.BlockSpec((tm, tk), lambda i, j, k: (i, k)),
        b_spec=pl.BlockSpec((tk, tn), lambda i, j, k: (k, j)),
        o_spec=pl.BlockSpec((tm, tn), lambda i, j, k: (i, j)),
        out_shape=jax.ShapeDtypeStruct((M, N), out_dtype), acc_shape=(tm, tn),
        res=res, res_spec=pl.BlockSpec((tm, tn), lambda i, j, k: (i, j)), carry=carry, name=name)


def mm_nt(a, b, *, out_dtype, res=None, carry=None, name):
    M, N = a.shape
    K = b.shape[0]
    tm, tn, tk = _tile(M, 1024), _tile(K, 1024), _tile(N, 2048)
    return _matmul(
        a, b, dims=NT, grid=(M // tm, K // tn, N // tk),
        a_spec=pl.BlockSpec((tm, tk), lambda i, j, k: (i, k)),
        b_spec=pl.BlockSpec((tn, tk), lambda i, j, k: (j, k)),
        o_spec=pl.BlockSpec((tm, tn), lambda i, j, k: (i, j)),
        out_shape=jax.ShapeDtypeStruct((M, K), out_dtype), acc_shape=(tm, tn),
        res=res, res_spec=pl.BlockSpec((tm, tn), lambda i, j, k: (i, j)), carry=carry, name=name)


def mm_tn(a, b, *, out_dtype, col_shards=None, shard_of_block=None, carry=None, name):
    T, K = a.shape
    N = b.shape[1]
    tt = _tile(T, 2048)
    tm = _tile(K, 1024)
    if col_shards is None:
        tn = _tile(N, 1024)
        o_spec = pl.BlockSpec((tm, tn), lambda i, j, k: (i, j))
        out_shape = jax.ShapeDtypeStruct((K, N), out_dtype)
    else:
        n = N // col_shards
        tn = _tile(n, 1536)
        per = n // tn
        sob = shard_of_block if shard_of_block is not None else (lambda s: s)
        o_spec = pl.BlockSpec((None, tm, tn), lambda i, j, k: (sob(j // per), i, j % per))
        out_shape = jax.ShapeDtypeStruct((col_shards, K, n), out_dtype)
    return _matmul(
        a, b, dims=TN, grid=(K // tm, N // tn, T // tt),
        a_spec=pl.BlockSpec((tt, tm), lambda i, j, k: (k, i)),
        b_spec=pl.BlockSpec((tt, tn), lambda i, j, k: (k, j)),
        o_spec=o_spec, out_shape=out_shape, acc_shape=(tm, tn), carry=carry, name=name)


def rmsnorm_fwd(x, g, *, name):
    T, D = x.shape
    tm = _tile(T, 512, 8)

    def body(x_ref, g_ref, o_ref):
        xv = x_ref[...]
        r = lax.rsqrt(jnp.mean(xv * xv, axis=-1, keepdims=True) + EPS)
        o_ref[...] = (xv * r * g_ref[...]).astype(o_ref.dtype)

    return pl.pallas_call(
        body, grid=(T // tm,),
        in_specs=[pl.BlockSpec((tm, D), lambda i: (i, 0)), pl.BlockSpec((1, D), lambda i: (0, 0))],
        out_specs=pl.BlockSpec((tm, D), lambda i: (i, 0)),
        out_shape=jax.ShapeDtypeStruct((T, D), BF16), compiler_params=_cp("parallel"), name=name)(x, g)


def rmsnorm_bwd(x, g, dh, dres, *, name):
    T, D = x.shape
    tm = _tile(T, 512, 8)

    def body(x_ref, g_ref, dh_ref, dres_ref, dx_ref, dg_ref):
        @pl.when(pl.program_id(0) == 0)
        def _():
            dg_ref[...] = jnp.zeros_like(dg_ref)

        xv = x_ref[...]
        dh_v = dh_ref[...].astype(F32)
        r = lax.rsqrt(jnp.mean(xv * xv, axis=-1, keepdims=True) + EPS)
        u = dh_v * g_ref[...]
        dot = jnp.mean(u * xv, axis=-1, keepdims=True)
        dx_ref[...] = dres_ref[...] + r * u - xv * (r * r * r * dot)
        dg_ref[...] += jnp.sum(dh_v * xv * r, axis=0, keepdims=True)

    row = pl.BlockSpec((tm, D), lambda i: (i, 0))
    vec = pl.BlockSpec((1, D), lambda i: (0, 0))
    return pl.pallas_call(
        body, grid=(T // tm,), in_specs=[row, vec, row, row], out_specs=[row, vec],
        out_shape=[jax.ShapeDtypeStruct((T, D), F32), jax.ShapeDtypeStruct((1, D), F32)],
        compiler_params=_cp("arbitrary"), name=name)(x, g, dh, dres)


def loss_head(y, target, *, name):
    T, D = y.shape
    tm = _tile(T, 512, 8)

    def body(y_ref, t_ref, l_ref, dy_ref):
        @pl.when(pl.program_id(0) == 0)
        def _():
            l_ref[...] = jnp.zeros_like(l_ref)

        e = y_ref[...] - t_ref[...]
        dy_ref[...] = e * (1.0 / D)
        l_ref[...] += jnp.sum(e * e, axis=0, keepdims=True) * (0.5 / D)

    row = pl.BlockSpec((tm, D), lambda i: (i, 0))
    vec = pl.BlockSpec((1, D), lambda i: (0, 0))
    return pl.pallas_call(
        body, grid=(T // tm,), in_specs=[row, row], out_specs=[vec, row],
        out_shape=[jax.ShapeDtypeStruct((1, D), F32), jax.ShapeDtypeStruct((T, D), F32)],
        compiler_params=_cp("arbitrary"), name=name)(y, target)


NEG = -1e30


def _kv_heads_per_step(hkv):
    return 4 if hkv % 4 == 0 else 2


def _headnorm(x):
    r = lax.rsqrt(jnp.mean(x * x, axis=-1, keepdims=True) + EPS)
    return x * r, r


def _attn_mask(n):
    qi = lax.broadcasted_iota(jnp.int32, (BLOCK, 2 * BLOCK), 0)
    kj = lax.broadcasted_iota(jnp.int32, (BLOCK, 2 * BLOCK), 1)
    rel = qi + BLOCK - kj
    m = (rel >= 0) & (rel < BLOCK) & ((kj >= BLOCK) | (n > 0))
    return jnp.concatenate([m] * GROUP, axis=0)


def _attn_prep(q_ref, kp_ref, kc_ref, qg, kg, hh):
    lo = hh * HEAD_DIM
    k_raw = jnp.concatenate([kp_ref[:, lo:lo + HEAD_DIM], kc_ref[:, lo:lo + HEAD_DIM]], axis=0)
    kn, kr = _headnorm(k_raw)
    khat = kn * kg
    q_raw, qn, qr = [], [], []
    for g in range(GROUP):
        c0 = (hh * GROUP + g) * HEAD_DIM
        x = q_ref[:, c0:c0 + HEAD_DIM]
        xn, r = _headnorm(x)
        q_raw.append(x)
        qn.append(xn)
        qr.append(r)
    q4 = jnp.concatenate([xn * qg for xn in qn], axis=0)
    return q4, khat, (q_raw, qn, qr), (k_raw, kn, kr)


def _attn_softmax(s, sink_ref, mask4, head0):
    s = s * (HEAD_DIM ** -0.5)
    sink = jnp.concatenate([jnp.full((BLOCK, 1), sink_ref[0, head0 + g], F32) for g in range(GROUP)], axis=0)
    m = jnp.maximum(jnp.max(jnp.where(mask4, s, NEG), axis=-1, keepdims=True), sink)
    e = jnp.where(mask4, jnp.exp(s - m), 0.0)
    es = jnp.exp(sink - m)
    inv = 1.0 / (jnp.sum(e, axis=-1, keepdims=True) + es)
    return e * inv, es * inv


def _window(p_ref, c_ref, hh):
    lo = hh * HEAD_DIM
    return jnp.concatenate([p_ref[:, lo:lo + HEAD_DIM], c_ref[:, lo:lo + HEAD_DIM]], axis=0)


def attn_fwd(qkv, q_gain, k_gain, sinks, *, carry=None, name):
    ncar = 0 if carry is None else len(carry["arrays"])
    T, W = qkv.shape
    hq = W // HEAD_DIM * GROUP // (GROUP + 2)
    dq = hq * HEAD_DIM
    PAIR = _kv_heads_per_step(hq // GROUP)
    QW, KW = PAIR * GROUP * HEAD_DIM, PAIR * HEAD_DIM
    npair = hq // (GROUP * PAIR)
    nb = T // BLOCK
    k0 = dq // KW
    v0 = k0 + npair

    def body(q_ref, kp_ref, kc_ref, vp_ref, vc_ref, qg_ref, kg_ref, sink_ref, *rest):
        car_in, o_ref, car_out, sems = rest[:ncar], rest[ncar], rest[ncar + 1:2 * ncar + 1], rest[2 * ncar + 1:]
        p, n = pl.program_id(0), pl.program_id(1)
        if carry is not None:
            @pl.when((p == 0) & (n == 0))
            def _():
                carry["start"](car_in, car_out, sems)

            @pl.when((p == npair - 1) & (n == nb - 1))
            def _():
                carry["finish"](car_in, car_out, sems)

        mask4 = _attn_mask(n)
        prep = [_attn_prep(q_ref, kp_ref, kc_ref, qg_ref[...], kg_ref[...], hh) for hh in range(PAIR)]
        scores = [_dot(q4, khat, NT) for q4, khat, _, _ in prep]
        probs = [_attn_softmax(s, sink_ref, mask4, (p * PAIR + hh) * GROUP)[0] for hh, s in enumerate(scores)]
        outs = []
        for hh in range(PAIR):
            o4 = _dot(probs[hh], _window(vp_ref, vc_ref, hh), NN)
            outs += [o4[g * BLOCK:(g + 1) * BLOCK] for g in range(GROUP)]
        o_ref[...] = jnp.concatenate(outs, axis=1).astype(o_ref.dtype)

    prev = lambda n: jnp.maximum(n - 1, 0)
    vec = pl.BlockSpec((1, HEAD_DIM), lambda p, n: (0, 0))
    car = carry if carry is not None else dict(arrays=[], out_structs=[], sems=[])
    outs = pl.pallas_call(
        body, grid=(npair, nb),
        in_specs=[pl.BlockSpec((BLOCK, QW), lambda p, n: (n, p)),
                  pl.BlockSpec((BLOCK, KW), lambda p, n: (prev(n), k0 + p)),
                  pl.BlockSpec((BLOCK, KW), lambda p, n: (n, k0 + p)),
                  pl.BlockSpec((BLOCK, KW), lambda p, n: (prev(n), v0 + p)),
                  pl.BlockSpec((BLOCK, KW), lambda p, n: (n, v0 + p)),
                  vec, vec, pl.BlockSpec(memory_space=pltpu.SMEM)] + [HBM] * ncar,
        out_specs=[pl.BlockSpec((BLOCK, QW), lambda p, n: (n, p))] + [HBM] * ncar,
        out_shape=[jax.ShapeDtypeStruct((T, dq), BF16)] + list(car["out_structs"]),
        scratch_shapes=list(car["sems"]),
        compiler_params=_cp(*(("parallel", "parallel") if carry is None else ("arbitrary", "arbitrary"))),
        name=name)(qkv, qkv, qkv, qkv, qkv, q_gain, k_gain, sinks, *car["arrays"])
    return outs[0] if carry is None else (outs[0], list(outs[1:]))


def attn_bwd(qkv, d_out, q_gain, k_gain, sinks, *, name):
    T, W = qkv.shape
    hq = W // HEAD_DIM * GROUP // (GROUP + 2)
    dq_w = hq * HEAD_DIM
    PAIR = _kv_heads_per_step(hq // GROUP)
    QW, KW = PAIR * GROUP * HEAD_DIM, PAIR * HEAD_DIM
    npair = hq // (GROUP * PAIR)
    nb = T // BLOCK
    k0 = dq_w // KW
    v0 = k0 + npair

    def body(q_ref, kp_ref, kc_ref, vp_ref, vc_ref, do_ref, qg_ref, kg_ref, sink_ref,
             dq_ref, dk_ref, dv_ref, dqg_ref, dkg_ref, dsink_ref, dk_carry, dv_carry):
        p, i = pl.program_id(0), pl.program_id(1)
        n = nb - 1 - i

        @pl.when(i == 0)
        def _():
            dk_carry[...] = jnp.zeros_like(dk_carry)
            dv_carry[...] = jnp.zeros_like(dv_carry)
            dqg_ref[...] = jnp.zeros_like(dqg_ref)
            dkg_ref[...] = jnp.zeros_like(dkg_ref)
            dsink_ref[...] = jnp.zeros_like(dsink_ref)

        qg, kg = qg_ref[...], kg_ref[...]
        mask4 = _attn_mask(n)
        dq_parts, dk_parts, dv_parts, dsink_rows = [], [], [], []
        dqg_acc = jnp.zeros((1, HEAD_DIM), F32)
        dkg_acc = jnp.zeros((1, HEAD_DIM), F32)
        prep = [_attn_prep(q_ref, kp_ref, kc_ref, qg, kg, hh) for hh in range(PAIR)]
        do4s = [jnp.concatenate(
            [do_ref[:, (hh * GROUP + g) * HEAD_DIM:(hh * GROUP + g + 1) * HEAD_DIM] for g in range(GROUP)],
            axis=0) for hh in range(PAIR)]
        scores = [_dot(q4, khat, NT) for q4, khat, _, _ in prep]
        dps = [_dot(do4s[hh], _window(vp_ref, vc_ref, hh), NT) for hh in range(PAIR)]
        soft = [_attn_softmax(s, sink_ref, mask4, (p * PAIR + hh) * GROUP) for hh, s in enumerate(scores)]
        dss = []
        for hh in range(PAIR):
            pr, psink = soft[hh]
            delta = jnp.sum(pr * dps[hh], axis=-1, keepdims=True)
            dss.append(pr * (dps[hh] - delta) * (HEAD_DIM ** -0.5))
            dsk = -psink * delta
            for g in range(GROUP):
                tot = jnp.sum(dsk[g * BLOCK:(g + 1) * BLOCK], axis=0, keepdims=True)
                dsink_rows.append(jnp.broadcast_to(tot, (1, 128)))
        dq4s = [_dot(dss[hh], prep[hh][1], NN) for hh in range(PAIR)]
        dkhats = [_dot(dss[hh], prep[hh][0], TN) for hh in range(PAIR)]
        dvs = [_dot(soft[hh][0], do4s[hh], TN) for hh in range(PAIR)]
        for hh in range(PAIR):
            _, _, (q_raw, qn, qr), (k_raw, kn, kr) = prep[hh]
            dq4, dkhat_win, dv_win = dq4s[hh], dkhats[hh], dvs[hh]
            for g in range(GROUP):
                dqh = dq4[g * BLOCK:(g + 1) * BLOCK]
                dqg_acc += jnp.sum(dqh * qn[g], axis=0, keepdims=True)
                dqn = dqh * qg
                r = qr[g]
                dq_parts.append(r * dqn - q_raw[g] * (r * r * r * jnp.mean(dqn * q_raw[g], axis=-1, keepdims=True)))
            dkh = dkhat_win[BLOCK:] + dk_carry[hh]
            dk_carry[hh] = dkhat_win[:BLOCK]
            dkg_acc += jnp.sum(dkh * kn[BLOCK:], axis=0, keepdims=True)
            dkn = dkh * kg
            r = kr[BLOCK:]
            kc_raw = k_raw[BLOCK:]
            dk_parts.append(r * dkn - kc_raw * (r * r * r * jnp.mean(dkn * kc_raw, axis=-1, keepdims=True)))
            dv_parts.append(dv_win[BLOCK:] + dv_carry[hh])
            dv_carry[hh] = dv_win[:BLOCK]
        dq_ref[...] = jnp.concatenate(dq_parts, axis=1).astype(dq_ref.dtype)
        dk_ref[...] = jnp.concatenate(dk_parts, axis=1).astype(dk_ref.dtype)
        dv_ref[...] = jnp.concatenate(dv_parts, axis=1).astype(dv_ref.dtype)
        dqg_ref[...] += dqg_acc
        dkg_ref[...] += dkg_acc
        dsink_ref[...] += jnp.concatenate(dsink_rows, axis=0)

    rev = lambda i: nb - 1 - i
    prev = lambda i: jnp.maximum(nb - 2 - i, 0)
    vec = pl.BlockSpec((1, HEAD_DIM), lambda p, i: (0, 0))
    acc64 = pl.BlockSpec((None, 1, HEAD_DIM), lambda p, i: (p, 0, 0))
    dq, dk, dv, dqg, dkg, dsink = pl.pallas_call(
        body, grid=(npair, nb),
        in_specs=[pl.BlockSpec((BLOCK, QW), lambda p, i: (rev(i), p)),
                  pl.BlockSpec((BLOCK, KW), lambda p, i: (prev(i), k0 + p)),
                  pl.BlockSpec((BLOCK, KW), lambda p, i: (rev(i), k0 + p)),
                  pl.BlockSpec((BLOCK, KW), lambda p, i: (prev(i), v0 + p)),
                  pl.BlockSpec((BLOCK, KW), lambda p, i: (rev(i), v0 + p)),
                  pl.BlockSpec((BLOCK, QW), lambda p, i: (rev(i), p)),
                  vec, vec, pl.BlockSpec(memory_space=pltpu.SMEM)],
        out_specs=[pl.BlockSpec((BLOCK, QW), lambda p, i: (rev(i), p)),
                   pl.BlockSpec((BLOCK, KW), lambda p, i: (rev(i), p)),
                   pl.BlockSpec((BLOCK, KW), lambda p, i: (rev(i), p)),
                   acc64, acc64,
                   pl.BlockSpec((None, PAIR * GROUP, 128), lambda p, i: (p, 0, 0))],
        out_shape=[jax.ShapeDtypeStruct((T, dq_w), BF16),
                   jax.ShapeDtypeStruct((T, npair * KW), BF16),
                   jax.ShapeDtypeStruct((T, npair * KW), BF16),
                   jax.ShapeDtypeStruct((npair, 1, HEAD_DIM), F32),
                   jax.ShapeDtypeStruct((npair, 1, HEAD_DIM), F32),
                   jax.ShapeDtypeStruct((npair, PAIR * GROUP, 128), F32)],
        scratch_shapes=[pltpu.VMEM((PAIR, BLOCK, HEAD_DIM), F32), pltpu.VMEM((PAIR, BLOCK, HEAD_DIM), F32)],
        compiler_params=_cp("parallel", "arbitrary"), name=name,
    )(qkv, qkv, qkv, qkv, qkv, d_out, q_gain, k_gain, sinks)
    return dq, dk, dv, jnp.sum(dqg, axis=0), jnp.sum(dkg, axis=0), dsink[:, :, 0].reshape(-1)


def _softplus_neg(lam):
    return jnp.maximum(-lam, 0.0) + jnp.log1p(jnp.exp(-jnp.abs(lam)))


def _conv_taps(xc, cw_ref, tt, lead):
    K = cw_ref.shape[0]
    acc = None
    for k in range(K):
        off = lead - (K - 1 - k)
        term = cw_ref[k:k + 1, :] * xc[off:off + tt]
        acc = term if acc is None else acc + term
    return acc


def _bcast_row(x, row):
    return jnp.broadcast_to(x[row:row + 1, :], x.shape)


def _rows8(x):
    return x.reshape(x.shape[0] // 8, 8, x.shape[1])


def _tapmul(x, w8):
    return (_rows8(x) * w8[None]).reshape(x.shape)


def _colsum8(x):
    return jnp.sum(_rows8(x), axis=0)


def _shift_down(x, s, prev8):
    r = pltpu.roll(x, s, 0)
    rowid = lax.broadcasted_iota(jnp.int32, (8, 1), 0)
    head = jnp.where(rowid >= s, r[0:8], pltpu.roll(prev8, s, 0))
    return jnp.concatenate([head, r[8:]], axis=0)


def _shift_up(x, s, next8):
    R = x.shape[0]
    r = pltpu.roll(x, R - s, 0)
    rowid = lax.broadcasted_iota(jnp.int32, (8, 1), 0)
    tail = jnp.where(rowid < 8 - s, r[R - 8:], pltpu.roll(next8, 8 - s, 0))
    return jnp.concatenate([r[:R - 8], tail], axis=0)


def _rep8(v):
    return jnp.broadcast_to(v[..., None, :], v.shape[:-1] + (8, v.shape[-1]))


def rec_fwd(z, cw, cb, wa, ba, wi, bi, lam, *, name):
    T, C2 = z.shape
    C = C2 // 2
    nblk, bd, _ = wa.shape
    tt = _tile(T, 128, HALO)
    ng = tt // 8

    def body(x_ref, y_ref, halo_ref, cw_ref, cb_ref, wa_ref, ba_ref, wi_ref, bi_ref, lam_ref,
             xb_ref, r_ref, i_ref, a_ref, h_ref, hp_ref, hg_ref, carry, u_scr):
        step = pl.program_id(0)

        @pl.when(step == 0)
        def _():
            carry[...] = jnp.zeros_like(carry)

        halo = jnp.where(step > 0, halo_ref[...], 0.0)
        xc = jnp.concatenate([halo, x_ref[...]], axis=0)
        xb = cb_ref[...] + _conv_taps(xc, cw_ref, tt, HALO)
        xb_ref[...] = xb
        pa, pi = [], []
        for b in range(nblk):
            xs = xb[:, b * bd:(b + 1) * bd]
            pa.append(_dot(xs, wa_ref[b], NN))
            pi.append(_dot(xs, wi_ref[b], NN))
        r = jax.nn.sigmoid(jnp.concatenate(pa, axis=1) + ba_ref[...])
        ig = jax.nn.sigmoid(jnp.concatenate(pi, axis=1) + bi_ref[...])
        r_ref[...] = r
        i_ref[...] = ig
        nl = LRU_C * r * _softplus_neg(lam_ref[...])
        a_ref[...] = jnp.exp(-nl)
        th = jnp.tanh(nl)
        u_scr[...] = jnp.sqrt(2.0 * th / (1.0 + th)) * (ig * xb)

        rowid = lax.broadcasted_iota(jnp.int32, (8, C), 0)

        def group(gi, hc):
            r0 = pl.multiple_of(gi * 8, 8)
            a8 = a_ref[pl.ds(r0, 8), :]
            u8 = u_scr[pl.ds(r0, 8), :]
            for d in (1, 2, 4):
                a_sh = jnp.where(rowid >= d, pltpu.roll(a8, d, 0), 1.0)
                u_sh = jnp.where(rowid >= d, pltpu.roll(u8, d, 0), 0.0)
                u8 = a8 * u_sh + u8
                a8 = a8 * a_sh
            h8 = u8 + a8 * hc
            h_ref[pl.ds(r0, 8), :] = h8
            hp_ref[pl.ds(r0, 8), :] = jnp.where(rowid >= 1, pltpu.roll(h8, 1, 0), hc)
            return _bcast_row(h8, 7)

        carry[...] = lax.fori_loop(0, ng, group, carry[...])
        hg_ref[...] = (h_ref[...] * _gelu(y_ref[...])).astype(hg_ref.dtype)

    row = lambda c: pl.BlockSpec((tt, C), lambda i, c=c: (i, c))
    vec = pl.BlockSpec((1, C), lambda i: (0, 0))
    full = lambda shp: pl.BlockSpec(shp, lambda i, n=len(shp): (0,) * n)
    per = tt // HALO
    outs = pl.pallas_call(
        body, grid=(T // tt,),
        in_specs=[row(0), row(1), pl.BlockSpec((HALO, C), lambda i: (jnp.maximum(i * per - 1, 0), 0)),
                  full(cw.shape), vec, full(wa.shape), vec, full(wi.shape), vec, vec],
        out_specs=[row(0)] * 7,
        out_shape=[jax.ShapeDtypeStruct((T, C), F32)] * 6 + [jax.ShapeDtypeStruct((T, C), BF16)],
        scratch_shapes=[pltpu.VMEM((8, C), F32), pltpu.VMEM((tt, C), F32)],
        compiler_params=_cp("arbitrary"), name=name,
    )(z, z, z, cw, cb, wa, ba, wi, bi, lam)
    return outs


def rec_bwd_scan(dhg, h, a, z, *, name):
    T, C = h.shape
    tt = _tile(T, 256, HALO)
    ng = tt // 8
    nb = T // tt

    def body(dhg_ref, h_ref, a_ref, y_ref, dy_ref, yb_ref, ycarry, acarry, g_scr):
        step = pl.program_id(0)

        @pl.when(step == 0)
        def _():
            ycarry[...] = jnp.zeros_like(ycarry)
            acarry[...] = jnp.zeros_like(acarry)

        gate, dgate = _gelu_and_grad(y_ref[...])
        dhg_v = dhg_ref[...].astype(F32)
        dy_ref[...] = (dhg_v * h_ref[...] * dgate).astype(dy_ref.dtype)
        g_scr[...] = dhg_v * gate
        rowid = lax.broadcasted_iota(jnp.int32, (8, C), 0)

        def group(j, c):
            yc, ac = c
            r0 = pl.multiple_of((ng - 1 - j) * 8, 8)
            a8 = a_ref[pl.ds(r0, 8), :]
            y8 = g_scr[pl.ds(r0, 8), :]
            b8 = jnp.where(rowid < 7, pltpu.roll(a8, 7, 0), ac)
            for d in (1, 2, 4):
                y_sh = jnp.where(rowid < 8 - d, pltpu.roll(y8, 8 - d, 0), 0.0)
                b_sh = jnp.where(rowid < 8 - d, pltpu.roll(b8, 8 - d, 0), 1.0)
                y8 = y8 + b8 * y_sh
                b8 = b8 * b_sh
            y8 = y8 + b8 * yc
            yb_ref[pl.ds(r0, 8), :] = y8
            return _bcast_row(y8, 0), _bcast_row(a8, 0)

        yc, ac = lax.fori_loop(0, ng, group, (ycarry[...], acarry[...]))
        ycarry[...] = yc
        acarry[...] = ac

    rev = lambda c: pl.BlockSpec((tt, C), lambda i, c=c: (nb - 1 - i, c))
    return pl.pallas_call(
        body, grid=(nb,), in_specs=[rev(0), rev(0), rev(0), rev(1)], out_specs=[rev(0), rev(0)],
        out_shape=[jax.ShapeDtypeStruct((T, C), BF16), jax.ShapeDtypeStruct((T, C), F32)],
        scratch_shapes=[pltpu.VMEM((8, C), F32), pltpu.VMEM((8, C), F32), pltpu.VMEM((tt, C), F32)],
        compiler_params=_cp("arbitrary"), name=name,
    )(dhg, h, a, z)


def rec_bwd_gates(ybar, hprev, a, r, ig, xb, lam, wa, wi, *, name):
    T, C = xb.shape
    nblk, bd, _ = wa.shape
    tt = _tile(T, 256, 8)
    nb = T // tt

    def body(y_ref, hp_ref, a_ref, r_ref, i_ref, xb_ref, lam_ref, wa_ref, wi_ref,
             dxb_ref, dwa_ref, dwi_ref, dba_ref, dbi_ref, dlam_ref):
        step = pl.program_id(0)

        @pl.when(step == 0)
        def _():
            for ref in (dwa_ref, dwi_ref, dba_ref, dbi_ref, dlam_ref):
                ref[...] = jnp.zeros_like(ref)

        y, av, rv, iv, xv = y_ref[...], a_ref[...], r_ref[...], i_ref[...], xb_ref[...]
        sp = _softplus_neg(lam_ref[...])
        th = jnp.tanh(LRU_C * rv * sp)
        s = jnp.sqrt(2.0 * th / (1.0 + th))
        d_nl = -(y * hp_ref[...] * av) + (y * iv * xv) * (av * av) / s
        dlam_ref[...] += jnp.sum(d_nl * rv, axis=0, keepdims=True) * LRU_C
        dr = d_nl * (LRU_C * sp)
        di = y * s * xv
        dpa = dr * rv * (1.0 - rv)
        dpi = di * iv * (1.0 - iv)
        dba_ref[...] += jnp.sum(dpa, axis=0, keepdims=True)
        dbi_ref[...] += jnp.sum(dpi, axis=0, keepdims=True)
        parts = []
        for b in range(nblk):
            sl = slice(b * bd, (b + 1) * bd)
            xs, da_b, di_b = xv[:, sl], dpa[:, sl], dpi[:, sl]
            dwa_ref[b] += _dot(xs, da_b, TN)
            dwi_ref[b] += _dot(xs, di_b, TN)
            parts.append(_dot(da_b, wa_ref[b], NT) + _dot(di_b, wi_ref[b], NT))
        dxb_ref[...] = y * s * iv + jnp.concatenate(parts, axis=1)

        @pl.when(step == nb - 1)
        def _():
            dlam_ref[...] = dlam_ref[...] * (-jax.nn.sigmoid(-lam_ref[...]))

    row = pl.BlockSpec((tt, C), lambda i: (i, 0))
    vec = pl.BlockSpec((1, C), lambda i: (0, 0))
    wsp = pl.BlockSpec(wa.shape, lambda i: (0, 0, 0))
    return pl.pallas_call(
        body, grid=(nb,), in_specs=[row] * 6 + [vec, wsp, wsp],
        out_specs=[row, wsp, wsp, vec, vec, vec],
        out_shape=[jax.ShapeDtypeStruct((T, C), F32), jax.ShapeDtypeStruct(wa.shape, F32),
                   jax.ShapeDtypeStruct(wa.shape, F32)] + [jax.ShapeDtypeStruct((1, C), F32)] * 3,
        compiler_params=_cp("arbitrary"), name=name,
    )(ybar, hprev, a, r, ig, xb, lam, wa, wi)


def conv_bwd(d, x0, cw, *, name):
    T, C = d.shape
    K = cw.shape[0]
    tt = _tile(T, 256, HALO)
    per = tt // HALO
    nb = T // tt

    def body(d_ref, dn_ref, x_ref, xp_ref, cw_ref, dx_ref, dcw_ref, dcb_ref):
        step = pl.program_id(0)

        @pl.when(step == 0)
        def _():
            dcw_ref[...] = jnp.zeros_like(dcw_ref)
            dcb_ref[...] = jnp.zeros_like(dcb_ref)

        dv = d_ref[...].astype(F32)
        dc = jnp.concatenate([dv, jnp.where(step < nb - 1, dn_ref[...].astype(F32), 0.0)], axis=0)
        xc = jnp.concatenate([jnp.where(step > 0, xp_ref[...].astype(F32), 0.0), x_ref[...].astype(F32)], axis=0)
        acc = None
        rows = []
        for k in range(K):
            sh = K - 1 - k
            term = cw_ref[k:k + 1, :] * dc[sh:sh + tt]
            acc = term if acc is None else acc + term
            rows.append(jnp.sum(dv * xc[HALO - sh:HALO - sh + tt], axis=0, keepdims=True))
        dx_ref[...] = acc.astype(dx_ref.dtype)
        dcw_ref[...] += jnp.concatenate(rows, axis=0)
        dcb_ref[...] += jnp.sum(dv, axis=0, keepdims=True)

    row = pl.BlockSpec((tt, C), lambda i: (i, 0))
    return pl.pallas_call(
        body, grid=(nb,),
        in_specs=[row, pl.BlockSpec((HALO, C), lambda i: (jnp.minimum((i + 1) * per, T // HALO - 1), 0)),
                  row, pl.BlockSpec((HALO, C), lambda i: (jnp.maximum(i * per - 1, 0), 0)),
                  pl.BlockSpec((K, C), lambda i: (0, 0))],
        out_specs=[row, pl.BlockSpec((K, C), lambda i: (0, 0)), pl.BlockSpec((1, C), lambda i: (0, 0))],
        out_shape=[jax.ShapeDtypeStruct((T, C), BF16), jax.ShapeDtypeStruct((K, C), F32),
                   jax.ShapeDtypeStruct((1, C), F32)],
        compiler_params=_cp("arbitrary"), name=name,
    )(d, d, x0, x0, cw)


def ffn_act_fwd(u0, cw, cb, *, n, name):
    T, W = u0.shape
    G = W // (2 * n)
    tt = _tile(T, 256, HALO)
    per = tt // HALO

    K = cw.shape[0]

    def body(u_ref, up_ref, cw_ref, cb_ref, a_ref, uo_ref):
        step = pl.program_id(1)
        x = u_ref[...].astype(F32)
        prev8 = jnp.where(step > 0, up_ref[...].astype(F32)[HALO - 8:], 0.0)
        acc = _tapmul(x, cw_ref[K - 1])
        for s in range(1, K):
            acc = acc + _tapmul(_shift_down(x, s, prev8), cw_ref[K - 1 - s])
        u = (_rows8(acc) + cb_ref[...][None]).reshape(tt, 2 * n)
        uo_ref[...] = u.astype(uo_ref.dtype)
        a_ref[...] = (_gelu(u[:, :n]) * u[:, n:]).astype(a_ref.dtype)

    return pl.pallas_call(
        body, grid=(G, T // tt),
        in_specs=[pl.BlockSpec((tt, 2 * n), lambda j, i: (i, j)),
                  pl.BlockSpec((HALO, 2 * n), lambda j, i: (jnp.maximum(i * per - 1, 0), j)),
                  pl.BlockSpec((K, 8, 2 * n), lambda j, i: (0, 0, j)),
                  pl.BlockSpec((8, 2 * n), lambda j, i: (0, j))],
        out_specs=[pl.BlockSpec((tt, n), lambda j, i: (i, j)), pl.BlockSpec((tt, 2 * n), lambda j, i: (i, j))],
        out_shape=[jax.ShapeDtypeStruct((T, G * n), BF16), jax.ShapeDtypeStruct((T, W), BF16)],
        compiler_params=_cp("parallel", "parallel"), name=name)(u0, u0, _rep8(cw), _rep8(cb[0]))


def ffn_act_bwd(u0, u, da, cw, *, n, name):
    T, W = u0.shape
    G = W // (2 * n)
    K = cw.shape[0]
    tt = _tile(T, 128, HALO)
    per = tt // HALO
    nb = T // tt

    def body(x_ref, u_ref, un_ref, da_ref, dan_ref, cw_ref, du_ref, dcw_ref, dcb_ref, dcw_acc, dcb_acc):
        step = pl.program_id(1)

        @pl.when(step == 0)
        def _():
            dcw_acc[...] = jnp.zeros_like(dcw_acc)
            dcb_acc[...] = jnp.zeros_like(dcb_acc)

        def d_conv_out(uf, daf):
            gl, dgl = _gelu_and_grad(uf[:, :n])
            return jnp.concatenate([daf * uf[:, n:] * dgl, daf * gl], axis=1)

        du = d_conv_out(u_ref[...].astype(F32), da_ref[...].astype(F32))
        next8 = d_conv_out(un_ref[...].astype(F32)[0:8],
                           jnp.where(step < nb - 1, dan_ref[...].astype(F32)[0:8], 0.0))
        xt = x_ref[...].astype(F32)
        acc = _tapmul(du, cw_ref[K - 1])
        dcw_acc[K - 1] += _colsum8(du * xt)
        dcb_acc[...] += _colsum8(du)
        for sh in range(1, K):
            dsh = _shift_up(du, sh, next8)
            acc = acc + _tapmul(dsh, cw_ref[K - 1 - sh])
            dcw_acc[K - 1 - sh] += _colsum8(dsh * xt)
        du_ref[...] = acc.astype(du_ref.dtype)

        @pl.when(step == nb - 1)
        def _():
            dcw_ref[...] = jnp.sum(dcw_acc[...], axis=1)
            dcb_ref[...] = jnp.sum(dcb_acc[...], axis=0, keepdims=True)

    last = T // HALO - 1
    nxt = lambda j, i: (jnp.minimum((i + 1) * per, last), j)
    return pl.pallas_call(
        body, grid=(G, nb),
        in_specs=[pl.BlockSpec((tt, 2 * n), lambda j, i: (i, j)),
                  pl.BlockSpec((tt, 2 * n), lambda j, i: (i, j)),
                  pl.BlockSpec((HALO, 2 * n), nxt),
                  pl.BlockSpec((tt, n), lambda j, i: (i, j)),
                  pl.BlockSpec((HALO, n), nxt),
                  pl.BlockSpec((K, 8, 2 * n), lambda j, i: (0, 0, j))],
        out_specs=[pl.BlockSpec((tt, 2 * n), lambda j, i: (i, j)),
                   pl.BlockSpec((K, 2 * n), lambda j, i: (0, j)),
                   pl.BlockSpec((1, 2 * n), lambda j, i: (0, j))],
        out_shape=[jax.ShapeDtypeStruct((T, W), BF16), jax.ShapeDtypeStruct((K, W), F32),
                   jax.ShapeDtypeStruct((1, W), F32)],
        scratch_shapes=[pltpu.VMEM((K, 8, 2 * n), F32), pltpu.VMEM((8, 2 * n), F32)],
        compiler_params=_cp("parallel", "arbitrary"), name=name,
    )(u0, u, u, da, da, _rep8(cw))


def _ffn_shard_of_block(q):
    return (q % 2) * (N_DEV // 2) + q // 2


def _ffn_block_of_shard(s):
    return (s % (N_DEV // 2)) * 2 + s // (N_DEV // 2)


def _group_cols(v):
    lead = v.shape[:-1]
    n = v.shape[-1] // N_DEV
    return jnp.swapaxes(v.reshape(lead + (2, N_DEV // 2, n)), -3, -2).reshape(v.shape)


def _ungroup_cols(v):
    lead = v.shape[:-1]
    n = v.shape[-1] // N_DEV
    return jnp.swapaxes(v.reshape(lead + (N_DEV // 2, 2, n)), -3, -2).reshape(v.shape)


def local_step(x, target, w, plan=None):
    depth = w["mix_norm"].shape[0]
    n_up = w["ffn_conv_w"].shape[-1] // N_DEV
    g = {k: [None] * (w[k].shape[0] if hasattr(w[k], "shape") else len(w[k])) for k in w}

    def run(fn, stage, l, slot, *args, **kw):
        carry = None if plan is None else plan.carry(stage, l, slot, g)
        if carry is None:
            return fn(*args, **kw)
        out, extra = fn(*args, carry=carry, **kw)
        plan.done(stage, l, slot, extra)
        return out

    saved = []
    for l in range(depth):
        j = l // 2
        h = rmsnorm_fwd(x, w["mix_norm"][l:l + 1], name="mix_norm_fwd")
        if l % 2 == 0:
            qkv = run(mm_nn, "fwd", l, "in", h, w["attn_w_qkv"][j], out_dtype=F32, name="qkv_proj")
            ao = run(attn_fwd, "fwd", l, "attn", qkv, w["attn_q_gain"][j:j + 1], w["attn_k_gain"][j:j + 1],
                     w["attn_sinks"][j:j + 1], name="attn_fwd")
            x1 = run(mm_nn, "fwd", l, "out", ao, w["attn_w_o"][j], out_dtype=F32, res=x, name="attn_out_proj")
            mix = (qkv, ao)
        else:
            z = run(mm_nn, "fwd", l, "in", h, w["rec_w_in"][j], out_dtype=F32, name="rec_in_proj")
            xb, r, ig, a, hs, hprev, hg = rec_fwd(
                z, w["rec_conv_w"][j], w["rec_conv_b"][j:j + 1], w["rec_w_a"][j], w["rec_b_a"][j:j + 1],
                w["rec_w_i"][j], w["rec_b_i"][j:j + 1], w["rec_lambda"][j:j + 1], name="rec_fwd")
            x1 = run(mm_nn, "fwd", l, "out", hg, w["rec_w_out"][j], out_dtype=F32, res=x, name="rec_out_proj")
            mix = (z, xb, r, ig, a, hs, hprev, hg)
        h2 = rmsnorm_fwd(x1, w["ffn_norm"][l:l + 1], name="ffn_norm_fwd")
        u0 = run(mm_nn, "fwd", l, "up", h2, w["ffn_w_up"][l], out_dtype=BF16, name="ffn_up_proj")
        act, u = ffn_act_fwd(u0, w["ffn_conv_w"][l], w["ffn_conv_b"][l:l + 1], n=n_up, name="ffn_act_fwd")
        x2 = run(mm_nn, "fwd", l, "down", act, w["ffn_w_down"][l], out_dtype=F32, res=x1, name="ffn_down_proj")
        saved.append((x, h, mix, x1, h2, u0, u, act))
        x = x2

    loss_vec, dx = loss_head(x, target, name="loss_head")

    for l in reversed(range(depth)):
        j = l // 2
        x0, h, mix, x1, h2, u0, u, act = saved[l]
        dact = mm_nt(dx, w["ffn_w_down"][l], out_dtype=BF16, name="ffn_down_dx")
        g["ffn_w_down"][l] = mm_tn(act, dx, out_dtype=BF16, name="ffn_down_dw")
        du0, dcw, dcb = ffn_act_bwd(u0, u, dact, w["ffn_conv_w"][l], n=n_up, name="ffn_act_bwd")
        g["ffn_conv_w"][l], g["ffn_conv_b"][l] = _ungroup_cols(dcw), _ungroup_cols(dcb)[0]
        g["ffn_w_up"][l] = run(mm_tn, "bwd", l, "up_dw", h2, du0, out_dtype=BF16, col_shards=N_DEV, shard_of_block=_ffn_shard_of_block,
                                 name="ffn_up_dw")
        dh2 = run(mm_nt, "bwd", l, "up_dx", du0, w["ffn_w_up"][l], out_dtype=F32, name="ffn_up_dx")
        dx1, dgf = rmsnorm_bwd(x1, w["ffn_norm"][l:l + 1], dh2, dx, name="ffn_norm_bwd")
        g["ffn_norm"][l] = dgf[0]
        if l % 2 == 0:
            qkv, ao = mix
            dao = mm_nt(dx1, w["attn_w_o"][j], out_dtype=BF16, name="attn_out_dx")
            g["attn_w_o"][j] = mm_tn(ao, dx1, out_dtype=BF16, name="attn_out_dw")
            dq, dk, dv, dqg, dkg, dsk = attn_bwd(qkv, dao, w["attn_q_gain"][j:j + 1], w["attn_k_gain"][j:j + 1],
                                                 w["attn_sinks"][j:j + 1], name="attn_bwd")
            g["attn_q_gain"][j], g["attn_k_gain"][j], g["attn_sinks"][j] = dqg[0], dkg[0], dsk
            dqkv = jnp.concatenate([dq, dk, dv], axis=1)
            g["attn_w_qkv"][j] = run(mm_tn, "bwd", l, "in_dw", h, dqkv, out_dtype=BF16, col_shards=N_DEV, name="qkv_dw")
            dh = run(mm_nt, "bwd", l, "in_dx", dqkv, w["attn_w_qkv"][j], out_dtype=F32, name="qkv_dx")
        else:
            z, xb, r, ig, a, hs, hprev, hg = mix
            dhg = mm_nt(dx1, w["rec_w_out"][j], out_dtype=BF16, name="rec_out_dx")
            g["rec_w_out"][j] = mm_tn(hg, dx1, out_dtype=BF16, name="rec_out_dw")
            dyb, ybar = rec_bwd_scan(dhg, hs, a, z, name="rec_bwd_scan")
            dxb, dwa, dwi, dba, dbi, dlam = rec_bwd_gates(
                ybar, hprev, a, r, ig, xb, w["rec_lambda"][j:j + 1], w["rec_w_a"][j], w["rec_w_i"][j],
                name="rec_bwd_gates")
            dxb0, dcw, dcb = conv_bwd(dxb, z, w["rec_conv_w"][j], name="rec_conv_bwd")
            g["rec_w_a"][j], g["rec_w_i"][j] = dwa, dwi
            g["rec_b_a"][j], g["rec_b_i"][j], g["rec_lambda"][j] = dba[0], dbi[0], dlam[0]
            g["rec_conv_w"][j], g["rec_conv_b"][j] = dcw, dcb[0]
            dz = jnp.concatenate([dxb0, dyb], axis=1)
            g["rec_w_in"][j] = run(mm_tn, "bwd", l, "in_dw", h, dz, out_dtype=BF16, col_shards=N_DEV, name="rec_in_dw")
            dh = run(mm_nt, "bwd", l, "in_dx", dz, w["rec_w_in"][j], out_dtype=F32, name="rec_in_dx")
        dx, dgm = rmsnorm_bwd(x0, w["mix_norm"][l:l + 1], dh, dx1, name="mix_norm_bwd")
        g["mix_norm"][l] = dgm[0]
    return loss_vec, dx, g


HBM = pl.BlockSpec(memory_space=pltpu.HBM)
N_PEER = N_DEV - 1


def _here():
    return lax.axis_index("x"), lax.axis_index("y"), lax.axis_index("c")


def _sid(dev):
    return 4 * dev[0] + 2 * dev[1] + dev[2]


def _exchange_sems(n):
    return [pltpu.SemaphoreType.DMA((n * N_PEER,)), pltpu.SemaphoreType.DMA((n * N_PEER,)),
            pltpu.SemaphoreType.DMA((n,))]


def gather_exchange(shards, out_structs, windows):
    n = len(shards)

    def parts(outs, sems):
        send_sems, recv_sems, _ = sems
        x, y, c = _here()
        me, sib = (x, y, c), (x, y, 1 - c)
        chips = [(1 - x, y), (x, 1 - y), (1 - x, 1 - y)]

        def copy(i, k, block, to, src=None):
            dst = windows[i](outs[i], _sid(block))
            return pltpu.make_async_remote_copy(
                src_ref=dst if src is None else src, dst_ref=dst,
                send_sem=send_sems.at[i * N_PEER + k], recv_sem=recv_sems.at[i * N_PEER + k],
                device_id=to, device_id_type=MESH)

        return me, sib, chips, c, copy

    def own_copies(ins, outs, sems):
        me, sib, chips, c, copy = parts(outs, sems)
        local = [pltpu.make_async_copy(ins[i], windows[i](outs[i], _sid(me)), sems[2].at[i]) for i in range(n)]
        first = []
        for i in range(n):
            first.append(copy(i, 0, me, sib, src=ins[i]))
            first += [copy(i, 1 + j, me, (*chip, c), src=ins[i]) for j, chip in enumerate(chips)]
        return local, first

    def start(ins, outs, sems):
        local, first = own_copies(ins, outs, sems)
        for cp in local + first:
            cp.start()

    def finish(ins, outs, sems):
        me, sib, chips, c, copy = parts(outs, sems)
        local, first = own_copies(ins, outs, sems)
        passed = []
        for i in range(n):
            for j, chip in enumerate(chips):
                copy(i, 1 + j, (*chip, c), me).wait_recv()
                fwd = copy(i, 4 + j, (*chip, c), sib)
                fwd.start()
                passed.append(fwd)
        for i in range(n):
            copy(i, 0, sib, me).wait_recv()
            for j, chip in enumerate(chips):
                copy(i, 4 + j, (*chip, 1 - c), me).wait_recv()
        for cp in first + passed:
            cp.wait_send()
        for cp in local:
            cp.wait()

    return dict(arrays=list(shards), out_structs=list(out_structs), sems=_exchange_sems(n), start=start,
                finish=finish)


def run_exchange(ex, *, name):
    n = len(ex["arrays"])

    def body(*refs):
        ins, outs, sems = refs[:n], refs[n:2 * n], refs[2 * n:]
        ex["start"](ins, outs, sems)
        ex["finish"](ins, outs, sems)

    return pl.pallas_call(
        body, in_specs=[HBM] * n, out_specs=[HBM] * n, out_shape=ex["out_structs"], scratch_shapes=ex["sems"],
        name=name)(*ex["arrays"])


def scatter_exchange(grads):
    n = len(grads)

    def parts(ins, outs, sems):
        send_sems, recv_sems, local_sems = sems
        x, y, c = _here()
        me = (x, y, c)
        peers = []
        for k in range(1, N_DEV):
            kx, ky, kc = (k >> 2) & 1, (k >> 1) & 1, k & 1
            peers.append((1 - x if kx else x, 1 - y if ky else y, 1 - c if kc else c))

        def copy(i, k):
            return pltpu.make_async_remote_copy(
                src_ref=ins[i].at[_sid(peers[k])], dst_ref=outs[i].at[_sid(me)],
                send_sem=send_sems.at[i * N_PEER + k], recv_sem=recv_sems.at[i * N_PEER + k],
                device_id=peers[k], device_id_type=MESH)

        def arrival(i, k):
            return pltpu.make_async_remote_copy(
                src_ref=ins[i].at[_sid(me)], dst_ref=outs[i].at[_sid(peers[k])],
                send_sem=send_sems.at[i * N_PEER + k], recv_sem=recv_sems.at[i * N_PEER + k],
                device_id=peers[k], device_id_type=MESH)

        local = [pltpu.make_async_copy(ins[i].at[_sid(me)], outs[i].at[_sid(me)], local_sems.at[i]) for i in range(n)]
        sends = [copy(i, k) for i in range(n) for k in range(N_PEER)]
        return local, sends, arrival

    def start(ins, outs, sems):
        local, sends, _ = parts(ins, outs, sems)
        for cp in local + sends:
            cp.start()

    def finish(ins, outs, sems):
        local, sends, arrival = parts(ins, outs, sems)
        for i in range(n):
            for k in range(N_PEER):
                arrival(i, k).wait_recv()
        for cp in sends:
            cp.wait_send()
        for cp in local:
            cp.wait()

    return dict(arrays=list(grads), out_structs=[jax.ShapeDtypeStruct(g.shape, g.dtype) for g in grads],
                sems=_exchange_sems(n), start=start, finish=finish)


def adamw_family(contribs, w, m, v, *, name):
    L, R, C = w.shape
    S = contribs[0].shape[0]
    tr = _tile(R, max(8, (1 << 20) // (C * S)), 8)
    nr = R // tr
    c1 = 1.0 / (1.0 - ADAM_B1 ** ADAM_STEP)
    c2 = 1.0 / (1.0 - ADAM_B2 ** ADAM_STEP)

    def body(*refs):
        c_refs = refs[:L]
        w_ref, m_ref, v_ref, g_ref, d_ref, nm_ref, nv_ref = refs[L:]
        layer = pl.program_id(0)
        for l in range(L):
            @pl.when(layer == l)
            def _(l=l):
                g = c_refs[l][0].astype(F32)
                for s in range(1, S):
                    g = g + c_refs[l][s].astype(F32)
                mm = ADAM_B1 * m_ref[...] + (1.0 - ADAM_B1) * g
                vv = ADAM_B2 * v_ref[...] + (1.0 - ADAM_B2) * (g * g)
                g_ref[...] = g
                nm_ref[...] = mm
                nv_ref[...] = vv
                d_ref[...] = -ADAM_LR * ((mm * c1) / (jnp.sqrt(vv * c2) + ADAM_EPS) + ADAM_WD * w_ref[...])

    def cspec(l):
        return pl.BlockSpec((S, tr, C), lambda ll, i, l=l: (0, jnp.where(ll == l, i, 0), 0))

    lay = pl.BlockSpec((None, tr, C), lambda ll, i: (ll, i, 0))
    return pl.pallas_call(
        body, grid=(L, nr), in_specs=[cspec(l) for l in range(L)] + [lay] * 3, out_specs=[lay] * 4,
        out_shape=[jax.ShapeDtypeStruct((L, R, C), F32)] * 4,
        compiler_params=_cp("arbitrary", "arbitrary"), name=name)(*contribs, w, m, v)


def sum_slots(a, *, name):
    S, R, C = a.shape
    tr = _tile(R, 256, 8)

    def body(a_ref, o_ref):
        t = a_ref[0]
        for s in range(1, S):
            t = t + a_ref[s]
        o_ref[...] = t

    return pl.pallas_call(
        body, grid=(R // tr,), in_specs=[pl.BlockSpec((S, tr, C), lambda i: (0, i, 0))],
        out_specs=pl.BlockSpec((tr, C), lambda i: (i, 0)), out_shape=jax.ShapeDtypeStruct((R, C), F32),
        compiler_params=_cp("parallel"), name=name)(a)


LANES = 128


def _pack(arrs):
    flat = jnp.concatenate([a.reshape(-1).astype(F32) for a in arrs])
    rows = -(-flat.shape[0] // LANES)
    rows = -(-rows // 8) * 8
    return jnp.pad(flat, (0, rows * LANES - flat.shape[0])).reshape(rows, LANES)


def _unpack(buf, shapes):
    flat = buf.reshape(-1)
    out, off = [], 0
    for shp in shapes:
        size = int(np.prod(shp))
        out.append(flat[off:off + size].reshape(shp))
        off += size
    return out


def _gather_last(g):
    t = jnp.moveaxis(g, 0, -2)
    return t.reshape(t.shape[:-2] + (t.shape[-2] * t.shape[-1],))


def _own_last(full, s):
    n = full.shape[-1] // N_DEV
    t = full.reshape(full.shape[:-1] + (N_DEV, n))
    return lax.dynamic_index_in_dim(t, s, axis=t.ndim - 2, keepdims=False)


BIG = ["attn_w_qkv", "attn_w_o", "rec_w_in", "rec_w_out", "ffn_w_up", "ffn_w_down", "rec_w_a", "rec_w_i"]
SMALL_REPLICATED = ["mix_norm", "ffn_norm", "attn_q_gain", "attn_k_gain", "attn_sinks", "ffn_conv_b"]
SMALL_SHARDED = ["rec_conv_w", "rec_conv_b", "rec_b_a", "rec_b_i", "rec_lambda", "ffn_conv_w"]
SMALL = SMALL_REPLICATED + SMALL_SHARDED
WEIGHTS = ["mix_norm", "ffn_norm", "attn_w_qkv", "attn_q_gain", "attn_k_gain", "attn_sinks", "attn_w_o", "rec_w_in",
           "rec_conv_w", "rec_conv_b", "rec_w_a", "rec_b_a", "rec_w_i", "rec_b_i", "rec_lambda", "rec_w_out",
           "ffn_w_up", "ffn_conv_w", "ffn_conv_b", "ffn_w_down"]


def _col_window(n, block_of_shard=None):
    def win(ref, s):
        q = s if block_of_shard is None else block_of_shard(s)
        return ref.at[:, pl.ds(pl.multiple_of(q * n, 128), n)]
    return win


def _row_window(r):
    return lambda ref, s: ref.at[pl.ds(pl.multiple_of(s * r, 16), r), :]


def _gate_window(r):
    return lambda ref, s: ref.at[:, pl.ds(pl.multiple_of(s * r, 16), r), :]


def _slot_window(ref, s):
    return ref.at[s]


def _idx(name, layer):
    return layer if name.startswith("ffn") else layer // 2


def _slot_names(layer, slot):
    attn = layer % 2 == 0
    return {"in": ["attn_w_qkv"] if attn else ["rec_w_in"],
            "out": ["attn_w_o"] if attn else ["rec_w_out", "rec_w_a", "rec_w_i"],
            "up": ["ffn_w_up"], "down": ["ffn_w_down"]}[slot]


def _gather_for(p, items):
    shards, structs, wins = [], [], []
    for nme, layer in items:
        sh = p[nme][_idx(nme, layer)].astype(BF16)
        if nme in ("attn_w_qkv", "rec_w_in", "ffn_w_up"):
            K, n = sh.shape
            structs.append(jax.ShapeDtypeStruct((K, n * N_DEV), BF16))
            wins.append(_col_window(n, _ffn_block_of_shard if nme == "ffn_w_up" else None))
        elif nme in ("rec_w_a", "rec_w_i"):
            nblk, r, bd = sh.shape
            structs.append(jax.ShapeDtypeStruct((nblk, r * N_DEV, bd), BF16))
            wins.append(_gate_window(r))
        else:
            r, N = sh.shape
            structs.append(jax.ShapeDtypeStruct((r * N_DEV, N), BF16))
            wins.append(_row_window(r))
        shards.append(sh)
    return gather_exchange(shards, structs, wins)


def _send_layout(name, t):
    if name in ("rec_w_a", "rec_w_i"):
        nblk, bd, _ = t.shape
        return jnp.transpose(t.reshape(nblk, N_DEV, bd // N_DEV, bd), (1, 0, 2, 3)).astype(BF16)
    if name in ("attn_w_o", "rec_w_out", "ffn_w_down"):
        return t.reshape((N_DEV, t.shape[0] // N_DEV) + t.shape[1:])
    return t


class _Plan:
    BWD_SLOT = {"up_dw": "down", "up_dx": "up", "in_dw": "out", "in_dx": "in"}

    def __init__(self, p, w, contribs, depth):
        self.p, self.w, self.contribs, self.depth = p, w, contribs, depth
        self.pending = None

    def carry(self, stage, l, slot, g):
        if stage == "fwd":
            items = []
            if l == 0 and slot == "in":
                items += [(k, 0) for k in _slot_names(0, "out")]
            if l == 0 and slot == "attn":
                items += [(k, 0) for s in ("up", "down") for k in _slot_names(0, s)]
            if slot != "attn" and l + 1 < self.depth:
                items += [(k, l + 1) for k in _slot_names(l + 1, slot)]
            if not items:
                return None
            self.pending = items
            return _gather_for(self.p, items)
        self.pending = [(k, l) for k in _slot_names(l, self.BWD_SLOT[slot])]
        return scatter_exchange([_send_layout(k, g[k][_idx(k, layer)]) for k, layer in self.pending])

    def done(self, stage, l, slot, outs):
        dst = self.w if stage == "fwd" else self.contribs
        for (k, layer), o in zip(self.pending, outs):
            dst[k][_idx(k, layer)] = o


def _train_step(p, x, target, mom, vel):
    depth = p["mix_norm"].shape[0]
    s_me = _sid(_here())

    w = {k: [None] * p[k].shape[0] for k in BIG}
    first = [(k, 0) for k in _slot_names(0, "in")]
    for (k, _), t in zip(first, run_exchange(_gather_for(p, first), name="all_gather_first")):
        w[k][0] = t

    local_small = [p[k] for k in SMALL_SHARDED]
    packed = _pack(local_small)
    gathered, = run_exchange(
        gather_exchange([packed], [jax.ShapeDtypeStruct((N_DEV,) + packed.shape, F32)], [_slot_window]),
        name="all_gather_small")
    per_dev = [_unpack(gathered[s], [a.shape for a in local_small]) for s in range(N_DEV)]
    for i, k in enumerate(SMALL_SHARDED):
        w[k] = _gather_last(jnp.stack([per_dev[s][i] for s in range(N_DEV)]))
    for k in SMALL_REPLICATED:
        w[k] = p[k]
    nrec = w["rec_b_a"].shape[0]
    w["rec_b_a"] = w["rec_b_a"].reshape(nrec, -1)
    w["rec_b_i"] = w["rec_b_i"].reshape(nrec, -1)
    w["ffn_conv_w"] = _group_cols(w["ffn_conv_w"])
    w["ffn_conv_b"] = _group_cols(w["ffn_conv_b"])

    contribs = {k: [None] * len(w[k]) for k in BIG}
    loss_vec, dx, g = local_step(x[0], target[0], w, _Plan(p, w, contribs, depth))
    loss = lax.psum(jnp.sum(loss_vec), ("x", "y", "c"))

    out = {}
    for k in BIG:
        shp = p[k].shape
        L = shp[0]
        C = shp[-1]
        R = int(np.prod(shp[1:-1]))
        cs = [c.reshape(N_DEV, R, C) for c in contribs[k]]
        res = adamw_family(cs, p[k].reshape(L, R, C), mom[k].reshape(L, R, C), vel[k].reshape(L, R, C),
                           name="adamw_" + k)
        out[k] = [t.reshape(shp) for t in res]

    gsmall = [jnp.stack(g[k]) for k in SMALL]
    gp = _pack(gsmall)
    gall, = run_exchange(
        gather_exchange([gp], [jax.ShapeDtypeStruct((N_DEV,) + gp.shape, F32)], [_slot_window]),
        name="all_gather_small_grads")
    gsum = _unpack(sum_slots(gall, name="sum_small_grads"), [a.shape for a in gsmall])
    glocal = []
    for k, t in zip(SMALL, gsum):
        if k in SMALL_SHARDED:
            t = _own_last(t.reshape(p[k].shape[:-1] + (p[k].shape[-1] * N_DEV,)), s_me)
        glocal.append(t.reshape(p[k].shape))
    wp, mp, vp, gpk = (_pack([d[k] for k in SMALL]) for d in (p, mom, vel, dict(zip(SMALL, glocal))))
    res = adamw_family([gpk[None]], wp[None], mp[None], vp[None], name="adamw_small")
    shapes = [p[k].shape for k in SMALL]
    unp = [_unpack(t[0], shapes) for t in res]
    for i, k in enumerate(SMALL):
        out[k] = [glocal[i], unp[1][i], unp[2][i], unp[3][i]]

    return (loss, dx[None]) + tuple(out[k][q] for q in range(4) for k in WEIGHTS)


def kernel(x, mix_norm, ffn_norm, attn_w_qkv, attn_q_gain, attn_k_gain, attn_sinks, attn_w_o, rec_w_in, rec_conv_w, rec_conv_b, rec_w_a, rec_b_a, rec_w_i, rec_b_i, rec_lambda, rec_w_out, ffn_w_up, ffn_conv_w, ffn_conv_b, ffn_w_down, loss_target, m_mix_norm, m_ffn_norm, m_attn_w_qkv, m_attn_q_gain, m_attn_k_gain, m_attn_sinks, m_attn_w_o, m_rec_w_in, m_rec_conv_w, m_rec_conv_b, m_rec_w_a, m_rec_b_a, m_rec_w_i, m_rec_b_i, m_rec_lambda, m_rec_w_out, m_ffn_w_up, m_ffn_conv_w, m_ffn_conv_b, m_ffn_w_down, v_mix_norm, v_ffn_norm, v_attn_w_qkv, v_attn_q_gain, v_attn_k_gain, v_attn_sinks, v_attn_w_o, v_rec_w_in, v_rec_conv_w, v_rec_conv_b, v_rec_w_a, v_rec_b_a, v_rec_w_i, v_rec_b_i, v_rec_lambda, v_rec_w_out, v_ffn_w_up, v_ffn_conv_w, v_ffn_conv_b, v_ffn_w_down):
    p = dict(zip(WEIGHTS, (mix_norm, ffn_norm, attn_w_qkv, attn_q_gain, attn_k_gain, attn_sinks, attn_w_o, rec_w_in,
                           rec_conv_w, rec_conv_b, rec_w_a, rec_b_a, rec_w_i, rec_b_i, rec_lambda, rec_w_out,
                           ffn_w_up, ffn_conv_w, ffn_conv_b, ffn_w_down)))
    mom = dict(zip(WEIGHTS, (m_mix_norm, m_ffn_norm, m_attn_w_qkv, m_attn_q_gain, m_attn_k_gain, m_attn_sinks,
                             m_attn_w_o, m_rec_w_in, m_rec_conv_w, m_rec_conv_b, m_rec_w_a, m_rec_b_a, m_rec_w_i,
                             m_rec_b_i, m_rec_lambda, m_rec_w_out, m_ffn_w_up, m_ffn_conv_w, m_ffn_conv_b,
                             m_ffn_w_down)))
    vel = dict(zip(WEIGHTS, (v_mix_norm, v_ffn_norm, v_attn_w_qkv, v_attn_q_gain, v_attn_k_gain, v_attn_sinks,
                             v_attn_w_o, v_rec_w_in, v_rec_conv_w, v_rec_conv_b, v_rec_w_a, v_rec_b_a, v_rec_w_i,
                             v_rec_b_i, v_rec_lambda, v_rec_w_out, v_ffn_w_up, v_ffn_conv_w, v_ffn_conv_b,
                             v_ffn_w_down)))
    return _train_step(p, x, loss_target, mom, vel)
```

```python
import functools
import math

import jax
import jax.numpy as jnp
import numpy as np
from jax import lax
from jax.experimental import pallas as pl
from jax.experimental.pallas import tpu as pltpu

F32 = jnp.float32
BF16 = jnp.bfloat16

N_DEV = 8
HEAD_DIM = 64
GROUP = 4
BLOCK = 128
LRU_C = 8.0
EPS = 1e-6
HALO = 16
ADAM_LR, ADAM_B1, ADAM_B2, ADAM_EPS, ADAM_WD, ADAM_STEP = 0.001, 0.9, 0.999, 1e-08, 0.01, 10
VMEM_LIMIT = 56 * 1024 * 1024
MESH = pl.DeviceIdType.MESH
GELU_C = math.sqrt(2.0 / math.pi)


def _cp(*sem, vmem=VMEM_LIMIT):
    return pltpu.CompilerParams(dimension_semantics=tuple(sem), vmem_limit_bytes=vmem)


def _tile(dim, pref, mult=128):
    if dim <= pref:
        return dim
    t = (pref // mult) * mult
    while t >= mult:
        if dim % t == 0:
            return t
        t -= mult
    return dim


def _gelu(x):
    th = jnp.tanh(GELU_C * (x + 0.044715 * x * x * x))
    return 0.5 * x * (1.0 + th)


def _gelu_and_grad(x):
    x2 = x * x
    th = jnp.tanh(GELU_C * (x + 0.044715 * x2 * x))
    g = 0.5 * x * (1.0 + th)
    dg = 0.5 * (1.0 + th) + 0.5 * x * (1.0 - th * th) * GELU_C * (1.0 + 3.0 * 0.044715 * x2)
    return g, dg


def _dot(a, b, dims):
    return lax.dot_general(a.astype(BF16), b.astype(BF16), (dims, ((), ())), preferred_element_type=F32)


NN = ((1,), (0,))
NT = ((1,), (1,))
TN = ((0,), (0,))


def _matmul(a, b, *, dims, grid, a_spec, b_spec, o_spec, out_shape, acc_shape, res=None, res_spec=None,
            carry=None, name):
    ni, nj, nk = grid
    nres = 0 if res is None else 1
    ncar = 0 if carry is None else len(carry["arrays"])

    def body(*refs):
        a_ref, b_ref = refs[0], refs[1]
        r_ref = refs[2] if nres else None
        car_in = refs[2 + nres:2 + nres + ncar]
        o_ref = refs[2 + nres + ncar]
        car_out = refs[3 + nres + ncar:3 + nres + 2 * ncar]
        scratch = refs[3 + nres + 2 * ncar:]
        i, j, k = pl.program_id(0), pl.program_id(1), pl.program_id(2)

        if carry is not None:
            @pl.when((i == 0) & (j == 0) & (k == 0))
            def _():
                carry["start"](car_in, car_out, scratch[1:])

        def finish(acc):
            if r_ref is not None:
                acc = acc + r_ref[...].astype(F32)
            o_ref[...] = acc.astype(o_ref.dtype)

        if nk == 1:
            finish(_dot(a_ref[...], b_ref[...], dims))
        else:
            acc_ref = scratch[0]

            @pl.when(k == 0)
            def _():
                acc_ref[...] = _dot(a_ref[...], b_ref[...], dims)

            if nk > 2:
                @pl.when((k > 0) & (k < nk - 1))
                def _():
                    acc_ref[...] += _dot(a_ref[...], b_ref[...], dims)

            @pl.when(k == nk - 1)
            def _():
                finish(acc_ref[...] + _dot(a_ref[...], b_ref[...], dims))

        if carry is not None:
            @pl.when((i == ni - 1) & (j == nj - 1) & (k == nk - 1))
            def _():
                carry["finish"](car_in, car_out, scratch[1:])

    in_specs = [a_spec, b_spec] + ([res_spec] if nres else [])
    args = (a, b) + ((res,) if nres else ())
    out_specs, out_shapes = [o_spec], [out_shape]
    scratch_shapes = [pltpu.VMEM(acc_shape if nk > 1 else (8, 128), F32)]
    sem = ("parallel", "parallel", "arbitrary")
    if carry is not None:
        in_specs += [HBM] * ncar
        args += tuple(carry["arrays"])
        out_specs += [HBM] * ncar
        out_shapes += list(carry["out_structs"])
        scratch_shapes += carry["sems"]
        sem = ("arbitrary", "arbitrary", "arbitrary")
    outs = pl.pallas_call(
        body, grid=grid, in_specs=in_specs, out_specs=out_specs, out_shape=out_shapes,
        scratch_shapes=scratch_shapes, compiler_params=_cp(*sem), name=name,
    )(*args)
    return outs[0] if carry is None else (outs[0], list(outs[1:]))


def mm_nn(a, b, *, out_dtype, res=None, carry=None, name):
    M, K = a.shape
    N = b.shape[1]
    tm, tn, tk = _tile(M, 1024), _tile(N, 1024), _tile(K, 2048)
    return _matmul(
        a, b, dims=NN, grid=(M // tm, N // tn, K // tk),
        a_spec=pl.BlockSpec((tm, tk), lambda i, j, k: (i, k)),
        b_spec=pl.BlockSpec((tk, tn), lambda i, j, k: (k, j)),
        o_spec=pl.BlockSpec((tm, tn), lambda i, j, k: (i, j)),
        out_shape=jax.ShapeDtypeStruct((M, N), out_dtype), acc_shape=(tm, tn),
        res=res, res_spec=pl.BlockSpec((tm, tn), lambda i, j, k: (i, j)), carry=carry, name=name)


def mm_nt(a, b, *, out_dtype, res=None, carry=None, name):
    M, N = a.shape
    K = b.shape[0]
    tm, tn, tk = _tile(M, 1024), _tile(K, 1024), _tile(N, 2048)
    return _matmul(
        a, b, dims=NT, grid=(M // tm, K // tn, N // tk),
        a_spec=pl.BlockSpec((tm, tk), lambda i, j, k: (i, k)),
        b_spec=pl.BlockSpec((tn, tk), lambda i, j, k: (j, k)),
        o_spec=pl.BlockSpec((tm, tn), lambda i, j, k: (i, j)),
        out_shape=jax.ShapeDtypeStruct((M, K), out_dtype), acc_shape=(tm, tn),
        res=res, res_spec=pl.BlockSpec((tm, tn), lambda i, j, k: (i, j)), carry=carry, name=name)


def mm_tn(a, b, *, out_dtype, col_shards=None, shard_of_block=None, carry=None, name):
    T, K = a.shape
    N = b.shape[1]
    tt = _tile(T, 2048)
    tm = _tile(K, 1024)
    if col_shards is None:
        tn = _tile(N, 1024)
        o_spec = pl.BlockSpec((tm, tn), lambda i, j, k: (i, j))
        out_shape = jax.ShapeDtypeStruct((K, N), out_dtype)
    else:
        n = N // col_shards
        tn = _tile(n, 1536)
        per = n // tn
        sob = shard_of_block if shard_of_block is not None else (lambda s: s)
        o_spec = pl.BlockSpec((None, tm, tn), lambda i, j, k: (sob(j // per), i, j % per))
        out_shape = jax.ShapeDtypeStruct((col_shards, K, n), out_dtype)
    return _matmul(
        a, b, dims=TN, grid=(K // tm, N // tn, T // tt),
        a_spec=pl.BlockSpec((tt, tm), lambda i, j, k: (k, i)),
        b_spec=pl.BlockSpec((tt, tn), lambda i, j, k: (k, j)),
        o_spec=o_spec, out_shape=out_shape, acc_shape=(tm, tn), carry=carry, name=name)


def rmsnorm_fwd(x, g, *, name):
    T, D = x.shape
    tm = _tile(T, 512, 8)

    def body(x_ref, g_ref, o_ref):
        xv = x_ref[...]
        r = lax.rsqrt(jnp.mean(xv * xv, axis=-1, keepdims=True) + EPS)
        o_ref[...] = (xv * r * g_ref[...]).astype(o_ref.dtype)

    return pl.pallas_call(
        body, grid=(T // tm,),
        in_specs=[pl.BlockSpec((tm, D), lambda i: (i, 0)), pl.BlockSpec((1, D), lambda i: (0, 0))],
        out_specs=pl.BlockSpec((tm, D), lambda i: (i, 0)),
        out_shape=jax.ShapeDtypeStruct((T, D), BF16), compiler_params=_cp("parallel"), name=name)(x, g)


def rmsnorm_bwd(x, g, dh, dres, *, name):
    T, D = x.shape
    tm = _tile(T, 512, 8)

    def body(x_ref, g_ref, dh_ref, dres_ref, dx_ref, dg_ref):
        @pl.when(pl.program_id(0) == 0)
        def _():
            dg_ref[...] = jnp.zeros_like(dg_ref)

        xv = x_ref[...]
        dh_v = dh_ref[...].astype(F32)
        r = lax.rsqrt(jnp.mean(xv * xv, axis=-1, keepdims=True) + EPS)
        u = dh_v * g_ref[...]
        dot = jnp.mean(u * xv, axis=-1, keepdims=True)
        dx_ref[...] = dres_ref[...] + r * u - xv * (r * r * r * dot)
        dg_ref[...] += jnp.sum(dh_v * xv * r, axis=0, keepdims=True)

    row = pl.BlockSpec((tm, D), lambda i: (i, 0))
    vec = pl.BlockSpec((1, D), lambda i: (0, 0))
    return pl.pallas_call(
        body, grid=(T // tm,), in_specs=[row, vec, row, row], out_specs=[row, vec],
        out_shape=[jax.ShapeDtypeStruct((T, D), F32), jax.ShapeDtypeStruct((1, D), F32)],
        compiler_params=_cp("arbitrary"), name=name)(x, g, dh, dres)


def loss_head(y, target, *, name):
    T, D = y.shape
    tm = _tile(T, 512, 8)

    def body(y_ref, t_ref, l_ref, dy_ref):
        @pl.when(pl.program_id(0) == 0)
        def _():
            l_ref[...] = jnp.zeros_like(l_ref)

        e = y_ref[...] - t_ref[...]
        dy_ref[...] = e * (1.0 / D)
        l_ref[...] += jnp.sum(e * e, axis=0, keepdims=True) * (0.5 / D)

    row = pl.BlockSpec((tm, D), lambda i: (i, 0))
    vec = pl.BlockSpec((1, D), lambda i: (0, 0))
    return pl.pallas_call(
        body, grid=(T // tm,), in_specs=[row, row], out_specs=[vec, row],
        out_shape=[jax.ShapeDtypeStruct((1, D), F32), jax.ShapeDtypeStruct((T, D), F32)],
        compiler_params=_cp("arbitrary"), name=name)(y, target)


NEG = -1e30


def _kv_heads_per_step(hkv):
    return 4 if hkv % 4 == 0 else 2


LANE = 128


def _hi_lo_dot(x, w):
    hi = x.astype(BF16)
    lo = x - hi.astype(F32)
    return _dot(hi, w, NN) + _dot(lo, w, NN)


def _head_sum_matrix():
    r = lax.broadcasted_iota(jnp.int32, (LANE, LANE), 0) // HEAD_DIM
    c = lax.broadcasted_iota(jnp.int32, (LANE, LANE), 1) // HEAD_DIM
    return jnp.where(r == c, 1.0, 0.0).astype(BF16)


def _chunk(x, c):
    return x[:, c * LANE:(c + 1) * LANE]


def _head_sums(x, e):
    return jnp.concatenate([_hi_lo_dot(_chunk(x, c), e) for c in range(x.shape[1] // LANE)], axis=1)


def _row_sums(x):
    return _hi_lo_dot(x, jnp.ones((x.shape[1], LANE), BF16))


def _headnorm(x, e):
    r = lax.rsqrt(_head_sums(x * x, e) * (1.0 / HEAD_DIM) + EPS)
    return x * r, r


def _attn_mask(n):
    qi = lax.broadcasted_iota(jnp.int32, (BLOCK, 2 * BLOCK), 0)
    kj = lax.broadcasted_iota(jnp.int32, (BLOCK, 2 * BLOCK), 1)
    rel = qi + BLOCK - kj
    m = (rel >= 0) & (rel < BLOCK) & ((kj >= BLOCK) | (n > 0))
    return jnp.concatenate([m] * GROUP, axis=0)


def _qk_operands(qhat, khat, j, lo_half):
    kb = jnp.where(lo_half == (j % 2 == 0), _chunk(khat, j // 2), 0.0)
    kd = kb + pltpu.roll(kb, HEAD_DIM, 1)
    q4 = jnp.concatenate(
        [jnp.where(lo_half == (g % 2 == 0), _chunk(qhat, 2 * j + g // 2), 0.0) for g in range(GROUP)], axis=0)
    return q4, kd


def _attn_softmax(s, sink_ref, mask4, head0):
    s = s * (HEAD_DIM ** -0.5)
    sink = jnp.concatenate([jnp.full((BLOCK, 1), sink_ref[0, head0 + g], F32) for g in range(GROUP)], axis=0)
    m = jnp.maximum(jnp.max(jnp.where(mask4, s, NEG), axis=-1, keepdims=True), sink)
    ex = jnp.where(mask4, jnp.exp(s - m), 0.0)
    es = jnp.exp(sink - m)
    inv = 1.0 / (_row_sums(ex) + es)
    return ex * jnp.concatenate([inv] * (s.shape[1] // LANE), axis=1), es * inv[:, 0:1]


def attn_fwd(qkv, q_gain, k_gain, sinks, *, carry=None, name):
    ncar = 0 if carry is None else len(carry["arrays"])
    T, W = qkv.shape
    hq = W // HEAD_DIM * GROUP // (GROUP + 2)
    dq = hq * HEAD_DIM
    PAIR = _kv_heads_per_step(hq // GROUP)
    QW, KW = PAIR * GROUP * HEAD_DIM, PAIR * HEAD_DIM
    npair = hq // (GROUP * PAIR)
    nb = T // BLOCK
    k0 = dq // KW
    v0 = k0 + npair

    def body(q_ref, kp_ref, kc_ref, vp_ref, vc_ref, qg_ref, kg_ref, sink_ref, *rest):
        car_in, o_ref, car_out, sems = rest[:ncar], rest[ncar], rest[ncar + 1:2 * ncar + 1], rest[2 * ncar + 1:]
        p, n = pl.program_id(0), pl.program_id(1)
        if carry is not None:
            @pl.when((p == 0) & (n == 0))
            def _():
                carry["start"](car_in, car_out, sems)

            @pl.when((p == npair - 1) & (n == nb - 1))
            def _():
                carry["finish"](car_in, car_out, sems)

        e = _head_sum_matrix()
        lo_half = lax.broadcasted_iota(jnp.int32, (1, LANE), 1) < HEAD_DIM
        mask4 = _attn_mask(n)
        qn, _ = _headnorm(q_ref[...], e)
        qhat = _tapmul(qn, qg_ref[...])
        kn, _ = _headnorm(jnp.concatenate([kp_ref[...], kc_ref[...]], axis=0), e)
        khat = _tapmul(kn, kg_ref[...])
        vwin = jnp.concatenate([vp_ref[...], vc_ref[...]], axis=0).astype(BF16)
        ops = [_qk_operands(qhat, khat, j, lo_half) for j in range(PAIR)]
        scores = [_dot(q4, kd, NT) for q4, kd in ops]
        probs = [_attn_softmax(s, sink_ref, mask4, (p * PAIR + j) * GROUP)[0] for j, s in enumerate(scores)]
        o4 = [_dot(probs[j], _chunk(vwin, j // 2), NN) for j in range(PAIR)]
        chunks = []
        for c in range(2 * PAIR):
            j, t = c // 2, c % 2
            a = o4[j][(2 * t) * BLOCK:(2 * t + 1) * BLOCK]
            b = o4[j][(2 * t + 1) * BLOCK:(2 * t + 2) * BLOCK]
            if j % 2 == 0:
                b = pltpu.roll(b, HEAD_DIM, 1)
            else:
                a = pltpu.roll(a, HEAD_DIM, 1)
            chunks.append(jnp.where(lo_half, a, b))
        o_ref[...] = jnp.concatenate(chunks, axis=1).astype(o_ref.dtype)

    prev = lambda n: jnp.maximum(n - 1, 0)
    car = carry if carry is not None else dict(arrays=[], out_structs=[], sems=[])
    qg8 = _rep8(jnp.tile(q_gain[0], PAIR * GROUP))
    kg8 = _rep8(jnp.tile(k_gain[0], PAIR))
    outs = pl.pallas_call(
        body, grid=(npair, nb),
        in_specs=[pl.BlockSpec((BLOCK, QW), lambda p, n: (n, p)),
                  pl.BlockSpec((BLOCK, KW), lambda p, n: (prev(n), k0 + p)),
                  pl.BlockSpec((BLOCK, KW), lambda p, n: (n, k0 + p)),
                  pl.BlockSpec((BLOCK, KW), lambda p, n: (prev(n), v0 + p)),
                  pl.BlockSpec((BLOCK, KW), lambda p, n: (n, v0 + p)),
                  pl.BlockSpec((8, QW), lambda p, n: (0, 0)), pl.BlockSpec((8, KW), lambda p, n: (0, 0)),
                  pl.BlockSpec(memory_space=pltpu.SMEM)] + [HBM] * ncar,
        out_specs=[pl.BlockSpec((BLOCK, QW), lambda p, n: (n, p))] + [HBM] * ncar,
        out_shape=[jax.ShapeDtypeStruct((T, dq), BF16)] + list(car["out_structs"]),
        scratch_shapes=list(car["sems"]),
        compiler_params=_cp(*(("parallel", "parallel") if carry is None else ("arbitrary", "arbitrary"))),
        name=name)(qkv, qkv, qkv, qkv, qkv, qg8, kg8, sinks, *car["arrays"])
    return outs[0] if carry is None else (outs[0], list(outs[1:]))


def attn_bwd(qkv, d_out, q_gain, k_gain, sinks, *, name):
    T, W = qkv.shape
    hq = W // HEAD_DIM * GROUP // (GROUP + 2)
    dq_w = hq * HEAD_DIM
    PAIR = _kv_heads_per_step(hq // GROUP)
    QW, KW = PAIR * GROUP * HEAD_DIM, PAIR * HEAD_DIM
    npair = hq // (GROUP * PAIR)
    nb = T // BLOCK
    k0 = dq_w // KW
    v0 = k0 + npair

    def body(q_ref, kp_ref, kc_ref, vp_ref, vc_ref, do_ref, qg_ref, kg_ref, sink_ref,
             dq_ref, dk_ref, dv_ref, dqg_ref, dkg_ref, dsink_ref, dk_carry, dv_carry):
        p, i = pl.program_id(0), pl.program_id(1)
        n = nb - 1 - i

        @pl.when(i == 0)
        def _():
            dk_carry[...] = jnp.zeros_like(dk_carry)
            dv_carry[...] = jnp.zeros_like(dv_carry)
            dqg_ref[...] = jnp.zeros_like(dqg_ref)
            dkg_ref[...] = jnp.zeros_like(dkg_ref)
            dsink_ref[...] = jnp.zeros_like(dsink_ref)

        e = _head_sum_matrix()
        lo_half = lax.broadcasted_iota(jnp.int32, (1, LANE), 1) < HEAD_DIM
        mask4 = _attn_mask(n)
        qg, kg = qg_ref[...], kg_ref[...]
        q = q_ref[...]
        kwin = jnp.concatenate([kp_ref[...], kc_ref[...]], axis=0)
        qn, qr = _headnorm(q, e)
        qhat = _tapmul(qn, qg)
        kn, kr = _headnorm(kwin, e)
        khat = _tapmul(kn, kg)
        vwin = jnp.concatenate([vp_ref[...], vc_ref[...]], axis=0).astype(BF16)
        do = do_ref[...].astype(F32)
        ops = [_qk_operands(qhat, khat, j, lo_half) for j in range(PAIR)]
        do4 = []
        for j in range(PAIR):
            parts = []
            for g in range(GROUP):
                dc = _chunk(do, 2 * j + g // 2)
                if g % 2 != j % 2:
                    dc = pltpu.roll(dc, HEAD_DIM, 1)
                parts.append(jnp.where(lo_half == (j % 2 == 0), dc, 0.0))
            do4.append(jnp.concatenate(parts, axis=0))
        scores = [_dot(q4, kd, NT) for q4, kd in ops]
        dps = [_dot(do4[j], _chunk(vwin, j // 2), NT) for j in range(PAIR)]
        soft = [_attn_softmax(s, sink_ref, mask4, (p * PAIR + j) * GROUP) for j, s in enumerate(scores)]
        dss, dsink_rows = [], []
        for j in range(PAIR):
            pr, psink = soft[j]
            delta = _row_sums(pr * dps[j])
            dss.append(pr * (dps[j] - jnp.concatenate([delta] * 2, axis=1)) * (HEAD_DIM ** -0.5))
            dsk = -psink * delta[:, 0:1]
            for g in range(GROUP):
                tot = jnp.sum(dsk[g * BLOCK:(g + 1) * BLOCK], axis=0, keepdims=True)
                dsink_rows.append(jnp.broadcast_to(tot, (1, LANE)))
        dq4 = [_dot(dss[j], ops[j][1], NN) for j in range(PAIR)]
        dkd = [_dot(dss[j], ops[j][0], TN) for j in range(PAIR)]
        dvc = [_dot(soft[j][0], do4[j], TN) for j in range(PAIR)]
        dqhat = jnp.concatenate(
            [jnp.where(lo_half, dq4[c // 2][(2 * (c % 2)) * BLOCK:(2 * (c % 2) + 1) * BLOCK],
                       dq4[c // 2][(2 * (c % 2) + 1) * BLOCK:(2 * (c % 2) + 2) * BLOCK]) for c in range(2 * PAIR)],
            axis=1)
        dkhat_chunks, dv_chunks = [], []
        for kc in range(PAIR // 2):
            tot_k, tot_v = None, None
            for j in (2 * kc, 2 * kc + 1):
                t = jnp.where(lo_half == (j % 2 == 0), dkd[j] + pltpu.roll(dkd[j], HEAD_DIM, 1), 0.0)
                tot_k = t if tot_k is None else tot_k + t
                tot_v = dvc[j] if tot_v is None else tot_v + dvc[j]
            dkhat_chunks.append(tot_k)
            dv_chunks.append(tot_v)
        dkhat_win = jnp.concatenate(dkhat_chunks, axis=1)
        dv_win = jnp.concatenate(dv_chunks, axis=1)
        dqg_ref[...] += _colsum8(dqhat * qn)
        dqn = _tapmul(dqhat, qg)
        dq_ref[...] = (qr * dqn - q * (qr * qr * qr * _head_sums(dqn * q, e) * (1.0 / HEAD_DIM))).astype(dq_ref.dtype)
        dkh = dkhat_win[BLOCK:] + dk_carry[...]
        dk_carry[...] = dkhat_win[:BLOCK]
        kcur, knc, krc = kwin[BLOCK:], kn[BLOCK:], kr[BLOCK:]
        dkg_ref[...] += _colsum8(dkh * knc)
        dkn = _tapmul(dkh, kg)
        dk_ref[...] = (krc * dkn
                       - kcur * (krc * krc * krc * _head_sums(dkn * kcur, e) * (1.0 / HEAD_DIM))).astype(dk_ref.dtype)
        dv_ref[...] = (dv_win[BLOCK:] + dv_carry[...]).astype(dv_ref.dtype)
        dv_carry[...] = dv_win[:BLOCK]
        dsink_ref[...] += jnp.concatenate(dsink_rows, axis=0)

    rev = lambda i: nb - 1 - i
    prev = lambda i: jnp.maximum(nb - 2 - i, 0)
    qg8 = _rep8(jnp.tile(q_gain[0], PAIR * GROUP))
    kg8 = _rep8(jnp.tile(k_gain[0], PAIR))
    dq, dk, dv, dqg, dkg, dsink = pl.pallas_call(
        body, grid=(npair, nb),
        in_specs=[pl.BlockSpec((BLOCK, QW), lambda p, i: (rev(i), p)),
                  pl.BlockSpec((BLOCK, KW), lambda p, i: (prev(i), k0 + p)),
                  pl.BlockSpec((BLOCK, KW), lambda p, i: (rev(i), k0 + p)),
                  pl.BlockSpec((BLOCK, KW), lambda p, i: (prev(i), v0 + p)),
                  pl.BlockSpec((BLOCK, KW), lambda p, i: (rev(i), v0 + p)),
                  pl.BlockSpec((BLOCK, QW), lambda p, i: (rev(i), p)),
                  pl.BlockSpec((8, QW), lambda p, i: (0, 0)), pl.BlockSpec((8, KW), lambda p, i: (0, 0)),
                  pl.BlockSpec(memory_space=pltpu.SMEM)],
        out_specs=[pl.BlockSpec((BLOCK, QW), lambda p, i: (rev(i), p)),
                   pl.BlockSpec((BLOCK, KW), lambda p, i: (rev(i), p)),
                   pl.BlockSpec((BLOCK, KW), lambda p, i: (rev(i), p)),
                   pl.BlockSpec((None, 8, QW), lambda p, i: (p, 0, 0)),
                   pl.BlockSpec((None, 8, KW), lambda p, i: (p, 0, 0)),
                   pl.BlockSpec((None, PAIR * GROUP, LANE), lambda p, i: (p, 0, 0))],
        out_shape=[jax.ShapeDtypeStruct((T, dq_w), BF16),
                   jax.ShapeDtypeStruct((T, npair * KW), BF16),
                   jax.ShapeDtypeStruct((T, npair * KW), BF16),
                   jax.ShapeDtypeStruct((npair, 8, QW), F32),
                   jax.ShapeDtypeStruct((npair, 8, KW), F32),
                   jax.ShapeDtypeStruct((npair, PAIR * GROUP, LANE), F32)],
        scratch_shapes=[pltpu.VMEM((BLOCK, KW), F32), pltpu.VMEM((BLOCK, KW), F32)],
        compiler_params=_cp("parallel", "arbitrary"), name=name,
    )(qkv, qkv, qkv, qkv, qkv, d_out, qg8, kg8, sinks)
    dqg = jnp.sum(dqg.reshape(-1, HEAD_DIM), axis=0, keepdims=True)
    dkg = jnp.sum(dkg.reshape(-1, HEAD_DIM), axis=0, keepdims=True)
    return dq, dk, dv, dqg, dkg, dsink[:, :, 0].reshape(-1)


def _softplus_neg(lam):
    return jnp.maximum(-lam, 0.0) + jnp.log1p(jnp.exp(-jnp.abs(lam)))


def _conv_taps(xc, cw_ref, tt, lead):
    K = cw_ref.shape[0]
    acc = None
    for k in range(K):
        off = lead - (K - 1 - k)
        term = cw_ref[k:k + 1, :] * xc[off:off + tt]
        acc = term if acc is None else acc + term
    return acc


def _bcast_row(x, row):
    return jnp.broadcast_to(x[row:row + 1, :], x.shape)


def _rows8(x):
    return x.reshape(x.shape[0] // 8, 8, x.shape[1])


def _tapmul(x, w8):
    return (_rows8(x) * w8[None]).reshape(x.shape)


def _colsum8(x):
    return jnp.sum(_rows8(x), axis=0)


def _shift_down(x, s, prev8):
    r = pltpu.roll(x, s, 0)
    rowid = lax.broadcasted_iota(jnp.int32, (8, 1), 0)
    head = jnp.where(rowid >= s, r[0:8], pltpu.roll(prev8, s, 0))
    return jnp.concatenate([head, r[8:]], axis=0)


def _shift_up(x, s, next8):
    R = x.shape[0]
    r = pltpu.roll(x, R - s, 0)
    rowid = lax.broadcasted_iota(jnp.int32, (8, 1), 0)
    tail = jnp.where(rowid < 8 - s, r[R - 8:], pltpu.roll(next8, 8 - s, 0))
    return jnp.concatenate([r[:R - 8], tail], axis=0)


def _rep8(v):
    return jnp.broadcast_to(v[..., None, :], v.shape[:-1] + (8, v.shape[-1]))


def rec_fwd(z, cw, cb, wa, ba, wi, bi, lam, *, name):
    T, C2 = z.shape
    C = C2 // 2
    nblk, bd, _ = wa.shape
    tt = _tile(T, 128, HALO)
    ng = tt // 8

    def body(x_ref, y_ref, halo_ref, cw_ref, cb_ref, wa_ref, ba_ref, wi_ref, bi_ref, lam_ref,
             xb_ref, r_ref, i_ref, a_ref, h_ref, hp_ref, hg_ref, carry, u_scr):
        step = pl.program_id(0)

        @pl.when(step == 0)
        def _():
            carry[...] = jnp.zeros_like(carry)

        halo = jnp.where(step > 0, halo_ref[...], 0.0)
        xc = jnp.concatenate([halo, x_ref[...]], axis=0)
        xb = cb_ref[...] + _conv_taps(xc, cw_ref, tt, HALO)
        xb_ref[...] = xb
        pa, pi = [], []
        for b in range(nblk):
            xs = xb[:, b * bd:(b + 1) * bd]
            pa.append(_dot(xs, wa_ref[b], NN))
            pi.append(_dot(xs, wi_ref[b], NN))
        r = jax.nn.sigmoid(jnp.concatenate(pa, axis=1) + ba_ref[...])
        ig = jax.nn.sigmoid(jnp.concatenate(pi, axis=1) + bi_ref[...])
        r_ref[...] = r
        i_ref[...] = ig
        nl = LRU_C * r * _softplus_neg(lam_ref[...])
        a_ref[...] = jnp.exp(-nl)
        th = jnp.tanh(nl)
        u_scr[...] = jnp.sqrt(2.0 * th / (1.0 + th)) * (ig * xb)

        rowid = lax.broadcasted_iota(jnp.int32, (8, C), 0)

        def group(gi, hc):
            r0 = pl.multiple_of(gi * 8, 8)
            a8 = a_ref[pl.ds(r0, 8), :]
            u8 = u_scr[pl.ds(r0, 8), :]
            for d in (1, 2, 4):
                a_sh = jnp.where(rowid >= d, pltpu.roll(a8, d, 0), 1.0)
                u_sh = jnp.where(rowid >= d, pltpu.roll(u8, d, 0), 0.0)
                u8 = a8 * u_sh + u8
                a8 = a8 * a_sh
            h8 = u8 + a8 * hc
            h_ref[pl.ds(r0, 8), :] = h8
            hp_ref[pl.ds(r0, 8), :] = jnp.where(rowid >= 1, pltpu.roll(h8, 1, 0), hc)
            return _bcast_row(h8, 7)

        carry[...] = lax.fori_loop(0, ng, group, carry[...])
        hg_ref[...] = (h_ref[...] * _gelu(y_ref[...])).astype(hg_ref.dtype)

    row = lambda c: pl.BlockSpec((tt, C), lambda i, c=c: (i, c))
    vec = pl.BlockSpec((1, C), lambda i: (0, 0))
    full = lambda shp: pl.BlockSpec(shp, lambda i, n=len(shp): (0,) * n)
    per = tt // HALO
    outs = pl.pallas_call(
        body, grid=(T // tt,),
        in_specs=[row(0), row(1), pl.BlockSpec((HALO, C), lambda i: (jnp.maximum(i * per - 1, 0), 0)),
                  full(cw.shape), vec, full(wa.shape), vec, full(wi.shape), vec, vec],
        out_specs=[row(0)] * 7,
        out_shape=[jax.ShapeDtypeStruct((T, C), F32)] * 6 + [jax.ShapeDtypeStruct((T, C), BF16)],
        scratch_shapes=[pltpu.VMEM((8, C), F32), pltpu.VMEM((tt, C), F32)],
        compiler_params=_cp("arbitrary"), name=name,
    )(z, z, z, cw, cb, wa, ba, wi, bi, lam)
    return outs


def rec_bwd_scan(dhg, h, a, z, *, name):
    T, C = h.shape
    tt = _tile(T, 256, HALO)
    ng = tt // 8
    nb = T // tt

    def body(dhg_ref, h_ref, a_ref, y_ref, dy_ref, yb_ref, ycarry, acarry, g_scr):
        step = pl.program_id(0)

        @pl.when(step == 0)
        def _():
            ycarry[...] = jnp.zeros_like(ycarry)
            acarry[...] = jnp.zeros_like(acarry)

        gate, dgate = _gelu_and_grad(y_ref[...])
        dhg_v = dhg_ref[...].astype(F32)
        dy_ref[...] = (dhg_v * h_ref[...] * dgate).astype(dy_ref.dtype)
        g_scr[...] = dhg_v * gate
        rowid = lax.broadcasted_iota(jnp.int32, (8, C), 0)

        def group(j, c):
            yc, ac = c
            r0 = pl.multiple_of((ng - 1 - j) * 8, 8)
            a8 = a_ref[pl.ds(r0, 8), :]
            y8 = g_scr[pl.ds(r0, 8), :]
            b8 = jnp.where(rowid < 7, pltpu.roll(a8, 7, 0), ac)
            for d in (1, 2, 4):
                y_sh = jnp.where(rowid < 8 - d, pltpu.roll(y8, 8 - d, 0), 0.0)
                b_sh = jnp.where(rowid < 8 - d, pltpu.roll(b8, 8 - d, 0), 1.0)
                y8 = y8 + b8 * y_sh
                b8 = b8 * b_sh
            y8 = y8 + b8 * yc
            yb_ref[pl.ds(r0, 8), :] = y8
            return _bcast_row(y8, 0), _bcast_row(a8, 0)

        yc, ac = lax.fori_loop(0, ng, group, (ycarry[...], acarry[...]))
        ycarry[...] = yc
        acarry[...] = ac

    rev = lambda c: pl.BlockSpec((tt, C), lambda i, c=c: (nb - 1 - i, c))
    return pl.pallas_call(
        body, grid=(nb,), in_specs=[rev(0), rev(0), rev(0), rev(1)], out_specs=[rev(0), rev(0)],
        out_shape=[jax.ShapeDtypeStruct((T, C), BF16), jax.ShapeDtypeStruct((T, C), F32)],
        scratch_shapes=[pltpu.VMEM((8, C), F32), pltpu.VMEM((8, C), F32), pltpu.VMEM((tt, C), F32)],
        compiler_params=_cp("arbitrary"), name=name,
    )(dhg, h, a, z)


def rec_bwd_gates(ybar, hprev, a, r, ig, xb, lam, wa, wi, *, name):
    T, C = xb.shape
    nblk, bd, _ = wa.shape
    tt = _tile(T, 256, 8)
    nb = T // tt

    def body(y_ref, hp_ref, a_ref, r_ref, i_ref, xb_ref, lam_ref, wa_ref, wi_ref,
             dxb_ref, dwa_ref, dwi_ref, dba_ref, dbi_ref, dlam_ref):
        step = pl.program_id(0)

        @pl.when(step == 0)
        def _():
            for ref in (dwa_ref, dwi_ref, dba_ref, dbi_ref, dlam_ref):
                ref[...] = jnp.zeros_like(ref)

        y, av, rv, iv, xv = y_ref[...], a_ref[...], r_ref[...], i_ref[...], xb_ref[...]
        sp = _softplus_neg(lam_ref[...])
        th = jnp.tanh(LRU_C * rv * sp)
        s = jnp.sqrt(2.0 * th / (1.0 + th))
        d_nl = -(y * hp_ref[...] * av) + (y * iv * xv) * (av * av) / s
        dlam_ref[...] += jnp.sum(d_nl * rv, axis=0, keepdims=True) * LRU_C
        dr = d_nl * (LRU_C * sp)
        di = y * s * xv
        dpa = dr * rv * (1.0 - rv)
        dpi = di * iv * (1.0 - iv)
        dba_ref[...] += jnp.sum(dpa, axis=0, keepdims=True)
        dbi_ref[...] += jnp.sum(dpi, axis=0, keepdims=True)
        parts = []
        for b in range(nblk):
            sl = slice(b * bd, (b + 1) * bd)
            xs, da_b, di_b = xv[:, sl], dpa[:, sl], dpi[:, sl]
            dwa_ref[b] += _dot(xs, da_b, TN)
            dwi_ref[b] += _dot(xs, di_b, TN)
            parts.append(_dot(da_b, wa_ref[b], NT) + _dot(di_b, wi_ref[b], NT))
        dxb_ref[...] = y * s * iv + jnp.concatenate(parts, axis=1)

        @pl.when(step == nb - 1)
        def _():
            dlam_ref[...] = dlam_ref[...] * (-jax.nn.sigmoid(-lam_ref[...]))

    row = pl.BlockSpec((tt, C), lambda i: (i, 0))
    vec = pl.BlockSpec((1, C), lambda i: (0, 0))
    wsp = pl.BlockSpec(wa.shape, lambda i: (0, 0, 0))
    return pl.pallas_call(
        body, grid=(nb,), in_specs=[row] * 6 + [vec, wsp, wsp],
        out_specs=[row, wsp, wsp, vec, vec, vec],
        out_shape=[jax.ShapeDtypeStruct((T, C), F32), jax.ShapeDtypeStruct(wa.shape, F32),
                   jax.ShapeDtypeStruct(wa.shape, F32)] + [jax.ShapeDtypeStruct((1, C), F32)] * 3,
        compiler_params=_cp("arbitrary"), name=name,
    )(ybar, hprev, a, r, ig, xb, lam, wa, wi)


def conv_bwd(d, x0, cw, *, name):
    T, C = d.shape
    K = cw.shape[0]
    tt = _tile(T, 256, HALO)
    per = tt // HALO
    nb = T // tt

    def body(d_ref, dn_ref, x_ref, xp_ref, cw_ref, dx_ref, dcw_ref, dcb_ref):
        step = pl.program_id(0)

        @pl.when(step == 0)
        def _():
            dcw_ref[...] = jnp.zeros_like(dcw_ref)
            dcb_ref[...] = jnp.zeros_like(dcb_ref)

        dv = d_ref[...].astype(F32)
        dc = jnp.concatenate([dv, jnp.where(step < nb - 1, dn_ref[...].astype(F32), 0.0)], axis=0)
        xc = jnp.concatenate([jnp.where(step > 0, xp_ref[...].astype(F32), 0.0), x_ref[...].astype(F32)], axis=0)
        acc = None
        rows = []
        for k in range(K):
            sh = K - 1 - k
            term = cw_ref[k:k + 1, :] * dc[sh:sh + tt]
            acc = term if acc is None else acc + term
            rows.append(jnp.sum(dv * xc[HALO - sh:HALO - sh + tt], axis=0, keepdims=True))
        dx_ref[...] = acc.astype(dx_ref.dtype)
        dcw_ref[...] += jnp.concatenate(rows, axis=0)
        dcb_ref[...] += jnp.sum(dv, axis=0, keepdims=True)

    row = pl.BlockSpec((tt, C), lambda i: (i, 0))
    return pl.pallas_call(
        body, grid=(nb,),
        in_specs=[row, pl.BlockSpec((HALO, C), lambda i: (jnp.minimum((i + 1) * per, T // HALO - 1), 0)),
                  row, pl.BlockSpec((HALO, C), lambda i: (jnp.maximum(i * per - 1, 0), 0)),
                  pl.BlockSpec((K, C), lambda i: (0, 0))],
        out_specs=[row, pl.BlockSpec((K, C), lambda i: (0, 0)), pl.BlockSpec((1, C), lambda i: (0, 0))],
        out_shape=[jax.ShapeDtypeStruct((T, C), BF16), jax.ShapeDtypeStruct((K, C), F32),
                   jax.ShapeDtypeStruct((1, C), F32)],
        compiler_params=_cp("arbitrary"), name=name,
    )(d, d, x0, x0, cw)


def ffn_act_fwd(u0, cw, cb, *, n, name):
    T, W = u0.shape
    G = W // (2 * n)
    tt = _tile(T, 256, HALO)
    per = tt // HALO

    K = cw.shape[0]

    def body(u_ref, up_ref, cw_ref, cb_ref, a_ref, uo_ref):
        step = pl.program_id(1)
        x = u_ref[...].astype(F32)
        prev8 = jnp.where(step > 0, up_ref[...].astype(F32)[HALO - 8:], 0.0)
        acc = _tapmul(x, cw_ref[K - 1])
        for s in range(1, K):
            acc = acc + _tapmul(_shift_down(x, s, prev8), cw_ref[K - 1 - s])
        u = (_rows8(acc) + cb_ref[...][None]).reshape(tt, 2 * n)
        uo_ref[...] = u.astype(uo_ref.dtype)
        a_ref[...] = (_gelu(u[:, :n]) * u[:, n:]).astype(a_ref.dtype)

    return pl.pallas_call(
        body, grid=(G, T // tt),
        in_specs=[pl.BlockSpec((tt, 2 * n), lambda j, i: (i, j)),
                  pl.BlockSpec((HALO, 2 * n), lambda j, i: (jnp.maximum(i * per - 1, 0), j)),
                  pl.BlockSpec((K, 8, 2 * n), lambda j, i: (0, 0, j)),
                  pl.BlockSpec((8, 2 * n), lambda j, i: (0, j))],
        out_specs=[pl.BlockSpec((tt, n), lambda j, i: (i, j)), pl.BlockSpec((tt, 2 * n), lambda j, i: (i, j))],
        out_shape=[jax.ShapeDtypeStruct((T, G * n), BF16), jax.ShapeDtypeStruct((T, W), BF16)],
        compiler_params=_cp("parallel", "parallel"), name=name)(u0, u0, _rep8(cw), _rep8(cb[0]))


def ffn_act_bwd(u0, u, da, cw, *, n, name):
    T, W = u0.shape
    G = W // (2 * n)
    K = cw.shape[0]
    tt = _tile(T, 128, HALO)
    per = tt // HALO
    nb = T // tt

    def body(x_ref, u_ref, un_ref, da_ref, dan_ref, cw_ref, du_ref, dcw_ref, dcb_ref, dcw_acc, dcb_acc):
        step = pl.program_id(1)

        @pl.when(step == 0)
        def _():
            dcw_acc[...] = jnp.zeros_like(dcw_acc)
            dcb_acc[...] = jnp.zeros_like(dcb_acc)

        def d_conv_out(uf, daf):
            gl, dgl = _gelu_and_grad(uf[:, :n])
            return jnp.concatenate([daf * uf[:, n:] * dgl, daf * gl], axis=1)

        du = d_conv_out(u_ref[...].astype(F32), da_ref[...].astype(F32))
        next8 = d_conv_out(un_ref[...].astype(F32)[0:8],
                           jnp.where(step < nb - 1, dan_ref[...].astype(F32)[0:8], 0.0))
        xt = x_ref[...].astype(F32)
        acc = _tapmul(du, cw_ref[K - 1])
        dcw_acc[K - 1] += _colsum8(du * xt)
        dcb_acc[...] += _colsum8(du)
        for sh in range(1, K):
            dsh = _shift_up(du, sh, next8)
            acc = acc + _tapmul(dsh, cw_ref[K - 1 - sh])
            dcw_acc[K - 1 - sh] += _colsum8(dsh * xt)
        du_ref[...] = acc.astype(du_ref.dtype)

        @pl.when(step == nb - 1)
        def _():
            dcw_ref[...] = jnp.sum(dcw_acc[...], axis=1)
            dcb_ref[...] = jnp.sum(dcb_acc[...], axis=0, keepdims=True)

    last = T // HALO - 1
    nxt = lambda j, i: (jnp.minimum((i + 1) * per, last), j)
    return pl.pallas_call(
        body, grid=(G, nb),
        in_specs=[pl.BlockSpec((tt, 2 * n), lambda j, i: (i, j)),
                  pl.BlockSpec((tt, 2 * n), lambda j, i: (i, j)),
                  pl.BlockSpec((HALO, 2 * n), nxt),
                  pl.BlockSpec((tt, n), lambda j, i: (i, j)),
                  pl.BlockSpec((HALO, n), nxt),
                  pl.BlockSpec((K, 8, 2 * n), lambda j, i: (0, 0, j))],
        out_specs=[pl.BlockSpec((tt, 2 * n), lambda j, i: (i, j)),
                   pl.BlockSpec((K, 2 * n), lambda j, i: (0, j)),
                   pl.BlockSpec((1, 2 * n), lambda j, i: (0, j))],
        out_shape=[jax.ShapeDtypeStruct((T, W), BF16), jax.ShapeDtypeStruct((K, W), F32),
                   jax.ShapeDtypeStruct((1, W), F32)],
        scratch_shapes=[pltpu.VMEM((K, 8, 2 * n), F32), pltpu.VMEM((8, 2 * n), F32)],
        compiler_params=_cp("parallel", "arbitrary"), name=name,
    )(u0, u, u, da, da, _rep8(cw))


def _ffn_shard_of_block(q):
    return (q % 2) * (N_DEV // 2) + q // 2


def _ffn_block_of_shard(s):
    return (s % (N_DEV // 2)) * 2 + s // (N_DEV // 2)


def _group_cols(v):
    lead = v.shape[:-1]
    n = v.shape[-1] // N_DEV
    return jnp.swapaxes(v.reshape(lead + (2, N_DEV // 2, n)), -3, -2).reshape(v.shape)


def _ungroup_cols(v):
    lead = v.shape[:-1]
    n = v.shape[-1] // N_DEV
    return jnp.swapaxes(v.reshape(lead + (N_DEV // 2, 2, n)), -3, -2).reshape(v.shape)


def local_step(x, target, w, plan=None):
    depth = w["mix_norm"].shape[0]
    n_up = w["ffn_conv_w"].shape[-1] // N_DEV
    g = {k: [None] * (w[k].shape[0] if hasattr(w[k], "shape") else len(w[k])) for k in w}

    def run(fn, stage, l, slot, *args, **kw):
        carry = None if plan is None else plan.carry(stage, l, slot, g)
        if carry is None:
            return fn(*args, **kw)
        out, extra = fn(*args, carry=carry, **kw)
        plan.done(stage, l, slot, extra)
        return out

    saved = []
    for l in range(depth):
        j = l // 2
        h = rmsnorm_fwd(x, w["mix_norm"][l:l + 1], name="mix_norm_fwd")
        if l % 2 == 0:
            qkv = run(mm_nn, "fwd", l, "in", h, w["attn_w_qkv"][j], out_dtype=F32, name="qkv_proj")
            ao = run(attn_fwd, "fwd", l, "attn", qkv, w["attn_q_gain"][j:j + 1], w["attn_k_gain"][j:j + 1],
                     w["attn_sinks"][j:j + 1], name="attn_fwd")
            x1 = run(mm_nn, "fwd", l, "out", ao, w["attn_w_o"][j], out_dtype=F32, res=x, name="attn_out_proj")
            mix = (qkv, ao)
        else:
            z = run(mm_nn, "fwd", l, "in", h, w["rec_w_in"][j], out_dtype=F32, name="rec_in_proj")
            xb, r, ig, a, hs, hprev, hg = rec_fwd(
                z, w["rec_conv_w"][j], w["rec_conv_b"][j:j + 1], w["rec_w_a"][j], w["rec_b_a"][j:j + 1],
                w["rec_w_i"][j], w["rec_b_i"][j:j + 1], w["rec_lambda"][j:j + 1], name="rec_fwd")
            x1 = run(mm_nn, "fwd", l, "out", hg, w["rec_w_out"][j], out_dtype=F32, res=x, name="rec_out_proj")
            mix = (z, xb, r, ig, a, hs, hprev, hg)
        h2 = rmsnorm_fwd(x1, w["ffn_norm"][l:l + 1], name="ffn_norm_fwd")
        u0 = run(mm_nn, "fwd", l, "up", h2, w["ffn_w_up"][l], out_dtype=BF16, name="ffn_up_proj")
        act, u = ffn_act_fwd(u0, w["ffn_conv_w"][l], w["ffn_conv_b"][l:l + 1], n=n_up, name="ffn_act_fwd")
        x2 = run(mm_nn, "fwd", l, "down", act, w["ffn_w_down"][l], out_dtype=F32, res=x1, name="ffn_down_proj")
        saved.append((x, h, mix, x1, h2, u0, u, act))
        x = x2

    loss_vec, dx = loss_head(x, target, name="loss_head")

    for l in reversed(range(depth)):
        j = l // 2
        x0, h, mix, x1, h2, u0, u, act = saved[l]
        dact = mm_nt(dx, w["ffn_w_down"][l], out_dtype=BF16, name="ffn_down_dx")
        g["ffn_w_down"][l] = mm_tn(act, dx, out_dtype=BF16, name="ffn_down_dw")
        du0, dcw, dcb = ffn_act_bwd(u0, u, dact, w["ffn_conv_w"][l], n=n_up, name="ffn_act_bwd")
        g["ffn_conv_w"][l], g["ffn_conv_b"][l] = _ungroup_cols(dcw), _ungroup_cols(dcb)[0]
        g["ffn_w_up"][l] = run(mm_tn, "bwd", l, "up_dw", h2, du0, out_dtype=BF16, col_shards=N_DEV, shard_of_block=_ffn_shard_of_block,
                                 name="ffn_up_dw")
        dh2 = run(mm_nt, "bwd", l, "up_dx", du0, w["ffn_w_up"][l], out_dtype=F32, name="ffn_up_dx")
        dx1, dgf = rmsnorm_bwd(x1, w["ffn_norm"][l:l + 1], dh2, dx, name="ffn_norm_bwd")
        g["ffn_norm"][l] = dgf[0]
        if l % 2 == 0:
            qkv, ao = mix
            dao = mm_nt(dx1, w["attn_w_o"][j], out_dtype=BF16, name="attn_out_dx")
            g["attn_w_o"][j] = mm_tn(ao, dx1, out_dtype=BF16, name="attn_out_dw")
            dq, dk, dv, dqg, dkg, dsk = attn_bwd(qkv, dao, w["attn_q_gain"][j:j + 1], w["attn_k_gain"][j:j + 1],
                                                 w["attn_sinks"][j:j + 1], name="attn_bwd")
            g["attn_q_gain"][j], g["attn_k_gain"][j], g["attn_sinks"][j] = dqg[0], dkg[0], dsk
            dqkv = jnp.concatenate([dq, dk, dv], axis=1)
            g["attn_w_qkv"][j] = run(mm_tn, "bwd", l, "in_dw", h, dqkv, out_dtype=BF16, col_shards=N_DEV, name="qkv_dw")
            dh = run(mm_nt, "bwd", l, "in_dx", dqkv, w["attn_w_qkv"][j], out_dtype=F32, name="qkv_dx")
        else:
            z, xb, r, ig, a, hs, hprev, hg = mix
            dhg = mm_nt(dx1, w["rec_w_out"][j], out_dtype=BF16, name="rec_out_dx")
            g["rec_w_out"][j] = mm_tn(hg, dx1, out_dtype=BF16, name="rec_out_dw")
            dyb, ybar = rec_bwd_scan(dhg, hs, a, z, name="rec_bwd_scan")
            dxb, dwa, dwi, dba, dbi, dlam = rec_bwd_gates(
                ybar, hprev, a, r, ig, xb, w["rec_lambda"][j:j + 1], w["rec_w_a"][j], w["rec_w_i"][j],
                name="rec_bwd_gates")
            dxb0, dcw, dcb = conv_bwd(dxb, z, w["rec_conv_w"][j], name="rec_conv_bwd")
            g["rec_w_a"][j], g["rec_w_i"][j] = dwa, dwi
            g["rec_b_a"][j], g["rec_b_i"][j], g["rec_lambda"][j] = dba[0], dbi[0], dlam[0]
            g["rec_conv_w"][j], g["rec_conv_b"][j] = dcw, dcb[0]
            dz = jnp.concatenate([dxb0, dyb], axis=1)
            g["rec_w_in"][j] = run(mm_tn, "bwd", l, "in_dw", h, dz, out_dtype=BF16, col_shards=N_DEV, name="rec_in_dw")
            dh = run(mm_nt, "bwd", l, "in_dx", dz, w["rec_w_in"][j], out_dtype=F32, name="rec_in_dx")
        dx, dgm = rmsnorm_bwd(x0, w["mix_norm"][l:l + 1], dh, dx1, name="mix_norm_bwd")
        g["mix_norm"][l] = dgm[0]
    return loss_vec, dx, g


HBM = pl.BlockSpec(memory_space=pltpu.HBM)
N_PEER = N_DEV - 1


def _here():
    return lax.axis_index("x"), lax.axis_index("y"), lax.axis_index("c")


def _sid(dev):
    return 4 * dev[0] + 2 * dev[1] + dev[2]


def _exchange_sems(n):
    return [pltpu.SemaphoreType.DMA((n * N_PEER,)), pltpu.SemaphoreType.DMA((n * N_PEER,)),
            pltpu.SemaphoreType.DMA((n,))]


def gather_exchange(shards, out_structs, windows):
    n = len(shards)

    def parts(outs, sems):
        send_sems, recv_sems, _ = sems
        x, y, c = _here()
        me, sib = (x, y, c), (x, y, 1 - c)
        chips = [(1 - x, y), (x, 1 - y), (1 - x, 1 - y)]

        def copy(i, k, block, to, src=None):
            dst = windows[i](outs[i], _sid(block))
            return pltpu.make_async_remote_copy(
                src_ref=dst if src is None else src, dst_ref=dst,
                send_sem=send_sems.at[i * N_PEER + k], recv_sem=recv_sems.at[i * N_PEER + k],
                device_id=to, device_id_type=MESH)

        return me, sib, chips, c, copy

    def own_copies(ins, outs, sems):
        me, sib, chips, c, copy = parts(outs, sems)
        local = [pltpu.make_async_copy(ins[i], windows[i](outs[i], _sid(me)), sems[2].at[i]) for i in range(n)]
        first = []
        for i in range(n):
            first.append(copy(i, 0, me, sib, src=ins[i]))
            first += [copy(i, 1 + j, me, (*chip, c), src=ins[i]) for j, chip in enumerate(chips)]
        return local, first

    def start(ins, outs, sems):
        local, first = own_copies(ins, outs, sems)
        for cp in local + first:
            cp.start()

    def finish(ins, outs, sems):
        me, sib, chips, c, copy = parts(outs, sems)
        local, first = own_copies(ins, outs, sems)
        passed = []
        for i in range(n):
            for j, chip in enumerate(chips):
                copy(i, 1 + j, (*chip, c), me).wait_recv()
                fwd = copy(i, 4 + j, (*chip, c), sib)
                fwd.start()
                passed.append(fwd)
        for i in range(n):
            copy(i, 0, sib, me).wait_recv()
            for j, chip in enumerate(chips):
                copy(i, 4 + j, (*chip, 1 - c), me).wait_recv()
        for cp in first + passed:
            cp.wait_send()
        for cp in local:
            cp.wait()

    return dict(arrays=list(shards), out_structs=list(out_structs), sems=_exchange_sems(n), start=start,
                finish=finish)


def run_exchange(ex, *, name):
    n = len(ex["arrays"])

    def body(*refs):
        ins, outs, sems = refs[:n], refs[n:2 * n], refs[2 * n:]
        ex["start"](ins, outs, sems)
        ex["finish"](ins, outs, sems)

    return pl.pallas_call(
        body, in_specs=[HBM] * n, out_specs=[HBM] * n, out_shape=ex["out_structs"], scratch_shapes=ex["sems"],
        name=name)(*ex["arrays"])


def scatter_exchange(grads):
    n = len(grads)

    def parts(ins, outs, sems):
        send_sems, recv_sems, local_sems = sems
        x, y, c = _here()
        me = (x, y, c)
        peers = []
        for k in range(1, N_DEV):
            kx, ky, kc = (k >> 2) & 1, (k >> 1) & 1, k & 1
            peers.append((1 - x if kx else x, 1 - y if ky else y, 1 - c if kc else c))

        def copy(i, k):
            return pltpu.make_async_remote_copy(
                src_ref=ins[i].at[_sid(peers[k])], dst_ref=outs[i].at[_sid(me)],
                send_sem=send_sems.at[i * N_PEER + k], recv_sem=recv_sems.at[i * N_PEER + k],
                device_id=peers[k], device_id_type=MESH)

        def arrival(i, k):
            return pltpu.make_async_remote_copy(
                src_ref=ins[i].at[_sid(me)], dst_ref=outs[i].at[_sid(peers[k])],
                send_sem=send_sems.at[i * N_PEER + k], recv_sem=recv_sems.at[i * N_PEER + k],
                device_id=peers[k], device_id_type=MESH)

        local = [pltpu.make_async_copy(ins[i].at[_sid(me)], outs[i].at[_sid(me)], local_sems.at[i]) for i in range(n)]
        sends = [copy(i, k) for i in range(n) for k in range(N_PEER)]
        return local, sends, arrival

    def start(ins, outs, sems):
        local, sends, _ = parts(ins, outs, sems)
        for cp in local + sends:
            cp.start()

    def finish(ins, outs, sems):
        local, sends, arrival = parts(ins, outs, sems)
        for i in range(n):
            for k in range(N_PEER):
                arrival(i, k).wait_recv()
        for cp in sends:
            cp.wait_send()
        for cp in local:
            cp.wait()

    return dict(arrays=list(grads), out_structs=[jax.ShapeDtypeStruct(g.shape, g.dtype) for g in grads],
                sems=_exchange_sems(n), start=start, finish=finish)


def adamw_family(contribs, w, m, v, *, name):
    L, R, C = w.shape
    S = contribs[0].shape[0]
    tr = _tile(R, max(8, (1 << 20) // (C * S)), 8)
    nr = R // tr
    c1 = 1.0 / (1.0 - ADAM_B1 ** ADAM_STEP)
    c2 = 1.0 / (1.0 - ADAM_B2 ** ADAM_STEP)

    def body(*refs):
        c_refs = refs[:L]
        w_ref, m_ref, v_ref, g_ref, d_ref, nm_ref, nv_ref = refs[L:]
        layer = pl.program_id(0)
        for l in range(L):
            @pl.when(layer == l)
            def _(l=l):
                g = c_refs[l][0].astype(F32)
                for s in range(1, S):
                    g = g + c_refs[l][s].astype(F32)
                mm = ADAM_B1 * m_ref[...] + (1.0 - ADAM_B1) * g
                vv = ADAM_B2 * v_ref[...] + (1.0 - ADAM_B2) * (g * g)
                g_ref[...] = g
                nm_ref[...] = mm
                nv_ref[...] = vv
                d_ref[...] = -ADAM_LR * ((mm * c1) / (jnp.sqrt(vv * c2) + ADAM_EPS) + ADAM_WD * w_ref[...])

    def cspec(l):
        return pl.BlockSpec((S, tr, C), lambda ll, i, l=l: (0, jnp.where(ll == l, i, 0), 0))

    lay = pl.BlockSpec((None, tr, C), lambda ll, i: (ll, i, 0))
    return pl.pallas_call(
        body, grid=(L, nr), in_specs=[cspec(l) for l in range(L)] + [lay] * 3, out_specs=[lay] * 4,
        out_shape=[jax.ShapeDtypeStruct((L, R, C), F32)] * 4,
        compiler_params=_cp("arbitrary", "arbitrary"), name=name)(*contribs, w, m, v)


def sum_slots(a, *, name):
    S, R, C = a.shape
    tr = _tile(R, 256, 8)

    def body(a_ref, o_ref):
        t = a_ref[0]
        for s in range(1, S):
            t = t + a_ref[s]
        o_ref[...] = t

    return pl.pallas_call(
        body, grid=(R // tr,), in_specs=[pl.BlockSpec((S, tr, C), lambda i: (0, i, 0))],
        out_specs=pl.BlockSpec((tr, C), lambda i: (i, 0)), out_shape=jax.ShapeDtypeStruct((R, C), F32),
        compiler_params=_cp("parallel"), name=name)(a)


LANES = 128


def _pack(arrs):
    flat = jnp.concatenate([a.reshape(-1).astype(F32) for a in arrs])
    rows = -(-flat.shape[0] // LANES)
    rows = -(-rows // 256) * 256
    return jnp.pad(flat, (0, rows * LANES - flat.shape[0])).reshape(rows, LANES)


def _unpack(buf, shapes):
    flat = buf.reshape(-1)
    out, off = [], 0
    for shp in shapes:
        size = int(np.prod(shp))
        out.append(flat[off:off + size].reshape(shp))
        off += size
    return out


def _gather_last(g):
    t = jnp.moveaxis(g, 0, -2)
    return t.reshape(t.shape[:-2] + (t.shape[-2] * t.shape[-1],))


def _own_last(full, s):
    n = full.shape[-1] // N_DEV
    t = full.reshape(full.shape[:-1] + (N_DEV, n))
    return lax.dynamic_index_in_dim(t, s, axis=t.ndim - 2, keepdims=False)


BIG = ["attn_w_qkv", "attn_w_o", "rec_w_in", "rec_w_out", "ffn_w_up", "ffn_w_down", "rec_w_a", "rec_w_i"]
SMALL_REPLICATED = ["mix_norm", "ffn_norm", "attn_q_gain", "attn_k_gain", "attn_sinks", "ffn_conv_b"]
SMALL_SHARDED = ["rec_conv_w", "rec_conv_b", "rec_b_a", "rec_b_i", "rec_lambda", "ffn_conv_w"]
SMALL = SMALL_REPLICATED + SMALL_SHARDED
WEIGHTS = ["mix_norm", "ffn_norm", "attn_w_qkv", "attn_q_gain", "attn_k_gain", "attn_sinks", "attn_w_o", "rec_w_in",
           "rec_conv_w", "rec_conv_b", "rec_w_a", "rec_b_a", "rec_w_i", "rec_b_i", "rec_lambda", "rec_w_out",
           "ffn_w_up", "ffn_conv_w", "ffn_conv_b", "ffn_w_down"]


def _col_window(n, block_of_shard=None):
    def win(ref, s):
        q = s if block_of_shard is None else block_of_shard(s)
        return ref.at[:, pl.ds(pl.multiple_of(q * n, 128), n)]
    return win


def _row_window(r):
    return lambda ref, s: ref.at[pl.ds(pl.multiple_of(s * r, 16), r), :]


def _gate_window(r):
    return lambda ref, s: ref.at[:, pl.ds(pl.multiple_of(s * r, 16), r), :]


def _slot_window(ref, s):
    return ref.at[s]


def _idx(name, layer):
    return layer if name.startswith("ffn") else layer // 2


def _slot_names(layer, slot):
    attn = layer % 2 == 0
    return {"in": ["attn_w_qkv"] if attn else ["rec_w_in"],
            "out": ["attn_w_o"] if attn else ["rec_w_out", "rec_w_a", "rec_w_i"],
            "up": ["ffn_w_up"], "down": ["ffn_w_down"]}[slot]


def _gather_for(p, items):
    shards, structs, wins = [], [], []
    for nme, layer in items:
        sh = p[nme][_idx(nme, layer)].astype(BF16)
        if nme in ("attn_w_qkv", "rec_w_in", "ffn_w_up"):
            K, n = sh.shape
            structs.append(jax.ShapeDtypeStruct((K, n * N_DEV), BF16))
            wins.append(_col_window(n, _ffn_block_of_shard if nme == "ffn_w_up" else None))
        elif nme in ("rec_w_a", "rec_w_i"):
            nblk, r, bd = sh.shape
            structs.append(jax.ShapeDtypeStruct((nblk, r * N_DEV, bd), BF16))
            wins.append(_gate_window(r))
        else:
            r, N = sh.shape
            structs.append(jax.ShapeDtypeStruct((r * N_DEV, N), BF16))
            wins.append(_row_window(r))
        shards.append(sh)
    return gather_exchange(shards, structs, wins)


def _send_layout(name, t):
    if name in ("rec_w_a", "rec_w_i"):
        nblk, bd, _ = t.shape
        return jnp.transpose(t.reshape(nblk, N_DEV, bd // N_DEV, bd), (1, 0, 2, 3)).astype(BF16)
    if name in ("attn_w_o", "rec_w_out", "ffn_w_down"):
        return t.reshape((N_DEV, t.shape[0] // N_DEV) + t.shape[1:])
    return t


class _Plan:
    BWD_SLOT = {"up_dw": "down", "up_dx": "up", "in_dw": "out", "in_dx": "in"}

    def __init__(self, p, w, contribs, depth):
        self.p, self.w, self.contribs, self.depth = p, w, contribs, depth
        self.pending = None

    def carry(self, stage, l, slot, g):
        if stage == "fwd":
            items = []
            if l == 0 and slot == "in":
                items += [(k, 0) for k in _slot_names(0, "out")]
            if l == 0 and slot == "attn":
                items += [(k, 0) for s in ("up", "down") for k in _slot_names(0, s)]
            if slot != "attn" and l + 1 < self.depth:
                items += [(k, l + 1) for k in _slot_names(l + 1, slot)]
            if not items:
                return None
            self.pending = items
            return _gather_for(self.p, items)
        self.pending = [(k, l) for k in _slot_names(l, self.BWD_SLOT[slot])]
        return scatter_exchange([_send_layout(k, g[k][_idx(k, layer)]) for k, layer in self.pending])

    def done(self, stage, l, slot, outs):
        dst = self.w if stage == "fwd" else self.contribs
        for (k, layer), o in zip(self.pending, outs):
            dst[k][_idx(k, layer)] = o


def _train_step(p, x, target, mom, vel):
    depth = p["mix_norm"].shape[0]
    s_me = _sid(_here())

    w = {k: [None] * p[k].shape[0] for k in BIG}
    first = [(k, 0) for k in _slot_names(0, "in")]
    for (k, _), t in zip(first, run_exchange(_gather_for(p, first), name="all_gather_first")):
        w[k][0] = t

    local_small = [p[k] for k in SMALL_SHARDED]
    packed = _pack(local_small)
    gathered, = run_exchange(
        gather_exchange([packed], [jax.ShapeDtypeStruct((N_DEV,) + packed.shape, F32)], [_slot_window]),
        name="all_gather_small")
    per_dev = [_unpack(gathered[s], [a.shape for a in local_small]) for s in range(N_DEV)]
    for i, k in enumerate(SMALL_SHARDED):
        w[k] = _gather_last(jnp.stack([per_dev[s][i] for s in range(N_DEV)]))
    for k in SMALL_REPLICATED:
        w[k] = p[k]
    nrec = w["rec_b_a"].shape[0]
    w["rec_b_a"] = w["rec_b_a"].reshape(nrec, -1)
    w["rec_b_i"] = w["rec_b_i"].reshape(nrec, -1)
    w["ffn_conv_w"] = _group_cols(w["ffn_conv_w"])
    w["ffn_conv_b"] = _group_cols(w["ffn_conv_b"])

    contribs = {k: [None] * len(w[k]) for k in BIG}
    loss_vec, dx, g = local_step(x[0], target[0], w, _Plan(p, w, contribs, depth))
    loss = lax.psum(jnp.sum(loss_vec), ("x", "y", "c"))

    out = {}
    for k in BIG:
        shp = p[k].shape
        L = shp[0]
        C = shp[-1]
        R = int(np.prod(shp[1:-1]))
        cs = [c.reshape(N_DEV, R, C) for c in contribs[k]]
        res = adamw_family(cs, p[k].reshape(L, R, C), mom[k].reshape(L, R, C), vel[k].reshape(L, R, C),
                           name="adamw_" + k)
        out[k] = [t.reshape(shp) for t in res]

    gsmall = [jnp.stack(g[k]) for k in SMALL]
    gp = _pack(gsmall)
    gall, = run_exchange(
        gather_exchange([gp], [jax.ShapeDtypeStruct((N_DEV,) + gp.shape, F32)], [_slot_window]),
        name="all_gather_small_grads")
    gsum = _unpack(sum_slots(gall, name="sum_small_grads"), [a.shape for a in gsmall])
    glocal = []
    for k, t in zip(SMALL, gsum):
        if k in SMALL_SHARDED:
            t = _own_last(t.reshape(p[k].shape[:-1] + (p[k].shape[-1] * N_DEV,)), s_me)
        glocal.append(t.reshape(p[k].shape))
    wp, mp, vp, gpk = (_pack([d[k] for k in SMALL]) for d in (p, mom, vel, dict(zip(SMALL, glocal))))
    res = adamw_family([gpk[None]], wp[None], mp[None], vp[None], name="adamw_small")
    shapes = [p[k].shape for k in SMALL]
    unp = [_unpack(t[0], shapes) for t in res]
    for i, k in enumerate(SMALL):
        out[k] = [glocal[i], unp[1][i], unp[2][i], unp[3][i]]

    return (loss, dx[None]) + tuple(out[k][q] for q in range(4) for k in WEIGHTS)


def kernel(x, mix_norm, ffn_norm, attn_w_qkv, attn_q_gain, attn_k_gain, attn_sinks, attn_w_o, rec_w_in, rec_conv_w, rec_conv_b, rec_w_a, rec_b_a, rec_w_i, rec_b_i, rec_lambda, rec_w_out, ffn_w_up, ffn_conv_w, ffn_conv_b, ffn_w_down, loss_target, m_mix_norm, m_ffn_norm, m_attn_w_qkv, m_attn_q_gain, m_attn_k_gain, m_attn_sinks, m_attn_w_o, m_rec_w_in, m_rec_conv_w, m_rec_conv_b, m_rec_w_a, m_rec_b_a, m_rec_w_i, m_rec_b_i, m_rec_lambda, m_rec_w_out, m_ffn_w_up, m_ffn_conv_w, m_ffn_conv_b, m_ffn_w_down, v_mix_norm, v_ffn_norm, v_attn_w_qkv, v_attn_q_gain, v_attn_k_gain, v_attn_sinks, v_attn_w_o, v_rec_w_in, v_rec_conv_w, v_rec_conv_b, v_rec_w_a, v_rec_b_a, v_rec_w_i, v_rec_b_i, v_rec_lambda, v_rec_w_out, v_ffn_w_up, v_ffn_conv_w, v_ffn_conv_b, v_ffn_w_down):
    p = dict(zip(WEIGHTS, (mix_norm, ffn_norm, attn_w_qkv, attn_q_gain, attn_k_gain, attn_sinks, attn_w_o, rec_w_in,
                           rec_conv_w, rec_conv_b, rec_w_a, rec_b_a, rec_w_i, rec_b_i, rec_lambda, rec_w_out,
                           ffn_w_up, ffn_conv_w, ffn_conv_b, ffn_w_down)))
    mom = dict(zip(WEIGHTS, (m_mix_norm, m_ffn_norm, m_attn_w_qkv, m_attn_q_gain, m_attn_k_gain, m_attn_sinks,
                             m_attn_w_o, m_rec_w_in, m_rec_conv_w, m_rec_conv_b, m_rec_w_a, m_rec_b_a, m_rec_w_i,
                             m_rec_b_i, m_rec_lambda, m_rec_w_out, m_ffn_w_up, m_ffn_conv_w, m_ffn_conv_b,
                             m_ffn_w_down)))
    vel = dict(zip(WEIGHTS, (v_mix_norm, v_ffn_norm, v_attn_w_qkv, v_attn_q_gain, v_attn_k_gain, v_attn_sinks,
                             v_attn_w_o, v_rec_w_in, v_rec_conv_w, v_rec_conv_b, v_rec_w_a, v_rec_b_a, v_rec_w_i,
                             v_rec_b_i, v_rec_lambda, v_rec_w_out, v_ffn_w_up, v_ffn_conv_w, v_ffn_conv_b,
                             v_ffn_w_down)))
    return _train_step(p, x, loss_target, mom, vel)
```

```python
import functools
import math

import jax
import jax.numpy as jnp
import numpy as np
from jax import lax
from jax.experimental import pallas as pl
from jax.experimental.pallas import tpu as pltpu

F32 = jnp.float32
BF16 = jnp.bfloat16

N_DEV = 8
HEAD_DIM = 64
GROUP = 4
BLOCK = 128
LRU_C = 8.0
EPS = 1e-6
HALO = 16
ADAM_LR, ADAM_B1, ADAM_B2, ADAM_EPS, ADAM_WD, ADAM_STEP = 0.001, 0.9, 0.999, 1e-08, 0.01, 10
VMEM_LIMIT = 56 * 1024 * 1024
MESH = pl.DeviceIdType.MESH
GELU_C = math.sqrt(2.0 / math.pi)


def _cp(*sem, vmem=VMEM_LIMIT):
    return pltpu.CompilerParams(dimension_semantics=tuple(sem), vmem_limit_bytes=vmem)


def _tile(dim, pref, mult=128):
    if dim <= pref:
        return dim
    t = (pref // mult) * mult
    while t >= mult:
        if dim % t == 0:
            return t
        t -= mult
    return dim


def _gelu(x):
    th = jnp.tanh(GELU_C * (x + 0.044715 * x * x * x))
    return 0.5 * x * (1.0 + th)


def _gelu_and_grad(x):
    x2 = x * x
    th = jnp.tanh(GELU_C * (x + 0.044715 * x2 * x))
    g = 0.5 * x * (1.0 + th)
    dg = 0.5 * (1.0 + th) + 0.5 * x * (1.0 - th * th) * GELU_C * (1.0 + 3.0 * 0.044715 * x2)
    return g, dg


def _dot(a, b, dims):
    return lax.dot_general(a.astype(BF16), b.astype(BF16), (dims, ((), ())), preferred_element_type=F32)


NN = ((1,), (0,))
NT = ((1,), (1,))
TN = ((0,), (0,))


def _matmul(a, b, *, dims, grid, a_spec, b_spec, o_spec, out_shape, acc_shape, res=None, res_spec=None,
            carry=None, name):
    ni, nj, nk = grid
    nres = 0 if res is None else 1
    ncar = 0 if carry is None else len(carry["arrays"])

    def body(*refs):
        a_ref, b_ref = refs[0], refs[1]
        r_ref = refs[2] if nres else None
        car_in = refs[2 + nres:2 + nres + ncar]
        o_ref = refs[2 + nres + ncar]
        car_out = refs[3 + nres + ncar:3 + nres + 2 * ncar]
        scratch = refs[3 + nres + 2 * ncar:]
        i, j, k = pl.program_id(0), pl.program_id(1), pl.program_id(2)

        if carry is not None:
            @pl.when((i == 0) & (j == 0) & (k == 0))
            def _():
                carry["start"](car_in, car_out, scratch[1:])

        def finish(acc):
            if r_ref is not None:
                acc = acc + r_ref[...].astype(F32)
            o_ref[...] = acc.astype(o_ref.dtype)

        if nk == 1:
            finish(_dot(a_ref[...], b_ref[...], dims))
        else:
            acc_ref = scratch[0]

            @pl.when(k == 0)
            def _():
                acc_ref[...] = _dot(a_ref[...], b_ref[...], dims)

            if nk > 2:
                @pl.when((k > 0) & (k < nk - 1))
                def _():
                    acc_ref[...] += _dot(a_ref[...], b_ref[...], dims)

            @pl.when(k == nk - 1)
            def _():
                finish(acc_ref[...] + _dot(a_ref[...], b_ref[...], dims))

        if carry is not None:
            @pl.when((i == ni - 1) & (j == nj - 1) & (k == nk - 1))
            def _():
                carry["finish"](car_in, car_out, scratch[1:])

    in_specs = [a_spec, b_spec] + ([res_spec] if nres else [])
    args = (a, b) + ((res,) if nres else ())
    out_specs, out_shapes = [o_spec], [out_shape]
    scratch_shapes = [pltpu.VMEM(acc_shape if nk > 1 else (8, 128), F32)]
    sem = ("parallel", "parallel", "arbitrary")
    if carry is not None:
        in_specs += [HBM] * ncar
        args += tuple(carry["arrays"])
        out_specs += [HBM] * ncar
        out_shapes += list(carry["out_structs"])
        scratch_shapes += carry["sems"]
        sem = ("arbitrary", "arbitrary", "arbitrary")
    outs = pl.pallas_call(
        body, grid=grid, in_specs=in_specs, out_specs=out_specs, out_shape=out_shapes,
        scratch_shapes=scratch_shapes, compiler_params=_cp(*sem), name=name,
    )(*args)
    return outs[0] if carry is None else (outs[0], list(outs[1:]))


def mm_nn(a, b, *, out_dtype, res=None, carry=None, name):
    M, K = a.shape
    N = b.shape[1]
    tm, tn, tk = _tile(M, 1024), _tile(N, 1024), _tile(K, 2048)
    return _matmul(
        a, b, dims=NN, grid=(M // tm, N // tn, K // tk),
        a_spec=pl.BlockSpec((tm, tk), lambda i, j, k: (i, k)),
        b_spec=pl.BlockSpec((tk, tn), lambda i, j, k: (k, j)),
        o_spec=pl.BlockSpec((tm, tn), lambda i, j, k: (i, j)),
        out_shape=jax.ShapeDtypeStruct((M, N), out_dtype), acc_shape=(tm, tn),
        res=res, res_spec=pl.BlockSpec((tm, tn), lambda i, j, k: (i, j)), carry=carry, name=name)


def mm_nt(a, b, *, out_dtype, res=None, carry=None, name):
    M, N = a.shape
    K = b.shape[0]
    tm, tn, tk = _tile(M, 1024), _tile(K, 1024), _tile(N, 2048)
    return _matmul(
        a, b, dims=NT, grid=(M // tm, K // tn, N // tk),
        a_spec=pl.BlockSpec((tm, tk), lambda i, j, k: (i, k)),
        b_spec=pl.BlockSpec((tn, tk), lambda i, j, k: (j, k)),
        o_spec=pl.BlockSpec((tm, tn), lambda i, j, k: (i, j)),
        out_shape=jax.ShapeDtypeStruct((M, K), out_dtype), acc_shape=(tm, tn),
        res=res, res_spec=pl.BlockSpec((tm, tn), lambda i, j, k: (i, j)), carry=carry, name=name)


def mm_tn(a, b, *, out_dtype, col_shards=None, shard_of_block=None, carry=None, name):
    T, K = a.shape
    N = b.shape[1]
    tt = _tile(T, 2048)
    tm = _tile(K, 1024)
    if col_shards is None:
        tn = _tile(N, 1024)
        o_spec = pl.BlockSpec((tm, tn), lambda i, j, k: (i, j))
        out_shape = jax.ShapeDtypeStruct((K, N), out_dtype)
    else:
        n = N // col_shards
        tn = _tile(n, 1536)
        per = n // tn
        sob = shard_of_block if shard_of_block is not None else (lambda s: s)
        o_spec = pl.BlockSpec((None, tm, tn), lambda i, j, k: (sob(j // per), i, j % per))
        out_shape = jax.ShapeDtypeStruct((col_shards, K, n), out_dtype)
    return _matmul(
        a, b, dims=TN, grid=(K // tm, N // tn, T // tt),
        a_spec=pl.BlockSpec((tt, tm), lambda i, j, k: (k, i)),
        b_spec=pl.BlockSpec((tt, tn), lambda i, j, k: (k, j)),
        o_spec=o_spec, out_shape=out_shape, acc_shape=(tm, tn), carry=carry, name=name)


def rmsnorm_fwd(x, g, *, name):
    T, D = x.shape
    tm = _tile(T, 512, 8)

    def body(x_ref, g_ref, o_ref):
        xv = x_ref[...]
        r = lax.rsqrt(jnp.mean(xv * xv, axis=-1, keepdims=True) + EPS)
        o_ref[...] = (xv * r * g_ref[...]).astype(o_ref.dtype)

    return pl.pallas_call(
        body, grid=(T // tm,),
        in_specs=[pl.BlockSpec((tm, D), lambda i: (i, 0)), pl.BlockSpec((1, D), lambda i: (0, 0))],
        out_specs=pl.BlockSpec((tm, D), lambda i: (i, 0)),
        out_shape=jax.ShapeDtypeStruct((T, D), BF16), compiler_params=_cp("parallel"), name=name)(x, g)


def rmsnorm_bwd(x, g, dh, dres, *, name):
    T, D = x.shape
    tm = _tile(T, 512, 8)

    def body(x_ref, g_ref, dh_ref, dres_ref, dx_ref, dg_ref):
        @pl.when(pl.program_id(0) == 0)
        def _():
            dg_ref[...] = jnp.zeros_like(dg_ref)

        xv = x_ref[...]
        dh_v = dh_ref[...].astype(F32)
        r = lax.rsqrt(jnp.mean(xv * xv, axis=-1, keepdims=True) + EPS)
        u = dh_v * g_ref[...]
        dot = jnp.mean(u * xv, axis=-1, keepdims=True)
        dx_ref[...] = dres_ref[...] + r * u - xv * (r * r * r * dot)
        dg_ref[...] += jnp.sum(dh_v * xv * r, axis=0, keepdims=True)

    row = pl.BlockSpec((tm, D), lambda i: (i, 0))
    vec = pl.BlockSpec((1, D), lambda i: (0, 0))
    return pl.pallas_call(
        body, grid=(T // tm,), in_specs=[row, vec, row, row], out_specs=[row, vec],
        out_shape=[jax.ShapeDtypeStruct((T, D), F32), jax.ShapeDtypeStruct((1, D), F32)],
        compiler_params=_cp("arbitrary"), name=name)(x, g, dh, dres)


def loss_head(y, target, *, name):
    T, D = y.shape
    tm = _tile(T, 512, 8)

    def body(y_ref, t_ref, l_ref, dy_ref):
        @pl.when(pl.program_id(0) == 0)
        def _():
            l_ref[...] = jnp.zeros_like(l_ref)

        e = y_ref[...] - t_ref[...]
        dy_ref[...] = e * (1.0 / D)
        l_ref[...] += jnp.sum(e * e, axis=0, keepdims=True) * (0.5 / D)

    row = pl.BlockSpec((tm, D), lambda i: (i, 0))
    vec = pl.BlockSpec((1, D), lambda i: (0, 0))
    return pl.pallas_call(
        body, grid=(T // tm,), in_specs=[row, row], out_specs=[vec, row],
        out_shape=[jax.ShapeDtypeStruct((1, D), F32), jax.ShapeDtypeStruct((T, D), F32)],
        compiler_params=_cp("arbitrary"), name=name)(y, target)


NEG = -1e30


def _kv_heads_per_step(hkv):
    return 4 if hkv % 4 == 0 else 2


LANE = 128


def _hi_lo_dot(x, w):
    hi = x.astype(BF16)
    lo = x - hi.astype(F32)
    return _dot(hi, w, NN) + _dot(lo, w, NN)


def _head_sum_matrix():
    r = lax.broadcasted_iota(jnp.int32, (LANE, LANE), 0) // HEAD_DIM
    c = lax.broadcasted_iota(jnp.int32, (LANE, LANE), 1) // HEAD_DIM
    return jnp.where(r == c, 1.0, 0.0).astype(BF16)


def _chunk(x, c):
    return x[:, c * LANE:(c + 1) * LANE]


def _head_sums(x, e):
    return jnp.concatenate([_hi_lo_dot(_chunk(x, c), e) for c in range(x.shape[1] // LANE)], axis=1)


def _row_sums(x):
    return _hi_lo_dot(x, jnp.ones((x.shape[1], LANE), BF16))


def _headnorm(x, e):
    r = lax.rsqrt(_head_sums(x * x, e) * (1.0 / HEAD_DIM) + EPS)
    return x * r, r


def _attn_mask(n):
    qi = lax.broadcasted_iota(jnp.int32, (BLOCK, 2 * BLOCK), 0)
    kj = lax.broadcasted_iota(jnp.int32, (BLOCK, 2 * BLOCK), 1)
    rel = qi + BLOCK - kj
    m = (rel >= 0) & (rel < BLOCK) & ((kj >= BLOCK) | (n > 0))
    return jnp.concatenate([m] * GROUP, axis=0)


def _qk_operands(qhat, khat, j, lo_half):
    kb = jnp.where(lo_half == (j % 2 == 0), _chunk(khat, j // 2), 0.0)
    kd = kb + pltpu.roll(kb, HEAD_DIM, 1)
    q4 = jnp.concatenate(
        [jnp.where(lo_half == (g % 2 == 0), _chunk(qhat, 2 * j + g // 2), 0.0) for g in range(GROUP)], axis=0)
    return q4, kd


def _attn_softmax(s, sink_ref, mask4, head0):
    s = s * (HEAD_DIM ** -0.5)
    sink = jnp.concatenate([jnp.full((BLOCK, 1), sink_ref[0, head0 + g], F32) for g in range(GROUP)], axis=0)
    m = jnp.maximum(jnp.max(jnp.where(mask4, s, NEG), axis=-1, keepdims=True), sink)
    ex = jnp.where(mask4, jnp.exp(s - m), 0.0)
    es = jnp.exp(sink - m)
    inv = 1.0 / (_row_sums(ex) + es)
    return ex * jnp.concatenate([inv] * (s.shape[1] // LANE), axis=1), es * inv[:, 0:1]


def attn_fwd(qkv, q_gain, k_gain, sinks, *, carry=None, name):
    ncar = 0 if carry is None else len(carry["arrays"])
    T, W = qkv.shape
    hq = W // HEAD_DIM * GROUP // (GROUP + 2)
    dq = hq * HEAD_DIM
    PAIR = _kv_heads_per_step(hq // GROUP)
    QW, KW = PAIR * GROUP * HEAD_DIM, PAIR * HEAD_DIM
    npair = hq // (GROUP * PAIR)
    nb = T // BLOCK
    k0 = dq // KW
    v0 = k0 + npair

    def body(q_ref, kp_ref, kc_ref, vp_ref, vc_ref, qg_ref, kg_ref, sink_ref, *rest):
        car_in, o_ref, car_out, sems = rest[:ncar], rest[ncar], rest[ncar + 1:2 * ncar + 1], rest[2 * ncar + 1:]
        p, n = pl.program_id(0), pl.program_id(1)
        if carry is not None:
            @pl.when((p == 0) & (n == 0))
            def _():
                carry["start"](car_in, car_out, sems)

            @pl.when((p == npair - 1) & (n == nb - 1))
            def _():
                carry["finish"](car_in, car_out, sems)

        e = _head_sum_matrix()
        lo_half = lax.broadcasted_iota(jnp.int32, (1, LANE), 1) < HEAD_DIM
        mask4 = _attn_mask(n)
        qn, _ = _headnorm(q_ref[...], e)
        qhat = _tapmul(qn, qg_ref[...])
        kn, _ = _headnorm(jnp.concatenate([kp_ref[...], kc_ref[...]], axis=0), e)
        khat = _tapmul(kn, kg_ref[...])
        vwin = jnp.concatenate([vp_ref[...], vc_ref[...]], axis=0).astype(BF16)
        ops = [_qk_operands(qhat, khat, j, lo_half) for j in range(PAIR)]
        scores = [_dot(q4, kd, NT) for q4, kd in ops]
        probs = [_attn_softmax(s, sink_ref, mask4, (p * PAIR + j) * GROUP)[0] for j, s in enumerate(scores)]
        o4 = [_dot(probs[j], _chunk(vwin, j // 2), NN) for j in range(PAIR)]
        chunks = []
        for c in range(2 * PAIR):
            j, t = c // 2, c % 2
            a = o4[j][(2 * t) * BLOCK:(2 * t + 1) * BLOCK]
            b = o4[j][(2 * t + 1) * BLOCK:(2 * t + 2) * BLOCK]
            if j % 2 == 0:
                b = pltpu.roll(b, HEAD_DIM, 1)
            else:
                a = pltpu.roll(a, HEAD_DIM, 1)
            chunks.append(jnp.where(lo_half, a, b))
        o_ref[...] = jnp.concatenate(chunks, axis=1).astype(o_ref.dtype)

    prev = lambda n: jnp.maximum(n - 1, 0)
    car = carry if carry is not None else dict(arrays=[], out_structs=[], sems=[])
    qg8 = _rep8(jnp.tile(q_gain[0], PAIR * GROUP))
    kg8 = _rep8(jnp.tile(k_gain[0], PAIR))
    outs = pl.pallas_call(
        body, grid=(npair, nb),
        in_specs=[pl.BlockSpec((BLOCK, QW), lambda p, n: (n, p)),
                  pl.BlockSpec((BLOCK, KW), lambda p, n: (prev(n), k0 + p)),
                  pl.BlockSpec((BLOCK, KW), lambda p, n: (n, k0 + p)),
                  pl.BlockSpec((BLOCK, KW), lambda p, n: (prev(n), v0 + p)),
                  pl.BlockSpec((BLOCK, KW), lambda p, n: (n, v0 + p)),
                  pl.BlockSpec((8, QW), lambda p, n: (0, 0)), pl.BlockSpec((8, KW), lambda p, n: (0, 0)),
                  pl.BlockSpec(memory_space=pltpu.SMEM)] + [HBM] * ncar,
        out_specs=[pl.BlockSpec((BLOCK, QW), lambda p, n: (n, p))] + [HBM] * ncar,
        out_shape=[jax.ShapeDtypeStruct((T, dq), BF16)] + list(car["out_structs"]),
        scratch_shapes=list(car["sems"]),
        compiler_params=_cp(*(("parallel", "parallel") if carry is None else ("arbitrary", "arbitrary"))),
        name=name)(qkv, qkv, qkv, qkv, qkv, qg8, kg8, sinks, *car["arrays"])
    return outs[0] if carry is None else (outs[0], list(outs[1:]))


def attn_bwd(qkv, d_out, q_gain, k_gain, sinks, *, name):
    T, W = qkv.shape
    hq = W // HEAD_DIM * GROUP // (GROUP + 2)
    dq_w = hq * HEAD_DIM
    PAIR = _kv_heads_per_step(hq // GROUP)
    QW, KW = PAIR * GROUP * HEAD_DIM, PAIR * HEAD_DIM
    npair = hq // (GROUP * PAIR)
    nb = T // BLOCK
    k0 = dq_w // KW
    v0 = k0 + npair

    def body(q_ref, kp_ref, kc_ref, vp_ref, vc_ref, do_ref, qg_ref, kg_ref, sink_ref,
             dq_ref, dk_ref, dv_ref, dqg_ref, dkg_ref, dsink_ref, dk_carry, dv_carry):
        p, i = pl.program_id(0), pl.program_id(1)
        n = nb - 1 - i

        @pl.when(i == 0)
        def _():
            dk_carry[...] = jnp.zeros_like(dk_carry)
            dv_carry[...] = jnp.zeros_like(dv_carry)
            dqg_ref[...] = jnp.zeros_like(dqg_ref)
            dkg_ref[...] = jnp.zeros_like(dkg_ref)
            dsink_ref[...] = jnp.zeros_like(dsink_ref)

        e = _head_sum_matrix()
        lo_half = lax.broadcasted_iota(jnp.int32, (1, LANE), 1) < HEAD_DIM
        mask4 = _attn_mask(n)
        qg, kg = qg_ref[...], kg_ref[...]
        q = q_ref[...]
        kwin = jnp.concatenate([kp_ref[...], kc_ref[...]], axis=0)
        qn, qr = _headnorm(q, e)
        qhat = _tapmul(qn, qg)
        kn, kr = _headnorm(kwin, e)
        khat = _tapmul(kn, kg)
        vwin = jnp.concatenate([vp_ref[...], vc_ref[...]], axis=0).astype(BF16)
        do = do_ref[...].astype(F32)
        ops = [_qk_operands(qhat, khat, j, lo_half) for j in range(PAIR)]
        do4 = []
        for j in range(PAIR):
            parts = []
            for g in range(GROUP):
                dc = _chunk(do, 2 * j + g // 2)
                if g % 2 != j % 2:
                    dc = pltpu.roll(dc, HEAD_DIM, 1)
                parts.append(jnp.where(lo_half == (j % 2 == 0), dc, 0.0))
            do4.append(jnp.concatenate(parts, axis=0))
        scores = [_dot(q4, kd, NT) for q4, kd in ops]
        dps = [_dot(do4[j], _chunk(vwin, j // 2), NT) for j in range(PAIR)]
        soft = [_attn_softmax(s, sink_ref, mask4, (p * PAIR + j) * GROUP) for j, s in enumerate(scores)]
        dss, dsink_rows = [], []
        for j in range(PAIR):
            pr, psink = soft[j]
            delta = _row_sums(pr * dps[j])
            dss.append(pr * (dps[j] - jnp.concatenate([delta] * 2, axis=1)) * (HEAD_DIM ** -0.5))
            dsk = -psink * delta[:, 0:1]
            for g in range(GROUP):
                tot = jnp.sum(dsk[g * BLOCK:(g + 1) * BLOCK], axis=0, keepdims=True)
                dsink_rows.append(jnp.broadcast_to(tot, (1, LANE)))
        dq4 = [_dot(dss[j], ops[j][1], NN) for j in range(PAIR)]
        dkd = [_dot(dss[j], ops[j][0], TN) for j in range(PAIR)]
        dvc = [_dot(soft[j][0], do4[j], TN) for j in range(PAIR)]
        dqhat = jnp.concatenate(
            [jnp.where(lo_half, dq4[c // 2][(2 * (c % 2)) * BLOCK:(2 * (c % 2) + 1) * BLOCK],
                       dq4[c // 2][(2 * (c % 2) + 1) * BLOCK:(2 * (c % 2) + 2) * BLOCK]) for c in range(2 * PAIR)],
            axis=1)
        dkhat_chunks, dv_chunks = [], []
        for kc in range(PAIR // 2):
            tot_k, tot_v = None, None
            for j in (2 * kc, 2 * kc + 1):
                t = jnp.where(lo_half == (j % 2 == 0), dkd[j] + pltpu.roll(dkd[j], HEAD_DIM, 1), 0.0)
                tot_k = t if tot_k is None else tot_k + t
                tot_v = dvc[j] if tot_v is None else tot_v + dvc[j]
            dkhat_chunks.append(tot_k)
            dv_chunks.append(tot_v)
        dkhat_win = jnp.concatenate(dkhat_chunks, axis=1)
        dv_win = jnp.concatenate(dv_chunks, axis=1)
        dqg_ref[...] += _colsum8(dqhat * qn)
        dqn = _tapmul(dqhat, qg)
        dq_ref[...] = (qr * dqn - q * (qr * qr * qr * _head_sums(dqn * q, e) * (1.0 / HEAD_DIM))).astype(dq_ref.dtype)
        dkh = dkhat_win[BLOCK:] + dk_carry[...]
        dk_carry[...] = dkhat_win[:BLOCK]
        kcur, knc, krc = kwin[BLOCK:], kn[BLOCK:], kr[BLOCK:]
        dkg_ref[...] += _colsum8(dkh * knc)
        dkn = _tapmul(dkh, kg)
        dk_ref[...] = (krc * dkn
                       - kcur * (krc * krc * krc * _head_sums(dkn * kcur, e) * (1.0 / HEAD_DIM))).astype(dk_ref.dtype)
        dv_ref[...] = (dv_win[BLOCK:] + dv_carry[...]).astype(dv_ref.dtype)
        dv_carry[...] = dv_win[:BLOCK]
        dsink_ref[...] += jnp.concatenate(dsink_rows, axis=0)

    rev = lambda i: nb - 1 - i
    prev = lambda i: jnp.maximum(nb - 2 - i, 0)
    qg8 = _rep8(jnp.tile(q_gain[0], PAIR * GROUP))
    kg8 = _rep8(jnp.tile(k_gain[0], PAIR))
    dq, dk, dv, dqg, dkg, dsink = pl.pallas_call(
        body, grid=(npair, nb),
        in_specs=[pl.BlockSpec((BLOCK, QW), lambda p, i: (rev(i), p)),
                  pl.BlockSpec((BLOCK, KW), lambda p, i: (prev(i), k0 + p)),
                  pl.BlockSpec((BLOCK, KW), lambda p, i: (rev(i), k0 + p)),
                  pl.BlockSpec((BLOCK, KW), lambda p, i: (prev(i), v0 + p)),
                  pl.BlockSpec((BLOCK, KW), lambda p, i: (rev(i), v0 + p)),
                  pl.BlockSpec((BLOCK, QW), lambda p, i: (rev(i), p)),
                  pl.BlockSpec((8, QW), lambda p, i: (0, 0)), pl.BlockSpec((8, KW), lambda p, i: (0, 0)),
                  pl.BlockSpec(memory_space=pltpu.SMEM)],
        out_specs=[pl.BlockSpec((BLOCK, QW), lambda p, i: (rev(i), p)),
                   pl.BlockSpec((BLOCK, KW), lambda p, i: (rev(i), p)),
                   pl.BlockSpec((BLOCK, KW), lambda p, i: (rev(i), p)),
                   pl.BlockSpec((None, 8, QW), lambda p, i: (p, 0, 0)),
                   pl.BlockSpec((None, 8, KW), lambda p, i: (p, 0, 0)),
                   pl.BlockSpec((None, PAIR * GROUP, LANE), lambda p, i: (p, 0, 0))],
        out_shape=[jax.ShapeDtypeStruct((T, dq_w), BF16),
                   jax.ShapeDtypeStruct((T, npair * KW), BF16),
                   jax.ShapeDtypeStruct((T, npair * KW), BF16),
                   jax.ShapeDtypeStruct((npair, 8, QW), F32),
                   jax.ShapeDtypeStruct((npair, 8, KW), F32),
                   jax.ShapeDtypeStruct((npair, PAIR * GROUP, LANE), F32)],
        scratch_shapes=[pltpu.VMEM((BLOCK, KW), F32), pltpu.VMEM((BLOCK, KW), F32)],
        compiler_params=_cp("parallel", "arbitrary"), name=name,
    )(qkv, qkv, qkv, qkv, qkv, d_out, qg8, kg8, sinks)
    dqg = jnp.sum(dqg.reshape(-1, HEAD_DIM), axis=0, keepdims=True)
    dkg = jnp.sum(dkg.reshape(-1, HEAD_DIM), axis=0, keepdims=True)
    return dq, dk, dv, dqg, dkg, dsink[:, :, 0].reshape(-1)


def _softplus_neg(lam):
    return jnp.maximum(-lam, 0.0) + jnp.log1p(jnp.exp(-jnp.abs(lam)))


def _causal_conv(x, prev8, cw_ref, cb8):
    K = cw_ref.shape[0]
    acc = _tapmul(x, cw_ref[K - 1])
    for s in range(1, K):
        acc = acc + _tapmul(_shift_down(x, s, prev8), cw_ref[K - 1 - s])
    return (_rows8(acc) + cb8[None]).reshape(x.shape)


def _bcast_row(x, row):
    return jnp.broadcast_to(x[row:row + 1, :], x.shape)


def _rows8(x):
    return x.reshape(x.shape[0] // 8, 8, x.shape[1])


def _tapmul(x, w8):
    return (_rows8(x) * w8[None]).reshape(x.shape)


def _colsum8(x):
    return jnp.sum(_rows8(x), axis=0)


def _shift_down(x, s, prev8):
    r = pltpu.roll(x, s, 0)
    rowid = lax.broadcasted_iota(jnp.int32, (8, 1), 0)
    head = jnp.where(rowid >= s, r[0:8], pltpu.roll(prev8, s, 0))
    return jnp.concatenate([head, r[8:]], axis=0)


def _shift_up(x, s, next8):
    R = x.shape[0]
    r = pltpu.roll(x, R - s, 0)
    rowid = lax.broadcasted_iota(jnp.int32, (8, 1), 0)
    tail = jnp.where(rowid < 8 - s, r[R - 8:], pltpu.roll(next8, 8 - s, 0))
    return jnp.concatenate([r[:R - 8], tail], axis=0)


def _rep8(v):
    return jnp.broadcast_to(v[..., None, :], v.shape[:-1] + (8, v.shape[-1]))


def rec_fwd(z, cw, cb, wa, ba, wi, bi, lam, *, name):
    T, C2 = z.shape
    C = C2 // 2
    nblk, bd, _ = wa.shape
    tt = _tile(T, 128, HALO)
    ng = tt // 8

    def body(x_ref, y_ref, halo_ref, cw_ref, cb_ref, wa_ref, ba_ref, wi_ref, bi_ref, lam_ref,
             xb_ref, r_ref, i_ref, a_ref, h_ref, hp_ref, hg_ref, carry, u_scr):
        step = pl.program_id(0)

        @pl.when(step == 0)
        def _():
            carry[...] = jnp.zeros_like(carry)

        xb = _causal_conv(x_ref[...], jnp.where(step > 0, halo_ref[HALO - 8:, :], 0.0), cw_ref, cb_ref[...])
        xb_ref[...] = xb
        pa, pi = [], []
        for b in range(nblk):
            xs = xb[:, b * bd:(b + 1) * bd]
            pa.append(_dot(xs, wa_ref[b], NN))
            pi.append(_dot(xs, wi_ref[b], NN))
        r = jax.nn.sigmoid(jnp.concatenate(pa, axis=1) + ba_ref[...])
        ig = jax.nn.sigmoid(jnp.concatenate(pi, axis=1) + bi_ref[...])
        r_ref[...] = r
        i_ref[...] = ig
        nl = LRU_C * r * _softplus_neg(lam_ref[...])
        a_ref[...] = jnp.exp(-nl)
        th = jnp.tanh(nl)
        u_scr[...] = jnp.sqrt(2.0 * th / (1.0 + th)) * (ig * xb)

        rowid = lax.broadcasted_iota(jnp.int32, (8, C), 0)

        def group(gi, hc):
            r0 = pl.multiple_of(gi * 8, 8)
            a8 = a_ref[pl.ds(r0, 8), :]
            u8 = u_scr[pl.ds(r0, 8), :]
            for d in (1, 2, 4):
                a_sh = jnp.where(rowid >= d, pltpu.roll(a8, d, 0), 1.0)
                u_sh = jnp.where(rowid >= d, pltpu.roll(u8, d, 0), 0.0)
                u8 = a8 * u_sh + u8
                a8 = a8 * a_sh
            h8 = u8 + a8 * hc
            h_ref[pl.ds(r0, 8), :] = h8
            hp_ref[pl.ds(r0, 8), :] = jnp.where(rowid >= 1, pltpu.roll(h8, 1, 0), hc)
            return _bcast_row(h8, 7)

        carry[...] = lax.fori_loop(0, ng, group, carry[...])
        hg_ref[...] = (h_ref[...] * _gelu(y_ref[...])).astype(hg_ref.dtype)

    row = lambda c: pl.BlockSpec((tt, C), lambda i, c=c: (i, c))
    vec = pl.BlockSpec((1, C), lambda i: (0, 0))
    full = lambda shp: pl.BlockSpec(shp, lambda i, n=len(shp): (0,) * n)
    per = tt // HALO
    outs = pl.pallas_call(
        body, grid=(T // tt,),
        in_specs=[row(0), row(1), pl.BlockSpec((HALO, C), lambda i: (jnp.maximum(i * per - 1, 0), 0)),
                  full(cw.shape[:1] + (8, C)), full((8, C)), full(wa.shape), vec, full(wi.shape), vec, vec],
        out_specs=[row(0)] * 7,
        out_shape=[jax.ShapeDtypeStruct((T, C), F32)] * 6 + [jax.ShapeDtypeStruct((T, C), BF16)],
        scratch_shapes=[pltpu.VMEM((8, C), F32), pltpu.VMEM((tt, C), F32)],
        compiler_params=_cp("arbitrary"), name=name,
    )(z, z, z, _rep8(cw), _rep8(cb[0]), wa, ba, wi, bi, lam)
    return outs


def rec_bwd_scan(dhg, h, a, z, *, name):
    T, C = h.shape
    tt = _tile(T, 256, HALO)
    ng = tt // 8
    nb = T // tt

    def body(dhg_ref, h_ref, a_ref, y_ref, dy_ref, yb_ref, ycarry, acarry, g_scr):
        step = pl.program_id(0)

        @pl.when(step == 0)
        def _():
            ycarry[...] = jnp.zeros_like(ycarry)
            acarry[...] = jnp.zeros_like(acarry)

        gate, dgate = _gelu_and_grad(y_ref[...])
        dhg_v = dhg_ref[...].astype(F32)
        dy_ref[...] = (dhg_v * h_ref[...] * dgate).astype(dy_ref.dtype)
        g_scr[...] = dhg_v * gate
        rowid = lax.broadcasted_iota(jnp.int32, (8, C), 0)

        def group(j, c):
            yc, ac = c
            r0 = pl.multiple_of((ng - 1 - j) * 8, 8)
            a8 = a_ref[pl.ds(r0, 8), :]
            y8 = g_scr[pl.ds(r0, 8), :]
            b8 = jnp.where(rowid < 7, pltpu.roll(a8, 7, 0), ac)
            for d in (1, 2, 4):
                y_sh = jnp.where(rowid < 8 - d, pltpu.roll(y8, 8 - d, 0), 0.0)
                b_sh = jnp.where(rowid < 8 - d, pltpu.roll(b8, 8 - d, 0), 1.0)
                y8 = y8 + b8 * y_sh
                b8 = b8 * b_sh
            y8 = y8 + b8 * yc
            yb_ref[pl.ds(r0, 8), :] = y8
            return _bcast_row(y8, 0), _bcast_row(a8, 0)

        yc, ac = lax.fori_loop(0, ng, group, (ycarry[...], acarry[...]))
        ycarry[...] = yc
        acarry[...] = ac

    rev = lambda c: pl.BlockSpec((tt, C), lambda i, c=c: (nb - 1 - i, c))
    return pl.pallas_call(
        body, grid=(nb,), in_specs=[rev(0), rev(0), rev(0), rev(1)], out_specs=[rev(0), rev(0)],
        out_shape=[jax.ShapeDtypeStruct((T, C), BF16), jax.ShapeDtypeStruct((T, C), F32)],
        scratch_shapes=[pltpu.VMEM((8, C), F32), pltpu.VMEM((8, C), F32), pltpu.VMEM((tt, C), F32)],
        compiler_params=_cp("arbitrary"), name=name,
    )(dhg, h, a, z)


def rec_bwd_gates(ybar, hprev, a, r, ig, xb, lam, wa, wi, *, name):
    T, C = xb.shape
    nblk, bd, _ = wa.shape
    tt = _tile(T, 256, 8)
    nb = T // tt

    def body(y_ref, hp_ref, a_ref, r_ref, i_ref, xb_ref, lam_ref, wa_ref, wi_ref,
             dxb_ref, dwa_ref, dwi_ref, dba_ref, dbi_ref, dlam_ref):
        step = pl.program_id(0)

        @pl.when(step == 0)
        def _():
            for ref in (dwa_ref, dwi_ref, dba_ref, dbi_ref, dlam_ref):
                ref[...] = jnp.zeros_like(ref)

        y, av, rv, iv, xv = y_ref[...], a_ref[...], r_ref[...], i_ref[...], xb_ref[...]
        sp = _softplus_neg(lam_ref[...])
        th = jnp.tanh(LRU_C * rv * sp)
        s = jnp.sqrt(2.0 * th / (1.0 + th))
        d_nl = -(y * hp_ref[...] * av) + (y * iv * xv) * (av * av) / s
        dlam_ref[...] += jnp.sum(d_nl * rv, axis=0, keepdims=True) * LRU_C
        dr = d_nl * (LRU_C * sp)
        di = y * s * xv
        dpa = dr * rv * (1.0 - rv)
        dpi = di * iv * (1.0 - iv)
        dba_ref[...] += jnp.sum(dpa, axis=0, keepdims=True)
        dbi_ref[...] += jnp.sum(dpi, axis=0, keepdims=True)
        parts = []
        for b in range(nblk):
            sl = slice(b * bd, (b + 1) * bd)
            xs, da_b, di_b = xv[:, sl], dpa[:, sl], dpi[:, sl]
            dwa_ref[b] += _dot(xs, da_b, TN)
            dwi_ref[b] += _dot(xs, di_b, TN)
            parts.append(_dot(da_b, wa_ref[b], NT) + _dot(di_b, wi_ref[b], NT))
        dxb_ref[...] = y * s * iv + jnp.concatenate(parts, axis=1)

        @pl.when(step == nb - 1)
        def _():
            dlam_ref[...] = dlam_ref[...] * (-jax.nn.sigmoid(-lam_ref[...]))

    row = pl.BlockSpec((tt, C), lambda i: (i, 0))
    vec = pl.BlockSpec((1, C), lambda i: (0, 0))
    wsp = pl.BlockSpec(wa.shape, lambda i: (0, 0, 0))
    return pl.pallas_call(
        body, grid=(nb,), in_specs=[row] * 6 + [vec, wsp, wsp],
        out_specs=[row, wsp, wsp, vec, vec, vec],
        out_shape=[jax.ShapeDtypeStruct((T, C), F32), jax.ShapeDtypeStruct(wa.shape, F32),
                   jax.ShapeDtypeStruct(wa.shape, F32)] + [jax.ShapeDtypeStruct((1, C), F32)] * 3,
        compiler_params=_cp("arbitrary"), name=name,
    )(ybar, hprev, a, r, ig, xb, lam, wa, wi)


def conv_bwd(d, x0, cw, *, name):
    T, C = d.shape
    K = cw.shape[0]
    tt = _tile(T, 256, HALO)
    per = tt // HALO
    nb = T // tt

    def body(d_ref, dn_ref, x_ref, cw_ref, dx_ref, dcw_ref, dcb_ref, dcw_acc, dcb_acc):
        step = pl.program_id(1)

        @pl.when(step == 0)
        def _():
            dcw_acc[...] = jnp.zeros_like(dcw_acc)
            dcb_acc[...] = jnp.zeros_like(dcb_acc)

        dv = d_ref[...].astype(F32)
        next8 = jnp.where(step < nb - 1, dn_ref[...].astype(F32)[0:8], 0.0)
        xt = x_ref[...].astype(F32)
        acc = _tapmul(dv, cw_ref[K - 1])
        dcw_acc[K - 1] += _colsum8(dv * xt)
        dcb_acc[...] += _colsum8(dv)
        for sh in range(1, K):
            dsh = _shift_up(dv, sh, next8)
            acc = acc + _tapmul(dsh, cw_ref[K - 1 - sh])
            dcw_acc[K - 1 - sh] += _colsum8(dsh * xt)
        dx_ref[...] = acc.astype(dx_ref.dtype)

        @pl.when(step == nb - 1)
        def _():
            dcw_ref[...] = jnp.sum(dcw_acc[...], axis=1)
            dcb_ref[...] = jnp.sum(dcb_acc[...], axis=0, keepdims=True)

    tc = C
    row = pl.BlockSpec((tt, tc), lambda j, i: (i, j))
    return pl.pallas_call(
        body, grid=(C // tc, nb),
        in_specs=[row, pl.BlockSpec((HALO, tc), lambda j, i: (jnp.minimum((i + 1) * per, T // HALO - 1), j)),
                  row, pl.BlockSpec((K, 8, tc), lambda j, i: (0, 0, j))],
        out_specs=[row, pl.BlockSpec((K, tc), lambda j, i: (0, j)), pl.BlockSpec((1, tc), lambda j, i: (0, j))],
        out_shape=[jax.ShapeDtypeStruct((T, C), BF16), jax.ShapeDtypeStruct((K, C), F32),
                   jax.ShapeDtypeStruct((1, C), F32)],
        scratch_shapes=[pltpu.VMEM((K, 8, tc), F32), pltpu.VMEM((8, tc), F32)],
        compiler_params=_cp("parallel", "arbitrary"), name=name,
    )(d, d, x0, _rep8(cw))


def ffn_act_fwd(u0, cw, cb, *, n, name):
    T, W = u0.shape
    G = W // (2 * n)
    tt = _tile(T, 256, HALO)
    per = tt // HALO

    K = cw.shape[0]

    def body(u_ref, up_ref, cw_ref, cb_ref, a_ref, uo_ref):
        step = pl.program_id(1)
        prev8 = jnp.where(step > 0, up_ref[...].astype(F32)[HALO - 8:], 0.0)
        u = _causal_conv(u_ref[...].astype(F32), prev8, cw_ref, cb_ref[...])
        uo_ref[...] = u.astype(uo_ref.dtype)
        a_ref[...] = (_gelu(u[:, :n]) * u[:, n:]).astype(a_ref.dtype)

    return pl.pallas_call(
        body, grid=(G, T // tt),
        in_specs=[pl.BlockSpec((tt, 2 * n), lambda j, i: (i, j)),
                  pl.BlockSpec((HALO, 2 * n), lambda j, i: (jnp.maximum(i * per - 1, 0), j)),
                  pl.BlockSpec((K, 8, 2 * n), lambda j, i: (0, 0, j)),
                  pl.BlockSpec((8, 2 * n), lambda j, i: (0, j))],
        out_specs=[pl.BlockSpec((tt, n), lambda j, i: (i, j)), pl.BlockSpec((tt, 2 * n), lambda j, i: (i, j))],
        out_shape=[jax.ShapeDtypeStruct((T, G * n), BF16), jax.ShapeDtypeStruct((T, W), BF16)],
        compiler_params=_cp("parallel", "parallel"), name=name)(u0, u0, _rep8(cw), _rep8(cb[0]))


def ffn_down_act_bwd(dx, w_down, u0, u, cw, *, n, carry=None, name):
    T, W = u0.shape
    D = dx.shape[1]
    G = W // (2 * n)
    K = cw.shape[0]
    tt = _tile(T, 256, HALO)
    tc = _tile(n, 512)
    nb = T // tt
    car = carry if carry is not None else dict(arrays=[], out_structs=[], sems=[])
    ncar = len(car["arrays"])

    def body(dx_ref, w_ref, x_ref, u_ref, cw_ref, *rest):
        car_in, (du_ref, dcw_ref, dcb_ref) = rest[:ncar], rest[ncar:ncar + 3]
        car_out, (later8, dcw_acc, dcb_acc) = rest[ncar + 3:2 * ncar + 3], rest[2 * ncar + 3:2 * ncar + 6]
        sems = rest[2 * ncar + 6:]
        group, step = pl.program_id(0), pl.program_id(1)
        if carry is not None:
            @pl.when((group == 0) & (step == 0))
            def _():
                carry["start"](car_in, car_out, sems)

            @pl.when((group == G - 1) & (step == nb - 1))
            def _():
                carry["finish"](car_in, car_out, sems)

        @pl.when(step == 0)
        def _():
            later8[...] = jnp.zeros_like(later8)
            dcw_acc[...] = jnp.zeros_like(dcw_acc)
            dcb_acc[...] = jnp.zeros_like(dcb_acc)

        dact = _dot(dx_ref[...], w_ref[...], NT)
        for q in range(n // tc):
            gs, vs = slice(q * tc, (q + 1) * tc), slice(n + q * tc, n + (q + 1) * tc)
            gl, dgl = _gelu_and_grad(u_ref[:, gs].astype(F32))
            daf = dact[:, gs]
            for cols, d in ((gs, daf * u_ref[:, vs].astype(F32) * dgl), (vs, daf * gl)):
                next8 = later8[:, cols]
                xt = x_ref[:, cols].astype(F32)
                acc = _tapmul(d, cw_ref[K - 1, :, cols])
                dcw_acc[K - 1, :, cols] += _colsum8(d * xt)
                dcb_acc[:, cols] += _colsum8(d)
                for sh in range(1, K):
                    dsh = _shift_up(d, sh, next8)
                    acc = acc + _tapmul(dsh, cw_ref[K - 1 - sh, :, cols])
                    dcw_acc[K - 1 - sh, :, cols] += _colsum8(dsh * xt)
                du_ref[:, cols] = acc.astype(du_ref.dtype)
                later8[:, cols] = d[0:8]

        @pl.when(step == nb - 1)
        def _():
            dcw_ref[...] = jnp.sum(dcw_acc[...], axis=1)
            dcb_ref[...] = jnp.sum(dcb_acc[...], axis=0, keepdims=True)

    rev = lambda i: nb - 1 - i
    outs = pl.pallas_call(
        body, grid=(G, nb),
        in_specs=[pl.BlockSpec((tt, D), lambda j, i: (rev(i), 0)),
                  pl.BlockSpec((n, D), lambda j, i: (j, 0)),
                  pl.BlockSpec((tt, 2 * n), lambda j, i: (rev(i), j)),
                  pl.BlockSpec((tt, 2 * n), lambda j, i: (rev(i), j)),
                  pl.BlockSpec((K, 8, 2 * n), lambda j, i: (0, 0, j))] + [HBM] * ncar,
        out_specs=[pl.BlockSpec((tt, 2 * n), lambda j, i: (rev(i), j)),
                   pl.BlockSpec((K, 2 * n), lambda j, i: (0, j)),
                   pl.BlockSpec((1, 2 * n), lambda j, i: (0, j))] + [HBM] * ncar,
        out_shape=[jax.ShapeDtypeStruct((T, W), BF16), jax.ShapeDtypeStruct((K, W), F32),
                   jax.ShapeDtypeStruct((1, W), F32)] + list(car["out_structs"]),
        scratch_shapes=[pltpu.VMEM((8, 2 * n), F32), pltpu.VMEM((K, 8, 2 * n), F32), pltpu.VMEM((8, 2 * n), F32)]
        + list(car["sems"]),
        compiler_params=_cp(*(("parallel", "arbitrary") if carry is None else ("arbitrary", "arbitrary"))), name=name,
    )(dx, w_down, u0, u, _rep8(cw), *car["arrays"])
    return tuple(outs[:3]) if carry is None else (tuple(outs[:3]), list(outs[3:]))


def _ffn_shard_of_block(q):
    return (q % 2) * (N_DEV // 2) + q // 2


def _ffn_block_of_shard(s):
    return (s % (N_DEV // 2)) * 2 + s // (N_DEV // 2)


def _group_cols(v):
    lead = v.shape[:-1]
    n = v.shape[-1] // N_DEV
    return jnp.swapaxes(v.reshape(lead + (2, N_DEV // 2, n)), -3, -2).reshape(v.shape)


def _ungroup_cols(v):
    lead = v.shape[:-1]
    n = v.shape[-1] // N_DEV
    return jnp.swapaxes(v.reshape(lead + (N_DEV // 2, 2, n)), -3, -2).reshape(v.shape)


def local_step(x, target, w, plan=None):
    depth = w["mix_norm"].shape[0]
    n_up = w["ffn_conv_w"].shape[-1] // N_DEV
    g = {k: [None] * (w[k].shape[0] if hasattr(w[k], "shape") else len(w[k])) for k in w}

    def run(fn, stage, l, slot, *args, **kw):
        carry = None if plan is None else plan.carry(stage, l, slot, g)
        if carry is None:
            return fn(*args, **kw)
        out, extra = fn(*args, carry=carry, **kw)
        plan.done(stage, l, slot, extra)
        return out

    saved = []
    for l in range(depth):
        j = l // 2
        h = rmsnorm_fwd(x, w["mix_norm"][l:l + 1], name="mix_norm_fwd")
        if l % 2 == 0:
            qkv = run(mm_nn, "fwd", l, "in", h, w["attn_w_qkv"][j], out_dtype=F32, name="qkv_proj")
            ao = run(attn_fwd, "fwd", l, "attn", qkv, w["attn_q_gain"][j:j + 1], w["attn_k_gain"][j:j + 1],
                     w["attn_sinks"][j:j + 1], name="attn_fwd")
            x1 = run(mm_nn, "fwd", l, "out", ao, w["attn_w_o"][j], out_dtype=F32, res=x, name="attn_out_proj")
            mix = (qkv, ao)
        else:
            z = run(mm_nn, "fwd", l, "in", h, w["rec_w_in"][j], out_dtype=F32, name="rec_in_proj")
            xb, r, ig, a, hs, hprev, hg = rec_fwd(
                z, w["rec_conv_w"][j], w["rec_conv_b"][j:j + 1], w["rec_w_a"][j], w["rec_b_a"][j:j + 1],
                w["rec_w_i"][j], w["rec_b_i"][j:j + 1], w["rec_lambda"][j:j + 1], name="rec_fwd")
            x1 = run(mm_nn, "fwd", l, "out", hg, w["rec_w_out"][j], out_dtype=F32, res=x, name="rec_out_proj")
            mix = (z, xb, r, ig, a, hs, hprev, hg)
        h2 = rmsnorm_fwd(x1, w["ffn_norm"][l:l + 1], name="ffn_norm_fwd")
        u0 = run(mm_nn, "fwd", l, "up", h2, w["ffn_w_up"][l], out_dtype=BF16, name="ffn_up_proj")
        act, u = ffn_act_fwd(u0, w["ffn_conv_w"][l], w["ffn_conv_b"][l:l + 1], n=n_up, name="ffn_act_fwd")
        x2 = run(mm_nn, "fwd", l, "down", act, w["ffn_w_down"][l], out_dtype=F32, res=x1, name="ffn_down_proj")
        saved.append((x, h, mix, x1, h2, u0, u, act))
        x = x2

    loss_vec, dx = loss_head(x, target, name="loss_head")

    for l in reversed(range(depth)):
        j = l // 2
        x0, h, mix, x1, h2, u0, u, act = saved[l]
        g["ffn_w_down"][l] = mm_tn(act, dx, out_dtype=BF16, name="ffn_down_dw")
        du0, dcw, dcb = run(ffn_down_act_bwd, "bwd", l, "act_bwd", dx, w["ffn_w_down"][l], u0, u,
                            w["ffn_conv_w"][l], n=n_up, name="ffn_down_act_bwd")
        g["ffn_conv_w"][l], g["ffn_conv_b"][l] = _ungroup_cols(dcw), _ungroup_cols(dcb)[0]
        g["ffn_w_up"][l] = run(mm_tn, "bwd", l, "up_dw", h2, du0, out_dtype=BF16, col_shards=N_DEV, shard_of_block=_ffn_shard_of_block,
                                 name="ffn_up_dw")
        dh2 = run(mm_nt, "bwd", l, "up_dx", du0, w["ffn_w_up"][l], out_dtype=F32, name="ffn_up_dx")
        dx1, dgf = rmsnorm_bwd(x1, w["ffn_norm"][l:l + 1], dh2, dx, name="ffn_norm_bwd")
        g["ffn_norm"][l] = dgf[0]
        if l % 2 == 0:
            qkv, ao = mix
            dao = mm_nt(dx1, w["attn_w_o"][j], out_dtype=BF16, name="attn_out_dx")
            g["attn_w_o"][j] = mm_tn(ao, dx1, out_dtype=BF16, name="attn_out_dw")
            dq, dk, dv, dqg, dkg, dsk = attn_bwd(qkv, dao, w["attn_q_gain"][j:j + 1], w["attn_k_gain"][j:j + 1],
                                                 w["attn_sinks"][j:j + 1], name="attn_bwd")
            g["attn_q_gain"][j], g["attn_k_gain"][j], g["attn_sinks"][j] = dqg[0], dkg[0], dsk
            dqkv = jnp.concatenate([dq, dk, dv], axis=1)
            g["attn_w_qkv"][j] = run(mm_tn, "bwd", l, "in_dw", h, dqkv, out_dtype=BF16, col_shards=N_DEV, name="qkv_dw")
            dh = run(mm_nt, "bwd", l, "in_dx", dqkv, w["attn_w_qkv"][j], out_dtype=F32, name="qkv_dx")
        else:
            z, xb, r, ig, a, hs, hprev, hg = mix
            dhg = mm_nt(dx1, w["rec_w_out"][j], out_dtype=BF16, name="rec_out_dx")
            g["rec_w_out"][j] = mm_tn(hg, dx1, out_dtype=BF16, name="rec_out_dw")
            dyb, ybar = rec_bwd_scan(dhg, hs, a, z, name="rec_bwd_scan")
            dxb, dwa, dwi, dba, dbi, dlam = rec_bwd_gates(
                ybar, hprev, a, r, ig, xb, w["rec_lambda"][j:j + 1], w["rec_w_a"][j], w["rec_w_i"][j],
                name="rec_bwd_gates")
            dxb0, dcw, dcb = conv_bwd(dxb, z, w["rec_conv_w"][j], name="rec_conv_bwd")
            g["rec_w_a"][j], g["rec_w_i"][j] = dwa, dwi
            g["rec_b_a"][j], g["rec_b_i"][j], g["rec_lambda"][j] = dba[0], dbi[0], dlam[0]
            g["rec_conv_w"][j], g["rec_conv_b"][j] = dcw, dcb[0]
            dz = jnp.concatenate([dxb0, dyb], axis=1)
            g["rec_w_in"][j] = run(mm_tn, "bwd", l, "in_dw", h, dz, out_dtype=BF16, col_shards=N_DEV, name="rec_in_dw")
            dh = run(mm_nt, "bwd", l, "in_dx", dz, w["rec_w_in"][j], out_dtype=F32, name="rec_in_dx")
        dx, dgm = rmsnorm_bwd(x0, w["mix_norm"][l:l + 1], dh, dx1, name="mix_norm_bwd")
        g["mix_norm"][l] = dgm[0]
    return loss_vec, dx, g


HBM = pl.BlockSpec(memory_space=pltpu.HBM)
N_PEER = N_DEV - 1


def _here():
    return lax.axis_index("x"), lax.axis_index("y"), lax.axis_index("c")


def _sid(dev):
    return 4 * dev[0] + 2 * dev[1] + dev[2]


def _exchange_sems(n):
    return [pltpu.SemaphoreType.DMA((n * N_PEER,)), pltpu.SemaphoreType.DMA((n * N_PEER,)),
            pltpu.SemaphoreType.DMA((n,))]


def gather_exchange(shards, out_structs, windows):
    n = len(shards)

    def parts(outs, sems):
        send_sems, recv_sems, _ = sems
        x, y, c = _here()
        me, sib = (x, y, c), (x, y, 1 - c)
        chips = [(1 - x, y), (x, 1 - y), (1 - x, 1 - y)]

        def copy(i, k, block, to, src=None):
            dst = windows[i](outs[i], _sid(block))
            return pltpu.make_async_remote_copy(
                src_ref=dst if src is None else src, dst_ref=dst,
                send_sem=send_sems.at[i * N_PEER + k], recv_sem=recv_sems.at[i * N_PEER + k],
                device_id=to, device_id_type=MESH)

        return me, sib, chips, c, copy

    def own_copies(ins, outs, sems):
        me, sib, chips, c, copy = parts(outs, sems)
        local = [pltpu.make_async_copy(ins[i], windows[i](outs[i], _sid(me)), sems[2].at[i]) for i in range(n)]
        first = []
        for i in range(n):
            first.append(copy(i, 0, me, sib, src=ins[i]))
            first += [copy(i, 1 + j, me, (*chip, c), src=ins[i]) for j, chip in enumerate(chips)]
        return local, first

    def start(ins, outs, sems):
        local, first = own_copies(ins, outs, sems)
        for cp in local + first:
            cp.start()

    def finish(ins, outs, sems):
        me, sib, chips, c, copy = parts(outs, sems)
        local, first = own_copies(ins, outs, sems)
        passed = []
        for i in range(n):
            for j, chip in enumerate(chips):
                copy(i, 1 + j, (*chip, c), me).wait_recv()
                fwd = copy(i, 4 + j, (*chip, c), sib)
                fwd.start()
                passed.append(fwd)
        for i in range(n):
            copy(i, 0, sib, me).wait_recv()
            for j, chip in enumerate(chips):
                copy(i, 4 + j, (*chip, 1 - c), me).wait_recv()
        for cp in first + passed:
            cp.wait_send()
        for cp in local:
            cp.wait()

    return dict(arrays=list(shards), out_structs=list(out_structs), sems=_exchange_sems(n), start=start,
                finish=finish)


def run_exchange(ex, *, name):
    n = len(ex["arrays"])

    def body(*refs):
        ins, outs, sems = refs[:n], refs[n:2 * n], refs[2 * n:]
        ex["start"](ins, outs, sems)
        ex["finish"](ins, outs, sems)

    return pl.pallas_call(
        body, in_specs=[HBM] * n, out_specs=[HBM] * n, out_shape=ex["out_structs"], scratch_shapes=ex["sems"],
        name=name)(*ex["arrays"])


def scatter_exchange(grads):
    n = len(grads)

    def parts(ins, outs, sems):
        send_sems, recv_sems, local_sems = sems
        x, y, c = _here()
        me = (x, y, c)
        peers = []
        for k in range(1, N_DEV):
            kx, ky, kc = (k >> 2) & 1, (k >> 1) & 1, k & 1
            peers.append((1 - x if kx else x, 1 - y if ky else y, 1 - c if kc else c))

        def copy(i, k):
            return pltpu.make_async_remote_copy(
                src_ref=ins[i].at[_sid(peers[k])], dst_ref=outs[i].at[_sid(me)],
                send_sem=send_sems.at[i * N_PEER + k], recv_sem=recv_sems.at[i * N_PEER + k],
                device_id=peers[k], device_id_type=MESH)

        def arrival(i, k):
            return pltpu.make_async_remote_copy(
                src_ref=ins[i].at[_sid(me)], dst_ref=outs[i].at[_sid(peers[k])],
                send_sem=send_sems.at[i * N_PEER + k], recv_sem=recv_sems.at[i * N_PEER + k],
                device_id=peers[k], device_id_type=MESH)

        local = [pltpu.make_async_copy(ins[i].at[_sid(me)], outs[i].at[_sid(me)], local_sems.at[i]) for i in range(n)]
        sends = [copy(i, k) for i in range(n) for k in range(N_PEER)]
        return local, sends, arrival

    def start(ins, outs, sems):
        local, sends, _ = parts(ins, outs, sems)
        for cp in local + sends:
            cp.start()

    def finish(ins, outs, sems):
        local, sends, arrival = parts(ins, outs, sems)
        for i in range(n):
            for k in range(N_PEER):
                arrival(i, k).wait_recv()
        for cp in sends:
            cp.wait_send()
        for cp in local:
            cp.wait()

    return dict(arrays=list(grads), out_structs=[jax.ShapeDtypeStruct(g.shape, g.dtype) for g in grads],
                sems=_exchange_sems(n), start=start, finish=finish)


def adamw_family(contribs, w, m, v, *, name):
    L, R, C = w.shape
    S = contribs[0].shape[0]
    tr = _tile(R, max(8, (1 << 20) // (C * S)), 8)
    nr = R // tr
    c1 = 1.0 / (1.0 - ADAM_B1 ** ADAM_STEP)
    c2 = 1.0 / (1.0 - ADAM_B2 ** ADAM_STEP)

    def body(*refs):
        c_refs = refs[:L]
        w_ref, m_ref, v_ref, g_ref, d_ref, nm_ref, nv_ref = refs[L:]
        layer = pl.program_id(0)
        for l in range(L):
            @pl.when(layer == l)
            def _(l=l):
                g = c_refs[l][0].astype(F32)
                for s in range(1, S):
                    g = g + c_refs[l][s].astype(F32)
                mm = ADAM_B1 * m_ref[...] + (1.0 - ADAM_B1) * g
                vv = ADAM_B2 * v_ref[...] + (1.0 - ADAM_B2) * (g * g)
                g_ref[...] = g
                nm_ref[...] = mm
                nv_ref[...] = vv
                d_ref[...] = -ADAM_LR * ((mm * c1) / (jnp.sqrt(vv * c2) + ADAM_EPS) + ADAM_WD * w_ref[...])

    def cspec(l):
        return pl.BlockSpec((S, tr, C), lambda ll, i, l=l: (0, jnp.where(ll == l, i, 0), 0))

    lay = pl.BlockSpec((None, tr, C), lambda ll, i: (ll, i, 0))
    return pl.pallas_call(
        body, grid=(L, nr), in_specs=[cspec(l) for l in range(L)] + [lay] * 3, out_specs=[lay] * 4,
        out_shape=[jax.ShapeDtypeStruct((L, R, C), F32)] * 4,
        compiler_params=_cp("arbitrary", "arbitrary"), name=name)(*contribs, w, m, v)


def sum_slots(a, *, name):
    S, R, C = a.shape
    tr = _tile(R, 256, 8)

    def body(a_ref, o_ref):
        t = a_ref[0]
        for s in range(1, S):
            t = t + a_ref[s]
        o_ref[...] = t

    return pl.pallas_call(
        body, grid=(R // tr,), in_specs=[pl.BlockSpec((S, tr, C), lambda i: (0, i, 0))],
        out_specs=pl.BlockSpec((tr, C), lambda i: (i, 0)), out_shape=jax.ShapeDtypeStruct((R, C), F32),
        compiler_params=_cp("parallel"), name=name)(a)


LANES = 128


def _pack(arrs):
    flat = jnp.concatenate([a.reshape(-1).astype(F32) for a in arrs])
    rows = -(-flat.shape[0] // LANES)
    rows = -(-rows // 256) * 256
    return jnp.pad(flat, (0, rows * LANES - flat.shape[0])).reshape(rows, LANES)


def _unpack(buf, shapes):
    flat = buf.reshape(-1)
    out, off = [], 0
    for shp in shapes:
        size = int(np.prod(shp))
        out.append(flat[off:off + size].reshape(shp))
        off += size
    return out


def _gather_last(g):
    t = jnp.moveaxis(g, 0, -2)
    return t.reshape(t.shape[:-2] + (t.shape[-2] * t.shape[-1],))


def _own_last(full, s):
    n = full.shape[-1] // N_DEV
    t = full.reshape(full.shape[:-1] + (N_DEV, n))
    return lax.dynamic_index_in_dim(t, s, axis=t.ndim - 2, keepdims=False)


BIG = ["attn_w_qkv", "attn_w_o", "rec_w_in", "rec_w_out", "ffn_w_up", "ffn_w_down", "rec_w_a", "rec_w_i"]
SMALL_REPLICATED = ["mix_norm", "ffn_norm", "attn_q_gain", "attn_k_gain", "attn_sinks", "ffn_conv_b"]
SMALL_SHARDED = ["rec_conv_w", "rec_conv_b", "rec_b_a", "rec_b_i", "rec_lambda", "ffn_conv_w"]
SMALL = SMALL_REPLICATED + SMALL_SHARDED
WEIGHTS = ["mix_norm", "ffn_norm", "attn_w_qkv", "attn_q_gain", "attn_k_gain", "attn_sinks", "attn_w_o", "rec_w_in",
           "rec_conv_w", "rec_conv_b", "rec_w_a", "rec_b_a", "rec_w_i", "rec_b_i", "rec_lambda", "rec_w_out",
           "ffn_w_up", "ffn_conv_w", "ffn_conv_b", "ffn_w_down"]


def _col_window(n, block_of_shard=None):
    def win(ref, s):
        q = s if block_of_shard is None else block_of_shard(s)
        return ref.at[:, pl.ds(pl.multiple_of(q * n, 128), n)]
    return win


def _row_window(r):
    return lambda ref, s: ref.at[pl.ds(pl.multiple_of(s * r, 16), r), :]


def _gate_window(r):
    return lambda ref, s: ref.at[:, pl.ds(pl.multiple_of(s * r, 16), r), :]


def _slot_window(ref, s):
    return ref.at[s]


def _idx(name, layer):
    return layer if name.startswith("ffn") else layer // 2


def _slot_names(layer, slot):
    attn = layer % 2 == 0
    return {"in": ["attn_w_qkv"] if attn else ["rec_w_in"],
            "out": ["attn_w_o"] if attn else ["rec_w_out", "rec_w_a", "rec_w_i"],
            "up": ["ffn_w_up"], "down": ["ffn_w_down"]}[slot]


def _gather_for(p, items):
    shards, structs, wins = [], [], []
    for nme, layer in items:
        sh = p[nme][_idx(nme, layer)].astype(BF16)
        if nme in ("attn_w_qkv", "rec_w_in", "ffn_w_up"):
            K, n = sh.shape
            structs.append(jax.ShapeDtypeStruct((K, n * N_DEV), BF16))
            wins.append(_col_window(n, _ffn_block_of_shard if nme == "ffn_w_up" else None))
        elif nme in ("rec_w_a", "rec_w_i"):
            nblk, r, bd = sh.shape
            structs.append(jax.ShapeDtypeStruct((nblk, r * N_DEV, bd), BF16))
            wins.append(_gate_window(r))
        else:
            r, N = sh.shape
            structs.append(jax.ShapeDtypeStruct((r * N_DEV, N), BF16))
            wins.append(_row_window(r))
        shards.append(sh)
    return gather_exchange(shards, structs, wins)


def _send_layout(name, t):
    if name in ("rec_w_a", "rec_w_i"):
        nblk, bd, _ = t.shape
        return jnp.transpose(t.reshape(nblk, N_DEV, bd // N_DEV, bd), (1, 0, 2, 3)).astype(BF16)
    if name in ("attn_w_o", "rec_w_out", "ffn_w_down"):
        return t.reshape((N_DEV, t.shape[0] // N_DEV) + t.shape[1:])
    return t


class _Plan:
    BWD_SLOT = {"up_dw": "down", "in_dw": "out", "in_dx": "in"}

    def __init__(self, p, w, contribs, depth):
        self.p, self.w, self.contribs, self.depth = p, w, contribs, depth
        self.pending = None

    def carry(self, stage, l, slot, g):
        if stage == "fwd":
            items = []
            if l == 0 and slot == "in":
                items += [(k, 0) for k in _slot_names(0, "out")]
            if l == 0 and slot == "attn":
                items += [(k, 0) for s in ("up", "down") for k in _slot_names(0, s)]
            if slot != "attn" and l + 1 < self.depth:
                items += [(k, l + 1) for k in _slot_names(l + 1, slot)]
            if not items:
                return None
            self.pending = items
            return _gather_for(self.p, items)
        if slot == "act_bwd":
            items = [("ffn_w_up", l + 1)] if l + 1 < self.depth else []
        elif slot == "up_dx":
            items = [("ffn_w_up", 0)] if l == 0 else []
        else:
            items = [(k, l) for k in _slot_names(l, self.BWD_SLOT[slot])]
        if not items:
            return None
        self.pending = items
        return scatter_exchange([_send_layout(k, g[k][_idx(k, layer)]) for k, layer in items])

    def done(self, stage, l, slot, outs):
        dst = self.w if stage == "fwd" else self.contribs
        for (k, layer), o in zip(self.pending, outs):
            dst[k][_idx(k, layer)] = o


def _train_step(p, x, target, mom, vel):
    depth = p["mix_norm"].shape[0]
    s_me = _sid(_here())

    w = {k: [None] * p[k].shape[0] for k in BIG}
    first = [(k, 0) for k in _slot_names(0, "in")]
    for (k, _), t in zip(first, run_exchange(_gather_for(p, first), name="all_gather_first")):
        w[k][0] = t

    local_small = [p[k] for k in SMALL_SHARDED]
    packed = _pack(local_small)
    gathered, = run_exchange(
        gather_exchange([packed], [jax.ShapeDtypeStruct((N_DEV,) + packed.shape, F32)], [_slot_window]),
        name="all_gather_small")
    per_dev = [_unpack(gathered[s], [a.shape for a in local_small]) for s in range(N_DEV)]
    for i, k in enumerate(SMALL_SHARDED):
        w[k] = _gather_last(jnp.stack([per_dev[s][i] for s in range(N_DEV)]))
    for k in SMALL_REPLICATED:
        w[k] = p[k]
    nrec = w["rec_b_a"].shape[0]
    w["rec_b_a"] = w["rec_b_a"].reshape(nrec, -1)
    w["rec_b_i"] = w["rec_b_i"].reshape(nrec, -1)
    w["ffn_conv_w"] = _group_cols(w["ffn_conv_w"])
    w["ffn_conv_b"] = _group_cols(w["ffn_conv_b"])

    contribs = {k: [None] * len(w[k]) for k in BIG}
    loss_vec, dx, g = local_step(x[0], target[0], w, _Plan(p, w, contribs, depth))
    loss = lax.psum(jnp.sum(loss_vec), ("x", "y", "c"))

    out = {}
    for k in BIG:
        shp = p[k].shape
        L = shp[0]
        C = shp[-1]
        R = int(np.prod(shp[1:-1]))
        cs = [c.reshape(N_DEV, R, C) for c in contribs[k]]
        res = adamw_family(cs, p[k].reshape(L, R, C), mom[k].reshape(L, R, C), vel[k].reshape(L, R, C),
                           name="adamw_" + k)
        out[k] = [t.reshape(shp) for t in res]

    gsmall = [jnp.stack(g[k]) for k in SMALL]
    gp = _pack(gsmall)
    gall, = run_exchange(
        gather_exchange([gp], [jax.ShapeDtypeStruct((N_DEV,) + gp.shape, F32)], [_slot_window]),
        name="all_gather_small_grads")
    gsum = _unpack(sum_slots(gall, name="sum_small_grads"), [a.shape for a in gsmall])
    glocal = []
    for k, t in zip(SMALL, gsum):
        if k in SMALL_SHARDED:
            t = _own_last(t.reshape(p[k].shape[:-1] + (p[k].shape[-1] * N_DEV,)), s_me)
        glocal.append(t.reshape(p[k].shape))
    wp, mp, vp, gpk = (_pack([d[k] for k in SMALL]) for d in (p, mom, vel, dict(zip(SMALL, glocal))))
    res = adamw_family([gpk[None]], wp[None], mp[None], vp[None], name="adamw_small")
    shapes = [p[k].shape for k in SMALL]
    unp = [_unpack(t[0], shapes) for t in res]
    for i, k in enumerate(SMALL):
        out[k] = [glocal[i], unp[1][i], unp[2][i], unp[3][i]]

    return (loss, dx[None]) + tuple(out[k][q] for q in range(4) for k in WEIGHTS)


def kernel(x, mix_norm, ffn_norm, attn_w_qkv, attn_q_gain, attn_k_gain, attn_sinks, attn_w_o, rec_w_in, rec_conv_w, rec_conv_b, rec_w_a, rec_b_a, rec_w_i, rec_b_i, rec_lambda, rec_w_out, ffn_w_up, ffn_conv_w, ffn_conv_b, ffn_w_down, loss_target, m_mix_norm, m_ffn_norm, m_attn_w_qkv, m_attn_q_gain, m_attn_k_gain, m_attn_sinks, m_attn_w_o, m_rec_w_in, m_rec_conv_w, m_rec_conv_b, m_rec_w_a, m_rec_b_a, m_rec_w_i, m_rec_b_i, m_rec_lambda, m_rec_w_out, m_ffn_w_up, m_ffn_conv_w, m_ffn_conv_b, m_ffn_w_down, v_mix_norm, v_ffn_norm, v_attn_w_qkv, v_attn_q_gain, v_attn_k_gain, v_attn_sinks, v_attn_w_o, v_rec_w_in, v_rec_conv_w, v_rec_conv_b, v_rec_w_a, v_rec_b_a, v_rec_w_i, v_rec_b_i, v_rec_lambda, v_rec_w_out, v_ffn_w_up, v_ffn_conv_w, v_ffn_conv_b, v_ffn_w_down):
    p = dict(zip(WEIGHTS, (mix_norm, ffn_norm, attn_w_qkv, attn_q_gain, attn_k_gain, attn_sinks, attn_w_o, rec_w_in,
                           rec_conv_w, rec_conv_b, rec_w_a, rec_b_a, rec_w_i, rec_b_i, rec_lambda, rec_w_out,
                           ffn_w_up, ffn_conv_w, ffn_conv_b, ffn_w_down)))
    mom = dict(zip(WEIGHTS, (m_mix_norm, m_ffn_norm, m_attn_w_qkv, m_attn_q_gain, m_attn_k_gain, m_attn_sinks,
                             m_attn_w_o, m_rec_w_in, m_rec_conv_w, m_rec_conv_b, m_rec_w_a, m_rec_b_a, m_rec_w_i,
                             m_rec_b_i, m_rec_lambda, m_rec_w_out, m_ffn_w_up, m_ffn_conv_w, m_ffn_conv_b,
                             m_ffn_w_down)))
    vel = dict(zip(WEIGHTS, (v_mix_norm, v_ffn_norm, v_attn_w_qkv, v_attn_q_gain, v_attn_k_gain, v_attn_sinks,
                             v_attn_w_o, v_rec_w_in, v_rec_conv_w, v_rec_conv_b, v_rec_w_a, v_rec_b_a, v_rec_w_i,
                             v_rec_b_i, v_rec_lambda, v_rec_w_out, v_ffn_w_up, v_ffn_conv_w, v_ffn_conv_b,
                             v_ffn_w_down)))
    return _train_step(p, x, loss_target, mom, vel)
```

```python
import functools
import math

import jax
import jax.numpy as jnp
import numpy as np
from jax import lax
from jax.experimental import pallas as pl
from jax.experimental.pallas import tpu as pltpu

F32 = jnp.float32
BF16 = jnp.bfloat16

N_DEV = 8
HEAD_DIM = 64
GROUP = 4
BLOCK = 128
LRU_C = 8.0
EPS = 1e-6
HALO = 16
ADAM_LR, ADAM_B1, ADAM_B2, ADAM_EPS, ADAM_WD, ADAM_STEP = 0.001, 0.9, 0.999, 1e-08, 0.01, 10
VMEM_LIMIT = 56 * 1024 * 1024
MESH = pl.DeviceIdType.MESH
GELU_C = math.sqrt(2.0 / math.pi)


def _cp(*sem, vmem=VMEM_LIMIT):
    return pltpu.CompilerParams(dimension_semantics=tuple(sem), vmem_limit_bytes=vmem)


def _tile(dim, pref, mult=128):
    if dim <= pref:
        return dim
    t = (pref // mult) * mult
    while t >= mult:
        if dim % t == 0:
            return t
        t -= mult
    return dim


def _gelu(x):
    th = jnp.tanh(GELU_C * (x + 0.044715 * x * x * x))
    return 0.5 * x * (1.0 + th)


def _gelu_and_grad(x):
    x2 = x * x
    th = jnp.tanh(x * (GELU_C + (GELU_C * 0.044715) * x2))
    a = 0.5 + 0.5 * th
    g = x * a
    dg = a + g * (1.0 - th) * (GELU_C + (3.0 * GELU_C * 0.044715) * x2)
    return g, dg


def _dot(a, b, dims):
    return lax.dot_general(a.astype(BF16), b.astype(BF16), (dims, ((), ())), preferred_element_type=F32)


NN = ((1,), (0,))
NT = ((1,), (1,))
TN = ((0,), (0,))


def _matmul(a, b, *, dims, grid, a_spec, b_spec, o_spec, out_shape, acc_shape, res=None, res_spec=None,
            carry=None, name):
    ni, nj, nk = grid
    nres = 0 if res is None else 1
    ncar = 0 if carry is None else len(carry["arrays"])

    def body(*refs):
        a_ref, b_ref = refs[0], refs[1]
        r_ref = refs[2] if nres else None
        car_in = refs[2 + nres:2 + nres + ncar]
        o_ref = refs[2 + nres + ncar]
        car_out = refs[3 + nres + ncar:3 + nres + 2 * ncar]
        scratch = refs[3 + nres + 2 * ncar:]
        i, j, k = pl.program_id(0), pl.program_id(1), pl.program_id(2)

        if carry is not None:
            @pl.when((i == 0) & (j == 0) & (k == 0))
            def _():
                carry["start"](car_in, car_out, scratch[1:])

        def finish(acc):
            if r_ref is not None:
                acc = acc + r_ref[...].astype(F32)
            o_ref[...] = acc.astype(o_ref.dtype)

        if nk == 1:
            finish(_dot(a_ref[...], b_ref[...], dims))
        else:
            acc_ref = scratch[0]

            @pl.when(k == 0)
            def _():
                acc_ref[...] = _dot(a_ref[...], b_ref[...], dims)

            if nk > 2:
                @pl.when((k > 0) & (k < nk - 1))
                def _():
                    acc_ref[...] += _dot(a_ref[...], b_ref[...], dims)

            @pl.when(k == nk - 1)
            def _():
                finish(acc_ref[...] + _dot(a_ref[...], b_ref[...], dims))

        if carry is not None:
            @pl.when((i == ni - 1) & (j == nj - 1) & (k == nk - 1))
            def _():
                carry["finish"](car_in, car_out, scratch[1:])

    in_specs = [a_spec, b_spec] + ([res_spec] if nres else [])
    args = (a, b) + ((res,) if nres else ())
    out_specs, out_shapes = [o_spec], [out_shape]
    scratch_shapes = [pltpu.VMEM(acc_shape if nk > 1 else (8, 128), F32)]
    sem = ("parallel", "parallel", "arbitrary")
    if carry is not None:
        in_specs += [HBM] * ncar
        args += tuple(carry["arrays"])
        out_specs += [HBM] * ncar
        out_shapes += list(carry["out_structs"])
        scratch_shapes += carry["sems"]
        sem = ("arbitrary", "arbitrary", "arbitrary")
    outs = pl.pallas_call(
        body, grid=grid, in_specs=in_specs, out_specs=out_specs, out_shape=out_shapes,
        scratch_shapes=scratch_shapes, compiler_params=_cp(*sem), name=name,
    )(*args)
    return outs[0] if carry is None else (outs[0], list(outs[1:]))


def mm_nn(a, b, *, out_dtype, res=None, carry=None, name):
    M, K = a.shape
    N = b.shape[1]
    tm, tn, tk = _tile(M, 1024), _tile(N, 1024), _tile(K, 2048)
    return _matmul(
        a, b, dims=NN, grid=(M // tm, N // tn, K // tk),
        a_spec=pl.BlockSpec((tm, tk), lambda i, j, k: (i, k)),
        b_spec=pl.BlockSpec((tk, tn), lambda i, j, k: (k, j)),
        o_spec=pl.BlockSpec((tm, tn), lambda i, j, k: (i, j)),
        out_shape=jax.ShapeDtypeStruct((M, N), out_dtype), acc_shape=(tm, tn),
        res=res, res_spec=pl.BlockSpec((tm, tn), lambda i, j, k: (i, j)), carry=carry, name=name)


def mm_nt(a, b, *, out_dtype, res=None, carry=None, name):
    M, N = a.shape
    K = b.shape[0]
    tm, tn, tk = _tile(M, 1024), _tile(K, 1024), _tile(N, 2048)
    return _matmul(
        a, b, dims=NT, grid=(M // tm, K // tn, N // tk),
        a_spec=pl.BlockSpec((tm, tk), lambda i, j, k: (i, k)),
        b_spec=pl.BlockSpec((tn, tk), lambda i, j, k: (j, k)),
        o_spec=pl.BlockSpec((tm, tn), lambda i, j, k: (i, j)),
        out_shape=jax.ShapeDtypeStruct((M, K), out_dtype), acc_shape=(tm, tn),
        res=res, res_spec=pl.BlockSpec((tm, tn), lambda i, j, k: (i, j)), carry=carry, name=name)


def mm_tn(a, b, *, out_dtype, col_shards=None, shard_of_block=None, carry=None, name):
    T, K = a.shape
    N = b.shape[1]
    tt = _tile(T, 2048)
    tm = _tile(K, 1024)
    if col_shards is None:
        tn = _tile(N, 1024)
        o_spec = pl.BlockSpec((tm, tn), lambda i, j, k: (i, j))
        out_shape = jax.ShapeDtypeStruct((K, N), out_dtype)
    else:
        n = N // col_shards
        tn = _tile(n, 1536)
        per = n // tn
        sob = shard_of_block if shard_of_block is not None else (lambda s: s)
        o_spec = pl.BlockSpec((None, tm, tn), lambda i, j, k: (sob(j // per), i, j % per))
        out_shape = jax.ShapeDtypeStruct((col_shards, K, n), out_dtype)
    return _matmul(
        a, b, dims=TN, grid=(K // tm, N // tn, T // tt),
        a_spec=pl.BlockSpec((tt, tm), lambda i, j, k: (k, i)),
        b_spec=pl.BlockSpec((tt, tn), lambda i, j, k: (k, j)),
        o_spec=o_spec, out_shape=out_shape, acc_shape=(tm, tn), carry=carry, name=name)


def rmsnorm_fwd(x, g, *, name):
    T, D = x.shape
    tm = _tile(T, 512, 8)

    def body(x_ref, g_ref, o_ref):
        xv = x_ref[...]
        r = lax.rsqrt(jnp.mean(xv * xv, axis=-1, keepdims=True) + EPS)
        o_ref[...] = (xv * r * g_ref[...]).astype(o_ref.dtype)

    return pl.pallas_call(
        body, grid=(T // tm,),
        in_specs=[pl.BlockSpec((tm, D), lambda i: (i, 0)), pl.BlockSpec((1, D), lambda i: (0, 0))],
        out_specs=pl.BlockSpec((tm, D), lambda i: (i, 0)),
        out_shape=jax.ShapeDtypeStruct((T, D), BF16), compiler_params=_cp("parallel"), name=name)(x, g)


def rmsnorm_bwd(x, g, dh, dres, *, name):
    T, D = x.shape
    tm = _tile(T, 512, 16)

    def body(x_ref, g_ref, dh_ref, dres_ref, dx_ref, dxb_ref, dg_ref):
        @pl.when(pl.program_id(0) == 0)
        def _():
            dg_ref[...] = jnp.zeros_like(dg_ref)

        xv = x_ref[...]
        dh_v = dh_ref[...].astype(F32)
        r = lax.rsqrt(jnp.mean(xv * xv, axis=-1, keepdims=True) + EPS)
        u = dh_v * g_ref[...]
        dot = jnp.mean(u * xv, axis=-1, keepdims=True)
        dx = dres_ref[...] + r * u - xv * (r * r * r * dot)
        dx_ref[...] = dx
        dxb_ref[...] = dx.astype(dxb_ref.dtype)
        dg_ref[...] += jnp.sum(dh_v * xv * r, axis=0, keepdims=True)

    row = pl.BlockSpec((tm, D), lambda i: (i, 0))
    vec = pl.BlockSpec((1, D), lambda i: (0, 0))
    return pl.pallas_call(
        body, grid=(T // tm,), in_specs=[row, vec, row, row], out_specs=[row, row, vec],
        out_shape=[jax.ShapeDtypeStruct((T, D), F32), jax.ShapeDtypeStruct((T, D), BF16),
                   jax.ShapeDtypeStruct((1, D), F32)],
        compiler_params=_cp("arbitrary"), name=name)(x, g, dh, dres)


def loss_head(y, target, *, name):
    T, D = y.shape
    tm = _tile(T, 512, 16)

    def body(y_ref, t_ref, l_ref, dy_ref, dyb_ref):
        @pl.when(pl.program_id(0) == 0)
        def _():
            l_ref[...] = jnp.zeros_like(l_ref)

        e = y_ref[...] - t_ref[...]
        dy = e * (1.0 / D)
        dy_ref[...] = dy
        dyb_ref[...] = dy.astype(dyb_ref.dtype)
        l_ref[...] += jnp.sum(e * e, axis=0, keepdims=True) * (0.5 / D)

    row = pl.BlockSpec((tm, D), lambda i: (i, 0))
    vec = pl.BlockSpec((1, D), lambda i: (0, 0))
    return pl.pallas_call(
        body, grid=(T // tm,), in_specs=[row, row], out_specs=[vec, row, row],
        out_shape=[jax.ShapeDtypeStruct((1, D), F32), jax.ShapeDtypeStruct((T, D), F32),
                   jax.ShapeDtypeStruct((T, D), BF16)],
        compiler_params=_cp("arbitrary"), name=name)(y, target)


NEG = -1e30


def _kv_heads_per_step(hkv):
    return 4 if hkv % 4 == 0 else 2


LANE = 128


def _hi_lo_dot(x, w):
    hi = x.astype(BF16)
    lo = x - hi.astype(F32)
    return _dot(hi, w, NN) + _dot(lo, w, NN)


def _head_sum_matrix():
    r = lax.broadcasted_iota(jnp.int32, (LANE, LANE), 0) // HEAD_DIM
    c = lax.broadcasted_iota(jnp.int32, (LANE, LANE), 1) // HEAD_DIM
    return jnp.where(r == c, 1.0, 0.0).astype(BF16)


def _chunk(x, c):
    return x[:, c * LANE:(c + 1) * LANE]


def _head_sums(x, e):
    return jnp.concatenate([_hi_lo_dot(_chunk(x, c), e) for c in range(x.shape[1] // LANE)], axis=1)


def _row_sums(x):
    return _hi_lo_dot(x, jnp.ones((x.shape[1], LANE), BF16))


def _headnorm(x, e):
    r = lax.rsqrt(_head_sums(x * x, e) * (1.0 / HEAD_DIM) + EPS)
    return x * r, r


def _attn_mask(n):
    qi = lax.broadcasted_iota(jnp.int32, (BLOCK, 2 * BLOCK), 0)
    kj = lax.broadcasted_iota(jnp.int32, (BLOCK, 2 * BLOCK), 1)
    rel = qi + BLOCK - kj
    m = (rel >= 0) & (rel < BLOCK) & ((kj >= BLOCK) | (n > 0))
    return jnp.concatenate([m] * GROUP, axis=0)


def _qk_operands(qhat, khat, j, lo_half):
    kb = jnp.where(lo_half == (j % 2 == 0), _chunk(khat, j // 2), 0.0)
    kd = kb + pltpu.roll(kb, HEAD_DIM, 1)
    q4 = jnp.concatenate(
        [jnp.where(lo_half == (g % 2 == 0), _chunk(qhat, 2 * j + g // 2), 0.0) for g in range(GROUP)], axis=0)
    return q4, kd


def _attn_softmax(s, sink_ref, mask4, head0):
    s = s * (HEAD_DIM ** -0.5)
    sink = jnp.concatenate([jnp.full((BLOCK, 1), sink_ref[0, head0 + g], F32) for g in range(GROUP)], axis=0)
    m = jnp.maximum(jnp.max(jnp.where(mask4, s, NEG), axis=-1, keepdims=True), sink)
    ex = jnp.where(mask4, jnp.exp(s - m), 0.0)
    es = jnp.exp(sink - m)
    inv = 1.0 / (_row_sums(ex) + es)
    return ex * jnp.concatenate([inv] * (s.shape[1] // LANE), axis=1), es * inv[:, 0:1]


def attn_fwd(qkv, q_gain, k_gain, sinks, *, carry=None, name):
    ncar = 0 if carry is None else len(carry["arrays"])
    T, W = qkv.shape
    hq = W // HEAD_DIM * GROUP // (GROUP + 2)
    dq = hq * HEAD_DIM
    PAIR = _kv_heads_per_step(hq // GROUP)
    QW, KW = PAIR * GROUP * HEAD_DIM, PAIR * HEAD_DIM
    npair = hq // (GROUP * PAIR)
    nb = T // BLOCK
    k0 = dq // KW
    v0 = k0 + npair

    def body(q_ref, kp_ref, kc_ref, vp_ref, vc_ref, qg_ref, kg_ref, sink_ref, *rest):
        car_in, o_ref, car_out, sems = rest[:ncar], rest[ncar], rest[ncar + 1:2 * ncar + 1], rest[2 * ncar + 1:]
        p, n = pl.program_id(0), pl.program_id(1)
        if carry is not None:
            @pl.when((p == 0) & (n == 0))
            def _():
                carry["start"](car_in, car_out, sems)

            @pl.when((p == npair - 1) & (n == nb - 1))
            def _():
                carry["finish"](car_in, car_out, sems)

        e = _head_sum_matrix()
        lo_half = lax.broadcasted_iota(jnp.int32, (1, LANE), 1) < HEAD_DIM
        mask4 = _attn_mask(n)
        qn, _ = _headnorm(q_ref[...], e)
        qhat = _tapmul(qn, qg_ref[...])
        kn, _ = _headnorm(jnp.concatenate([kp_ref[...], kc_ref[...]], axis=0), e)
        khat = _tapmul(kn, kg_ref[...])
        vwin = jnp.concatenate([vp_ref[...], vc_ref[...]], axis=0).astype(BF16)
        ops = [_qk_operands(qhat, khat, j, lo_half) for j in range(PAIR)]
        scores = [_dot(q4, kd, NT) for q4, kd in ops]
        probs = [_attn_softmax(s, sink_ref, mask4, (p * PAIR + j) * GROUP)[0] for j, s in enumerate(scores)]
        o4 = [_dot(probs[j], _chunk(vwin, j // 2), NN) for j in range(PAIR)]
        chunks = []
        for c in range(2 * PAIR):
            j, t = c // 2, c % 2
            a = o4[j][(2 * t) * BLOCK:(2 * t + 1) * BLOCK]
            b = o4[j][(2 * t + 1) * BLOCK:(2 * t + 2) * BLOCK]
            if j % 2 == 0:
                b = pltpu.roll(b, HEAD_DIM, 1)
            else:
                a = pltpu.roll(a, HEAD_DIM, 1)
            chunks.append(jnp.where(lo_half, a, b))
        o_ref[...] = jnp.concatenate(chunks, axis=1).astype(o_ref.dtype)

    prev = lambda n: jnp.maximum(n - 1, 0)
    car = carry if carry is not None else dict(arrays=[], out_structs=[], sems=[])
    qg8 = _rep8(jnp.tile(q_gain[0], PAIR * GROUP))
    kg8 = _rep8(jnp.tile(k_gain[0], PAIR))
    outs = pl.pallas_call(
        body, grid=(npair, nb),
        in_specs=[pl.BlockSpec((BLOCK, QW), lambda p, n: (n, p)),
                  pl.BlockSpec((BLOCK, KW), lambda p, n: (prev(n), k0 + p)),
                  pl.BlockSpec((BLOCK, KW), lambda p, n: (n, k0 + p)),
                  pl.BlockSpec((BLOCK, KW), lambda p, n: (prev(n), v0 + p)),
                  pl.BlockSpec((BLOCK, KW), lambda p, n: (n, v0 + p)),
                  pl.BlockSpec((8, QW), lambda p, n: (0, 0)), pl.BlockSpec((8, KW), lambda p, n: (0, 0)),
                  pl.BlockSpec(memory_space=pltpu.SMEM)] + [HBM] * ncar,
        out_specs=[pl.BlockSpec((BLOCK, QW), lambda p, n: (n, p))] + [HBM] * ncar,
        out_shape=[jax.ShapeDtypeStruct((T, dq), BF16)] + list(car["out_structs"]),
        scratch_shapes=list(car["sems"]),
        compiler_params=_cp(*(("parallel", "parallel") if carry is None else ("arbitrary", "arbitrary"))),
        name=name)(qkv, qkv, qkv, qkv, qkv, qg8, kg8, sinks, *car["arrays"])
    return outs[0] if carry is None else (outs[0], list(outs[1:]))


def attn_bwd(qkv, d_out, q_gain, k_gain, sinks, *, name):
    T, W = qkv.shape
    hq = W // HEAD_DIM * GROUP // (GROUP + 2)
    dq_w = hq * HEAD_DIM
    PAIR = _kv_heads_per_step(hq // GROUP)
    QW, KW = PAIR * GROUP * HEAD_DIM, PAIR * HEAD_DIM
    npair = hq // (GROUP * PAIR)
    nb = T // BLOCK
    k0 = dq_w // KW
    v0 = k0 + npair

    def body(q_ref, kp_ref, kc_ref, vp_ref, vc_ref, do_ref, qg_ref, kg_ref, sink_ref,
             dq_ref, dk_ref, dv_ref, dqg_ref, dkg_ref, dsink_ref, dk_carry, dv_carry):
        p, i = pl.program_id(0), pl.program_id(1)
        n = nb - 1 - i

        @pl.when(i == 0)
        def _():
            dk_carry[...] = jnp.zeros_like(dk_carry)
            dv_carry[...] = jnp.zeros_like(dv_carry)
            dqg_ref[...] = jnp.zeros_like(dqg_ref)
            dkg_ref[...] = jnp.zeros_like(dkg_ref)
            dsink_ref[...] = jnp.zeros_like(dsink_ref)

        e = _head_sum_matrix()
        lo_half = lax.broadcasted_iota(jnp.int32, (1, LANE), 1) < HEAD_DIM
        mask4 = _attn_mask(n)
        qg, kg = qg_ref[...], kg_ref[...]
        q = q_ref[...]
        kwin = jnp.concatenate([kp_ref[...], kc_ref[...]], axis=0)
        qn, qr = _headnorm(q, e)
        qhat = _tapmul(qn, qg)
        kn, kr = _headnorm(kwin, e)
        khat = _tapmul(kn, kg)
        vwin = jnp.concatenate([vp_ref[...], vc_ref[...]], axis=0).astype(BF16)
        do = do_ref[...].astype(F32)
        ops = [_qk_operands(qhat, khat, j, lo_half) for j in range(PAIR)]
        do4 = []
        for j in range(PAIR):
            parts = []
            for g in range(GROUP):
                dc = _chunk(do, 2 * j + g // 2)
                if g % 2 != j % 2:
                    dc = pltpu.roll(dc, HEAD_DIM, 1)
                parts.append(jnp.where(lo_half == (j % 2 == 0), dc, 0.0))
            do4.append(jnp.concatenate(parts, axis=0))
        scores = [_dot(q4, kd, NT) for q4, kd in ops]
        dps = [_dot(do4[j], _chunk(vwin, j // 2), NT) for j in range(PAIR)]
        soft = [_attn_softmax(s, sink_ref, mask4, (p * PAIR + j) * GROUP) for j, s in enumerate(scores)]
        dss, dsink_rows = [], []
        for j in range(PAIR):
            pr, psink = soft[j]
            delta = _row_sums(pr * dps[j])
            dss.append(pr * (dps[j] - jnp.concatenate([delta] * 2, axis=1)) * (HEAD_DIM ** -0.5))
            dsk = -psink * delta[:, 0:1]
            for g in range(GROUP):
                tot = jnp.sum(dsk[g * BLOCK:(g + 1) * BLOCK], axis=0, keepdims=True)
                dsink_rows.append(jnp.broadcast_to(tot, (1, LANE)))
        dq4 = [_dot(dss[j], ops[j][1], NN) for j in range(PAIR)]
        dkd = [_dot(dss[j], ops[j][0], TN) for j in range(PAIR)]
        dvc = [_dot(soft[j][0], do4[j], TN) for j in range(PAIR)]
        dqhat = jnp.concatenate(
            [jnp.where(lo_half, dq4[c // 2][(2 * (c % 2)) * BLOCK:(2 * (c % 2) + 1) * BLOCK],
                       dq4[c // 2][(2 * (c % 2) + 1) * BLOCK:(2 * (c % 2) + 2) * BLOCK]) for c in range(2 * PAIR)],
            axis=1)
        dkhat_chunks, dv_chunks = [], []
        for kc in range(PAIR // 2):
            tot_k, tot_v = None, None
            for j in (2 * kc, 2 * kc + 1):
                t = jnp.where(lo_half == (j % 2 == 0), dkd[j] + pltpu.roll(dkd[j], HEAD_DIM, 1), 0.0)
                tot_k = t if tot_k is None else tot_k + t
                tot_v = dvc[j] if tot_v is None else tot_v + dvc[j]
            dkhat_chunks.append(tot_k)
            dv_chunks.append(tot_v)
        dkhat_win = jnp.concatenate(dkhat_chunks, axis=1)
        dv_win = jnp.concatenate(dv_chunks, axis=1)
        dqg_ref[...] += _colsum8(dqhat * qn)
        dqn = _tapmul(dqhat, qg)
        dq_ref[...] = (qr * dqn - q * (qr * qr * qr * _head_sums(dqn * q, e) * (1.0 / HEAD_DIM))).astype(dq_ref.dtype)
        dkh = dkhat_win[BLOCK:] + dk_carry[...]
        dk_carry[...] = dkhat_win[:BLOCK]
        kcur, knc, krc = kwin[BLOCK:], kn[BLOCK:], kr[BLOCK:]
        dkg_ref[...] += _colsum8(dkh * knc)
        dkn = _tapmul(dkh, kg)
        dk_ref[...] = (krc * dkn
                       - kcur * (krc * krc * krc * _head_sums(dkn * kcur, e) * (1.0 / HEAD_DIM))).astype(dk_ref.dtype)
        dv_ref[...] = (dv_win[BLOCK:] + dv_carry[...]).astype(dv_ref.dtype)
        dv_carry[...] = dv_win[:BLOCK]
        dsink_ref[...] += jnp.concatenate(dsink_rows, axis=0)

    rev = lambda i: nb - 1 - i
    prev = lambda i: jnp.maximum(nb - 2 - i, 0)
    qg8 = _rep8(jnp.tile(q_gain[0], PAIR * GROUP))
    kg8 = _rep8(jnp.tile(k_gain[0], PAIR))
    dq, dk, dv, dqg, dkg, dsink = pl.pallas_call(
        body, grid=(npair, nb),
        in_specs=[pl.BlockSpec((BLOCK, QW), lambda p, i: (rev(i), p)),
                  pl.BlockSpec((BLOCK, KW), lambda p, i: (prev(i), k0 + p)),
                  pl.BlockSpec((BLOCK, KW), lambda p, i: (rev(i), k0 + p)),
                  pl.BlockSpec((BLOCK, KW), lambda p, i: (prev(i), v0 + p)),
                  pl.BlockSpec((BLOCK, KW), lambda p, i: (rev(i), v0 + p)),
                  pl.BlockSpec((BLOCK, QW), lambda p, i: (rev(i), p)),
                  pl.BlockSpec((8, QW), lambda p, i: (0, 0)), pl.BlockSpec((8, KW), lambda p, i: (0, 0)),
                  pl.BlockSpec(memory_space=pltpu.SMEM)],
        out_specs=[pl.BlockSpec((BLOCK, QW), lambda p, i: (rev(i), p)),
                   pl.BlockSpec((BLOCK, KW), lambda p, i: (rev(i), p)),
                   pl.BlockSpec((BLOCK, KW), lambda p, i: (rev(i), p)),
                   pl.BlockSpec((None, 8, QW), lambda p, i: (p, 0, 0)),
                   pl.BlockSpec((None, 8, KW), lambda p, i: (p, 0, 0)),
                   pl.BlockSpec((None, PAIR * GROUP, LANE), lambda p, i: (p, 0, 0))],
        out_shape=[jax.ShapeDtypeStruct((T, dq_w), BF16),
                   jax.ShapeDtypeStruct((T, npair * KW), BF16),
                   jax.ShapeDtypeStruct((T, npair * KW), BF16),
                   jax.ShapeDtypeStruct((npair, 8, QW), F32),
                   jax.ShapeDtypeStruct((npair, 8, KW), F32),
                   jax.ShapeDtypeStruct((npair, PAIR * GROUP, LANE), F32)],
        scratch_shapes=[pltpu.VMEM((BLOCK, KW), F32), pltpu.VMEM((BLOCK, KW), F32)],
        compiler_params=_cp("parallel", "arbitrary"), name=name,
    )(qkv, qkv, qkv, qkv, qkv, d_out, qg8, kg8, sinks)
    dqg = jnp.sum(dqg.reshape(-1, HEAD_DIM), axis=0, keepdims=True)
    dkg = jnp.sum(dkg.reshape(-1, HEAD_DIM), axis=0, keepdims=True)
    return dq, dk, dv, dqg, dkg, dsink[:, :, 0].reshape(-1)


def _softplus_neg(lam):
    return jnp.maximum(-lam, 0.0) + jnp.log1p(jnp.exp(-jnp.abs(lam)))


def _causal_conv(x, prev8, cw_ref, cb8):
    K = cw_ref.shape[0]
    acc = _tapmul(x, cw_ref[K - 1])
    for s in range(1, K):
        acc = acc + _tapmul(_shift_down(x, s, prev8), cw_ref[K - 1 - s])
    return (_rows8(acc) + cb8[None]).reshape(x.shape)


def _bcast_row(x, row):
    return jnp.broadcast_to(x[row:row + 1, :], x.shape)


def _rows8(x):
    return x.reshape(x.shape[0] // 8, 8, x.shape[1])


def _tapmul(x, w8):
    return (_rows8(x) * w8[None]).reshape(x.shape)


def _colsum8(x):
    return jnp.sum(_rows8(x), axis=0)


def _shift_down(x, s, prev8):
    r = pltpu.roll(x, s, 0)
    rowid = lax.broadcasted_iota(jnp.int32, (8, 1), 0)
    head = jnp.where(rowid >= s, r[0:8], pltpu.roll(prev8, s, 0))
    return jnp.concatenate([head, r[8:]], axis=0)


def _shift_up(x, s, next8):
    R = x.shape[0]
    r = pltpu.roll(x, R - s, 0)
    rowid = lax.broadcasted_iota(jnp.int32, (8, 1), 0)
    tail = jnp.where(rowid < 8 - s, r[R - 8:], pltpu.roll(next8, 8 - s, 0))
    return jnp.concatenate([r[:R - 8], tail], axis=0)


def _rep8(v):
    return jnp.broadcast_to(v[..., None, :], v.shape[:-1] + (8, v.shape[-1]))


def rec_fwd(z, cw, cb, wa, ba, wi, bi, lam, *, name):
    T, C2 = z.shape
    C = C2 // 2
    nblk, bd, _ = wa.shape
    tt = _tile(T, 128, HALO)
    ng = tt // 8

    def body(x_ref, y_ref, halo_ref, cw_ref, cb_ref, wa_ref, ba_ref, wi_ref, bi_ref, lam_ref,
             xb_ref, r_ref, i_ref, a_ref, h_ref, hp_ref, hg_ref, carry, u_scr):
        step = pl.program_id(0)

        @pl.when(step == 0)
        def _():
            carry[...] = jnp.zeros_like(carry)

        xb = _causal_conv(x_ref[...], jnp.where(step > 0, halo_ref[HALO - 8:, :], 0.0), cw_ref, cb_ref[...])
        xb_ref[...] = xb
        pa, pi = [], []
        for b in range(nblk):
            xs = xb[:, b * bd:(b + 1) * bd]
            pa.append(_dot(xs, wa_ref[b], NN))
            pi.append(_dot(xs, wi_ref[b], NN))
        r = jax.nn.sigmoid(jnp.concatenate(pa, axis=1) + ba_ref[...])
        ig = jax.nn.sigmoid(jnp.concatenate(pi, axis=1) + bi_ref[...])
        r_ref[...] = r
        i_ref[...] = ig
        nl = LRU_C * r * _softplus_neg(lam_ref[...])
        a_ref[...] = jnp.exp(-nl)
        th = jnp.tanh(nl)
        u_scr[...] = jnp.sqrt(2.0 * th / (1.0 + th)) * (ig * xb)

        rowid = lax.broadcasted_iota(jnp.int32, (8, C), 0)

        def group(gi, hc):
            r0 = pl.multiple_of(gi * 8, 8)
            a8 = a_ref[pl.ds(r0, 8), :]
            u8 = u_scr[pl.ds(r0, 8), :]
            for d in (1, 2, 4):
                a_sh = jnp.where(rowid >= d, pltpu.roll(a8, d, 0), 1.0)
                u_sh = jnp.where(rowid >= d, pltpu.roll(u8, d, 0), 0.0)
                u8 = a8 * u_sh + u8
                a8 = a8 * a_sh
            h8 = u8 + a8 * hc
            h_ref[pl.ds(r0, 8), :] = h8
            hp_ref[pl.ds(r0, 8), :] = jnp.where(rowid >= 1, pltpu.roll(h8, 1, 0), hc)
            return _bcast_row(h8, 7)

        carry[...] = lax.fori_loop(0, ng, group, carry[...])
        hg_ref[...] = (h_ref[...] * _gelu(y_ref[...])).astype(hg_ref.dtype)

    row = lambda c: pl.BlockSpec((tt, C), lambda i, c=c: (i, c))
    vec = pl.BlockSpec((1, C), lambda i: (0, 0))
    full = lambda shp: pl.BlockSpec(shp, lambda i, n=len(shp): (0,) * n)
    per = tt // HALO
    outs = pl.pallas_call(
        body, grid=(T // tt,),
        in_specs=[row(0), row(1), pl.BlockSpec((HALO, C), lambda i: (jnp.maximum(i * per - 1, 0), 0)),
                  full(cw.shape[:1] + (8, C)), full((8, C)), full(wa.shape), vec, full(wi.shape), vec, vec],
        out_specs=[row(0)] * 7,
        out_shape=[jax.ShapeDtypeStruct((T, C), F32)] * 6 + [jax.ShapeDtypeStruct((T, C), BF16)],
        scratch_shapes=[pltpu.VMEM((8, C), F32), pltpu.VMEM((tt, C), F32)],
        compiler_params=_cp("arbitrary"), name=name,
    )(z, z, z, _rep8(cw), _rep8(cb[0]), wa, ba, wi, bi, lam)
    return outs


def rec_bwd_scan(dhg, h, a, z, *, name):
    T, C = h.shape
    tt = _tile(T, 256, HALO)
    ng = tt // 8
    nb = T // tt

    def body(dhg_ref, h_ref, a_ref, y_ref, dy_ref, yb_ref, ycarry, acarry, g_scr):
        step = pl.program_id(0)

        @pl.when(step == 0)
        def _():
            ycarry[...] = jnp.zeros_like(ycarry)
            acarry[...] = jnp.zeros_like(acarry)

        gate, dgate = _gelu_and_grad(y_ref[...])
        dhg_v = dhg_ref[...].astype(F32)
        dy_ref[...] = (dhg_v * h_ref[...] * dgate).astype(dy_ref.dtype)
        g_scr[...] = dhg_v * gate
        rowid = lax.broadcasted_iota(jnp.int32, (8, C), 0)

        def group(j, c):
            yc, ac = c
            r0 = pl.multiple_of((ng - 1 - j) * 8, 8)
            a8 = a_ref[pl.ds(r0, 8), :]
            y8 = g_scr[pl.ds(r0, 8), :]
            b8 = jnp.where(rowid < 7, pltpu.roll(a8, 7, 0), ac)
            for d in (1, 2, 4):
                y_sh = jnp.where(rowid < 8 - d, pltpu.roll(y8, 8 - d, 0), 0.0)
                b_sh = jnp.where(rowid < 8 - d, pltpu.roll(b8, 8 - d, 0), 1.0)
                y8 = y8 + b8 * y_sh
                b8 = b8 * b_sh
            y8 = y8 + b8 * yc
            yb_ref[pl.ds(r0, 8), :] = y8
            return _bcast_row(y8, 0), _bcast_row(a8, 0)

        yc, ac = lax.fori_loop(0, ng, group, (ycarry[...], acarry[...]))
        ycarry[...] = yc
        acarry[...] = ac

    rev = lambda c: pl.BlockSpec((tt, C), lambda i, c=c: (nb - 1 - i, c))
    return pl.pallas_call(
        body, grid=(nb,), in_specs=[rev(0), rev(0), rev(0), rev(1)], out_specs=[rev(0), rev(0)],
        out_shape=[jax.ShapeDtypeStruct((T, C), BF16), jax.ShapeDtypeStruct((T, C), F32)],
        scratch_shapes=[pltpu.VMEM((8, C), F32), pltpu.VMEM((8, C), F32), pltpu.VMEM((tt, C), F32)],
        compiler_params=_cp("arbitrary"), name=name,
    )(dhg, h, a, z)


def rec_bwd_gates(ybar, hprev, a, r, ig, xb, lam, wa, wi, *, name):
    T, C = xb.shape
    nblk, bd, _ = wa.shape
    tt = _tile(T, 256, 8)
    nb = T // tt

    def body(y_ref, hp_ref, a_ref, r_ref, i_ref, xb_ref, lam_ref, wa_ref, wi_ref,
             dxb_ref, dwa_ref, dwi_ref, dba_ref, dbi_ref, dlam_ref):
        step = pl.program_id(0)

        @pl.when(step == 0)
        def _():
            for ref in (dwa_ref, dwi_ref, dba_ref, dbi_ref, dlam_ref):
                ref[...] = jnp.zeros_like(ref)

        y, av, rv, iv, xv = y_ref[...], a_ref[...], r_ref[...], i_ref[...], xb_ref[...]
        sp = _softplus_neg(lam_ref[...])
        th = jnp.tanh(LRU_C * rv * sp)
        s = jnp.sqrt(2.0 * th / (1.0 + th))
        d_nl = -(y * hp_ref[...] * av) + (y * iv * xv) * (av * av) / s
        dlam_ref[...] += jnp.sum(d_nl * rv, axis=0, keepdims=True) * LRU_C
        dr = d_nl * (LRU_C * sp)
        di = y * s * xv
        dpa = dr * rv * (1.0 - rv)
        dpi = di * iv * (1.0 - iv)
        dba_ref[...] += jnp.sum(dpa, axis=0, keepdims=True)
        dbi_ref[...] += jnp.sum(dpi, axis=0, keepdims=True)
        parts = []
        for b in range(nblk):
            sl = slice(b * bd, (b + 1) * bd)
            xs, da_b, di_b = xv[:, sl], dpa[:, sl], dpi[:, sl]
            dwa_ref[b] += _dot(xs, da_b, TN)
            dwi_ref[b] += _dot(xs, di_b, TN)
            parts.append(_dot(da_b, wa_ref[b], NT) + _dot(di_b, wi_ref[b], NT))
        dxb_ref[...] = y * s * iv + jnp.concatenate(parts, axis=1)

        @pl.when(step == nb - 1)
        def _():
            dlam_ref[...] = dlam_ref[...] * (-jax.nn.sigmoid(-lam_ref[...]))

    row = pl.BlockSpec((tt, C), lambda i: (i, 0))
    vec = pl.BlockSpec((1, C), lambda i: (0, 0))
    wsp = pl.BlockSpec(wa.shape, lambda i: (0, 0, 0))
    return pl.pallas_call(
        body, grid=(nb,), in_specs=[row] * 6 + [vec, wsp, wsp],
        out_specs=[row, wsp, wsp, vec, vec, vec],
        out_shape=[jax.ShapeDtypeStruct((T, C), F32), jax.ShapeDtypeStruct(wa.shape, F32),
                   jax.ShapeDtypeStruct(wa.shape, F32)] + [jax.ShapeDtypeStruct((1, C), F32)] * 3,
        compiler_params=_cp("arbitrary"), name=name,
    )(ybar, hprev, a, r, ig, xb, lam, wa, wi)


def conv_bwd(d, x0, cw, *, name):
    T, C = d.shape
    K = cw.shape[0]
    tt = _tile(T, 256, HALO)
    per = tt // HALO
    nb = T // tt

    def body(d_ref, dn_ref, x_ref, cw_ref, dx_ref, dcw_ref, dcb_ref, dcw_acc, dcb_acc):
        step = pl.program_id(1)

        @pl.when(step == 0)
        def _():
            dcw_acc[...] = jnp.zeros_like(dcw_acc)
            dcb_acc[...] = jnp.zeros_like(dcb_acc)

        dv = d_ref[...].astype(F32)
        next8 = jnp.where(step < nb - 1, dn_ref[...].astype(F32)[0:8], 0.0)
        xt = x_ref[...].astype(F32)
        acc = _tapmul(dv, cw_ref[K - 1])
        dcw_acc[K - 1] += _colsum8(dv * xt)
        dcb_acc[...] += _colsum8(dv)
        for sh in range(1, K):
            dsh = _shift_up(dv, sh, next8)
            acc = acc + _tapmul(dsh, cw_ref[K - 1 - sh])
            dcw_acc[K - 1 - sh] += _colsum8(dsh * xt)
        dx_ref[...] = acc.astype(dx_ref.dtype)

        @pl.when(step == nb - 1)
        def _():
            dcw_ref[...] = jnp.sum(dcw_acc[...], axis=1)
            dcb_ref[...] = jnp.sum(dcb_acc[...], axis=0, keepdims=True)

    tc = C
    row = pl.BlockSpec((tt, tc), lambda j, i: (i, j))
    return pl.pallas_call(
        body, grid=(C // tc, nb),
        in_specs=[row, pl.BlockSpec((HALO, tc), lambda j, i: (jnp.minimum((i + 1) * per, T // HALO - 1), j)),
                  row, pl.BlockSpec((K, 8, tc), lambda j, i: (0, 0, j))],
        out_specs=[row, pl.BlockSpec((K, tc), lambda j, i: (0, j)), pl.BlockSpec((1, tc), lambda j, i: (0, j))],
        out_shape=[jax.ShapeDtypeStruct((T, C), BF16), jax.ShapeDtypeStruct((K, C), F32),
                   jax.ShapeDtypeStruct((1, C), F32)],
        scratch_shapes=[pltpu.VMEM((K, 8, tc), F32), pltpu.VMEM((8, tc), F32)],
        compiler_params=_cp("parallel", "arbitrary"), name=name,
    )(d, d, x0, _rep8(cw))


def ffn_act_fwd(u0, cw, cb, *, n, name):
    T, W = u0.shape
    G = W // (2 * n)
    tt = _tile(T, 256, HALO)
    per = tt // HALO

    K = cw.shape[0]

    def body(u_ref, up_ref, cw_ref, cb_ref, a_ref, uo_ref):
        step = pl.program_id(1)
        prev8 = jnp.where(step > 0, up_ref[...].astype(F32)[HALO - 8:], 0.0)
        u = _causal_conv(u_ref[...].astype(F32), prev8, cw_ref, cb_ref[...])
        uo_ref[...] = u.astype(uo_ref.dtype)
        a_ref[...] = (_gelu(u[:, :n]) * u[:, n:]).astype(a_ref.dtype)

    return pl.pallas_call(
        body, grid=(G, T // tt),
        in_specs=[pl.BlockSpec((tt, 2 * n), lambda j, i: (i, j)),
                  pl.BlockSpec((HALO, 2 * n), lambda j, i: (jnp.maximum(i * per - 1, 0), j)),
                  pl.BlockSpec((K, 8, 2 * n), lambda j, i: (0, 0, j)),
                  pl.BlockSpec((8, 2 * n), lambda j, i: (0, j))],
        out_specs=[pl.BlockSpec((tt, n), lambda j, i: (i, j)), pl.BlockSpec((tt, 2 * n), lambda j, i: (i, j))],
        out_shape=[jax.ShapeDtypeStruct((T, G * n), BF16), jax.ShapeDtypeStruct((T, W), BF16)],
        compiler_params=_cp("parallel", "parallel"), name=name)(u0, u0, _rep8(cw), _rep8(cb[0]))


def ffn_down_act_bwd(dx, w_down, u0, u, cw, *, n, carry=None, name):
    T, W = u0.shape
    D = dx.shape[1]
    G = W // (2 * n)
    K = cw.shape[0]
    tt = _tile(T, 256, HALO)
    tc = _tile(n, 512)
    nb = T // tt
    car = carry if carry is not None else dict(arrays=[], out_structs=[], sems=[])
    ncar = len(car["arrays"])

    def body(dx_ref, w_ref, x_ref, u_ref, cw_ref, *rest):
        car_in, (du_ref, dcw_ref, dcb_ref) = rest[:ncar], rest[ncar:ncar + 3]
        car_out, (later8, dcw_acc, dcb_acc) = rest[ncar + 3:2 * ncar + 3], rest[2 * ncar + 3:2 * ncar + 6]
        sems = rest[2 * ncar + 6:]
        group, step = pl.program_id(0), pl.program_id(1)
        if carry is not None:
            @pl.when((group == 0) & (step == 0))
            def _():
                carry["start"](car_in, car_out, sems)

            @pl.when((group == G - 1) & (step == nb - 1))
            def _():
                carry["finish"](car_in, car_out, sems)

        @pl.when(step == 0)
        def _():
            later8[...] = jnp.zeros_like(later8)
            dcw_acc[...] = jnp.zeros_like(dcw_acc)
            dcb_acc[...] = jnp.zeros_like(dcb_acc)

        dact = _dot(dx_ref[...], w_ref[...], NT)
        for q in range(n // tc):
            gs, vs = slice(q * tc, (q + 1) * tc), slice(n + q * tc, n + (q + 1) * tc)
            gl, dgl = _gelu_and_grad(u_ref[:, gs].astype(F32))
            daf = dact[:, gs]
            for cols, d in ((gs, daf * u_ref[:, vs].astype(F32) * dgl), (vs, daf * gl)):
                next8 = later8[:, cols]
                xt = x_ref[:, cols].astype(F32)
                acc = _tapmul(d, cw_ref[K - 1, :, cols])
                dcw_acc[K - 1, :, cols] += _colsum8(d * xt)
                dcb_acc[:, cols] += _colsum8(d)
                for sh in range(1, K):
                    dsh = _shift_up(d, sh, next8)
                    acc = acc + _tapmul(dsh, cw_ref[K - 1 - sh, :, cols])
                    dcw_acc[K - 1 - sh, :, cols] += _colsum8(dsh * xt)
                du_ref[:, cols] = acc.astype(du_ref.dtype)
                later8[:, cols] = d[0:8]

        @pl.when(step == nb - 1)
        def _():
            dcw_ref[...] = jnp.sum(dcw_acc[...], axis=1)
            dcb_ref[...] = jnp.sum(dcb_acc[...], axis=0, keepdims=True)

    rev = lambda i: nb - 1 - i
    outs = pl.pallas_call(
        body, grid=(G, nb),
        in_specs=[pl.BlockSpec((tt, D), lambda j, i: (rev(i), 0)),
                  pl.BlockSpec((n, D), lambda j, i: (j, 0)),
                  pl.BlockSpec((tt, 2 * n), lambda j, i: (rev(i), j)),
                  pl.BlockSpec((tt, 2 * n), lambda j, i: (rev(i), j)),
                  pl.BlockSpec((K, 8, 2 * n), lambda j, i: (0, 0, j))] + [HBM] * ncar,
        out_specs=[pl.BlockSpec((tt, 2 * n), lambda j, i: (rev(i), j)),
                   pl.BlockSpec((K, 2 * n), lambda j, i: (0, j)),
                   pl.BlockSpec((1, 2 * n), lambda j, i: (0, j))] + [HBM] * ncar,
        out_shape=[jax.ShapeDtypeStruct((T, W), BF16), jax.ShapeDtypeStruct((K, W), F32),
                   jax.ShapeDtypeStruct((1, W), F32)] + list(car["out_structs"]),
        scratch_shapes=[pltpu.VMEM((8, 2 * n), F32), pltpu.VMEM((K, 8, 2 * n), F32), pltpu.VMEM((8, 2 * n), F32)]
        + list(car["sems"]),
        compiler_params=_cp(*(("parallel", "arbitrary") if carry is None else ("arbitrary", "arbitrary"))), name=name,
    )(dx, w_down, u0, u, _rep8(cw), *car["arrays"])
    return tuple(outs[:3]) if carry is None else (tuple(outs[:3]), list(outs[3:]))


def _ffn_shard_of_block(q):
    return (q % 2) * (N_DEV // 2) + q // 2


def _ffn_block_of_shard(s):
    return (s % (N_DEV // 2)) * 2 + s // (N_DEV // 2)


def _group_cols(v):
    lead = v.shape[:-1]
    n = v.shape[-1] // N_DEV
    return jnp.swapaxes(v.reshape(lead + (2, N_DEV // 2, n)), -3, -2).reshape(v.shape)


def _ungroup_cols(v):
    lead = v.shape[:-1]
    n = v.shape[-1] // N_DEV
    return jnp.swapaxes(v.reshape(lead + (N_DEV // 2, 2, n)), -3, -2).reshape(v.shape)


def local_step(x, target, w, plan=None):
    depth = w["mix_norm"].shape[0]
    n_up = w["ffn_conv_w"].shape[-1] // N_DEV
    g = {k: [None] * (w[k].shape[0] if hasattr(w[k], "shape") else len(w[k])) for k in w}

    def run(fn, stage, l, slot, *args, **kw):
        carry = None if plan is None else plan.carry(stage, l, slot, g)
        if carry is None:
            return fn(*args, **kw)
        out, extra = fn(*args, carry=carry, **kw)
        plan.done(stage, l, slot, extra)
        return out

    saved = []
    for l in range(depth):
        j = l // 2
        h = rmsnorm_fwd(x, w["mix_norm"][l:l + 1], name="mix_norm_fwd")
        if l % 2 == 0:
            qkv = run(mm_nn, "fwd", l, "in", h, w["attn_w_qkv"][j], out_dtype=F32, name="qkv_proj")
            ao = run(attn_fwd, "fwd", l, "attn", qkv, w["attn_q_gain"][j:j + 1], w["attn_k_gain"][j:j + 1],
                     w["attn_sinks"][j:j + 1], name="attn_fwd")
            x1 = run(mm_nn, "fwd", l, "out", ao, w["attn_w_o"][j], out_dtype=F32, res=x, name="attn_out_proj")
            mix = (qkv, ao)
        else:
            z = run(mm_nn, "fwd", l, "in", h, w["rec_w_in"][j], out_dtype=F32, name="rec_in_proj")
            xb, r, ig, a, hs, hprev, hg = rec_fwd(
                z, w["rec_conv_w"][j], w["rec_conv_b"][j:j + 1], w["rec_w_a"][j], w["rec_b_a"][j:j + 1],
                w["rec_w_i"][j], w["rec_b_i"][j:j + 1], w["rec_lambda"][j:j + 1], name="rec_fwd")
            x1 = run(mm_nn, "fwd", l, "out", hg, w["rec_w_out"][j], out_dtype=F32, res=x, name="rec_out_proj")
            mix = (z, xb, r, ig, a, hs, hprev, hg)
        h2 = rmsnorm_fwd(x1, w["ffn_norm"][l:l + 1], name="ffn_norm_fwd")
        u0 = run(mm_nn, "fwd", l, "up", h2, w["ffn_w_up"][l], out_dtype=BF16, name="ffn_up_proj")
        act, u = ffn_act_fwd(u0, w["ffn_conv_w"][l], w["ffn_conv_b"][l:l + 1], n=n_up, name="ffn_act_fwd")
        x2 = run(mm_nn, "fwd", l, "down", act, w["ffn_w_down"][l], out_dtype=F32, res=x1, name="ffn_down_proj")
        saved.append((x, h, mix, x1, h2, u0, u, act))
        x = x2

    loss_vec, dx, dxb = loss_head(x, target, name="loss_head")

    for l in reversed(range(depth)):
        j = l // 2
        x0, h, mix, x1, h2, u0, u, act = saved[l]
        g["ffn_w_down"][l] = mm_tn(act, dxb, out_dtype=BF16, name="ffn_down_dw")
        du0, dcw, dcb = run(ffn_down_act_bwd, "bwd", l, "act_bwd", dxb, w["ffn_w_down"][l], u0, u,
                            w["ffn_conv_w"][l], n=n_up, name="ffn_down_act_bwd")
        g["ffn_conv_w"][l], g["ffn_conv_b"][l] = _ungroup_cols(dcw), _ungroup_cols(dcb)[0]
        g["ffn_w_up"][l] = run(mm_tn, "bwd", l, "up_dw", h2, du0, out_dtype=BF16, col_shards=N_DEV, shard_of_block=_ffn_shard_of_block,
                                 name="ffn_up_dw")
        dh2 = run(mm_nt, "bwd", l, "up_dx", du0, w["ffn_w_up"][l], out_dtype=BF16, name="ffn_up_dx")
        dx1, dx1b, dgf = rmsnorm_bwd(x1, w["ffn_norm"][l:l + 1], dh2, dx, name="ffn_norm_bwd")
        g["ffn_norm"][l] = dgf[0]
        if l % 2 == 0:
            qkv, ao = mix
            dao = mm_nt(dx1b, w["attn_w_o"][j], out_dtype=BF16, name="attn_out_dx")
            g["attn_w_o"][j] = mm_tn(ao, dx1b, out_dtype=BF16, name="attn_out_dw")
            dq, dk, dv, dqg, dkg, dsk = attn_bwd(qkv, dao, w["attn_q_gain"][j:j + 1], w["attn_k_gain"][j:j + 1],
                                                 w["attn_sinks"][j:j + 1], name="attn_bwd")
            g["attn_q_gain"][j], g["attn_k_gain"][j], g["attn_sinks"][j] = dqg[0], dkg[0], dsk
            dqkv = jnp.concatenate([dq, dk, dv], axis=1)
            g["attn_w_qkv"][j] = run(mm_tn, "bwd", l, "in_dw", h, dqkv, out_dtype=BF16, col_shards=N_DEV, name="qkv_dw")
            dh = run(mm_nt, "bwd", l, "in_dx", dqkv, w["attn_w_qkv"][j], out_dtype=BF16, name="qkv_dx")
        else:
            z, xb, r, ig, a, hs, hprev, hg = mix
            dhg = mm_nt(dx1b, w["rec_w_out"][j], out_dtype=BF16, name="rec_out_dx")
            g["rec_w_out"][j] = mm_tn(hg, dx1b, out_dtype=BF16, name="rec_out_dw")
            dyb, ybar = rec_bwd_scan(dhg, hs, a, z, name="rec_bwd_scan")
            dxb, dwa, dwi, dba, dbi, dlam = rec_bwd_gates(
                ybar, hprev, a, r, ig, xb, w["rec_lambda"][j:j + 1], w["rec_w_a"][j], w["rec_w_i"][j],
                name="rec_bwd_gates")
            dxb0, dcw, dcb = conv_bwd(dxb, z, w["rec_conv_w"][j], name="rec_conv_bwd")
            g["rec_w_a"][j], g["rec_w_i"][j] = dwa, dwi
            g["rec_b_a"][j], g["rec_b_i"][j], g["rec_lambda"][j] = dba[0], dbi[0], dlam[0]
            g["rec_conv_w"][j], g["rec_conv_b"][j] = dcw, dcb[0]
            dz = jnp.concatenate([dxb0, dyb], axis=1)
            g["rec_w_in"][j] = run(mm_tn, "bwd", l, "in_dw", h, dz, out_dtype=BF16, col_shards=N_DEV, name="rec_in_dw")
            dh = run(mm_nt, "bwd", l, "in_dx", dz, w["rec_w_in"][j], out_dtype=BF16, name="rec_in_dx")
        dx, dxb, dgm = rmsnorm_bwd(x0, w["mix_norm"][l:l + 1], dh, dx1, name="mix_norm_bwd")
        g["mix_norm"][l] = dgm[0]
    return loss_vec, dx, g


HBM = pl.BlockSpec(memory_space=pltpu.HBM)
N_PEER = N_DEV - 1


def _here():
    return lax.axis_index("x"), lax.axis_index("y"), lax.axis_index("c")


def _sid(dev):
    return 4 * dev[0] + 2 * dev[1] + dev[2]


def _exchange_sems(n):
    return [pltpu.SemaphoreType.DMA((n * N_PEER,)), pltpu.SemaphoreType.DMA((n * N_PEER,)),
            pltpu.SemaphoreType.DMA((n,))]


def gather_exchange(shards, out_structs, windows):
    n = len(shards)

    def parts(outs, sems):
        send_sems, recv_sems, _ = sems
        x, y, c = _here()
        me, sib = (x, y, c), (x, y, 1 - c)
        chips = [(1 - x, y), (x, 1 - y), (1 - x, 1 - y)]

        def copy(i, k, block, to, src=None):
            dst = windows[i](outs[i], _sid(block))
            return pltpu.make_async_remote_copy(
                src_ref=dst if src is None else src, dst_ref=dst,
                send_sem=send_sems.at[i * N_PEER + k], recv_sem=recv_sems.at[i * N_PEER + k],
                device_id=to, device_id_type=MESH)

        return me, sib, chips, c, copy

    def own_copies(ins, outs, sems):
        me, sib, chips, c, copy = parts(outs, sems)
        local = [pltpu.make_async_copy(ins[i], windows[i](outs[i], _sid(me)), sems[2].at[i]) for i in range(n)]
        first = []
        for i in range(n):
            first.append(copy(i, 0, me, sib, src=ins[i]))
            first += [copy(i, 1 + j, me, (*chip, c), src=ins[i]) for j, chip in enumerate(chips)]
        return local, first

    def start(ins, outs, sems):
        local, first = own_copies(ins, outs, sems)
        for cp in local + first:
            cp.start()

    def finish(ins, outs, sems):
        me, sib, chips, c, copy = parts(outs, sems)
        local, first = own_copies(ins, outs, sems)
        passed = []
        for i in range(n):
            for j, chip in enumerate(chips):
                copy(i, 1 + j, (*chip, c), me).wait_recv()
                fwd = copy(i, 4 + j, (*chip, c), sib)
                fwd.start()
                passed.append(fwd)
        for i in range(n):
            copy(i, 0, sib, me).wait_recv()
            for j, chip in enumerate(chips):
                copy(i, 4 + j, (*chip, 1 - c), me).wait_recv()
        for cp in first + passed:
            cp.wait_send()
        for cp in local:
            cp.wait()

    return dict(arrays=list(shards), out_structs=list(out_structs), sems=_exchange_sems(n), start=start,
                finish=finish)


def run_exchange(ex, *, name):
    n = len(ex["arrays"])

    def body(*refs):
        ins, outs, sems = refs[:n], refs[n:2 * n], refs[2 * n:]
        ex["start"](ins, outs, sems)
        ex["finish"](ins, outs, sems)

    return pl.pallas_call(
        body, in_specs=[HBM] * n, out_specs=[HBM] * n, out_shape=ex["out_structs"], scratch_shapes=ex["sems"],
        name=name)(*ex["arrays"])


def scatter_exchange(grads):
    n = len(grads)

    def parts(ins, outs, sems):
        send_sems, recv_sems, local_sems = sems
        x, y, c = _here()
        me = (x, y, c)
        peers = []
        for k in range(1, N_DEV):
            kx, ky, kc = (k >> 2) & 1, (k >> 1) & 1, k & 1
            peers.append((1 - x if kx else x, 1 - y if ky else y, 1 - c if kc else c))

        def copy(i, k):
            return pltpu.make_async_remote_copy(
                src_ref=ins[i].at[_sid(peers[k])], dst_ref=outs[i].at[_sid(me)],
                send_sem=send_sems.at[i * N_PEER + k], recv_sem=recv_sems.at[i * N_PEER + k],
                device_id=peers[k], device_id_type=MESH)

        def arrival(i, k):
            return pltpu.make_async_remote_copy(
                src_ref=ins[i].at[_sid(me)], dst_ref=outs[i].at[_sid(peers[k])],
                send_sem=send_sems.at[i * N_PEER + k], recv_sem=recv_sems.at[i * N_PEER + k],
                device_id=peers[k], device_id_type=MESH)

        local = [pltpu.make_async_copy(ins[i].at[_sid(me)], outs[i].at[_sid(me)], local_sems.at[i]) for i in range(n)]
        sends = [copy(i, k) for i in range(n) for k in range(N_PEER)]
        return local, sends, arrival

    def start(ins, outs, sems):
        local, sends, _ = parts(ins, outs, sems)
        for cp in local + sends:
            cp.start()

    def finish(ins, outs, sems):
        local, sends, arrival = parts(ins, outs, sems)
        for i in range(n):
            for k in range(N_PEER):
                arrival(i, k).wait_recv()
        for cp in sends:
            cp.wait_send()
        for cp in local:
            cp.wait()

    return dict(arrays=list(grads), out_structs=[jax.ShapeDtypeStruct(g.shape, g.dtype) for g in grads],
                sems=_exchange_sems(n), start=start, finish=finish)


def adamw_family(contribs, w, m, v, *, name):
    L, R, C = w.shape
    S = contribs[0].shape[0]
    tr = _tile(R, max(8, (1 << 20) // (C * S)), 8)
    nr = R // tr
    c1 = 1.0 / (1.0 - ADAM_B1 ** ADAM_STEP)
    c2 = 1.0 / (1.0 - ADAM_B2 ** ADAM_STEP)

    def body(*refs):
        c_refs = refs[:L]
        w_ref, m_ref, v_ref, g_ref, d_ref, nm_ref, nv_ref = refs[L:]
        layer = pl.program_id(0)
        for l in range(L):
            @pl.when(layer == l)
            def _(l=l):
                g = c_refs[l][0].astype(F32)
                for s in range(1, S):
                    g = g + c_refs[l][s].astype(F32)
                mm = ADAM_B1 * m_ref[...] + (1.0 - ADAM_B1) * g
                vv = ADAM_B2 * v_ref[...] + (1.0 - ADAM_B2) * (g * g)
                g_ref[...] = g
                nm_ref[...] = mm
                nv_ref[...] = vv
                d_ref[...] = -ADAM_LR * ((mm * c1) / (jnp.sqrt(vv * c2) + ADAM_EPS) + ADAM_WD * w_ref[...])

    def cspec(l):
        return pl.BlockSpec((S, tr, C), lambda ll, i, l=l: (0, jnp.where(ll == l, i, 0), 0))

    lay = pl.BlockSpec((None, tr, C), lambda ll, i: (ll, i, 0))
    return pl.pallas_call(
        body, grid=(L, nr), in_specs=[cspec(l) for l in range(L)] + [lay] * 3, out_specs=[lay] * 4,
        out_shape=[jax.ShapeDtypeStruct((L, R, C), F32)] * 4,
        compiler_params=_cp("arbitrary", "arbitrary"), name=name)(*contribs, w, m, v)


def sum_slots(a, *, name):
    S, R, C = a.shape
    tr = _tile(R, 256, 8)

    def body(a_ref, o_ref):
        t = a_ref[0]
        for s in range(1, S):
            t = t + a_ref[s]
        o_ref[...] = t

    return pl.pallas_call(
        body, grid=(R // tr,), in_specs=[pl.BlockSpec((S, tr, C), lambda i: (0, i, 0))],
        out_specs=pl.BlockSpec((tr, C), lambda i: (i, 0)), out_shape=jax.ShapeDtypeStruct((R, C), F32),
        compiler_params=_cp("parallel"), name=name)(a)


LANES = 128


def _pack(arrs):
    flat = jnp.concatenate([a.reshape(-1).astype(F32) for a in arrs])
    rows = -(-flat.shape[0] // LANES)
    rows = -(-rows // 256) * 256
    return jnp.pad(flat, (0, rows * LANES - flat.shape[0])).reshape(rows, LANES)


def _unpack(buf, shapes):
    flat = buf.reshape(-1)
    out, off = [], 0
    for shp in shapes:
        size = int(np.prod(shp))
        out.append(flat[off:off + size].reshape(shp))
        off += size
    return out


def _gather_last(g):
    t = jnp.moveaxis(g, 0, -2)
    return t.reshape(t.shape[:-2] + (t.shape[-2] * t.shape[-1],))


def _own_last(full, s):
    n = full.shape[-1] // N_DEV
    t = full.reshape(full.shape[:-1] + (N_DEV, n))
    return lax.dynamic_index_in_dim(t, s, axis=t.ndim - 2, keepdims=False)


BIG = ["attn_w_qkv", "attn_w_o", "rec_w_in", "rec_w_out", "ffn_w_up", "ffn_w_down", "rec_w_a", "rec_w_i"]
SMALL_REPLICATED = ["mix_norm", "ffn_norm", "attn_q_gain", "attn_k_gain", "attn_sinks", "ffn_conv_b"]
SMALL_SHARDED = ["rec_conv_w", "rec_conv_b", "rec_b_a", "rec_b_i", "rec_lambda", "ffn_conv_w"]
SMALL = SMALL_REPLICATED + SMALL_SHARDED
WEIGHTS = ["mix_norm", "ffn_norm", "attn_w_qkv", "attn_q_gain", "attn_k_gain", "attn_sinks", "attn_w_o", "rec_w_in",
           "rec_conv_w", "rec_conv_b", "rec_w_a", "rec_b_a", "rec_w_i", "rec_b_i", "rec_lambda", "rec_w_out",
           "ffn_w_up", "ffn_conv_w", "ffn_conv_b", "ffn_w_down"]


def _col_window(n, block_of_shard=None):
    def win(ref, s):
        q = s if block_of_shard is None else block_of_shard(s)
        return ref.at[:, pl.ds(pl.multiple_of(q * n, 128), n)]
    return win


def _row_window(r):
    return lambda ref, s: ref.at[pl.ds(pl.multiple_of(s * r, 16), r), :]


def _gate_window(r):
    return lambda ref, s: ref.at[:, pl.ds(pl.multiple_of(s * r, 16), r), :]


def _slot_window(ref, s):
    return ref.at[s]


def _idx(name, layer):
    return layer if name.startswith("ffn") else layer // 2


def _slot_names(layer, slot):
    attn = layer % 2 == 0
    return {"in": ["attn_w_qkv"] if attn else ["rec_w_in"],
            "out": ["attn_w_o"] if attn else ["rec_w_out", "rec_w_a", "rec_w_i"],
            "up": ["ffn_w_up"], "down": ["ffn_w_down"]}[slot]


def _gather_for(p, items):
    shards, structs, wins = [], [], []
    for nme, layer in items:
        sh = p[nme][_idx(nme, layer)].astype(BF16)
        if nme in ("attn_w_qkv", "rec_w_in", "ffn_w_up"):
            K, n = sh.shape
            structs.append(jax.ShapeDtypeStruct((K, n * N_DEV), BF16))
            wins.append(_col_window(n, _ffn_block_of_shard if nme == "ffn_w_up" else None))
        elif nme in ("rec_w_a", "rec_w_i"):
            nblk, r, bd = sh.shape
            structs.append(jax.ShapeDtypeStruct((nblk, r * N_DEV, bd), BF16))
            wins.append(_gate_window(r))
        else:
            r, N = sh.shape
            structs.append(jax.ShapeDtypeStruct((r * N_DEV, N), BF16))
            wins.append(_row_window(r))
        shards.append(sh)
    return gather_exchange(shards, structs, wins)


def _send_layout(name, t):
    if name in ("rec_w_a", "rec_w_i"):
        nblk, bd, _ = t.shape
        return jnp.transpose(t.reshape(nblk, N_DEV, bd // N_DEV, bd), (1, 0, 2, 3)).astype(BF16)
    if name in ("attn_w_o", "rec_w_out", "ffn_w_down"):
        return t.reshape((N_DEV, t.shape[0] // N_DEV) + t.shape[1:])
    return t


class _Plan:
    BWD_SLOT = {"up_dw": "down", "in_dw": "out", "in_dx": "in"}

    def __init__(self, p, w, contribs, depth):
        self.p, self.w, self.contribs, self.depth = p, w, contribs, depth
        self.pending = None

    def carry(self, stage, l, slot, g):
        if stage == "fwd":
            items = []
            if l == 0 and slot == "in":
                items += [(k, 0) for k in _slot_names(0, "out")]
            if l == 0 and slot == "attn":
                items += [(k, 0) for s in ("up", "down") for k in _slot_names(0, s)]
            if slot != "attn" and l + 1 < self.depth:
                items += [(k, l + 1) for k in _slot_names(l + 1, slot)]
            if not items:
                return None
            self.pending = items
            return _gather_for(self.p, items)
        if slot == "act_bwd":
            items = [("ffn_w_up", l + 1)] if l + 1 < self.depth else []
        elif slot == "up_dx":
            items = [("ffn_w_up", 0)] if l == 0 else []
        else:
            items = [(k, l) for k in _slot_names(l, self.BWD_SLOT[slot])]
        if not items:
            return None
        self.pending = items
        return scatter_exchange([_send_layout(k, g[k][_idx(k, layer)]) for k, layer in items])

    def done(self, stage, l, slot, outs):
        dst = self.w if stage == "fwd" else self.contribs
        for (k, layer), o in zip(self.pending, outs):
            dst[k][_idx(k, layer)] = o


def _train_step(p, x, target, mom, vel):
    depth = p["mix_norm"].shape[0]
    s_me = _sid(_here())

    w = {k: [None] * p[k].shape[0] for k in BIG}
    first = [(k, 0) for k in _slot_names(0, "in")]
    for (k, _), t in zip(first, run_exchange(_gather_for(p, first), name="all_gather_first")):
        w[k][0] = t

    local_small = [p[k] for k in SMALL_SHARDED]
    packed = _pack(local_small)
    gathered, = run_exchange(
        gather_exchange([packed], [jax.ShapeDtypeStruct((N_DEV,) + packed.shape, F32)], [_slot_window]),
        name="all_gather_small")
    per_dev = [_unpack(gathered[s], [a.shape for a in local_small]) for s in range(N_DEV)]
    for i, k in enumerate(SMALL_SHARDED):
        w[k] = _gather_last(jnp.stack([per_dev[s][i] for s in range(N_DEV)]))
    for k in SMALL_REPLICATED:
        w[k] = p[k]
    nrec = w["rec_b_a"].shape[0]
    w["rec_b_a"] = w["rec_b_a"].reshape(nrec, -1)
    w["rec_b_i"] = w["rec_b_i"].reshape(nrec, -1)
    w["ffn_conv_w"] = _group_cols(w["ffn_conv_w"])
    w["ffn_conv_b"] = _group_cols(w["ffn_conv_b"])

    contribs = {k: [None] * len(w[k]) for k in BIG}
    loss_vec, dx, g = local_step(x[0], target[0], w, _Plan(p, w, contribs, depth))
    loss = lax.psum(jnp.sum(loss_vec), ("x", "y", "c"))

    out = {}
    for k in BIG:
        shp = p[k].shape
        L = shp[0]
        C = shp[-1]
        R = int(np.prod(shp[1:-1]))
        cs = [c.reshape(N_DEV, R, C) for c in contribs[k]]
        res = adamw_family(cs, p[k].reshape(L, R, C), mom[k].reshape(L, R, C), vel[k].reshape(L, R, C),
                           name="adamw_" + k)
        out[k] = [t.reshape(shp) for t in res]

    gsmall = [jnp.stack(g[k]) for k in SMALL]
    gp = _pack(gsmall)
    gall, = run_exchange(
        gather_exchange([gp], [jax.ShapeDtypeStruct((N_DEV,) + gp.shape, F32)], [_slot_window]),
        name="all_gather_small_grads")
    gsum = _unpack(sum_slots(gall, name="sum_small_grads"), [a.shape for a in gsmall])
    glocal = []
    for k, t in zip(SMALL, gsum):
        if k in SMALL_SHARDED:
            t = _own_last(t.reshape(p[k].shape[:-1] + (p[k].shape[-1] * N_DEV,)), s_me)
        glocal.append(t.reshape(p[k].shape))
    wp, mp, vp, gpk = (_pack([d[k] for k in SMALL]) for d in (p, mom, vel, dict(zip(SMALL, glocal))))
    res = adamw_family([gpk[None]], wp[None], mp[None], vp[None], name="adamw_small")
    shapes = [p[k].shape for k in SMALL]
    unp = [_unpack(t[0], shapes) for t in res]
    for i, k in enumerate(SMALL):
        out[k] = [glocal[i], unp[1][i], unp[2][i], unp[3][i]]

    return (loss, dx[None]) + tuple(out[k][q] for q in range(4) for k in WEIGHTS)


def kernel(x, mix_norm, ffn_norm, attn_w_qkv, attn_q_gain, attn_k_gain, attn_sinks, attn_w_o, rec_w_in, rec_conv_w, rec_conv_b, rec_w_a, rec_b_a, rec_w_i, rec_b_i, rec_lambda, rec_w_out, ffn_w_up, ffn_conv_w, ffn_conv_b, ffn_w_down, loss_target, m_mix_norm, m_ffn_norm, m_attn_w_qkv, m_attn_q_gain, m_attn_k_gain, m_attn_sinks, m_attn_w_o, m_rec_w_in, m_rec_conv_w, m_rec_conv_b, m_rec_w_a, m_rec_b_a, m_rec_w_i, m_rec_b_i, m_rec_lambda, m_rec_w_out, m_ffn_w_up, m_ffn_conv_w, m_ffn_conv_b, m_ffn_w_down, v_mix_norm, v_ffn_norm, v_attn_w_qkv, v_attn_q_gain, v_attn_k_gain, v_attn_sinks, v_attn_w_o, v_rec_w_in, v_rec_conv_w, v_rec_conv_b, v_rec_w_a, v_rec_b_a, v_rec_w_i, v_rec_b_i, v_rec_lambda, v_rec_w_out, v_ffn_w_up, v_ffn_conv_w, v_ffn_conv_b, v_ffn_w_down):
    p = dict(zip(WEIGHTS, (mix_norm, ffn_norm, attn_w_qkv, attn_q_gain, attn_k_gain, attn_sinks, attn_w_o, rec_w_in,
                           rec_conv_w, rec_conv_b, rec_w_a, rec_b_a, rec_w_i, rec_b_i, rec_lambda, rec_w_out,
                           ffn_w_up, ffn_conv_w, ffn_conv_b, ffn_w_down)))
    mom = dict(zip(WEIGHTS, (m_mix_norm, m_ffn_norm, m_attn_w_qkv, m_attn_q_gain, m_attn_k_gain, m_attn_sinks,
                             m_attn_w_o, m_rec_w_in, m_rec_conv_w, m_rec_conv_b, m_rec_w_a, m_rec_b_a, m_rec_w_i,
                             m_rec_b_i, m_rec_lambda, m_rec_w_out, m_ffn_w_up, m_ffn_conv_w, m_ffn_conv_b,
                             m_ffn_w_down)))
    vel = dict(zip(WEIGHTS, (v_mix_norm, v_ffn_norm, v_attn_w_qkv, v_attn_q_gain, v_attn_k_gain, v_attn_sinks,
                             v_attn_w_o, v_rec_w_in, v_rec_conv_w, v_rec_conv_b, v_rec_w_a, v_rec_b_a, v_rec_w_i,
                             v_rec_b_i, v_rec_lambda, v_rec_w_out, v_ffn_w_up, v_ffn_conv_w, v_ffn_conv_b,
                             v_ffn_w_down)))
    return _train_step(p, x, loss_target, mom, vel)
```

```python
import functools
import math

import jax
import jax.numpy as jnp
import numpy as np
from jax import lax
from jax.experimental import pallas as pl
from jax.experimental.pallas import tpu as pltpu

F32 = jnp.float32
BF16 = jnp.bfloat16

N_DEV = 8
HEAD_DIM = 64
GROUP = 4
BLOCK = 128
LRU_C = 8.0
EPS = 1e-6
HALO = 16
ADAM_LR, ADAM_B1, ADAM_B2, ADAM_EPS, ADAM_WD, ADAM_STEP = 0.001, 0.9, 0.999, 1e-08, 0.01, 10
VMEM_LIMIT = 56 * 1024 * 1024
MESH = pl.DeviceIdType.MESH
GELU_C = math.sqrt(2.0 / math.pi)


def _cp(*sem, vmem=VMEM_LIMIT):
    return pltpu.CompilerParams(dimension_semantics=tuple(sem), vmem_limit_bytes=vmem)


def _tile(dim, pref, mult=128):
    if dim <= pref:
        return dim
    t = (pref // mult) * mult
    while t >= mult:
        if dim % t == 0:
            return t
        t -= mult
    return dim


def _gelu(x):
    th = jnp.tanh(GELU_C * (x + 0.044715 * x * x * x))
    return 0.5 * x * (1.0 + th)


def _gelu_and_grad(x):
    x2 = x * x
    th = jnp.tanh(x * (GELU_C + (GELU_C * 0.044715) * x2))
    a = 0.5 + 0.5 * th
    g = x * a
    dg = a + g * (1.0 - th) * (GELU_C + (3.0 * GELU_C * 0.044715) * x2)
    return g, dg


def _dot(a, b, dims):
    return lax.dot_general(a.astype(BF16), b.astype(BF16), (dims, ((), ())), preferred_element_type=F32)


NN = ((1,), (0,))
NT = ((1,), (1,))
TN = ((0,), (0,))


def _matmul(a, b, *, dims, grid, a_spec, b_spec, o_spec, out_shape, acc_shape, res=None, res_spec=None,
            carry=None, name):
    ni, nj, nk = grid
    nres = 0 if res is None else 1
    ncar = 0 if carry is None else len(carry["arrays"])

    def body(*refs):
        a_ref, b_ref = refs[0], refs[1]
        r_ref = refs[2] if nres else None
        car_in = refs[2 + nres:2 + nres + ncar]
        o_ref = refs[2 + nres + ncar]
        car_out = refs[3 + nres + ncar:3 + nres + 2 * ncar]
        scratch = refs[3 + nres + 2 * ncar:]
        i, j, k = pl.program_id(0), pl.program_id(1), pl.program_id(2)

        if carry is not None:
            @pl.when((i == 0) & (j == 0) & (k == 0))
            def _():
                carry["start"](car_in, car_out, scratch[1:])

        def finish(acc):
            if r_ref is not None:
                acc = acc + r_ref[...].astype(F32)
            o_ref[...] = acc.astype(o_ref.dtype)

        if nk == 1:
            finish(_dot(a_ref[...], b_ref[...], dims))
        else:
            acc_ref = scratch[0]

            @pl.when(k == 0)
            def _():
                acc_ref[...] = _dot(a_ref[...], b_ref[...], dims)

            if nk > 2:
                @pl.when((k > 0) & (k < nk - 1))
                def _():
                    acc_ref[...] += _dot(a_ref[...], b_ref[...], dims)

            @pl.when(k == nk - 1)
            def _():
                finish(acc_ref[...] + _dot(a_ref[...], b_ref[...], dims))

        if carry is not None:
            @pl.when((i == ni - 1) & (j == nj - 1) & (k == nk - 1))
            def _():
                carry["finish"](car_in, car_out, scratch[1:])

    in_specs = [a_spec, b_spec] + ([res_spec] if nres else [])
    args = (a, b) + ((res,) if nres else ())
    out_specs, out_shapes = [o_spec], [out_shape]
    scratch_shapes = [pltpu.VMEM(acc_shape if nk > 1 else (8, 128), F32)]
    sem = ("parallel", "parallel", "arbitrary")
    if carry is not None:
        in_specs += [HBM] * ncar
        args += tuple(carry["arrays"])
        out_specs += [HBM] * ncar
        out_shapes += list(carry["out_structs"])
        scratch_shapes += carry["sems"]
        sem = ("arbitrary", "arbitrary", "arbitrary")
    outs = pl.pallas_call(
        body, grid=grid, in_specs=in_specs, out_specs=out_specs, out_shape=out_shapes,
        scratch_shapes=scratch_shapes, compiler_params=_cp(*sem), name=name,
    )(*args)
    return outs[0] if carry is None else (outs[0], list(outs[1:]))


def mm_nn(a, b, *, out_dtype, res=None, carry=None, name):
    M, K = a.shape
    N = b.shape[1]
    wide = res is None and out_dtype == BF16
    tm, tn, tk = _tile(M, 1024), _tile(N, 2048 if wide else 1024), _tile(K, 2048)
    return _matmul(
        a, b, dims=NN, grid=(M // tm, N // tn, K // tk),
        a_spec=pl.BlockSpec((tm, tk), lambda i, j, k: (i, k)),
        b_spec=pl.BlockSpec((tk, tn), lambda i, j, k: (k, j)),
        o_spec=pl.BlockSpec((tm, tn), lambda i, j, k: (i, j)),
        out_shape=jax.ShapeDtypeStruct((M, N), out_dtype), acc_shape=(tm, tn),
        res=res, res_spec=pl.BlockSpec((tm, tn), lambda i, j, k: (i, j)), carry=carry, name=name)


def mm_nt(a, b, *, out_dtype, res=None, carry=None, name):
    M, N = a.shape
    K = b.shape[0]
    wide = res is None and out_dtype == BF16 and a.dtype == BF16
    tm, tn, tk = _tile(M, 1024), _tile(K, 2048 if wide else 1024), _tile(N, 2048)
    return _matmul(
        a, b, dims=NT, grid=(M // tm, K // tn, N // tk),
        a_spec=pl.BlockSpec((tm, tk), lambda i, j, k: (i, k)),
        b_spec=pl.BlockSpec((tn, tk), lambda i, j, k: (j, k)),
        o_spec=pl.BlockSpec((tm, tn), lambda i, j, k: (i, j)),
        out_shape=jax.ShapeDtypeStruct((M, K), out_dtype), acc_shape=(tm, tn),
        res=res, res_spec=pl.BlockSpec((tm, tn), lambda i, j, k: (i, j)), carry=carry, name=name)


def mm_tn(a, b, *, out_dtype, col_shards=None, shard_of_block=None, carry=None, name):
    T, K = a.shape
    N = b.shape[1]
    tt = _tile(T, 2048)
    tm = _tile(K, 1024)
    if col_shards is None:
        tn = _tile(N, 1024)
        o_spec = pl.BlockSpec((tm, tn), lambda i, j, k: (i, j))
        out_shape = jax.ShapeDtypeStruct((K, N), out_dtype)
    else:
        n = N // col_shards
        tn = _tile(n, 1536)
        per = n // tn
        sob = shard_of_block if shard_of_block is not None else (lambda s: s)
        o_spec = pl.BlockSpec((None, tm, tn), lambda i, j, k: (sob(j // per), i, j % per))
        out_shape = jax.ShapeDtypeStruct((col_shards, K, n), out_dtype)
    return _matmul(
        a, b, dims=TN, grid=(K // tm, N // tn, T // tt),
        a_spec=pl.BlockSpec((tt, tm), lambda i, j, k: (k, i)),
        b_spec=pl.BlockSpec((tt, tn), lambda i, j, k: (k, j)),
        o_spec=o_spec, out_shape=out_shape, acc_shape=(tm, tn), carry=carry, name=name)


def rmsnorm_fwd(x, g, *, name):
    T, D = x.shape
    tm = _tile(T, 512, 8)

    def body(x_ref, g_ref, o_ref):
        xv = x_ref[...]
        r = lax.rsqrt(jnp.mean(xv * xv, axis=-1, keepdims=True) + EPS)
        o_ref[...] = (xv * r * g_ref[...]).astype(o_ref.dtype)

    return pl.pallas_call(
        body, grid=(T // tm,),
        in_specs=[pl.BlockSpec((tm, D), lambda i: (i, 0)), pl.BlockSpec((1, D), lambda i: (0, 0))],
        out_specs=pl.BlockSpec((tm, D), lambda i: (i, 0)),
        out_shape=jax.ShapeDtypeStruct((T, D), BF16), compiler_params=_cp("parallel"), name=name)(x, g)


def rmsnorm_bwd(x, g, dh, dres, *, name):
    T, D = x.shape
    tm = _tile(T, 512, 16)

    def body(x_ref, g_ref, dh_ref, dres_ref, dx_ref, dxb_ref, dg_ref):
        @pl.when(pl.program_id(0) == 0)
        def _():
            dg_ref[...] = jnp.zeros_like(dg_ref)

        xv = x_ref[...]
        dh_v = dh_ref[...].astype(F32)
        r = lax.rsqrt(jnp.mean(xv * xv, axis=-1, keepdims=True) + EPS)
        u = dh_v * g_ref[...]
        dot = jnp.mean(u * xv, axis=-1, keepdims=True)
        dx = dres_ref[...] + r * u - xv * (r * r * r * dot)
        dx_ref[...] = dx
        dxb_ref[...] = dx.astype(dxb_ref.dtype)
        dg_ref[...] += jnp.sum(dh_v * xv * r, axis=0, keepdims=True)

    row = pl.BlockSpec((tm, D), lambda i: (i, 0))
    vec = pl.BlockSpec((1, D), lambda i: (0, 0))
    return pl.pallas_call(
        body, grid=(T // tm,), in_specs=[row, vec, row, row], out_specs=[row, row, vec],
        out_shape=[jax.ShapeDtypeStruct((T, D), F32), jax.ShapeDtypeStruct((T, D), BF16),
                   jax.ShapeDtypeStruct((1, D), F32)],
        compiler_params=_cp("arbitrary"), name=name)(x, g, dh, dres)


def loss_head(y, target, *, name):
    T, D = y.shape
    tm = _tile(T, 512, 16)

    def body(y_ref, t_ref, l_ref, dy_ref, dyb_ref):
        @pl.when(pl.program_id(0) == 0)
        def _():
            l_ref[...] = jnp.zeros_like(l_ref)

        e = y_ref[...] - t_ref[...]
        dy = e * (1.0 / D)
        dy_ref[...] = dy
        dyb_ref[...] = dy.astype(dyb_ref.dtype)
        l_ref[...] += jnp.sum(e * e, axis=0, keepdims=True) * (0.5 / D)

    row = pl.BlockSpec((tm, D), lambda i: (i, 0))
    vec = pl.BlockSpec((1, D), lambda i: (0, 0))
    return pl.pallas_call(
        body, grid=(T // tm,), in_specs=[row, row], out_specs=[vec, row, row],
        out_shape=[jax.ShapeDtypeStruct((1, D), F32), jax.ShapeDtypeStruct((T, D), F32),
                   jax.ShapeDtypeStruct((T, D), BF16)],
        compiler_params=_cp("arbitrary"), name=name)(y, target)


NEG = -1e30


def _kv_heads_per_step(hkv):
    return 4 if hkv % 4 == 0 else 2


def _query_blocks_per_step(nblocks, want):
    while nblocks % want:
        want //= 2
    return want


LANE = 128


def _hi_lo_dot(x, w):
    hi = x.astype(BF16)
    lo = x - hi.astype(F32)
    return _dot(hi, w, NN) + _dot(lo, w, NN)


def _head_sum_matrix():
    r = lax.broadcasted_iota(jnp.int32, (LANE, LANE), 0) // HEAD_DIM
    c = lax.broadcasted_iota(jnp.int32, (LANE, LANE), 1) // HEAD_DIM
    return jnp.where(r == c, 1.0, 0.0).astype(BF16)


def _chunk(x, c):
    return x[:, c * LANE:(c + 1) * LANE]


def _head_sums(x, e):
    return jnp.concatenate([_hi_lo_dot(_chunk(x, c), e) for c in range(x.shape[1] // LANE)], axis=1)


def _row_sums(x):
    return _hi_lo_dot(x, jnp.ones((x.shape[1], LANE), BF16))


def _headnorm(x, e):
    r = lax.rsqrt(_head_sums(x * x, e) * (1.0 / HEAD_DIM) + EPS)
    return x * r, r


def _attn_mask(n):
    qi = lax.broadcasted_iota(jnp.int32, (BLOCK, 2 * BLOCK), 0)
    kj = lax.broadcasted_iota(jnp.int32, (BLOCK, 2 * BLOCK), 1)
    rel = qi + BLOCK - kj
    m = (rel >= 0) & (rel < BLOCK) & ((kj >= BLOCK) | (n > 0))
    return jnp.concatenate([m] * GROUP, axis=0)


def _qk_operands(qhat, khat, j, lo_half):
    kb = jnp.where(lo_half == (j % 2 == 0), _chunk(khat, j // 2), 0.0)
    kd = kb + pltpu.roll(kb, HEAD_DIM, 1)
    q4 = jnp.concatenate(
        [jnp.where(lo_half == (g % 2 == 0), _chunk(qhat, 2 * j + g // 2), 0.0) for g in range(GROUP)], axis=0)
    return q4, kd


def _attn_softmax(s, sink_ref, mask4, head0):
    s = s * (HEAD_DIM ** -0.5)
    sink = jnp.concatenate([jnp.full((BLOCK, 1), sink_ref[0, head0 + g], F32) for g in range(GROUP)], axis=0)
    m = jnp.maximum(jnp.max(jnp.where(mask4, s, NEG), axis=-1, keepdims=True), sink)
    ex = jnp.where(mask4, jnp.exp(s - m), 0.0)
    es = jnp.exp(sink - m)
    inv = 1.0 / (_row_sums(ex) + es)
    return ex * jnp.concatenate([inv] * (s.shape[1] // LANE), axis=1), es * inv[:, 0:1]


def attn_fwd(qkv, q_gain, k_gain, sinks, *, carry=None, name):
    ncar = 0 if carry is None else len(carry["arrays"])
    T, W = qkv.shape
    hq = W // HEAD_DIM * GROUP // (GROUP + 2)
    dq = hq * HEAD_DIM
    PAIR = _kv_heads_per_step(hq // GROUP)
    QW, KW = PAIR * GROUP * HEAD_DIM, PAIR * HEAD_DIM
    npair = hq // (GROUP * PAIR)
    QB = _query_blocks_per_step(T // BLOCK, 4)
    nb = T // (QB * BLOCK)
    k0 = dq // KW
    v0 = k0 + npair

    def body(q_ref, kp_ref, kc_ref, vp_ref, vc_ref, qg_ref, kg_ref, sink_ref, *rest):
        car_in, o_ref, car_out, sems = rest[:ncar], rest[ncar], rest[ncar + 1:2 * ncar + 1], rest[2 * ncar + 1:]
        p, m = pl.program_id(0), pl.program_id(1)
        if carry is not None:
            @pl.when((p == 0) & (m == 0))
            def _():
                carry["start"](car_in, car_out, sems)

            @pl.when((p == npair - 1) & (m == nb - 1))
            def _():
                carry["finish"](car_in, car_out, sems)

        e = _head_sum_matrix()
        lo_half = lax.broadcasted_iota(jnp.int32, (1, LANE), 1) < HEAD_DIM
        qn, _ = _headnorm(q_ref[...], e)
        qhat_all = _tapmul(qn, qg_ref[...])
        kn, _ = _headnorm(jnp.concatenate([kp_ref[...], kc_ref[...]], axis=0), e)
        khat_all = _tapmul(kn, kg_ref[...])
        v_all = jnp.concatenate([vp_ref[...], vc_ref[...]], axis=0).astype(BF16)
        ops = [[_qk_operands(qhat_all[qb * BLOCK:(qb + 1) * BLOCK], khat_all[qb * BLOCK:(qb + 2) * BLOCK], j, lo_half)
                for j in range(PAIR)] for qb in range(QB)]
        scores = [[_dot(q4, kd, NT) for q4, kd in ops[qb]] for qb in range(QB)]
        probs = [[_attn_softmax(s, sink_ref, _attn_mask(QB * m + qb), (p * PAIR + j) * GROUP)[0]
                  for j, s in enumerate(scores[qb])] for qb in range(QB)]
        outs = []
        for qb in range(QB):
            vwin = v_all[qb * BLOCK:(qb + 2) * BLOCK]
            o4 = [_dot(probs[qb][j], _chunk(vwin, j // 2), NN) for j in range(PAIR)]
            chunks = []
            for c in range(2 * PAIR):
                j, t = c // 2, c % 2
                a = o4[j][(2 * t) * BLOCK:(2 * t + 1) * BLOCK]
                b = o4[j][(2 * t + 1) * BLOCK:(2 * t + 2) * BLOCK]
                if j % 2 == 0:
                    b = pltpu.roll(b, HEAD_DIM, 1)
                else:
                    a = pltpu.roll(a, HEAD_DIM, 1)
                chunks.append(jnp.where(lo_half, a, b))
            outs.append(jnp.concatenate(chunks, axis=1))
        o_ref[...] = jnp.concatenate(outs, axis=0).astype(o_ref.dtype)

    prev = lambda m: jnp.maximum(QB * m - 1, 0)
    car = carry if carry is not None else dict(arrays=[], out_structs=[], sems=[])
    qg8 = _rep8(jnp.tile(q_gain[0], PAIR * GROUP))
    kg8 = _rep8(jnp.tile(k_gain[0], PAIR))
    outs = pl.pallas_call(
        body, grid=(npair, nb),
        in_specs=[pl.BlockSpec((QB * BLOCK, QW), lambda p, n: (n, p)),
                  pl.BlockSpec((BLOCK, KW), lambda p, n: (prev(n), k0 + p)),
                  pl.BlockSpec((QB * BLOCK, KW), lambda p, n: (n, k0 + p)),
                  pl.BlockSpec((BLOCK, KW), lambda p, n: (prev(n), v0 + p)),
                  pl.BlockSpec((QB * BLOCK, KW), lambda p, n: (n, v0 + p)),
                  pl.BlockSpec((8, QW), lambda p, n: (0, 0)), pl.BlockSpec((8, KW), lambda p, n: (0, 0)),
                  pl.BlockSpec(memory_space=pltpu.SMEM)] + [HBM] * ncar,
        out_specs=[pl.BlockSpec((QB * BLOCK, QW), lambda p, n: (n, p))] + [HBM] * ncar,
        out_shape=[jax.ShapeDtypeStruct((T, dq), BF16)] + list(car["out_structs"]),
        scratch_shapes=list(car["sems"]),
        compiler_params=_cp(*(("parallel", "parallel") if carry is None else ("arbitrary", "arbitrary"))),
        name=name)(qkv, qkv, qkv, qkv, qkv, qg8, kg8, sinks, *car["arrays"])
    return outs[0] if carry is None else (outs[0], list(outs[1:]))


def attn_bwd(qkv, d_out, q_gain, k_gain, sinks, *, name):
    T, W = qkv.shape
    hq = W // HEAD_DIM * GROUP // (GROUP + 2)
    dq_w = hq * HEAD_DIM
    PAIR = _kv_heads_per_step(hq // GROUP)
    QW, KW = PAIR * GROUP * HEAD_DIM, PAIR * HEAD_DIM
    npair = hq // (GROUP * PAIR)
    nb = T // BLOCK
    k0 = dq_w // KW
    v0 = k0 + npair

    def body(q_ref, kp_ref, kc_ref, vp_ref, vc_ref, do_ref, qg_ref, kg_ref, sink_ref,
             dq_ref, dk_ref, dv_ref, dqg_ref, dkg_ref, dsink_ref, dk_carry, dv_carry):
        p, i = pl.program_id(0), pl.program_id(1)
        n = nb - 1 - i

        @pl.when(i == 0)
        def _():
            dk_carry[...] = jnp.zeros_like(dk_carry)
            dv_carry[...] = jnp.zeros_like(dv_carry)
            dqg_ref[...] = jnp.zeros_like(dqg_ref)
            dkg_ref[...] = jnp.zeros_like(dkg_ref)
            dsink_ref[...] = jnp.zeros_like(dsink_ref)

        e = _head_sum_matrix()
        lo_half = lax.broadcasted_iota(jnp.int32, (1, LANE), 1) < HEAD_DIM
        mask4 = _attn_mask(n)
        qg, kg = qg_ref[...], kg_ref[...]
        q = q_ref[...]
        kwin = jnp.concatenate([kp_ref[...], kc_ref[...]], axis=0)
        qn, qr = _headnorm(q, e)
        qhat = _tapmul(qn, qg)
        kn, kr = _headnorm(kwin, e)
        khat = _tapmul(kn, kg)
        vwin = jnp.concatenate([vp_ref[...], vc_ref[...]], axis=0).astype(BF16)
        do = do_ref[...].astype(F32)
        ops = [_qk_operands(qhat, khat, j, lo_half) for j in range(PAIR)]
        do4 = []
        for j in range(PAIR):
            parts = []
            for g in range(GROUP):
                dc = _chunk(do, 2 * j + g // 2)
                if g % 2 != j % 2:
                    dc = pltpu.roll(dc, HEAD_DIM, 1)
                parts.append(jnp.where(lo_half == (j % 2 == 0), dc, 0.0))
            do4.append(jnp.concatenate(parts, axis=0))
        scores = [_dot(q4, kd, NT) for q4, kd in ops]
        dps = [_dot(do4[j], _chunk(vwin, j // 2), NT) for j in range(PAIR)]
        soft = [_attn_softmax(s, sink_ref, mask4, (p * PAIR + j) * GROUP) for j, s in enumerate(scores)]
        dss, dsink_rows = [], []
        for j in range(PAIR):
            pr, psink = soft[j]
            delta = _row_sums(pr * dps[j])
            dss.append(pr * (dps[j] - jnp.concatenate([delta] * 2, axis=1)) * (HEAD_DIM ** -0.5))
            dsk = -psink * delta[:, 0:1]
            for g in range(GROUP):
                tot = jnp.sum(dsk[g * BLOCK:(g + 1) * BLOCK], axis=0, keepdims=True)
                dsink_rows.append(jnp.broadcast_to(tot, (1, LANE)))
        dq4 = [_dot(dss[j], ops[j][1], NN) for j in range(PAIR)]
        dkd = [_dot(dss[j], ops[j][0], TN) for j in range(PAIR)]
        dvc = [_dot(soft[j][0], do4[j], TN) for j in range(PAIR)]
        dqhat = jnp.concatenate(
            [jnp.where(lo_half, dq4[c // 2][(2 * (c % 2)) * BLOCK:(2 * (c % 2) + 1) * BLOCK],
                       dq4[c // 2][(2 * (c % 2) + 1) * BLOCK:(2 * (c % 2) + 2) * BLOCK]) for c in range(2 * PAIR)],
            axis=1)
        dkhat_chunks, dv_chunks = [], []
        for kc in range(PAIR // 2):
            tot_k, tot_v = None, None
            for j in (2 * kc, 2 * kc + 1):
                t = jnp.where(lo_half == (j % 2 == 0), dkd[j] + pltpu.roll(dkd[j], HEAD_DIM, 1), 0.0)
                tot_k = t if tot_k is None else tot_k + t
                tot_v = dvc[j] if tot_v is None else tot_v + dvc[j]
            dkhat_chunks.append(tot_k)
            dv_chunks.append(tot_v)
        dkhat_win = jnp.concatenate(dkhat_chunks, axis=1)
        dv_win = jnp.concatenate(dv_chunks, axis=1)
        dqg_ref[...] += _colsum8(dqhat * qn)
        dqn = _tapmul(dqhat, qg)
        dq_ref[...] = (qr * dqn - q * (qr * qr * qr * _head_sums(dqn * q, e) * (1.0 / HEAD_DIM))).astype(dq_ref.dtype)
        dkh = dkhat_win[BLOCK:] + dk_carry[...]
        dk_carry[...] = dkhat_win[:BLOCK]
        kcur, knc, krc = kwin[BLOCK:], kn[BLOCK:], kr[BLOCK:]
        dkg_ref[...] += _colsum8(dkh * knc)
        dkn = _tapmul(dkh, kg)
        dk_ref[...] = (krc * dkn
                       - kcur * (krc * krc * krc * _head_sums(dkn * kcur, e) * (1.0 / HEAD_DIM))).astype(dk_ref.dtype)
        dv_ref[...] = (dv_win[BLOCK:] + dv_carry[...]).astype(dv_ref.dtype)
        dv_carry[...] = dv_win[:BLOCK]
        dsink_ref[...] += jnp.concatenate(dsink_rows, axis=0)

    rev = lambda i: nb - 1 - i
    prev = lambda i: jnp.maximum(nb - 2 - i, 0)
    qg8 = _rep8(jnp.tile(q_gain[0], PAIR * GROUP))
    kg8 = _rep8(jnp.tile(k_gain[0], PAIR))
    dq, dk, dv, dqg, dkg, dsink = pl.pallas_call(
        body, grid=(npair, nb),
        in_specs=[pl.BlockSpec((BLOCK, QW), lambda p, i: (rev(i), p)),
                  pl.BlockSpec((BLOCK, KW), lambda p, i: (prev(i), k0 + p)),
                  pl.BlockSpec((BLOCK, KW), lambda p, i: (rev(i), k0 + p)),
                  pl.BlockSpec((BLOCK, KW), lambda p, i: (prev(i), v0 + p)),
                  pl.BlockSpec((BLOCK, KW), lambda p, i: (rev(i), v0 + p)),
                  pl.BlockSpec((BLOCK, QW), lambda p, i: (rev(i), p)),
                  pl.BlockSpec((8, QW), lambda p, i: (0, 0)), pl.BlockSpec((8, KW), lambda p, i: (0, 0)),
                  pl.BlockSpec(memory_space=pltpu.SMEM)],
        out_specs=[pl.BlockSpec((BLOCK, QW), lambda p, i: (rev(i), p)),
                   pl.BlockSpec((BLOCK, KW), lambda p, i: (rev(i), p)),
                   pl.BlockSpec((BLOCK, KW), lambda p, i: (rev(i), p)),
                   pl.BlockSpec((None, 8, QW), lambda p, i: (p, 0, 0)),
                   pl.BlockSpec((None, 8, KW), lambda p, i: (p, 0, 0)),
                   pl.BlockSpec((None, PAIR * GROUP, LANE), lambda p, i: (p, 0, 0))],
        out_shape=[jax.ShapeDtypeStruct((T, dq_w), BF16),
                   jax.ShapeDtypeStruct((T, npair * KW), BF16),
                   jax.ShapeDtypeStruct((T, npair * KW), BF16),
                   jax.ShapeDtypeStruct((npair, 8, QW), F32),
                   jax.ShapeDtypeStruct((npair, 8, KW), F32),
                   jax.ShapeDtypeStruct((npair, PAIR * GROUP, LANE), F32)],
        scratch_shapes=[pltpu.VMEM((BLOCK, KW), F32), pltpu.VMEM((BLOCK, KW), F32)],
        compiler_params=_cp("parallel", "arbitrary"), name=name,
    )(qkv, qkv, qkv, qkv, qkv, d_out, qg8, kg8, sinks)
    dqg = jnp.sum(dqg.reshape(-1, HEAD_DIM), axis=0, keepdims=True)
    dkg = jnp.sum(dkg.reshape(-1, HEAD_DIM), axis=0, keepdims=True)
    return dq, dk, dv, dqg, dkg, dsink[:, :, 0].reshape(-1)


def _softplus_neg(lam):
    return jnp.maximum(-lam, 0.0) + jnp.log1p(jnp.exp(-jnp.abs(lam)))


def _causal_conv(x, prev8, cw_ref, cb8):
    K = cw_ref.shape[0]
    acc = _tapmul(x, cw_ref[K - 1])
    for s in range(1, K):
        acc = acc + _tapmul(_shift_down(x, s, prev8), cw_ref[K - 1 - s])
    return (_rows8(acc) + cb8[None]).reshape(x.shape)


def _bcast_row(x, row):
    return jnp.broadcast_to(x[row:row + 1, :], x.shape)


def _rows8(x):
    return x.reshape(x.shape[0] // 8, 8, x.shape[1])


def _tapmul(x, w8):
    return (_rows8(x) * w8[None]).reshape(x.shape)


def _colsum8(x):
    return jnp.sum(_rows8(x), axis=0)


def _shift_down(x, s, prev8):
    R = x.shape[0]
    return _shift_up(jnp.concatenate([prev8, x[:R - 8]], axis=0), 8 - s, x[R - 8:])


def _shift_up(x, s, next8):
    R = x.shape[0]
    r = pltpu.roll(x, R - s, 0)
    rowid = lax.broadcasted_iota(jnp.int32, (8, 1), 0)
    tail = jnp.where(rowid < 8 - s, r[R - 8:], pltpu.roll(next8, 8 - s, 0))
    return jnp.concatenate([r[:R - 8], tail], axis=0)


def _rep8(v):
    return jnp.broadcast_to(v[..., None, :], v.shape[:-1] + (8, v.shape[-1]))


def rec_fwd(z, cw, cb, wa, ba, wi, bi, lam, *, name):
    T, C2 = z.shape
    C = C2 // 2
    nblk, bd, _ = wa.shape
    tt = _tile(T, 128, HALO)
    ng = tt // 8

    def body(x_ref, y_ref, halo_ref, cw_ref, cb_ref, wa_ref, ba_ref, wi_ref, bi_ref, lam_ref,
             xb_ref, r_ref, i_ref, a_ref, h_ref, hp_ref, hg_ref, carry, u_scr):
        step = pl.program_id(0)

        @pl.when(step == 0)
        def _():
            carry[...] = jnp.zeros_like(carry)

        xb = _causal_conv(x_ref[...], jnp.where(step > 0, halo_ref[HALO - 8:, :], 0.0), cw_ref, cb_ref[...])
        xb_ref[...] = xb
        pa, pi = [], []
        for b in range(nblk):
            xs = xb[:, b * bd:(b + 1) * bd]
            pa.append(_dot(xs, wa_ref[b], NN))
            pi.append(_dot(xs, wi_ref[b], NN))
        r = jax.nn.sigmoid(jnp.concatenate(pa, axis=1) + ba_ref[...])
        ig = jax.nn.sigmoid(jnp.concatenate(pi, axis=1) + bi_ref[...])
        r_ref[...] = r
        i_ref[...] = ig
        nl = LRU_C * r * _softplus_neg(lam_ref[...])
        a_ref[...] = jnp.exp(-nl)
        th = jnp.tanh(nl)
        u_scr[...] = jnp.sqrt(2.0 * th / (1.0 + th)) * (ig * xb)

        rowid = lax.broadcasted_iota(jnp.int32, (8, C), 0)

        def group(gi, hc):
            r0 = pl.multiple_of(gi * 8, 8)
            a8 = a_ref[pl.ds(r0, 8), :]
            u8 = u_scr[pl.ds(r0, 8), :]
            for d in (1, 2, 4):
                a_sh = jnp.where(rowid >= d, pltpu.roll(a8, d, 0), 1.0)
                u_sh = jnp.where(rowid >= d, pltpu.roll(u8, d, 0), 0.0)
                u8 = a8 * u_sh + u8
                a8 = a8 * a_sh
            h8 = u8 + a8 * hc
            h_ref[pl.ds(r0, 8), :] = h8
            hp_ref[pl.ds(r0, 8), :] = jnp.where(rowid >= 1, pltpu.roll(h8, 1, 0), hc)
            return _bcast_row(h8, 7)

        carry[...] = lax.fori_loop(0, ng, group, carry[...])
        hg_ref[...] = (h_ref[...] * _gelu(y_ref[...])).astype(hg_ref.dtype)

    row = lambda c: pl.BlockSpec((tt, C), lambda i, c=c: (i, c))
    vec = pl.BlockSpec((1, C), lambda i: (0, 0))
    full = lambda shp: pl.BlockSpec(shp, lambda i, n=len(shp): (0,) * n)
    per = tt // HALO
    outs = pl.pallas_call(
        body, grid=(T // tt,),
        in_specs=[row(0), row(1), pl.BlockSpec((HALO, C), lambda i: (jnp.maximum(i * per - 1, 0), 0)),
                  full(cw.shape[:1] + (8, C)), full((8, C)), full(wa.shape), vec, full(wi.shape), vec, vec],
        out_specs=[row(0)] * 7,
        out_shape=[jax.ShapeDtypeStruct((T, C), F32)] * 6 + [jax.ShapeDtypeStruct((T, C), BF16)],
        scratch_shapes=[pltpu.VMEM((8, C), F32), pltpu.VMEM((tt, C), F32)],
        compiler_params=_cp("arbitrary"), name=name,
    )(z, z, z, _rep8(cw), _rep8(cb[0]), wa, ba, wi, bi, lam)
    return outs


def rec_bwd_scan(dhg, h, a, z, *, name):
    T, C = h.shape
    tt = _tile(T, 256, HALO)
    ng = tt // 8
    nb = T // tt

    def body(dhg_ref, h_ref, a_ref, y_ref, dy_ref, yb_ref, ycarry, acarry, g_scr):
        step = pl.program_id(0)

        @pl.when(step == 0)
        def _():
            ycarry[...] = jnp.zeros_like(ycarry)
            acarry[...] = jnp.zeros_like(acarry)

        gate, dgate = _gelu_and_grad(y_ref[...])
        dhg_v = dhg_ref[...].astype(F32)
        dy_ref[...] = (dhg_v * h_ref[...] * dgate).astype(dy_ref.dtype)
        g_scr[...] = dhg_v * gate
        rowid = lax.broadcasted_iota(jnp.int32, (8, C), 0)

        def group(j, c):
            yc, ac = c
            r0 = pl.multiple_of((ng - 1 - j) * 8, 8)
            a8 = a_ref[pl.ds(r0, 8), :]
            y8 = g_scr[pl.ds(r0, 8), :]
            b8 = jnp.where(rowid < 7, pltpu.roll(a8, 7, 0), ac)
            for d in (1, 2, 4):
                y_sh = jnp.where(rowid < 8 - d, pltpu.roll(y8, 8 - d, 0), 0.0)
                b_sh = jnp.where(rowid < 8 - d, pltpu.roll(b8, 8 - d, 0), 1.0)
                y8 = y8 + b8 * y_sh
                b8 = b8 * b_sh
            y8 = y8 + b8 * yc
            yb_ref[pl.ds(r0, 8), :] = y8
            return _bcast_row(y8, 0), _bcast_row(a8, 0)

        yc, ac = lax.fori_loop(0, ng, group, (ycarry[...], acarry[...]))
        ycarry[...] = yc
        acarry[...] = ac

    rev = lambda c: pl.BlockSpec((tt, C), lambda i, c=c: (nb - 1 - i, c))
    return pl.pallas_call(
        body, grid=(nb,), in_specs=[rev(0), rev(0), rev(0), rev(1)], out_specs=[rev(0), rev(0)],
        out_shape=[jax.ShapeDtypeStruct((T, C), BF16), jax.ShapeDtypeStruct((T, C), F32)],
        scratch_shapes=[pltpu.VMEM((8, C), F32), pltpu.VMEM((8, C), F32), pltpu.VMEM((tt, C), F32)],
        compiler_params=_cp("arbitrary"), name=name,
    )(dhg, h, a, z)


def rec_bwd_gates(ybar, hprev, a, r, ig, xb, lam, wa, wi, *, name):
    T, C = xb.shape
    nblk, bd, _ = wa.shape
    tt = _tile(T, 256, 8)
    nb = T // tt

    def body(y_ref, hp_ref, a_ref, r_ref, i_ref, xb_ref, lam_ref, wa_ref, wi_ref,
             dxb_ref, dwa_ref, dwi_ref, dba_ref, dbi_ref, dlam_ref):
        step = pl.program_id(0)

        @pl.when(step == 0)
        def _():
            for ref in (dwa_ref, dwi_ref, dba_ref, dbi_ref, dlam_ref):
                ref[...] = jnp.zeros_like(ref)

        y, av, rv, iv, xv = y_ref[...], a_ref[...], r_ref[...], i_ref[...], xb_ref[...]
        sp = _softplus_neg(lam_ref[...])
        th = jnp.tanh(LRU_C * rv * sp)
        s = jnp.sqrt(2.0 * th / (1.0 + th))
        d_nl = -(y * hp_ref[...] * av) + (y * iv * xv) * (av * av) / s
        dlam_ref[...] += jnp.sum(d_nl * rv, axis=0, keepdims=True) * LRU_C
        dr = d_nl * (LRU_C * sp)
        di = y * s * xv
        dpa = dr * rv * (1.0 - rv)
        dpi = di * iv * (1.0 - iv)
        dba_ref[...] += jnp.sum(dpa, axis=0, keepdims=True)
        dbi_ref[...] += jnp.sum(dpi, axis=0, keepdims=True)
        parts = []
        for b in range(nblk):
            sl = slice(b * bd, (b + 1) * bd)
            xs, da_b, di_b = xv[:, sl], dpa[:, sl], dpi[:, sl]
            dwa_ref[b] += _dot(xs, da_b, TN)
            dwi_ref[b] += _dot(xs, di_b, TN)
            parts.append(_dot(da_b, wa_ref[b], NT) + _dot(di_b, wi_ref[b], NT))
        dxb_ref[...] = y * s * iv + jnp.concatenate(parts, axis=1)

        @pl.when(step == nb - 1)
        def _():
            dlam_ref[...] = dlam_ref[...] * (-jax.nn.sigmoid(-lam_ref[...]))

    row = pl.BlockSpec((tt, C), lambda i: (i, 0))
    vec = pl.BlockSpec((1, C), lambda i: (0, 0))
    wsp = pl.BlockSpec(wa.shape, lambda i: (0, 0, 0))
    return pl.pallas_call(
        body, grid=(nb,), in_specs=[row] * 6 + [vec, wsp, wsp],
        out_specs=[row, wsp, wsp, vec, vec, vec],
        out_shape=[jax.ShapeDtypeStruct((T, C), F32), jax.ShapeDtypeStruct(wa.shape, F32),
                   jax.ShapeDtypeStruct(wa.shape, F32)] + [jax.ShapeDtypeStruct((1, C), F32)] * 3,
        compiler_params=_cp("arbitrary"), name=name,
    )(ybar, hprev, a, r, ig, xb, lam, wa, wi)


def conv_bwd(d, x0, cw, *, name):
    T, C = d.shape
    K = cw.shape[0]
    tt = _tile(T, 256, HALO)
    per = tt // HALO
    nb = T // tt

    def body(d_ref, dn_ref, x_ref, cw_ref, dx_ref, dcw_ref, dcb_ref, dcw_acc, dcb_acc):
        step = pl.program_id(1)

        @pl.when(step == 0)
        def _():
            dcw_acc[...] = jnp.zeros_like(dcw_acc)
            dcb_acc[...] = jnp.zeros_like(dcb_acc)

        dv = d_ref[...].astype(F32)
        next8 = jnp.where(step < nb - 1, dn_ref[...].astype(F32)[0:8], 0.0)
        xt = x_ref[...].astype(F32)
        acc = _tapmul(dv, cw_ref[K - 1])
        dcw_acc[K - 1] += _colsum8(dv * xt)
        dcb_acc[...] += _colsum8(dv)
        for sh in range(1, K):
            dsh = _shift_up(dv, sh, next8)
            acc = acc + _tapmul(dsh, cw_ref[K - 1 - sh])
            dcw_acc[K - 1 - sh] += _colsum8(dsh * xt)
        dx_ref[...] = acc.astype(dx_ref.dtype)

        @pl.when(step == nb - 1)
        def _():
            dcw_ref[...] = jnp.sum(dcw_acc[...], axis=1)
            dcb_ref[...] = jnp.sum(dcb_acc[...], axis=0, keepdims=True)

    tc = C
    row = pl.BlockSpec((tt, tc), lambda j, i: (i, j))
    return pl.pallas_call(
        body, grid=(C // tc, nb),
        in_specs=[row, pl.BlockSpec((HALO, tc), lambda j, i: (jnp.minimum((i + 1) * per, T // HALO - 1), j)),
                  row, pl.BlockSpec((K, 8, tc), lambda j, i: (0, 0, j))],
        out_specs=[row, pl.BlockSpec((K, tc), lambda j, i: (0, j)), pl.BlockSpec((1, tc), lambda j, i: (0, j))],
        out_shape=[jax.ShapeDtypeStruct((T, C), BF16), jax.ShapeDtypeStruct((K, C), F32),
                   jax.ShapeDtypeStruct((1, C), F32)],
        scratch_shapes=[pltpu.VMEM((K, 8, tc), F32), pltpu.VMEM((8, tc), F32)],
        compiler_params=_cp("parallel", "arbitrary"), name=name,
    )(d, d, x0, _rep8(cw))


def ffn_act_fwd(u0, cw, cb, *, n, name):
    T, W = u0.shape
    G = W // (2 * n)
    tt = _tile(T, 256, HALO)
    per = tt // HALO

    K = cw.shape[0]

    def body(u_ref, up_ref, cw_ref, cb_ref, a_ref, uo_ref):
        step = pl.program_id(1)
        prev8 = jnp.where(step > 0, up_ref[...].astype(F32)[HALO - 8:], 0.0)
        u = _causal_conv(u_ref[...].astype(F32), prev8, cw_ref, cb_ref[...])
        uo_ref[...] = u.astype(uo_ref.dtype)
        a_ref[...] = (_gelu(u[:, :n]) * u[:, n:]).astype(a_ref.dtype)

    return pl.pallas_call(
        body, grid=(G, T // tt),
        in_specs=[pl.BlockSpec((tt, 2 * n), lambda j, i: (i, j)),
                  pl.BlockSpec((HALO, 2 * n), lambda j, i: (jnp.maximum(i * per - 1, 0), j)),
                  pl.BlockSpec((K, 8, 2 * n), lambda j, i: (0, 0, j)),
                  pl.BlockSpec((8, 2 * n), lambda j, i: (0, j))],
        out_specs=[pl.BlockSpec((tt, n), lambda j, i: (i, j)), pl.BlockSpec((tt, 2 * n), lambda j, i: (i, j))],
        out_shape=[jax.ShapeDtypeStruct((T, G * n), BF16), jax.ShapeDtypeStruct((T, W), BF16)],
        compiler_params=_cp("parallel", "parallel"), name=name)(u0, u0, _rep8(cw), _rep8(cb[0]))


def ffn_down_act_bwd(dx, w_down, u0, u, cw, *, n, carry=None, name):
    T, W = u0.shape
    D = dx.shape[1]
    G = W // (2 * n)
    K = cw.shape[0]
    tt = _tile(T, 256, HALO)
    tc = _tile(n, 512)
    nb = T // tt
    car = carry if carry is not None else dict(arrays=[], out_structs=[], sems=[])
    ncar = len(car["arrays"])

    def body(dx_ref, w_ref, x_ref, u_ref, cw_ref, *rest):
        car_in, (du_ref, dcw_ref, dcb_ref) = rest[:ncar], rest[ncar:ncar + 3]
        car_out, (later8, dcw_acc, dcb_acc) = rest[ncar + 3:2 * ncar + 3], rest[2 * ncar + 3:2 * ncar + 6]
        sems = rest[2 * ncar + 6:]
        group, step = pl.program_id(0), pl.program_id(1)
        if carry is not None:
            @pl.when((group == 0) & (step == 0))
            def _():
                carry["start"](car_in, car_out, sems)

            @pl.when((group == G - 1) & (step == nb - 1))
            def _():
                carry["finish"](car_in, car_out, sems)

        @pl.when(step == 0)
        def _():
            later8[...] = jnp.zeros_like(later8)
            dcw_acc[...] = jnp.zeros_like(dcw_acc)
            dcb_acc[...] = jnp.zeros_like(dcb_acc)

        dact = _dot(dx_ref[...], w_ref[...], NT)
        for q in range(n // tc):
            gs, vs = slice(q * tc, (q + 1) * tc), slice(n + q * tc, n + (q + 1) * tc)
            gl, dgl = _gelu_and_grad(u_ref[:, gs].astype(F32))
            daf = dact[:, gs]
            for cols, d in ((gs, daf * u_ref[:, vs].astype(F32) * dgl), (vs, daf * gl)):
                next8 = later8[:, cols]
                xt = x_ref[:, cols].astype(F32)
                acc = _tapmul(d, cw_ref[K - 1, :, cols])
                dcw_acc[K - 1, :, cols] += _colsum8(d * xt)
                dcb_acc[:, cols] += _colsum8(d)
                for sh in range(1, K):
                    dsh = _shift_up(d, sh, next8)
                    acc = acc + _tapmul(dsh, cw_ref[K - 1 - sh, :, cols])
                    dcw_acc[K - 1 - sh, :, cols] += _colsum8(dsh * xt)
                du_ref[:, cols] = acc.astype(du_ref.dtype)
                later8[:, cols] = d[0:8]

        @pl.when(step == nb - 1)
        def _():
            dcw_ref[...] = jnp.sum(dcw_acc[...], axis=1)
            dcb_ref[...] = jnp.sum(dcb_acc[...], axis=0, keepdims=True)

    rev = lambda i: nb - 1 - i
    outs = pl.pallas_call(
        body, grid=(G, nb),
        in_specs=[pl.BlockSpec((tt, D), lambda j, i: (rev(i), 0)),
                  pl.BlockSpec((n, D), lambda j, i: (j, 0)),
                  pl.BlockSpec((tt, 2 * n), lambda j, i: (rev(i), j)),
                  pl.BlockSpec((tt, 2 * n), lambda j, i: (rev(i), j)),
                  pl.BlockSpec((K, 8, 2 * n), lambda j, i: (0, 0, j))] + [HBM] * ncar,
        out_specs=[pl.BlockSpec((tt, 2 * n), lambda j, i: (rev(i), j)),
                   pl.BlockSpec((K, 2 * n), lambda j, i: (0, j)),
                   pl.BlockSpec((1, 2 * n), lambda j, i: (0, j))] + [HBM] * ncar,
        out_shape=[jax.ShapeDtypeStruct((T, W), BF16), jax.ShapeDtypeStruct((K, W), F32),
                   jax.ShapeDtypeStruct((1, W), F32)] + list(car["out_structs"]),
        scratch_shapes=[pltpu.VMEM((8, 2 * n), F32), pltpu.VMEM((K, 8, 2 * n), F32), pltpu.VMEM((8, 2 * n), F32)]
        + list(car["sems"]),
        compiler_params=_cp(*(("parallel", "arbitrary") if carry is None else ("arbitrary", "arbitrary"))), name=name,
    )(dx, w_down, u0, u, _rep8(cw), *car["arrays"])
    return tuple(outs[:3]) if carry is None else (tuple(outs[:3]), list(outs[3:]))


def _ffn_shard_of_block(q):
    return (q % 2) * (N_DEV // 2) + q // 2


def _ffn_block_of_shard(s):
    return (s % (N_DEV // 2)) * 2 + s // (N_DEV // 2)


def _group_cols(v):
    lead = v.shape[:-1]
    n = v.shape[-1] // N_DEV
    return jnp.swapaxes(v.reshape(lead + (2, N_DEV // 2, n)), -3, -2).reshape(v.shape)


def _ungroup_cols(v):
    lead = v.shape[:-1]
    n = v.shape[-1] // N_DEV
    return jnp.swapaxes(v.reshape(lead + (N_DEV // 2, 2, n)), -3, -2).reshape(v.shape)


def local_step(x, target, w, plan=None):
    depth = w["mix_norm"].shape[0]
    n_up = w["ffn_conv_w"].shape[-1] // N_DEV
    g = {k: [None] * (w[k].shape[0] if hasattr(w[k], "shape") else len(w[k])) for k in w}

    def run(fn, stage, l, slot, *args, **kw):
        carry = None if plan is None else plan.carry(stage, l, slot, g)
        if carry is None:
            return fn(*args, **kw)
        out, extra = fn(*args, carry=carry, **kw)
        plan.done(stage, l, slot, extra)
        return out

    saved = []
    for l in range(depth):
        j = l // 2
        h = rmsnorm_fwd(x, w["mix_norm"][l:l + 1], name="mix_norm_fwd")
        if l % 2 == 0:
            qkv = run(mm_nn, "fwd", l, "in", h, w["attn_w_qkv"][j], out_dtype=F32, name="qkv_proj")
            ao = run(attn_fwd, "fwd", l, "attn", qkv, w["attn_q_gain"][j:j + 1], w["attn_k_gain"][j:j + 1],
                     w["attn_sinks"][j:j + 1], name="attn_fwd")
            x1 = run(mm_nn, "fwd", l, "out", ao, w["attn_w_o"][j], out_dtype=F32, res=x, name="attn_out_proj")
            mix = (qkv, ao)
        else:
            z = run(mm_nn, "fwd", l, "in", h, w["rec_w_in"][j], out_dtype=F32, name="rec_in_proj")
            xb, r, ig, a, hs, hprev, hg = rec_fwd(
                z, w["rec_conv_w"][j], w["rec_conv_b"][j:j + 1], w["rec_w_a"][j], w["rec_b_a"][j:j + 1],
                w["rec_w_i"][j], w["rec_b_i"][j:j + 1], w["rec_lambda"][j:j + 1], name="rec_fwd")
            x1 = run(mm_nn, "fwd", l, "out", hg, w["rec_w_out"][j], out_dtype=F32, res=x, name="rec_out_proj")
            mix = (z, xb, r, ig, a, hs, hprev, hg)
        h2 = rmsnorm_fwd(x1, w["ffn_norm"][l:l + 1], name="ffn_norm_fwd")
        u0 = run(mm_nn, "fwd", l, "up", h2, w["ffn_w_up"][l], out_dtype=BF16, name="ffn_up_proj")
        act, u = ffn_act_fwd(u0, w["ffn_conv_w"][l], w["ffn_conv_b"][l:l + 1], n=n_up, name="ffn_act_fwd")
        x2 = run(mm_nn, "fwd", l, "down", act, w["ffn_w_down"][l], out_dtype=F32, res=x1, name="ffn_down_proj")
        saved.append((x, h, mix, x1, h2, u0, u, act))
        x = x2

    loss_vec, dx, dxb = loss_head(x, target, name="loss_head")

    for l in reversed(range(depth)):
        j = l // 2
        x0, h, mix, x1, h2, u0, u, act = saved[l]
        g["ffn_w_down"][l] = mm_tn(act, dxb, out_dtype=BF16, name="ffn_down_dw")
        du0, dcw, dcb = run(ffn_down_act_bwd, "bwd", l, "act_bwd", dxb, w["ffn_w_down"][l], u0, u,
                            w["ffn_conv_w"][l], n=n_up, name="ffn_down_act_bwd")
        g["ffn_conv_w"][l], g["ffn_conv_b"][l] = _ungroup_cols(dcw), _ungroup_cols(dcb)[0]
        g["ffn_w_up"][l] = run(mm_tn, "bwd", l, "up_dw", h2, du0, out_dtype=BF16, col_shards=N_DEV, shard_of_block=_ffn_shard_of_block,
                                 name="ffn_up_dw")
        dh2 = run(mm_nt, "bwd", l, "up_dx", du0, w["ffn_w_up"][l], out_dtype=BF16, name="ffn_up_dx")
        dx1, dx1b, dgf = rmsnorm_bwd(x1, w["ffn_norm"][l:l + 1], dh2, dx, name="ffn_norm_bwd")
        g["ffn_norm"][l] = dgf[0]
        if l % 2 == 0:
            qkv, ao = mix
            dao = mm_nt(dx1b, w["attn_w_o"][j], out_dtype=BF16, name="attn_out_dx")
            g["attn_w_o"][j] = mm_tn(ao, dx1b, out_dtype=BF16, name="attn_out_dw")
            dq, dk, dv, dqg, dkg, dsk = attn_bwd(qkv, dao, w["attn_q_gain"][j:j + 1], w["attn_k_gain"][j:j + 1],
                                                 w["attn_sinks"][j:j + 1], name="attn_bwd")
            g["attn_q_gain"][j], g["attn_k_gain"][j], g["attn_sinks"][j] = dqg[0], dkg[0], dsk
            dqkv = jnp.concatenate([dq, dk, dv], axis=1)
            g["attn_w_qkv"][j] = run(mm_tn, "bwd", l, "in_dw", h, dqkv, out_dtype=BF16, col_shards=N_DEV, name="qkv_dw")
            dh = run(mm_nt, "bwd", l, "in_dx", dqkv, w["attn_w_qkv"][j], out_dtype=BF16, name="qkv_dx")
        else:
            z, xb, r, ig, a, hs, hprev, hg = mix
            dhg = mm_nt(dx1b, w["rec_w_out"][j], out_dtype=BF16, name="rec_out_dx")
            g["rec_w_out"][j] = mm_tn(hg, dx1b, out_dtype=BF16, name="rec_out_dw")
            dyb, ybar = rec_bwd_scan(dhg, hs, a, z, name="rec_bwd_scan")
            dxb, dwa, dwi, dba, dbi, dlam = rec_bwd_gates(
                ybar, hprev, a, r, ig, xb, w["rec_lambda"][j:j + 1], w["rec_w_a"][j], w["rec_w_i"][j],
                name="rec_bwd_gates")
            dxb0, dcw, dcb = conv_bwd(dxb, z, w["rec_conv_w"][j], name="rec_conv_bwd")
            g["rec_w_a"][j], g["rec_w_i"][j] = dwa, dwi
            g["rec_b_a"][j], g["rec_b_i"][j], g["rec_lambda"][j] = dba[0], dbi[0], dlam[0]
            g["rec_conv_w"][j], g["rec_conv_b"][j] = dcw, dcb[0]
            dz = jnp.concatenate([dxb0, dyb], axis=1)
            g["rec_w_in"][j] = run(mm_tn, "bwd", l, "in_dw", h, dz, out_dtype=BF16, col_shards=N_DEV, name="rec_in_dw")
            dh = run(mm_nt, "bwd", l, "in_dx", dz, w["rec_w_in"][j], out_dtype=BF16, name="rec_in_dx")
        dx, dxb, dgm = rmsnorm_bwd(x0, w["mix_norm"][l:l + 1], dh, dx1, name="mix_norm_bwd")
        g["mix_norm"][l] = dgm[0]
    return loss_vec, dx, g


HBM = pl.BlockSpec(memory_space=pltpu.HBM)
N_PEER = N_DEV - 1


def _here():
    return lax.axis_index("x"), lax.axis_index("y"), lax.axis_index("c")


def _sid(dev):
    return 4 * dev[0] + 2 * dev[1] + dev[2]


def _exchange_sems(n):
    return [pltpu.SemaphoreType.DMA((n * N_PEER,)), pltpu.SemaphoreType.DMA((n * N_PEER,)),
            pltpu.SemaphoreType.DMA((n,))]


def gather_exchange(shards, out_structs, windows):
    n = len(shards)

    def parts(outs, sems):
        send_sems, recv_sems, _ = sems
        x, y, c = _here()
        me, sib = (x, y, c), (x, y, 1 - c)
        chips = [(1 - x, y), (x, 1 - y), (1 - x, 1 - y)]

        def copy(i, k, block, to, src=None):
            dst = windows[i](outs[i], _sid(block))
            return pltpu.make_async_remote_copy(
                src_ref=dst if src is None else src, dst_ref=dst,
                send_sem=send_sems.at[i * N_PEER + k], recv_sem=recv_sems.at[i * N_PEER + k],
                device_id=to, device_id_type=MESH)

        return me, sib, chips, c, copy

    def own_copies(ins, outs, sems):
        me, sib, chips, c, copy = parts(outs, sems)
        local = [pltpu.make_async_copy(ins[i], windows[i](outs[i], _sid(me)), sems[2].at[i]) for i in range(n)]
        first = []
        for i in range(n):
            first.append(copy(i, 0, me, sib, src=ins[i]))
            first += [copy(i, 1 + j, me, (*chip, c), src=ins[i]) for j, chip in enumerate(chips)]
        return local, first

    def start(ins, outs, sems):
        local, first = own_copies(ins, outs, sems)
        for cp in local + first:
            cp.start()

    def finish(ins, outs, sems):
        me, sib, chips, c, copy = parts(outs, sems)
        local, first = own_copies(ins, outs, sems)
        passed = []
        for i in range(n):
            for j, chip in enumerate(chips):
                copy(i, 1 + j, (*chip, c), me).wait_recv()
                fwd = copy(i, 4 + j, (*chip, c), sib)
                fwd.start()
                passed.append(fwd)
        for i in range(n):
            copy(i, 0, sib, me).wait_recv()
            for j, chip in enumerate(chips):
                copy(i, 4 + j, (*chip, 1 - c), me).wait_recv()
        for cp in first + passed:
            cp.wait_send()
        for cp in local:
            cp.wait()

    return dict(arrays=list(shards), out_structs=list(out_structs), sems=_exchange_sems(n), start=start,
                finish=finish)


def run_exchange(ex, *, name):
    n = len(ex["arrays"])

    def body(*refs):
        ins, outs, sems = refs[:n], refs[n:2 * n], refs[2 * n:]
        ex["start"](ins, outs, sems)
        ex["finish"](ins, outs, sems)

    return pl.pallas_call(
        body, in_specs=[HBM] * n, out_specs=[HBM] * n, out_shape=ex["out_structs"], scratch_shapes=ex["sems"],
        name=name)(*ex["arrays"])


def scatter_exchange(grads):
    n = len(grads)

    def parts(ins, outs, sems):
        send_sems, recv_sems, local_sems = sems
        x, y, c = _here()
        me = (x, y, c)
        peers = []
        for k in range(1, N_DEV):
            kx, ky, kc = (k >> 2) & 1, (k >> 1) & 1, k & 1
            peers.append((1 - x if kx else x, 1 - y if ky else y, 1 - c if kc else c))

        def copy(i, k):
            return pltpu.make_async_remote_copy(
                src_ref=ins[i].at[_sid(peers[k])], dst_ref=outs[i].at[_sid(me)],
                send_sem=send_sems.at[i * N_PEER + k], recv_sem=recv_sems.at[i * N_PEER + k],
                device_id=peers[k], device_id_type=MESH)

        def arrival(i, k):
            return pltpu.make_async_remote_copy(
                src_ref=ins[i].at[_sid(me)], dst_ref=outs[i].at[_sid(peers[k])],
                send_sem=send_sems.at[i * N_PEER + k], recv_sem=recv_sems.at[i * N_PEER + k],
                device_id=peers[k], device_id_type=MESH)

        local = [pltpu.make_async_copy(ins[i].at[_sid(me)], outs[i].at[_sid(me)], local_sems.at[i]) for i in range(n)]
        sends = [copy(i, k) for i in range(n) for k in range(N_PEER)]
        return local, sends, arrival

    def start(ins, outs, sems):
        local, sends, _ = parts(ins, outs, sems)
        for cp in local + sends:
            cp.start()

    def finish(ins, outs, sems):
        local, sends, arrival = parts(ins, outs, sems)
        for i in range(n):
            for k in range(N_PEER):
                arrival(i, k).wait_recv()
        for cp in sends:
            cp.wait_send()
        for cp in local:
            cp.wait()

    return dict(arrays=list(grads), out_structs=[jax.ShapeDtypeStruct(g.shape, g.dtype) for g in grads],
                sems=_exchange_sems(n), start=start, finish=finish)


def adamw_family(contribs, w, m, v, *, name):
    L, R, C = w.shape
    S = contribs[0].shape[0]
    tr = _tile(R, max(8, (1 << 20) // (C * S)), 8)
    nr = R // tr
    c1 = 1.0 / (1.0 - ADAM_B1 ** ADAM_STEP)
    c2 = 1.0 / (1.0 - ADAM_B2 ** ADAM_STEP)

    def body(*refs):
        c_refs = refs[:L]
        w_ref, m_ref, v_ref, g_ref, d_ref, nm_ref, nv_ref = refs[L:]
        layer = pl.program_id(0)
        for l in range(L):
            @pl.when(layer == l)
            def _(l=l):
                g = c_refs[l][0].astype(F32)
                for s in range(1, S):
                    g = g + c_refs[l][s].astype(F32)
                mm = ADAM_B1 * m_ref[...] + (1.0 - ADAM_B1) * g
                vv = ADAM_B2 * v_ref[...] + (1.0 - ADAM_B2) * (g * g)
                g_ref[...] = g
                nm_ref[...] = mm
                nv_ref[...] = vv
                d_ref[...] = -ADAM_LR * ((mm * c1) / (jnp.sqrt(vv * c2) + ADAM_EPS) + ADAM_WD * w_ref[...])

    def cspec(l):
        return pl.BlockSpec((S, tr, C), lambda ll, i, l=l: (0, jnp.where(ll == l, i, 0), 0))

    lay = pl.BlockSpec((None, tr, C), lambda ll, i: (ll, i, 0))
    return pl.pallas_call(
        body, grid=(L, nr), in_specs=[cspec(l) for l in range(L)] + [lay] * 3, out_specs=[lay] * 4,
        out_shape=[jax.ShapeDtypeStruct((L, R, C), F32)] * 4,
        compiler_params=_cp("arbitrary", "arbitrary"), name=name)(*contribs, w, m, v)


def sum_slots(a, *, name):
    S, R, C = a.shape
    tr = _tile(R, 256, 8)

    def body(a_ref, o_ref):
        t = a_ref[0]
        for s in range(1, S):
            t = t + a_ref[s]
        o_ref[...] = t

    return pl.pallas_call(
        body, grid=(R // tr,), in_specs=[pl.BlockSpec((S, tr, C), lambda i: (0, i, 0))],
        out_specs=pl.BlockSpec((tr, C), lambda i: (i, 0)), out_shape=jax.ShapeDtypeStruct((R, C), F32),
        compiler_params=_cp("parallel"), name=name)(a)


LANES = 128


def _pack(arrs):
    flat = jnp.concatenate([a.reshape(-1).astype(F32) for a in arrs])
    rows = -(-flat.shape[0] // LANES)
    rows = -(-rows // 256) * 256
    return jnp.pad(flat, (0, rows * LANES - flat.shape[0])).reshape(rows, LANES)


def _unpack(buf, shapes):
    flat = buf.reshape(-1)
    out, off = [], 0
    for shp in shapes:
        size = int(np.prod(shp))
        out.append(flat[off:off + size].reshape(shp))
        off += size
    return out


def _gather_last(g):
    t = jnp.moveaxis(g, 0, -2)
    return t.reshape(t.shape[:-2] + (t.shape[-2] * t.shape[-1],))


def _own_last(full, s):
    n = full.shape[-1] // N_DEV
    t = full.reshape(full.shape[:-1] + (N_DEV, n))
    return lax.dynamic_index_in_dim(t, s, axis=t.ndim - 2, keepdims=False)


BIG = ["attn_w_qkv", "attn_w_o", "rec_w_in", "rec_w_out", "ffn_w_up", "ffn_w_down", "rec_w_a", "rec_w_i"]
SMALL_REPLICATED = ["mix_norm", "ffn_norm", "attn_q_gain", "attn_k_gain", "attn_sinks", "ffn_conv_b"]
SMALL_SHARDED = ["rec_conv_w", "rec_conv_b", "rec_b_a", "rec_b_i", "rec_lambda", "ffn_conv_w"]
SMALL = SMALL_REPLICATED + SMALL_SHARDED
WEIGHTS = ["mix_norm", "ffn_norm", "attn_w_qkv", "attn_q_gain", "attn_k_gain", "attn_sinks", "attn_w_o", "rec_w_in",
           "rec_conv_w", "rec_conv_b", "rec_w_a", "rec_b_a", "rec_w_i", "rec_b_i", "rec_lambda", "rec_w_out",
           "ffn_w_up", "ffn_conv_w", "ffn_conv_b", "ffn_w_down"]


def _col_window(n, block_of_shard=None):
    def win(ref, s):
        q = s if block_of_shard is None else block_of_shard(s)
        return ref.at[:, pl.ds(pl.multiple_of(q * n, 128), n)]
    return win


def _row_window(r):
    return lambda ref, s: ref.at[pl.ds(pl.multiple_of(s * r, 16), r), :]


def _gate_window(r):
    return lambda ref, s: ref.at[:, pl.ds(pl.multiple_of(s * r, 16), r), :]


def _slot_window(ref, s):
    return ref.at[s]


def _idx(name, layer):
    return layer if name.startswith("ffn") else layer // 2


def _slot_names(layer, slot):
    attn = layer % 2 == 0
    return {"in": ["attn_w_qkv"] if attn else ["rec_w_in"],
            "out": ["attn_w_o"] if attn else ["rec_w_out", "rec_w_a", "rec_w_i"],
            "up": ["ffn_w_up"], "down": ["ffn_w_down"]}[slot]


def _gather_for(p, items):
    shards, structs, wins = [], [], []
    for nme, layer in items:
        sh = p[nme][_idx(nme, layer)].astype(BF16)
        if nme in ("attn_w_qkv", "rec_w_in", "ffn_w_up"):
            K, n = sh.shape
            structs.append(jax.ShapeDtypeStruct((K, n * N_DEV), BF16))
            wins.append(_col_window(n, _ffn_block_of_shard if nme == "ffn_w_up" else None))
        elif nme in ("rec_w_a", "rec_w_i"):
            nblk, r, bd = sh.shape
            structs.append(jax.ShapeDtypeStruct((nblk, r * N_DEV, bd), BF16))
            wins.append(_gate_window(r))
        else:
            r, N = sh.shape
            structs.append(jax.ShapeDtypeStruct((r * N_DEV, N), BF16))
            wins.append(_row_window(r))
        shards.append(sh)
    return gather_exchange(shards, structs, wins)


def _send_layout(name, t):
    if name in ("rec_w_a", "rec_w_i"):
        nblk, bd, _ = t.shape
        return jnp.transpose(t.reshape(nblk, N_DEV, bd // N_DEV, bd), (1, 0, 2, 3)).astype(BF16)
    if name in ("attn_w_o", "rec_w_out", "ffn_w_down"):
        return t.reshape((N_DEV, t.shape[0] // N_DEV) + t.shape[1:])
    return t


class _Plan:
    BWD_SLOT = {"up_dw": "down", "in_dw": "out", "in_dx": "in"}

    def __init__(self, p, w, contribs, depth):
        self.p, self.w, self.contribs, self.depth = p, w, contribs, depth
        self.pending = None

    def carry(self, stage, l, slot, g):
        if stage == "fwd":
            items = []
            if l == 0 and slot == "in":
                items += [(k, 0) for k in _slot_names(0, "out")]
            if l == 0 and slot == "attn":
                items += [(k, 0) for s in ("up", "down") for k in _slot_names(0, s)]
            if slot != "attn" and l + 1 < self.depth:
                items += [(k, l + 1) for k in _slot_names(l + 1, slot)]
            if not items:
                return None
            self.pending = items
            return _gather_for(self.p, items)
        if slot == "act_bwd":
            items = [("ffn_w_up", l + 1)] if l + 1 < self.depth else []
        elif slot == "up_dx":
            items = [("ffn_w_up", 0)] if l == 0 else []
        else:
            items = [(k, l) for k in _slot_names(l, self.BWD_SLOT[slot])]
        if not items:
            return None
        self.pending = items
        return scatter_exchange([_send_layout(k, g[k][_idx(k, layer)]) for k, layer in items])

    def done(self, stage, l, slot, outs):
        dst = self.w if stage == "fwd" else self.contribs
        for (k, layer), o in zip(self.pending, outs):
            dst[k][_idx(k, layer)] = o


def _train_step(p, x, target, mom, vel):
    depth = p["mix_norm"].shape[0]
    s_me = _sid(_here())

    w = {k: [None] * p[k].shape[0] for k in BIG}
    first = [(k, 0) for k in _slot_names(0, "in")]
    for (k, _), t in zip(first, run_exchange(_gather_for(p, first), name="all_gather_first")):
        w[k][0] = t

    local_small = [p[k] for k in SMALL_SHARDED]
    packed = _pack(local_small)
    gathered, = run_exchange(
        gather_exchange([packed], [jax.ShapeDtypeStruct((N_DEV,) + packed.shape, F32)], [_slot_window]),
        name="all_gather_small")
    per_dev = [_unpack(gathered[s], [a.shape for a in local_small]) for s in range(N_DEV)]
    for i, k in enumerate(SMALL_SHARDED):
        w[k] = _gather_last(jnp.stack([per_dev[s][i] for s in range(N_DEV)]))
    for k in SMALL_REPLICATED:
        w[k] = p[k]
    nrec = w["rec_b_a"].shape[0]
    w["rec_b_a"] = w["rec_b_a"].reshape(nrec, -1)
    w["rec_b_i"] = w["rec_b_i"].reshape(nrec, -1)
    w["ffn_conv_w"] = _group_cols(w["ffn_conv_w"])
    w["ffn_conv_b"] = _group_cols(w["ffn_conv_b"])

    contribs = {k: [None] * len(w[k]) for k in BIG}
    loss_vec, dx, g = local_step(x[0], target[0], w, _Plan(p, w, contribs, depth))
    loss = lax.psum(jnp.sum(loss_vec), ("x", "y", "c"))

    out = {}
    for k in BIG:
        shp = p[k].shape
        L = shp[0]
        C = shp[-1]
        R = int(np.prod(shp[1:-1]))
        cs = [c.reshape(N_DEV, R, C) for c in contribs[k]]
        res = adamw_family(cs, p[k].reshape(L, R, C), mom[k].reshape(L, R, C), vel[k].reshape(L, R, C),
                           name="adamw_" + k)
        out[k] = [t.reshape(shp) for t in res]

    gsmall = [jnp.stack(g[k]) for k in SMALL]
    gp = _pack(gsmall)
    gall, = run_exchange(
        gather_exchange([gp], [jax.ShapeDtypeStruct((N_DEV,) + gp.shape, F32)], [_slot_window]),
        name="all_gather_small_grads")
    gsum = _unpack(sum_slots(gall, name="sum_small_grads"), [a.shape for a in gsmall])
    glocal = []
    for k, t in zip(SMALL, gsum):
        if k in SMALL_SHARDED:
            t = _own_last(t.reshape(p[k].shape[:-1] + (p[k].shape[-1] * N_DEV,)), s_me)
        glocal.append(t.reshape(p[k].shape))
    wp, mp, vp, gpk = (_pack([d[k] for k in SMALL]) for d in (p, mom, vel, dict(zip(SMALL, glocal))))
    res = adamw_family([gpk[None]], wp[None], mp[None], vp[None], name="adamw_small")
    shapes = [p[k].shape for k in SMALL]
    unp = [_unpack(t[0], shapes) for t in res]
    for i, k in enumerate(SMALL):
        out[k] = [glocal[i], unp[1][i], unp[2][i], unp[3][i]]

    return (loss, dx[None]) + tuple(out[k][q] for q in range(4) for k in WEIGHTS)


def kernel(x, mix_norm, ffn_norm, attn_w_qkv, attn_q_gain, attn_k_gain, attn_sinks, attn_w_o, rec_w_in, rec_conv_w, rec_conv_b, rec_w_a, rec_b_a, rec_w_i, rec_b_i, rec_lambda, rec_w_out, ffn_w_up, ffn_conv_w, ffn_conv_b, ffn_w_down, loss_target, m_mix_norm, m_ffn_norm, m_attn_w_qkv, m_attn_q_gain, m_attn_k_gain, m_attn_sinks, m_attn_w_o, m_rec_w_in, m_rec_conv_w, m_rec_conv_b, m_rec_w_a, m_rec_b_a, m_rec_w_i, m_rec_b_i, m_rec_lambda, m_rec_w_out, m_ffn_w_up, m_ffn_conv_w, m_ffn_conv_b, m_ffn_w_down, v_mix_norm, v_ffn_norm, v_attn_w_qkv, v_attn_q_gain, v_attn_k_gain, v_attn_sinks, v_attn_w_o, v_rec_w_in, v_rec_conv_w, v_rec_conv_b, v_rec_w_a, v_rec_b_a, v_rec_w_i, v_rec_b_i, v_rec_lambda, v_rec_w_out, v_ffn_w_up, v_ffn_conv_w, v_ffn_conv_b, v_ffn_w_down):
    p = dict(zip(WEIGHTS, (mix_norm, ffn_norm, attn_w_qkv, attn_q_gain, attn_k_gain, attn_sinks, attn_w_o, rec_w_in,
                           rec_conv_w, rec_conv_b, rec_w_a, rec_b_a, rec_w_i, rec_b_i, rec_lambda, rec_w_out,
                           ffn_w_up, ffn_conv_w, ffn_conv_b, ffn_w_down)))
    mom = dict(zip(WEIGHTS, (m_mix_norm, m_ffn_norm, m_attn_w_qkv, m_attn_q_gain, m_attn_k_gain, m_attn_sinks,
                             m_attn_w_o, m_rec_w_in, m_rec_conv_w, m_rec_conv_b, m_rec_w_a, m_rec_b_a, m_rec_w_i,
                             m_rec_b_i, m_rec_lambda, m_rec_w_out, m_ffn_w_up, m_ffn_conv_w, m_ffn_conv_b,
                             m_ffn_w_down)))
    vel = dict(zip(WEIGHTS, (v_mix_norm, v_ffn_norm, v_attn_w_qkv, v_attn_q_gain, v_attn_k_gain, v_attn_sinks,
                             v_attn_w_o, v_rec_w_in, v_rec_conv_w, v_rec_conv_b, v_rec_w_a, v_rec_b_a, v_rec_w_i,
                             v_rec_b_i, v_rec_lambda, v_rec_w_out, v_ffn_w_up, v_ffn_conv_w, v_ffn_conv_b,
                             v_ffn_w_down)))
    return _train_step(p, x, loss_target, mom, vel)
```

```python
import math

import jax
import jax.numpy as jnp
import numpy as np
from jax import lax
from jax.experimental import pallas as pl
from jax.experimental.pallas import tpu as pltpu

F32 = jnp.float32
BF16 = jnp.bfloat16

N_DEV = 8
HEAD_DIM = 64
GROUP = 4
BLOCK = 128
LRU_C = 8.0
EPS = 1e-6
HALO = 16
ADAM_LR, ADAM_B1, ADAM_B2, ADAM_EPS, ADAM_WD, ADAM_STEP = 0.001, 0.9, 0.999, 1e-08, 0.01, 10
VMEM_LIMIT = 56 * 1024 * 1024
MESH = pl.DeviceIdType.MESH
GELU_C = math.sqrt(2.0 / math.pi)


def _cp(*sem, vmem=VMEM_LIMIT):
    return pltpu.CompilerParams(dimension_semantics=tuple(sem), vmem_limit_bytes=vmem)


def _tile(dim, pref, mult=128):
    if dim <= pref:
        return dim
    t = (pref // mult) * mult
    while t >= mult:
        if dim % t == 0:
            return t
        t -= mult
    return dim


def _gelu(x):
    th = jnp.tanh(GELU_C * (x + 0.044715 * x * x * x))
    return 0.5 * x * (1.0 + th)


def _gelu_and_grad(x):
    x2 = x * x
    th = jnp.tanh(x * (GELU_C + (GELU_C * 0.044715) * x2))
    a = 0.5 + 0.5 * th
    g = x * a
    dg = a + g * (1.0 - th) * (GELU_C + (3.0 * GELU_C * 0.044715) * x2)
    return g, dg


def _dot(a, b, dims):
    return lax.dot_general(a.astype(BF16), b.astype(BF16), (dims, ((), ())), preferred_element_type=F32)


NN = ((1,), (0,))
NT = ((1,), (1,))
TN = ((0,), (0,))


def _matmul(a, b, *, dims, grid, a_spec, b_spec, o_spec, out_shape, acc_shape, res=None, res_spec=None,
            norm_gain=None, carry=None, name):
    ni, nj, nk = grid
    nres = 0 if res is None else 1
    nnorm = 0 if norm_gain is None else 1
    ncar = 0 if carry is None else len(carry["arrays"])
    nin = 2 + nres + nnorm

    def body(*refs):
        a_ref, b_ref = refs[0], refs[1]
        r_ref = refs[2] if nres else None
        car_in = refs[nin:nin + ncar]
        o_ref = refs[nin + ncar]
        car_out = refs[nin + ncar + 1 + nnorm:nin + 2 * ncar + 1 + nnorm]
        scratch = refs[nin + 2 * ncar + 1 + nnorm:]
        sems = scratch[1 + nnorm:]
        i, j, k = pl.program_id(0), pl.program_id(1), pl.program_id(2)

        if carry is not None:
            @pl.when((i == 0) & (j == 0) & (k == 0))
            def _():
                carry["start"](car_in, car_out, sems)

        def finish(acc):
            if r_ref is not None:
                acc = acc + r_ref[...].astype(F32)
            o_ref[...] = acc.astype(o_ref.dtype)

        if nnorm:
            g_ref, h_ref, h_scr = refs[2 + nres], refs[nin + ncar + 1], scratch[1]

            @pl.when(j == 0)
            def _():
                xv = a_ref[...]
                r = lax.rsqrt(jnp.mean(xv * xv, axis=-1, keepdims=True) + EPS)
                hv = (xv * r * g_ref[...]).astype(h_scr.dtype)
                h_scr[...] = hv
                h_ref[...] = hv

            finish(_dot(h_scr[...], b_ref[...], dims))
        elif nk == 1:
            finish(_dot(a_ref[...], b_ref[...], dims))
        else:
            acc_ref = scratch[0]

            @pl.when(k == 0)
            def _():
                acc_ref[...] = _dot(a_ref[...], b_ref[...], dims)

            if nk > 2:
                @pl.when((k > 0) & (k < nk - 1))
                def _():
                    acc_ref[...] += _dot(a_ref[...], b_ref[...], dims)

            @pl.when(k == nk - 1)
            def _():
                finish(acc_ref[...] + _dot(a_ref[...], b_ref[...], dims))

        if carry is not None:
            @pl.when((i == ni - 1) & (j == nj - 1) & (k == nk - 1))
            def _():
                carry["finish"](car_in, car_out, sems)

    in_specs = [a_spec, b_spec] + ([res_spec] if nres else [])
    args = (a, b) + ((res,) if nres else ())
    out_specs, out_shapes = [o_spec], [out_shape]
    scratch_shapes = [pltpu.VMEM(acc_shape if nk > 1 else (8, 128), F32)]
    sem = ("parallel", "parallel", "arbitrary")
    if nnorm:
        assert nk == 1, "the normalised operand needs whole rows in one block"
        rows, width = a_spec.block_shape
        in_specs.append(pl.BlockSpec((1, width), lambda i, j, k: (0, 0)))
        args += (norm_gain,)
        out_specs.append(pl.BlockSpec((rows, width), lambda i, j, k: (i, 0)))
        out_shapes.append(jax.ShapeDtypeStruct(a.shape, BF16))
        scratch_shapes.append(pltpu.VMEM((rows, width), BF16))
        sem = ("parallel", "arbitrary", "arbitrary")
    if carry is not None:
        in_specs += [HBM] * ncar
        args += tuple(carry["arrays"])
        out_specs += [HBM] * ncar
        out_shapes += list(carry["out_structs"])
        scratch_shapes += carry["sems"]
        sem = ("arbitrary", "arbitrary", "arbitrary")
    outs = pl.pallas_call(
        body, grid=grid, in_specs=in_specs, out_specs=out_specs, out_shape=out_shapes,
        scratch_shapes=scratch_shapes, compiler_params=_cp(*sem), name=name,
    )(*args)
    main = outs[0] if not nnorm else (outs[0], outs[1])
    return main if carry is None else (main, list(outs[1 + nnorm:]))


def mm_nn(a, b, *, out_dtype, res=None, norm_gain=None, carry=None, name):
    M, K = a.shape
    N = b.shape[1]
    wide = res is None and out_dtype == BF16 and norm_gain is None
    tm, tn, tk = _tile(M, 1024), _tile(N, 2048 if wide else 1024), _tile(K, 2048)
    return _matmul(
        a, b, dims=NN, grid=(M // tm, N // tn, K // tk),
        a_spec=pl.BlockSpec((tm, tk), lambda i, j, k: (i, k)),
        b_spec=pl.BlockSpec((tk, tn), lambda i, j, k: (k, j)),
        o_spec=pl.BlockSpec((tm, tn), lambda i, j, k: (i, j)),
        out_shape=jax.ShapeDtypeStruct((M, N), out_dtype), acc_shape=(tm, tn),
        res=res, res_spec=pl.BlockSpec((tm, tn), lambda i, j, k: (i, j)), norm_gain=norm_gain, carry=carry, name=name)


def mm_nt(a, b, *, out_dtype, res=None, carry=None, name):
    M, N = a.shape
    K = b.shape[0]
    wide = res is None and out_dtype == BF16 and a.dtype == BF16
    tm, tn, tk = _tile(M, 1024), _tile(K, 2048 if wide else 1024), _tile(N, 2048)
    return _matmul(
        a, b, dims=NT, grid=(M // tm, K // tn, N // tk),
        a_spec=pl.BlockSpec((tm, tk), lambda i, j, k: (i, k)),
        b_spec=pl.BlockSpec((tn, tk), lambda i, j, k: (j, k)),
        o_spec=pl.BlockSpec((tm, tn), lambda i, j, k: (i, j)),
        out_shape=jax.ShapeDtypeStruct((M, K), out_dtype), acc_shape=(tm, tn),
        res=res, res_spec=pl.BlockSpec((tm, tn), lambda i, j, k: (i, j)), carry=carry, name=name)


def mm_tn(a, b, *, out_dtype, col_shards=None, shard_of_block=None, carry=None, name):
    T, K = a.shape
    N = b.shape[1]
    tt = _tile(T, 2048)
    tm = _tile(K, 1024)
    if col_shards is None:
        tn = _tile(N, 1024)
        o_spec = pl.BlockSpec((tm, tn), lambda i, j, k: (i, j))
        out_shape = jax.ShapeDtypeStruct((K, N), out_dtype)
    else:
        n = N // col_shards
        tn = _tile(n, 1536)
        per = n // tn
        sob = shard_of_block if shard_of_block is not None else (lambda s: s)
        o_spec = pl.BlockSpec((None, tm, tn), lambda i, j, k: (sob(j // per), i, j % per))
        out_shape = jax.ShapeDtypeStruct((col_shards, K, n), out_dtype)
    return _matmul(
        a, b, dims=TN, grid=(K // tm, N // tn, T // tt),
        a_spec=pl.BlockSpec((tt, tm), lambda i, j, k: (k, i)),
        b_spec=pl.BlockSpec((tt, tn), lambda i, j, k: (k, j)),
        o_spec=o_spec, out_shape=out_shape, acc_shape=(tm, tn), carry=carry, name=name)


def rmsnorm_bwd(x, g, dh, dres, *, name):
    T, D = x.shape
    tm = _tile(T, 512, 16)

    def body(x_ref, g_ref, dh_ref, dres_ref, dx_ref, dxb_ref, dg_ref):
        @pl.when(pl.program_id(0) == 0)
        def _():
            dg_ref[...] = jnp.zeros_like(dg_ref)

        xv = x_ref[...]
        dh_v = dh_ref[...].astype(F32)
        r = lax.rsqrt(jnp.mean(xv * xv, axis=-1, keepdims=True) + EPS)
        u = dh_v * g_ref[...]
        dot = jnp.mean(u * xv, axis=-1, keepdims=True)
        dx = dres_ref[...] + r * u - xv * (r * r * r * dot)
        dx_ref[...] = dx
        dxb_ref[...] = dx.astype(dxb_ref.dtype)
        dg_ref[...] += jnp.sum(dh_v * xv * r, axis=0, keepdims=True)

    row = pl.BlockSpec((tm, D), lambda i: (i, 0))
    vec = pl.BlockSpec((1, D), lambda i: (0, 0))
    return pl.pallas_call(
        body, grid=(T // tm,), in_specs=[row, vec, row, row], out_specs=[row, row, vec],
        out_shape=[jax.ShapeDtypeStruct((T, D), F32), jax.ShapeDtypeStruct((T, D), BF16),
                   jax.ShapeDtypeStruct((1, D), F32)],
        compiler_params=_cp("arbitrary"), name=name)(x, g, dh, dres)


def loss_head(y, target, *, name):
    T, D = y.shape
    tm = _tile(T, 512, 16)

    def body(y_ref, t_ref, l_ref, dy_ref, dyb_ref):
        @pl.when(pl.program_id(0) == 0)
        def _():
            l_ref[...] = jnp.zeros_like(l_ref)

        e = y_ref[...] - t_ref[...]
        dy = e * (1.0 / D)
        dy_ref[...] = dy
        dyb_ref[...] = dy.astype(dyb_ref.dtype)
        l_ref[...] += jnp.sum(e * e, axis=0, keepdims=True) * (0.5 / D)

    row = pl.BlockSpec((tm, D), lambda i: (i, 0))
    vec = pl.BlockSpec((1, D), lambda i: (0, 0))
    return pl.pallas_call(
        body, grid=(T // tm,), in_specs=[row, row], out_specs=[vec, row, row],
        out_shape=[jax.ShapeDtypeStruct((1, D), F32), jax.ShapeDtypeStruct((T, D), F32),
                   jax.ShapeDtypeStruct((T, D), BF16)],
        compiler_params=_cp("arbitrary"), name=name)(y, target)


NEG = -1e30


def _kv_heads_per_step(hkv):
    return 4 if hkv % 4 == 0 else 2


def _query_blocks_per_step(nblocks, want):
    while nblocks % want:
        want //= 2
    return want


LANE = 128


def _hi_lo_dot(x, w):
    hi = x.astype(BF16)
    lo = x - hi.astype(F32)
    return _dot(hi, w, NN) + _dot(lo, w, NN)


def _head_sum_matrix():
    r = lax.broadcasted_iota(jnp.int32, (LANE, LANE), 0) // HEAD_DIM
    c = lax.broadcasted_iota(jnp.int32, (LANE, LANE), 1) // HEAD_DIM
    return jnp.where(r == c, 1.0, 0.0).astype(BF16)


def _chunk(x, c):
    return x[:, c * LANE:(c + 1) * LANE]


def _head_sums(x, e):
    return jnp.concatenate([_hi_lo_dot(_chunk(x, c), e) for c in range(x.shape[1] // LANE)], axis=1)


def _row_sums(x):
    return _hi_lo_dot(x, jnp.ones((x.shape[1], LANE), BF16))


def _headnorm(x, e):
    r = lax.rsqrt(_head_sums(x * x, e) * (1.0 / HEAD_DIM) + EPS)
    return x * r, r


def _attn_mask(n):
    qi = lax.broadcasted_iota(jnp.int32, (BLOCK, 2 * BLOCK), 0)
    kj = lax.broadcasted_iota(jnp.int32, (BLOCK, 2 * BLOCK), 1)
    rel = qi + BLOCK - kj
    m = (rel >= 0) & (rel < BLOCK) & ((kj >= BLOCK) | (n > 0))
    return jnp.concatenate([m] * GROUP, axis=0)


def _qk_operands(qhat, khat, j, lo_half):
    kb = jnp.where(lo_half == (j % 2 == 0), _chunk(khat, j // 2), 0.0)
    kd = kb + pltpu.roll(kb, HEAD_DIM, 1)
    q4 = jnp.concatenate(
        [jnp.where(lo_half == (g % 2 == 0), _chunk(qhat, 2 * j + g // 2), 0.0) for g in range(GROUP)], axis=0)
    return q4, kd


def _attn_softmax(s, sink_ref, mask4, head0):
    s = s * (HEAD_DIM ** -0.5)
    sink = jnp.concatenate([jnp.full((BLOCK, 1), sink_ref[0, head0 + g], F32) for g in range(GROUP)], axis=0)
    m = jnp.maximum(jnp.max(jnp.where(mask4, s, NEG), axis=-1, keepdims=True), sink)
    ex = jnp.where(mask4, jnp.exp(s - m), 0.0)
    es = jnp.exp(sink - m)
    inv = 1.0 / (_row_sums(ex) + es)
    return ex * jnp.concatenate([inv] * (s.shape[1] // LANE), axis=1), es * inv[:, 0:1]


def attn_fwd(qkv, q_gain, k_gain, sinks, *, carry=None, name):
    ncar = 0 if carry is None else len(carry["arrays"])
    T, W = qkv.shape
    hq = W // HEAD_DIM * GROUP // (GROUP + 2)
    dq = hq * HEAD_DIM
    PAIR = _kv_heads_per_step(hq // GROUP)
    QW, KW = PAIR * GROUP * HEAD_DIM, PAIR * HEAD_DIM
    npair = hq // (GROUP * PAIR)
    QB = _query_blocks_per_step(T // BLOCK, 4)
    nb = T // (QB * BLOCK)
    k0 = dq // KW
    v0 = k0 + npair

    def body(q_ref, kp_ref, kc_ref, vp_ref, vc_ref, qg_ref, kg_ref, sink_ref, *rest):
        car_in, o_ref, car_out, sems = rest[:ncar], rest[ncar], rest[ncar + 1:2 * ncar + 1], rest[2 * ncar + 1:]
        p, m = pl.program_id(0), pl.program_id(1)
        if carry is not None:
            @pl.when((p == 0) & (m == 0))
            def _():
                carry["start"](car_in, car_out, sems)

            @pl.when((p == npair - 1) & (m == nb - 1))
            def _():
                carry["finish"](car_in, car_out, sems)

        e = _head_sum_matrix()
        lo_half = lax.broadcasted_iota(jnp.int32, (1, LANE), 1) < HEAD_DIM
        qn, _ = _headnorm(q_ref[...], e)
        qhat_all = _tapmul(qn, qg_ref[...])
        kn, _ = _headnorm(jnp.concatenate([kp_ref[...], kc_ref[...]], axis=0), e)
        khat_all = _tapmul(kn, kg_ref[...])
        v_all = jnp.concatenate([vp_ref[...], vc_ref[...]], axis=0).astype(BF16)
        ops = [[_qk_operands(qhat_all[qb * BLOCK:(qb + 1) * BLOCK], khat_all[qb * BLOCK:(qb + 2) * BLOCK], j, lo_half)
                for j in range(PAIR)] for qb in range(QB)]
        scores = [[_dot(q4, kd, NT) for q4, kd in ops[qb]] for qb in range(QB)]
        probs = [[_attn_softmax(s, sink_ref, _attn_mask(QB * m + qb), (p * PAIR + j) * GROUP)[0]
                  for j, s in enumerate(scores[qb])] for qb in range(QB)]
        outs = []
        for qb in range(QB):
            vwin = v_all[qb * BLOCK:(qb + 2) * BLOCK]
            o4 = [_dot(probs[qb][j], _chunk(vwin, j // 2), NN) for j in range(PAIR)]
            chunks = []
            for c in range(2 * PAIR):
                j, t = c // 2, c % 2
                a = o4[j][(2 * t) * BLOCK:(2 * t + 1) * BLOCK]
                b = o4[j][(2 * t + 1) * BLOCK:(2 * t + 2) * BLOCK]
                if j % 2 == 0:
                    b = pltpu.roll(b, HEAD_DIM, 1)
                else:
                    a = pltpu.roll(a, HEAD_DIM, 1)
                chunks.append(jnp.where(lo_half, a, b))
            outs.append(jnp.concatenate(chunks, axis=1))
        o_ref[...] = jnp.concatenate(outs, axis=0).astype(o_ref.dtype)

    prev = lambda m: jnp.maximum(QB * m - 1, 0)
    car = carry if carry is not None else dict(arrays=[], out_structs=[], sems=[])
    qg8 = _rep8(jnp.tile(q_gain[0], PAIR * GROUP))
    kg8 = _rep8(jnp.tile(k_gain[0], PAIR))
    outs = pl.pallas_call(
        body, grid=(npair, nb),
        in_specs=[pl.BlockSpec((QB * BLOCK, QW), lambda p, n: (n, p)),
                  pl.BlockSpec((BLOCK, KW), lambda p, n: (prev(n), k0 + p)),
                  pl.BlockSpec((QB * BLOCK, KW), lambda p, n: (n, k0 + p)),
                  pl.BlockSpec((BLOCK, KW), lambda p, n: (prev(n), v0 + p)),
                  pl.BlockSpec((QB * BLOCK, KW), lambda p, n: (n, v0 + p)),
                  pl.BlockSpec((8, QW), lambda p, n: (0, 0)), pl.BlockSpec((8, KW), lambda p, n: (0, 0)),
                  pl.BlockSpec(memory_space=pltpu.SMEM)] + [HBM] * ncar,
        out_specs=[pl.BlockSpec((QB * BLOCK, QW), lambda p, n: (n, p))] + [HBM] * ncar,
        out_shape=[jax.ShapeDtypeStruct((T, dq), BF16)] + list(car["out_structs"]),
        scratch_shapes=list(car["sems"]),
        compiler_params=_cp(*(("parallel", "parallel") if carry is None else ("arbitrary", "arbitrary"))),
        name=name)(qkv, qkv, qkv, qkv, qkv, qg8, kg8, sinks, *car["arrays"])
    return outs[0] if carry is None else (outs[0], list(outs[1:]))


def attn_bwd(qkv, d_out, q_gain, k_gain, sinks, *, name):
    T, W = qkv.shape
    hq = W // HEAD_DIM * GROUP // (GROUP + 2)
    dq_w = hq * HEAD_DIM
    PAIR = _kv_heads_per_step(hq // GROUP)
    QW, KW = PAIR * GROUP * HEAD_DIM, PAIR * HEAD_DIM
    npair = hq // (GROUP * PAIR)
    nb = T // BLOCK
    k0 = dq_w // KW
    v0 = k0 + npair

    def body(q_ref, kp_ref, kc_ref, vp_ref, vc_ref, do_ref, qg_ref, kg_ref, sink_ref,
             dq_ref, dk_ref, dv_ref, dqg_ref, dkg_ref, dsink_ref, dk_carry, dv_carry):
        p, i = pl.program_id(0), pl.program_id(1)
        n = nb - 1 - i

        @pl.when(i == 0)
        def _():
            dk_carry[...] = jnp.zeros_like(dk_carry)
            dv_carry[...] = jnp.zeros_like(dv_carry)
            dqg_ref[...] = jnp.zeros_like(dqg_ref)
            dkg_ref[...] = jnp.zeros_like(dkg_ref)
            dsink_ref[...] = jnp.zeros_like(dsink_ref)

        e = _head_sum_matrix()
        lo_half = lax.broadcasted_iota(jnp.int32, (1, LANE), 1) < HEAD_DIM
        mask4 = _attn_mask(n)
        qg, kg = qg_ref[...], kg_ref[...]
        q = q_ref[...]
        kwin = jnp.concatenate([kp_ref[...], kc_ref[...]], axis=0)
        qn, qr = _headnorm(q, e)
        qhat = _tapmul(qn, qg)
        kn, kr = _headnorm(kwin, e)
        khat = _tapmul(kn, kg)
        vwin = jnp.concatenate([vp_ref[...], vc_ref[...]], axis=0).astype(BF16)
        do = do_ref[...].astype(F32)
        ops = [_qk_operands(qhat, khat, j, lo_half) for j in range(PAIR)]
        do4 = []
        for j in range(PAIR):
            parts = []
            for g in range(GROUP):
                dc = _chunk(do, 2 * j + g // 2)
                if g % 2 != j % 2:
                    dc = pltpu.roll(dc, HEAD_DIM, 1)
                parts.append(jnp.where(lo_half == (j % 2 == 0), dc, 0.0))
            do4.append(jnp.concatenate(parts, axis=0))
        scores = [_dot(q4, kd, NT) for q4, kd in ops]
        dps = [_dot(do4[j], _chunk(vwin, j // 2), NT) for j in range(PAIR)]
        soft = [_attn_softmax(s, sink_ref, mask4, (p * PAIR + j) * GROUP) for j, s in enumerate(scores)]
        dss, dsink_rows = [], []
        for j in range(PAIR):
            pr, psink = soft[j]
            delta = _row_sums(pr * dps[j])
            dss.append(pr * (dps[j] - jnp.concatenate([delta] * 2, axis=1)) * (HEAD_DIM ** -0.5))
            dsk = -psink * delta[:, 0:1]
            for g in range(GROUP):
                tot = jnp.sum(dsk[g * BLOCK:(g + 1) * BLOCK], axis=0, keepdims=True)
                dsink_rows.append(jnp.broadcast_to(tot, (1, LANE)))
        dq4 = [_dot(dss[j], ops[j][1], NN) for j in range(PAIR)]
        dkd = [_dot(dss[j], ops[j][0], TN) for j in range(PAIR)]
        dvc = [_dot(soft[j][0], do4[j], TN) for j in range(PAIR)]
        dqhat = jnp.concatenate(
            [jnp.where(lo_half, dq4[c // 2][(2 * (c % 2)) * BLOCK:(2 * (c % 2) + 1) * BLOCK],
                       dq4[c // 2][(2 * (c % 2) + 1) * BLOCK:(2 * (c % 2) + 2) * BLOCK]) for c in range(2 * PAIR)],
            axis=1)
        dkhat_chunks, dv_chunks = [], []
        for kc in range(PAIR // 2):
            tot_k, tot_v = None, None
            for j in (2 * kc, 2 * kc + 1):
                t = jnp.where(lo_half == (j % 2 == 0), dkd[j] + pltpu.roll(dkd[j], HEAD_DIM, 1), 0.0)
                tot_k = t if tot_k is None else tot_k + t
                tot_v = dvc[j] if tot_v is None else tot_v + dvc[j]
            dkhat_chunks.append(tot_k)
            dv_chunks.append(tot_v)
        dkhat_win = jnp.concatenate(dkhat_chunks, axis=1)
        dv_win = jnp.concatenate(dv_chunks, axis=1)
        dqg_ref[...] += _colsum8(dqhat * qn)
        dqn = _tapmul(dqhat, qg)
        dq_ref[...] = (qr * dqn - q * (qr * qr * qr * _head_sums(dqn * q, e) * (1.0 / HEAD_DIM))).astype(dq_ref.dtype)
        dkh = dkhat_win[BLOCK:] + dk_carry[...]
        dk_carry[...] = dkhat_win[:BLOCK]
        kcur, knc, krc = kwin[BLOCK:], kn[BLOCK:], kr[BLOCK:]
        dkg_ref[...] += _colsum8(dkh * knc)
        dkn = _tapmul(dkh, kg)
        dk_ref[...] = (krc * dkn
                       - kcur * (krc * krc * krc * _head_sums(dkn * kcur, e) * (1.0 / HEAD_DIM))).astype(dk_ref.dtype)
        dv_ref[...] = (dv_win[BLOCK:] + dv_carry[...]).astype(dv_ref.dtype)
        dv_carry[...] = dv_win[:BLOCK]
        dsink_ref[...] += jnp.concatenate(dsink_rows, axis=0)

    rev = lambda i: nb - 1 - i
    prev = lambda i: jnp.maximum(nb - 2 - i, 0)
    qg8 = _rep8(jnp.tile(q_gain[0], PAIR * GROUP))
    kg8 = _rep8(jnp.tile(k_gain[0], PAIR))
    dq, dk, dv, dqg, dkg, dsink = pl.pallas_call(
        body, grid=(npair, nb),
        in_specs=[pl.BlockSpec((BLOCK, QW), lambda p, i: (rev(i), p)),
                  pl.BlockSpec((BLOCK, KW), lambda p, i: (prev(i), k0 + p)),
                  pl.BlockSpec((BLOCK, KW), lambda p, i: (rev(i), k0 + p)),
                  pl.BlockSpec((BLOCK, KW), lambda p, i: (prev(i), v0 + p)),
                  pl.BlockSpec((BLOCK, KW), lambda p, i: (rev(i), v0 + p)),
                  pl.BlockSpec((BLOCK, QW), lambda p, i: (rev(i), p)),
                  pl.BlockSpec((8, QW), lambda p, i: (0, 0)), pl.BlockSpec((8, KW), lambda p, i: (0, 0)),
                  pl.BlockSpec(memory_space=pltpu.SMEM)],
        out_specs=[pl.BlockSpec((BLOCK, QW), lambda p, i: (rev(i), p)),
                   pl.BlockSpec((BLOCK, KW), lambda p, i: (rev(i), p)),
                   pl.BlockSpec((BLOCK, KW), lambda p, i: (rev(i), p)),
                   pl.BlockSpec((None, 8, QW), lambda p, i: (p, 0, 0)),
                   pl.BlockSpec((None, 8, KW), lambda p, i: (p, 0, 0)),
                   pl.BlockSpec((None, PAIR * GROUP, LANE), lambda p, i: (p, 0, 0))],
        out_shape=[jax.ShapeDtypeStruct((T, dq_w), BF16),
                   jax.ShapeDtypeStruct((T, npair * KW), BF16),
                   jax.ShapeDtypeStruct((T, npair * KW), BF16),
                   jax.ShapeDtypeStruct((npair, 8, QW), F32),
                   jax.ShapeDtypeStruct((npair, 8, KW), F32),
                   jax.ShapeDtypeStruct((npair, PAIR * GROUP, LANE), F32)],
        scratch_shapes=[pltpu.VMEM((BLOCK, KW), F32), pltpu.VMEM((BLOCK, KW), F32)],
        compiler_params=_cp("parallel", "arbitrary"), name=name,
    )(qkv, qkv, qkv, qkv, qkv, d_out, qg8, kg8, sinks)
    dqg = jnp.sum(dqg.reshape(-1, HEAD_DIM), axis=0, keepdims=True)
    dkg = jnp.sum(dkg.reshape(-1, HEAD_DIM), axis=0, keepdims=True)
    return dq, dk, dv, dqg, dkg, dsink[:, :, 0].reshape(-1)


def _softplus_neg(lam):
    return jnp.maximum(-lam, 0.0) + jnp.log1p(jnp.exp(-jnp.abs(lam)))


def _causal_conv(x, prev8, cw_ref, cb8):
    K = cw_ref.shape[0]
    acc = _tapmul(x, cw_ref[K - 1])
    for s in range(1, K):
        acc = acc + _tapmul(_shift_down(x, s, prev8), cw_ref[K - 1 - s])
    return (_rows8(acc) + cb8[None]).reshape(x.shape)


def _bcast_row(x, row):
    return jnp.broadcast_to(x[row:row + 1, :], x.shape)


def _rows8(x):
    return x.reshape(x.shape[0] // 8, 8, x.shape[1])


def _tapmul(x, w8):
    return (_rows8(x) * w8[None]).reshape(x.shape)


def _colsum8(x):
    return jnp.sum(_rows8(x), axis=0)


def _shift_down(x, s, prev8):
    R = x.shape[0]
    return _shift_up(jnp.concatenate([prev8, x[:R - 8]], axis=0), 8 - s, x[R - 8:])


def _shift_up(x, s, next8):
    R = x.shape[0]
    r = pltpu.roll(x, R - s, 0)
    rowid = lax.broadcasted_iota(jnp.int32, (8, 1), 0)
    tail = jnp.where(rowid < 8 - s, r[R - 8:], pltpu.roll(next8, 8 - s, 0))
    return jnp.concatenate([r[:R - 8], tail], axis=0)


def _rep8(v):
    return jnp.broadcast_to(v[..., None, :], v.shape[:-1] + (8, v.shape[-1]))


def rec_fwd(z, cw, cb, wa, ba, wi, bi, lam, *, name):
    T, C2 = z.shape
    C = C2 // 2
    nblk, bd, _ = wa.shape
    tt = _tile(T, 128, HALO)
    ng = tt // 8

    def body(x_ref, y_ref, halo_ref, cw_ref, cb_ref, wa_ref, ba_ref, wi_ref, bi_ref, lam_ref,
             xb_ref, r_ref, i_ref, a_ref, h_ref, hp_ref, hg_ref, carry, u_scr):
        step = pl.program_id(0)

        @pl.when(step == 0)
        def _():
            carry[...] = jnp.zeros_like(carry)

        xb = _causal_conv(x_ref[...], jnp.where(step > 0, halo_ref[HALO - 8:, :], 0.0), cw_ref, cb_ref[...])
        xb_ref[...] = xb
        pa, pi = [], []
        for b in range(nblk):
            xs = xb[:, b * bd:(b + 1) * bd]
            pa.append(_dot(xs, wa_ref[b], NN))
            pi.append(_dot(xs, wi_ref[b], NN))
        r = jax.nn.sigmoid(jnp.concatenate(pa, axis=1) + ba_ref[...])
        ig = jax.nn.sigmoid(jnp.concatenate(pi, axis=1) + bi_ref[...])
        r_ref[...] = r
        i_ref[...] = ig
        nl = LRU_C * r * _softplus_neg(lam_ref[...])
        a_ref[...] = jnp.exp(-nl)
        th = jnp.tanh(nl)
        u_scr[...] = jnp.sqrt(2.0 * th / (1.0 + th)) * (ig * xb)

        rowid = lax.broadcasted_iota(jnp.int32, (8, C), 0)

        def group(gi, hc):
            r0 = pl.multiple_of(gi * 8, 8)
            a8 = a_ref[pl.ds(r0, 8), :]
            u8 = u_scr[pl.ds(r0, 8), :]
            for d in (1, 2, 4):
                a_sh = jnp.where(rowid >= d, pltpu.roll(a8, d, 0), 1.0)
                u_sh = jnp.where(rowid >= d, pltpu.roll(u8, d, 0), 0.0)
                u8 = a8 * u_sh + u8
                a8 = a8 * a_sh
            h8 = u8 + a8 * hc
            h_ref[pl.ds(r0, 8), :] = h8
            hp_ref[pl.ds(r0, 8), :] = jnp.where(rowid >= 1, pltpu.roll(h8, 1, 0), hc)
            return _bcast_row(h8, 7)

        carry[...] = lax.fori_loop(0, ng, group, carry[...])
        hg_ref[...] = (h_ref[...] * _gelu(y_ref[...])).astype(hg_ref.dtype)

    row = lambda c: pl.BlockSpec((tt, C), lambda i, c=c: (i, c))
    vec = pl.BlockSpec((1, C), lambda i: (0, 0))
    full = lambda shp: pl.BlockSpec(shp, lambda i, n=len(shp): (0,) * n)
    per = tt // HALO
    outs = pl.pallas_call(
        body, grid=(T // tt,),
        in_specs=[row(0), row(1), pl.BlockSpec((HALO, C), lambda i: (jnp.maximum(i * per - 1, 0), 0)),
                  full(cw.shape[:1] + (8, C)), full((8, C)), full(wa.shape), vec, full(wi.shape), vec, vec],
        out_specs=[row(0)] * 7,
        out_shape=[jax.ShapeDtypeStruct((T, C), F32)] * 6 + [jax.ShapeDtypeStruct((T, C), BF16)],
        scratch_shapes=[pltpu.VMEM((8, C), F32), pltpu.VMEM((tt, C), F32)],
        compiler_params=_cp("arbitrary"), name=name,
    )(z, z, z, _rep8(cw), _rep8(cb[0]), wa, ba, wi, bi, lam)
    return outs


def rec_bwd_scan(dhg, h, a, z, *, name):
    T, C = h.shape
    tt = _tile(T, 256, HALO)
    ng = tt // 8
    nb = T // tt

    def body(dhg_ref, h_ref, a_ref, y_ref, dy_ref, yb_ref, ycarry, acarry, g_scr):
        step = pl.program_id(0)

        @pl.when(step == 0)
        def _():
            ycarry[...] = jnp.zeros_like(ycarry)
            acarry[...] = jnp.zeros_like(acarry)

        gate, dgate = _gelu_and_grad(y_ref[...])
        dhg_v = dhg_ref[...].astype(F32)
        dy_ref[...] = (dhg_v * h_ref[...] * dgate).astype(dy_ref.dtype)
        g_scr[...] = dhg_v * gate
        rowid = lax.broadcasted_iota(jnp.int32, (8, C), 0)

        def group(j, c):
            yc, ac = c
            r0 = pl.multiple_of((ng - 1 - j) * 8, 8)
            a8 = a_ref[pl.ds(r0, 8), :]
            y8 = g_scr[pl.ds(r0, 8), :]
            b8 = jnp.where(rowid < 7, pltpu.roll(a8, 7, 0), ac)
            for d in (1, 2, 4):
                y_sh = jnp.where(rowid < 8 - d, pltpu.roll(y8, 8 - d, 0), 0.0)
                b_sh = jnp.where(rowid < 8 - d, pltpu.roll(b8, 8 - d, 0), 1.0)
                y8 = y8 + b8 * y_sh
                b8 = b8 * b_sh
            y8 = y8 + b8 * yc
            yb_ref[pl.ds(r0, 8), :] = y8
            return _bcast_row(y8, 0), _bcast_row(a8, 0)

        yc, ac = lax.fori_loop(0, ng, group, (ycarry[...], acarry[...]))
        ycarry[...] = yc
        acarry[...] = ac

    rev = lambda c: pl.BlockSpec((tt, C), lambda i, c=c: (nb - 1 - i, c))
    return pl.pallas_call(
        body, grid=(nb,), in_specs=[rev(0), rev(0), rev(0), rev(1)], out_specs=[rev(0), rev(0)],
        out_shape=[jax.ShapeDtypeStruct((T, C), BF16), jax.ShapeDtypeStruct((T, C), F32)],
        scratch_shapes=[pltpu.VMEM((8, C), F32), pltpu.VMEM((8, C), F32), pltpu.VMEM((tt, C), F32)],
        compiler_params=_cp("arbitrary"), name=name,
    )(dhg, h, a, z)


def rec_bwd_gates(ybar, hprev, a, r, ig, xb, lam, wa, wi, *, name):
    T, C = xb.shape
    nblk, bd, _ = wa.shape
    tt = _tile(T, 256, 8)
    nb = T // tt

    def body(y_ref, hp_ref, a_ref, r_ref, i_ref, xb_ref, lam_ref, wa_ref, wi_ref,
             dxb_ref, dwa_ref, dwi_ref, dba_ref, dbi_ref, dlam_ref):
        step = pl.program_id(0)

        @pl.when(step == 0)
        def _():
            for ref in (dwa_ref, dwi_ref, dba_ref, dbi_ref, dlam_ref):
                ref[...] = jnp.zeros_like(ref)

        y, av, rv, iv, xv = y_ref[...], a_ref[...], r_ref[...], i_ref[...], xb_ref[...]
        sp = _softplus_neg(lam_ref[...])
        th = jnp.tanh(LRU_C * rv * sp)
        s = jnp.sqrt(2.0 * th / (1.0 + th))
        d_nl = -(y * hp_ref[...] * av) + (y * iv * xv) * (av * av) / s
        dlam_ref[...] += jnp.sum(d_nl * rv, axis=0, keepdims=True) * LRU_C
        dr = d_nl * (LRU_C * sp)
        di = y * s * xv
        dpa = dr * rv * (1.0 - rv)
        dpi = di * iv * (1.0 - iv)
        dba_ref[...] += jnp.sum(dpa, axis=0, keepdims=True)
        dbi_ref[...] += jnp.sum(dpi, axis=0, keepdims=True)
        parts = []
        for b in range(nblk):
            sl = slice(b * bd, (b + 1) * bd)
            xs, da_b, di_b = xv[:, sl], dpa[:, sl], dpi[:, sl]
            dwa_ref[b] += _dot(xs, da_b, TN)
            dwi_ref[b] += _dot(xs, di_b, TN)
            parts.append(_dot(da_b, wa_ref[b], NT) + _dot(di_b, wi_ref[b], NT))
        dxb_ref[...] = y * s * iv + jnp.concatenate(parts, axis=1)

        @pl.when(step == nb - 1)
        def _():
            dlam_ref[...] = dlam_ref[...] * (-jax.nn.sigmoid(-lam_ref[...]))

    row = pl.BlockSpec((tt, C), lambda i: (i, 0))
    vec = pl.BlockSpec((1, C), lambda i: (0, 0))
    wsp = pl.BlockSpec(wa.shape, lambda i: (0, 0, 0))
    return pl.pallas_call(
        body, grid=(nb,), in_specs=[row] * 6 + [vec, wsp, wsp],
        out_specs=[row, wsp, wsp, vec, vec, vec],
        out_shape=[jax.ShapeDtypeStruct((T, C), F32), jax.ShapeDtypeStruct(wa.shape, F32),
                   jax.ShapeDtypeStruct(wa.shape, F32)] + [jax.ShapeDtypeStruct((1, C), F32)] * 3,
        compiler_params=_cp("arbitrary"), name=name,
    )(ybar, hprev, a, r, ig, xb, lam, wa, wi)


def conv_bwd(d, x0, cw, *, name):
    T, C = d.shape
    K = cw.shape[0]
    tt = _tile(T, 256, HALO)
    per = tt // HALO
    nb = T // tt

    def body(d_ref, dn_ref, x_ref, cw_ref, dx_ref, dcw_ref, dcb_ref, dcw_acc, dcb_acc):
        step = pl.program_id(1)

        @pl.when(step == 0)
        def _():
            dcw_acc[...] = jnp.zeros_like(dcw_acc)
            dcb_acc[...] = jnp.zeros_like(dcb_acc)

        dv = d_ref[...].astype(F32)
        next8 = jnp.where(step < nb - 1, dn_ref[...].astype(F32)[0:8], 0.0)
        xt = x_ref[...].astype(F32)
        acc = _tapmul(dv, cw_ref[K - 1])
        dcw_acc[K - 1] += _colsum8(dv * xt)
        dcb_acc[...] += _colsum8(dv)
        for sh in range(1, K):
            dsh = _shift_up(dv, sh, next8)
            acc = acc + _tapmul(dsh, cw_ref[K - 1 - sh])
            dcw_acc[K - 1 - sh] += _colsum8(dsh * xt)
        dx_ref[...] = acc.astype(dx_ref.dtype)

        @pl.when(step == nb - 1)
        def _():
            dcw_ref[...] = jnp.sum(dcw_acc[...], axis=1)
            dcb_ref[...] = jnp.sum(dcb_acc[...], axis=0, keepdims=True)

    tc = C
    row = pl.BlockSpec((tt, tc), lambda j, i: (i, j))
    return pl.pallas_call(
        body, grid=(C // tc, nb),
        in_specs=[row, pl.BlockSpec((HALO, tc), lambda j, i: (jnp.minimum((i + 1) * per, T // HALO - 1), j)),
                  row, pl.BlockSpec((K, 8, tc), lambda j, i: (0, 0, j))],
        out_specs=[row, pl.BlockSpec((K, tc), lambda j, i: (0, j)), pl.BlockSpec((1, tc), lambda j, i: (0, j))],
        out_shape=[jax.ShapeDtypeStruct((T, C), BF16), jax.ShapeDtypeStruct((K, C), F32),
                   jax.ShapeDtypeStruct((1, C), F32)],
        scratch_shapes=[pltpu.VMEM((K, 8, tc), F32), pltpu.VMEM((8, tc), F32)],
        compiler_params=_cp("parallel", "arbitrary"), name=name,
    )(d, d, x0, _rep8(cw))


def ffn_act_fwd(u0, cw, cb, *, n, name):
    T, W = u0.shape
    G = W // (2 * n)
    tt = _tile(T, 256, HALO)
    per = tt // HALO

    K = cw.shape[0]

    def body(u_ref, up_ref, cw_ref, cb_ref, a_ref, uo_ref):
        step = pl.program_id(1)
        prev8 = jnp.where(step > 0, up_ref[...].astype(F32)[HALO - 8:], 0.0)
        u = _causal_conv(u_ref[...].astype(F32), prev8, cw_ref, cb_ref[...])
        uo_ref[...] = u.astype(uo_ref.dtype)
        a_ref[...] = (_gelu(u[:, :n]) * u[:, n:]).astype(a_ref.dtype)

    return pl.pallas_call(
        body, grid=(G, T // tt),
        in_specs=[pl.BlockSpec((tt, 2 * n), lambda j, i: (i, j)),
                  pl.BlockSpec((HALO, 2 * n), lambda j, i: (jnp.maximum(i * per - 1, 0), j)),
                  pl.BlockSpec((K, 8, 2 * n), lambda j, i: (0, 0, j)),
                  pl.BlockSpec((8, 2 * n), lambda j, i: (0, j))],
        out_specs=[pl.BlockSpec((tt, n), lambda j, i: (i, j)), pl.BlockSpec((tt, 2 * n), lambda j, i: (i, j))],
        out_shape=[jax.ShapeDtypeStruct((T, G * n), BF16), jax.ShapeDtypeStruct((T, W), BF16)],
        compiler_params=_cp("parallel", "parallel"), name=name)(u0, u0, _rep8(cw), _rep8(cb[0]))


def ffn_down_act_bwd(dx, w_down, u0, u, cw, *, n, carry=None, name):
    T, W = u0.shape
    D = dx.shape[1]
    G = W // (2 * n)
    K = cw.shape[0]
    tt = _tile(T, 256, HALO)
    tc = _tile(n, 512)
    nb = T // tt
    car = carry if carry is not None else dict(arrays=[], out_structs=[], sems=[])
    ncar = len(car["arrays"])

    def body(dx_ref, w_ref, x_ref, u_ref, cw_ref, *rest):
        car_in, (du_ref, dcw_ref, dcb_ref) = rest[:ncar], rest[ncar:ncar + 3]
        car_out, (later8, dcw_acc, dcb_acc) = rest[ncar + 3:2 * ncar + 3], rest[2 * ncar + 3:2 * ncar + 6]
        sems = rest[2 * ncar + 6:]
        group, step = pl.program_id(0), pl.program_id(1)
        if carry is not None:
            @pl.when((group == 0) & (step == 0))
            def _():
                carry["start"](car_in, car_out, sems)

            @pl.when((group == G - 1) & (step == nb - 1))
            def _():
                carry["finish"](car_in, car_out, sems)

        @pl.when(step == 0)
        def _():
            later8[...] = jnp.zeros_like(later8)
            dcw_acc[...] = jnp.zeros_like(dcw_acc)
            dcb_acc[...] = jnp.zeros_like(dcb_acc)

        dact = _dot(dx_ref[...], w_ref[...], NT)
        for q in range(n // tc):
            gs, vs = slice(q * tc, (q + 1) * tc), slice(n + q * tc, n + (q + 1) * tc)
            gl, dgl = _gelu_and_grad(u_ref[:, gs].astype(F32))
            daf = dact[:, gs]
            for cols, d in ((gs, daf * u_ref[:, vs].astype(F32) * dgl), (vs, daf * gl)):
                next8 = later8[:, cols]
                xt = x_ref[:, cols].astype(F32)
                acc = _tapmul(d, cw_ref[K - 1, :, cols])
                dcw_acc[K - 1, :, cols] += _colsum8(d * xt)
                dcb_acc[:, cols] += _colsum8(d)
                for sh in range(1, K):
                    dsh = _shift_up(d, sh, next8)
                    acc = acc + _tapmul(dsh, cw_ref[K - 1 - sh, :, cols])
                    dcw_acc[K - 1 - sh, :, cols] += _colsum8(dsh * xt)
                du_ref[:, cols] = acc.astype(du_ref.dtype)
                later8[:, cols] = d[0:8]

        @pl.when(step == nb - 1)
        def _():
            dcw_ref[...] = jnp.sum(dcw_acc[...], axis=1)
            dcb_ref[...] = jnp.sum(dcb_acc[...], axis=0, keepdims=True)

    rev = lambda i: nb - 1 - i
    outs = pl.pallas_call(
        body, grid=(G, nb),
        in_specs=[pl.BlockSpec((tt, D), lambda j, i: (rev(i), 0)),
                  pl.BlockSpec((n, D), lambda j, i: (j, 0)),
                  pl.BlockSpec((tt, 2 * n), lambda j, i: (rev(i), j)),
                  pl.BlockSpec((tt, 2 * n), lambda j, i: (rev(i), j)),
                  pl.BlockSpec((K, 8, 2 * n), lambda j, i: (0, 0, j))] + [HBM] * ncar,
        out_specs=[pl.BlockSpec((tt, 2 * n), lambda j, i: (rev(i), j)),
                   pl.BlockSpec((K, 2 * n), lambda j, i: (0, j)),
                   pl.BlockSpec((1, 2 * n), lambda j, i: (0, j))] + [HBM] * ncar,
        out_shape=[jax.ShapeDtypeStruct((T, W), BF16), jax.ShapeDtypeStruct((K, W), F32),
                   jax.ShapeDtypeStruct((1, W), F32)] + list(car["out_structs"]),
        scratch_shapes=[pltpu.VMEM((8, 2 * n), F32), pltpu.VMEM((K, 8, 2 * n), F32), pltpu.VMEM((8, 2 * n), F32)]
        + list(car["sems"]),
        compiler_params=_cp(*(("parallel", "arbitrary") if carry is None else ("arbitrary", "arbitrary"))), name=name,
    )(dx, w_down, u0, u, _rep8(cw), *car["arrays"])
    return tuple(outs[:3]) if carry is None else (tuple(outs[:3]), list(outs[3:]))


def _ffn_shard_of_block(q):
    return (q % 2) * (N_DEV // 2) + q // 2


def _ffn_block_of_shard(s):
    return (s % (N_DEV // 2)) * 2 + s // (N_DEV // 2)


def _group_cols(v):
    lead = v.shape[:-1]
    n = v.shape[-1] // N_DEV
    return jnp.swapaxes(v.reshape(lead + (2, N_DEV // 2, n)), -3, -2).reshape(v.shape)


def _ungroup_cols(v):
    lead = v.shape[:-1]
    n = v.shape[-1] // N_DEV
    return jnp.swapaxes(v.reshape(lead + (N_DEV // 2, 2, n)), -3, -2).reshape(v.shape)


def local_step(x, target, w, plan=None):
    depth = w["mix_norm"].shape[0]
    n_up = w["ffn_conv_w"].shape[-1] // N_DEV
    g = {k: [None] * (w[k].shape[0] if hasattr(w[k], "shape") else len(w[k])) for k in w}

    def run(fn, stage, l, slot, *args, **kw):
        carry = None if plan is None else plan.carry(stage, l, slot, g)
        if carry is None:
            return fn(*args, **kw)
        out, extra = fn(*args, carry=carry, **kw)
        plan.done(stage, l, slot, extra)
        return out

    saved = []
    for l in range(depth):
        j = l // 2
        gain = w["mix_norm"][l:l + 1]
        if l % 2 == 0:
            qkv, h = run(mm_nn, "fwd", l, "in", x, w["attn_w_qkv"][j], out_dtype=F32, norm_gain=gain,
                         name="qkv_proj")
            ao = run(attn_fwd, "fwd", l, "attn", qkv, w["attn_q_gain"][j:j + 1], w["attn_k_gain"][j:j + 1],
                     w["attn_sinks"][j:j + 1], name="attn_fwd")
            x1 = run(mm_nn, "fwd", l, "out", ao, w["attn_w_o"][j], out_dtype=F32, res=x, name="attn_out_proj")
            mix = (qkv, ao)
        else:
            z, h = run(mm_nn, "fwd", l, "in", x, w["rec_w_in"][j], out_dtype=F32, norm_gain=gain,
                       name="rec_in_proj")
            xb, r, ig, a, hs, hprev, hg = rec_fwd(
                z, w["rec_conv_w"][j], w["rec_conv_b"][j:j + 1], w["rec_w_a"][j], w["rec_b_a"][j:j + 1],
                w["rec_w_i"][j], w["rec_b_i"][j:j + 1], w["rec_lambda"][j:j + 1], name="rec_fwd")
            x1 = run(mm_nn, "fwd", l, "out", hg, w["rec_w_out"][j], out_dtype=F32, res=x, name="rec_out_proj")
            mix = (z, xb, r, ig, a, hs, hprev, hg)
        u0, h2 = run(mm_nn, "fwd", l, "up", x1, w["ffn_w_up"][l], out_dtype=BF16,
                     norm_gain=w["ffn_norm"][l:l + 1], name="ffn_up_proj")
        act, u = ffn_act_fwd(u0, w["ffn_conv_w"][l], w["ffn_conv_b"][l:l + 1], n=n_up, name="ffn_act_fwd")
        x2 = run(mm_nn, "fwd", l, "down", act, w["ffn_w_down"][l], out_dtype=F32, res=x1, name="ffn_down_proj")
        saved.append((x, h, mix, x1, h2, u0, u, act))
        x = x2

    loss_vec, dx, dxb = loss_head(x, target, name="loss_head")

    for l in reversed(range(depth)):
        j = l // 2
        x0, h, mix, x1, h2, u0, u, act = saved[l]
        g["ffn_w_down"][l] = mm_tn(act, dxb, out_dtype=BF16, name="ffn_down_dw")
        du0, dcw, dcb = run(ffn_down_act_bwd, "bwd", l, "act_bwd", dxb, w["ffn_w_down"][l], u0, u,
                            w["ffn_conv_w"][l], n=n_up, name="ffn_down_act_bwd")
        g["ffn_conv_w"][l], g["ffn_conv_b"][l] = _ungroup_cols(dcw), _ungroup_cols(dcb)[0]
        g["ffn_w_up"][l] = run(mm_tn, "bwd", l, "up_dw", h2, du0, out_dtype=BF16, col_shards=N_DEV, shard_of_block=_ffn_shard_of_block,
                                 name="ffn_up_dw")
        dh2 = run(mm_nt, "bwd", l, "up_dx", du0, w["ffn_w_up"][l], out_dtype=BF16, name="ffn_up_dx")
        dx1, dx1b, dgf = rmsnorm_bwd(x1, w["ffn_norm"][l:l + 1], dh2, dx, name="ffn_norm_bwd")
        g["ffn_norm"][l] = dgf[0]
        if l % 2 == 0:
            qkv, ao = mix
            dao = mm_nt(dx1b, w["attn_w_o"][j], out_dtype=BF16, name="attn_out_dx")
            g["attn_w_o"][j] = mm_tn(ao, dx1b, out_dtype=BF16, name="attn_out_dw")
            dq, dk, dv, dqg, dkg, dsk = attn_bwd(qkv, dao, w["attn_q_gain"][j:j + 1], w["attn_k_gain"][j:j + 1],
                                                 w["attn_sinks"][j:j + 1], name="attn_bwd")
            g["attn_q_gain"][j], g["attn_k_gain"][j], g["attn_sinks"][j] = dqg[0], dkg[0], dsk
            dqkv = jnp.concatenate([dq, dk, dv], axis=1)
            g["attn_w_qkv"][j] = run(mm_tn, "bwd", l, "in_dw", h, dqkv, out_dtype=BF16, col_shards=N_DEV, name="qkv_dw")
            dh = run(mm_nt, "bwd", l, "in_dx", dqkv, w["attn_w_qkv"][j], out_dtype=BF16, name="qkv_dx")
        else:
            z, xb, r, ig, a, hs, hprev, hg = mix
            dhg = mm_nt(dx1b, w["rec_w_out"][j], out_dtype=BF16, name="rec_out_dx")
            g["rec_w_out"][j] = mm_tn(hg, dx1b, out_dtype=BF16, name="rec_out_dw")
            dyb, ybar = rec_bwd_scan(dhg, hs, a, z, name="rec_bwd_scan")
            dxb, dwa, dwi, dba, dbi, dlam = rec_bwd_gates(
                ybar, hprev, a, r, ig, xb, w["rec_lambda"][j:j + 1], w["rec_w_a"][j], w["rec_w_i"][j],
                name="rec_bwd_gates")
            dxb0, dcw, dcb = conv_bwd(dxb, z, w["rec_conv_w"][j], name="rec_conv_bwd")
            g["rec_w_a"][j], g["rec_w_i"][j] = dwa, dwi
            g["rec_b_a"][j], g["rec_b_i"][j], g["rec_lambda"][j] = dba[0], dbi[0], dlam[0]
            g["rec_conv_w"][j], g["rec_conv_b"][j] = dcw, dcb[0]
            dz = jnp.concatenate([dxb0, dyb], axis=1)
            g["rec_w_in"][j] = run(mm_tn, "bwd", l, "in_dw", h, dz, out_dtype=BF16, col_shards=N_DEV, name="rec_in_dw")
            dh = run(mm_nt, "bwd", l, "in_dx", dz, w["rec_w_in"][j], out_dtype=BF16, name="rec_in_dx")
        dx, dxb, dgm = rmsnorm_bwd(x0, w["mix_norm"][l:l + 1], dh, dx1, name="mix_norm_bwd")
        g["mix_norm"][l] = dgm[0]
    return loss_vec, dx, g


HBM = pl.BlockSpec(memory_space=pltpu.HBM)
N_PEER = N_DEV - 1


def _here():
    return lax.axis_index("x"), lax.axis_index("y"), lax.axis_index("c")


def _sid(dev):
    return 4 * dev[0] + 2 * dev[1] + dev[2]


def _exchange_sems(n):
    return [pltpu.SemaphoreType.DMA((n * N_PEER,)), pltpu.SemaphoreType.DMA((n * N_PEER,)),
            pltpu.SemaphoreType.DMA((n,))]


def gather_exchange(shards, out_structs, windows):
    n = len(shards)

    def parts(outs, sems):
        send_sems, recv_sems, _ = sems
        x, y, c = _here()
        me, sib = (x, y, c), (x, y, 1 - c)
        chips = [(1 - x, y), (x, 1 - y), (1 - x, 1 - y)]

        def copy(i, k, block, to, src=None):
            dst = windows[i](outs[i], _sid(block))
            return pltpu.make_async_remote_copy(
                src_ref=dst if src is None else src, dst_ref=dst,
                send_sem=send_sems.at[i * N_PEER + k], recv_sem=recv_sems.at[i * N_PEER + k],
                device_id=to, device_id_type=MESH)

        return me, sib, chips, c, copy

    def own_copies(ins, outs, sems):
        me, sib, chips, c, copy = parts(outs, sems)
        local = [pltpu.make_async_copy(ins[i], windows[i](outs[i], _sid(me)), sems[2].at[i]) for i in range(n)]
        first = []
        for i in range(n):
            first.append(copy(i, 0, me, sib, src=ins[i]))
            first += [copy(i, 1 + j, me, (*chip, c), src=ins[i]) for j, chip in enumerate(chips)]
        return local, first

    def start(ins, outs, sems):
        local, first = own_copies(ins, outs, sems)
        for cp in local + first:
            cp.start()

    def finish(ins, outs, sems):
        me, sib, chips, c, copy = parts(outs, sems)
        local, first = own_copies(ins, outs, sems)
        passed = []
        for i in range(n):
            for j, chip in enumerate(chips):
                copy(i, 1 + j, (*chip, c), me).wait_recv()
                fwd = copy(i, 4 + j, (*chip, c), sib)
                fwd.start()
                passed.append(fwd)
        for i in range(n):
            copy(i, 0, sib, me).wait_recv()
            for j, chip in enumerate(chips):
                copy(i, 4 + j, (*chip, 1 - c), me).wait_recv()
        for cp in first + passed:
            cp.wait_send()
        for cp in local:
            cp.wait()

    return dict(arrays=list(shards), out_structs=list(out_structs), sems=_exchange_sems(n), start=start,
                finish=finish)


def run_exchange(ex, *, name):
    n = len(ex["arrays"])

    def body(*refs):
        ins, outs, sems = refs[:n], refs[n:2 * n], refs[2 * n:]
        ex["start"](ins, outs, sems)
        ex["finish"](ins, outs, sems)

    return pl.pallas_call(
        body, in_specs=[HBM] * n, out_specs=[HBM] * n, out_shape=ex["out_structs"], scratch_shapes=ex["sems"],
        name=name)(*ex["arrays"])


def scatter_exchange(grads):
    n = len(grads)

    def parts(ins, outs, sems):
        send_sems, recv_sems, local_sems = sems
        x, y, c = _here()
        me = (x, y, c)
        peers = []
        for k in range(1, N_DEV):
            kx, ky, kc = (k >> 2) & 1, (k >> 1) & 1, k & 1
            peers.append((1 - x if kx else x, 1 - y if ky else y, 1 - c if kc else c))

        def copy(i, k):
            return pltpu.make_async_remote_copy(
                src_ref=ins[i].at[_sid(peers[k])], dst_ref=outs[i].at[_sid(me)],
                send_sem=send_sems.at[i * N_PEER + k], recv_sem=recv_sems.at[i * N_PEER + k],
                device_id=peers[k], device_id_type=MESH)

        def arrival(i, k):
            return pltpu.make_async_remote_copy(
                src_ref=ins[i].at[_sid(me)], dst_ref=outs[i].at[_sid(peers[k])],
                send_sem=send_sems.at[i * N_PEER + k], recv_sem=recv_sems.at[i * N_PEER + k],
                device_id=peers[k], device_id_type=MESH)

        local = [pltpu.make_async_copy(ins[i].at[_sid(me)], outs[i].at[_sid(me)], local_sems.at[i]) for i in range(n)]
        sends = [copy(i, k) for i in range(n) for k in range(N_PEER)]
        return local, sends, arrival

    def start(ins, outs, sems):
        local, sends, _ = parts(ins, outs, sems)
        for cp in local + sends:
            cp.start()

    def finish(ins, outs, sems):
        local, sends, arrival = parts(ins, outs, sems)
        for i in range(n):
            for k in range(N_PEER):
                arrival(i, k).wait_recv()
        for cp in sends:
            cp.wait_send()
        for cp in local:
            cp.wait()

    return dict(arrays=list(grads), out_structs=[jax.ShapeDtypeStruct(g.shape, g.dtype) for g in grads],
                sems=_exchange_sems(n), start=start, finish=finish)


def adamw_family(contribs, w, m, v, *, name):
    L, R, C = w.shape
    S = contribs[0].shape[0]
    tr = _tile(R, max(8, (1 << 20) // (C * S)), 8)
    nr = R // tr
    c1 = 1.0 / (1.0 - ADAM_B1 ** ADAM_STEP)
    c2 = 1.0 / (1.0 - ADAM_B2 ** ADAM_STEP)

    def body(*refs):
        c_refs = refs[:L]
        w_ref, m_ref, v_ref, g_ref, d_ref, nm_ref, nv_ref = refs[L:]
        layer = pl.program_id(0)
        for l in range(L):
            @pl.when(layer == l)
            def _(l=l):
                g = c_refs[l][0].astype(F32)
                for s in range(1, S):
                    g = g + c_refs[l][s].astype(F32)
                mm = ADAM_B1 * m_ref[...] + (1.0 - ADAM_B1) * g
                vv = ADAM_B2 * v_ref[...] + (1.0 - ADAM_B2) * (g * g)
                g_ref[...] = g
                nm_ref[...] = mm
                nv_ref[...] = vv
                d_ref[...] = -ADAM_LR * ((mm * c1) / (jnp.sqrt(vv * c2) + ADAM_EPS) + ADAM_WD * w_ref[...])

    def cspec(l):
        return pl.BlockSpec((S, tr, C), lambda ll, i, l=l: (0, jnp.where(ll == l, i, 0), 0))

    lay = pl.BlockSpec((None, tr, C), lambda ll, i: (ll, i, 0))
    return pl.pallas_call(
        body, grid=(L, nr), in_specs=[cspec(l) for l in range(L)] + [lay] * 3, out_specs=[lay] * 4,
        out_shape=[jax.ShapeDtypeStruct((L, R, C), F32)] * 4,
        compiler_params=_cp("arbitrary", "arbitrary"), name=name)(*contribs, w, m, v)


def sum_slots(a, *, name):
    S, R, C = a.shape
    tr = _tile(R, 256, 8)

    def body(a_ref, o_ref):
        t = a_ref[0]
        for s in range(1, S):
            t = t + a_ref[s]
        o_ref[...] = t

    return pl.pallas_call(
        body, grid=(R // tr,), in_specs=[pl.BlockSpec((S, tr, C), lambda i: (0, i, 0))],
        out_specs=pl.BlockSpec((tr, C), lambda i: (i, 0)), out_shape=jax.ShapeDtypeStruct((R, C), F32),
        compiler_params=_cp("parallel"), name=name)(a)


LANES = 128


def _pack(arrs):
    flat = jnp.concatenate([a.reshape(-1).astype(F32) for a in arrs])
    rows = -(-flat.shape[0] // LANES)
    rows = -(-rows // 256) * 256
    return jnp.pad(flat, (0, rows * LANES - flat.shape[0])).reshape(rows, LANES)


def _unpack(buf, shapes):
    flat = buf.reshape(-1)
    out, off = [], 0
    for shp in shapes:
        size = int(np.prod(shp))
        out.append(flat[off:off + size].reshape(shp))
        off += size
    return out


def _gather_last(g):
    t = jnp.moveaxis(g, 0, -2)
    return t.reshape(t.shape[:-2] + (t.shape[-2] * t.shape[-1],))


def _own_last(full, s):
    n = full.shape[-1] // N_DEV
    t = full.reshape(full.shape[:-1] + (N_DEV, n))
    return lax.dynamic_index_in_dim(t, s, axis=t.ndim - 2, keepdims=False)


BIG = ["attn_w_qkv", "attn_w_o", "rec_w_in", "rec_w_out", "ffn_w_up", "ffn_w_down", "rec_w_a", "rec_w_i"]
SMALL_REPLICATED = ["mix_norm", "ffn_norm", "attn_q_gain", "attn_k_gain", "attn_sinks", "ffn_conv_b"]
SMALL_SHARDED = ["rec_conv_w", "rec_conv_b", "rec_b_a", "rec_b_i", "rec_lambda", "ffn_conv_w"]
SMALL = SMALL_REPLICATED + SMALL_SHARDED
WEIGHTS = ["mix_norm", "ffn_norm", "attn_w_qkv", "attn_q_gain", "attn_k_gain", "attn_sinks", "attn_w_o", "rec_w_in",
           "rec_conv_w", "rec_conv_b", "rec_w_a", "rec_b_a", "rec_w_i", "rec_b_i", "rec_lambda", "rec_w_out",
           "ffn_w_up", "ffn_conv_w", "ffn_conv_b", "ffn_w_down"]


def _col_window(n, block_of_shard=None):
    def win(ref, s):
        q = s if block_of_shard is None else block_of_shard(s)
        return ref.at[:, pl.ds(pl.multiple_of(q * n, 128), n)]
    return win


def _row_window(r):
    return lambda ref, s: ref.at[pl.ds(pl.multiple_of(s * r, 16), r), :]


def _gate_window(r):
    return lambda ref, s: ref.at[:, pl.ds(pl.multiple_of(s * r, 16), r), :]


def _slot_window(ref, s):
    return ref.at[s]


def _idx(name, layer):
    return layer if name.startswith("ffn") else layer // 2


def _slot_names(layer, slot):
    attn = layer % 2 == 0
    return {"in": ["attn_w_qkv"] if attn else ["rec_w_in"],
            "out": ["attn_w_o"] if attn else ["rec_w_out", "rec_w_a", "rec_w_i"],
            "up": ["ffn_w_up"], "down": ["ffn_w_down"]}[slot]


def _gather_for(p, items):
    shards, structs, wins = [], [], []
    for nme, layer in items:
        sh = p[nme][_idx(nme, layer)].astype(BF16)
        if nme in ("attn_w_qkv", "rec_w_in", "ffn_w_up"):
            K, n = sh.shape
            structs.append(jax.ShapeDtypeStruct((K, n * N_DEV), BF16))
            wins.append(_col_window(n, _ffn_block_of_shard if nme == "ffn_w_up" else None))
        elif nme in ("rec_w_a", "rec_w_i"):
            nblk, r, bd = sh.shape
            structs.append(jax.ShapeDtypeStruct((nblk, r * N_DEV, bd), BF16))
            wins.append(_gate_window(r))
        else:
            r, N = sh.shape
            structs.append(jax.ShapeDtypeStruct((r * N_DEV, N), BF16))
            wins.append(_row_window(r))
        shards.append(sh)
    return gather_exchange(shards, structs, wins)


def _send_layout(name, t):
    if name in ("rec_w_a", "rec_w_i"):
        nblk, bd, _ = t.shape
        return jnp.transpose(t.reshape(nblk, N_DEV, bd // N_DEV, bd), (1, 0, 2, 3)).astype(BF16)
    if name in ("attn_w_o", "rec_w_out", "ffn_w_down"):
        return t.reshape((N_DEV, t.shape[0] // N_DEV) + t.shape[1:])
    return t


class _Plan:
    BWD_SLOT = {"up_dw": "down", "in_dw": "out", "in_dx": "in"}

    def __init__(self, p, w, contribs, depth):
        self.p, self.w, self.contribs, self.depth = p, w, contribs, depth
        self.pending = None

    def carry(self, stage, l, slot, g):
        if stage == "fwd":
            items = []
            if l == 0 and slot == "in":
                items += [(k, 0) for k in _slot_names(0, "out")]
            if l == 0 and slot == "attn":
                items += [(k, 0) for s in ("up", "down") for k in _slot_names(0, s)]
            if slot != "attn" and l + 1 < self.depth:
                items += [(k, l + 1) for k in _slot_names(l + 1, slot)]
            if not items:
                return None
            self.pending = items
            return _gather_for(self.p, items)
        if slot == "act_bwd":
            items = [("ffn_w_up", l + 1)] if l + 1 < self.depth else []
        elif slot == "up_dx":
            items = [("ffn_w_up", 0)] if l == 0 else []
        else:
            items = [(k, l) for k in _slot_names(l, self.BWD_SLOT[slot])]
        if not items:
            return None
        self.pending = items
        return scatter_exchange([_send_layout(k, g[k][_idx(k, layer)]) for k, layer in items])

    def done(self, stage, l, slot, outs):
        dst = self.w if stage == "fwd" else self.contribs
        for (k, layer), o in zip(self.pending, outs):
            dst[k][_idx(k, layer)] = o


def _train_step(p, x, target, mom, vel):
    depth = p["mix_norm"].shape[0]
    s_me = _sid(_here())

    w = {k: [None] * p[k].shape[0] for k in BIG}
    first = [(k, 0) for k in _slot_names(0, "in")]
    for (k, _), t in zip(first, run_exchange(_gather_for(p, first), name="all_gather_first")):
        w[k][0] = t

    local_small = [p[k] for k in SMALL_SHARDED]
    packed = _pack(local_small)
    gathered, = run_exchange(
        gather_exchange([packed], [jax.ShapeDtypeStruct((N_DEV,) + packed.shape, F32)], [_slot_window]),
        name="all_gather_small")
    per_dev = [_unpack(gathered[s], [a.shape for a in local_small]) for s in range(N_DEV)]
    for i, k in enumerate(SMALL_SHARDED):
        w[k] = _gather_last(jnp.stack([per_dev[s][i] for s in range(N_DEV)]))
    for k in SMALL_REPLICATED:
        w[k] = p[k]
    nrec = w["rec_b_a"].shape[0]
    w["rec_b_a"] = w["rec_b_a"].reshape(nrec, -1)
    w["rec_b_i"] = w["rec_b_i"].reshape(nrec, -1)
    w["ffn_conv_w"] = _group_cols(w["ffn_conv_w"])
    w["ffn_conv_b"] = _group_cols(w["ffn_conv_b"])

    contribs = {k: [None] * len(w[k]) for k in BIG}
    loss_vec, dx, g = local_step(x[0], target[0], w, _Plan(p, w, contribs, depth))
    loss = lax.psum(jnp.sum(loss_vec), ("x", "y", "c"))

    out = {}
    for k in BIG:
        shp = p[k].shape
        L = shp[0]
        C = shp[-1]
        R = int(np.prod(shp[1:-1]))
        cs = [c.reshape(N_DEV, R, C) for c in contribs[k]]
        res = adamw_family(cs, p[k].reshape(L, R, C), mom[k].reshape(L, R, C), vel[k].reshape(L, R, C),
                           name="adamw_" + k)
        out[k] = [t.reshape(shp) for t in res]

    gsmall = [jnp.stack(g[k]) for k in SMALL]
    gp = _pack(gsmall)
    gall, = run_exchange(
        gather_exchange([gp], [jax.ShapeDtypeStruct((N_DEV,) + gp.shape, F32)], [_slot_window]),
        name="all_gather_small_grads")
    gsum = _unpack(sum_slots(gall, name="sum_small_grads"), [a.shape for a in gsmall])
    glocal = []
    for k, t in zip(SMALL, gsum):
        if k in SMALL_SHARDED:
            t = _own_last(t.reshape(p[k].shape[:-1] + (p[k].shape[-1] * N_DEV,)), s_me)
        glocal.append(t.reshape(p[k].shape))
    wp, mp, vp, gpk = (_pack([d[k] for k in SMALL]) for d in (p, mom, vel, dict(zip(SMALL, glocal))))
    res = adamw_family([gpk[None]], wp[None], mp[None], vp[None], name="adamw_small")
    shapes = [p[k].shape for k in SMALL]
    unp = [_unpack(t[0], shapes) for t in res]
    for i, k in enumerate(SMALL):
        out[k] = [glocal[i], unp[1][i], unp[2][i], unp[3][i]]

    return (loss, dx[None]) + tuple(out[k][q] for q in range(4) for k in WEIGHTS)


def kernel(x, mix_norm, ffn_norm, attn_w_qkv, attn_q_gain, attn_k_gain, attn_sinks, attn_w_o, rec_w_in, rec_conv_w, rec_conv_b, rec_w_a, rec_b_a, rec_w_i, rec_b_i, rec_lambda, rec_w_out, ffn_w_up, ffn_conv_w, ffn_conv_b, ffn_w_down, loss_target, m_mix_norm, m_ffn_norm, m_attn_w_qkv, m_attn_q_gain, m_attn_k_gain, m_attn_sinks, m_attn_w_o, m_rec_w_in, m_rec_conv_w, m_rec_conv_b, m_rec_w_a, m_rec_b_a, m_rec_w_i, m_rec_b_i, m_rec_lambda, m_rec_w_out, m_ffn_w_up, m_ffn_conv_w, m_ffn_conv_b, m_ffn_w_down, v_mix_norm, v_ffn_norm, v_attn_w_qkv, v_attn_q_gain, v_attn_k_gain, v_attn_sinks, v_attn_w_o, v_rec_w_in, v_rec_conv_w, v_rec_conv_b, v_rec_w_a, v_rec_b_a, v_rec_w_i, v_rec_b_i, v_rec_lambda, v_rec_w_out, v_ffn_w_up, v_ffn_conv_w, v_ffn_conv_b, v_ffn_w_down):
    p = dict(zip(WEIGHTS, (mix_norm, ffn_norm, attn_w_qkv, attn_q_gain, attn_k_gain, attn_sinks, attn_w_o, rec_w_in,
                           rec_conv_w, rec_conv_b, rec_w_a, rec_b_a, rec_w_i, rec_b_i, rec_lambda, rec_w_out,
                           ffn_w_up, ffn_conv_w, ffn_conv_b, ffn_w_down)))
    mom = dict(zip(WEIGHTS, (m_mix_norm, m_ffn_norm, m_attn_w_qkv, m_attn_q_gain, m_attn_k_gain, m_attn_sinks,
                             m_attn_w_o, m_rec_w_in, m_rec_conv_w, m_rec_conv_b, m_rec_w_a, m_rec_b_a, m_rec_w_i,
                             m_rec_b_i, m_rec_lambda, m_rec_w_out, m_ffn_w_up, m_ffn_conv_w, m_ffn_conv_b,
                             m_ffn_w_down)))
    vel = dict(zip(WEIGHTS, (v_mix_norm, v_ffn_norm, v_attn_w_qkv, v_attn_q_gain, v_attn_k_gain, v_attn_sinks,
                             v_attn_w_o, v_rec_w_in, v_rec_conv_w, v_rec_conv_b, v_rec_w_a, v_rec_b_a, v_rec_w_i,
                             v_rec_b_i, v_rec_lambda, v_rec_w_out, v_ffn_w_up, v_ffn_conv_w, v_ffn_conv_b,
                             v_ffn_w_down)))
    return _train_step(p, x, loss_target, mom, vel)
```

```python
import math

import jax
import jax.numpy as jnp
import numpy as np
from jax import lax
from jax.experimental import pallas as pl
from jax.experimental.pallas import tpu as pltpu

F32 = jnp.float32
BF16 = jnp.bfloat16

N_DEV = 8
HEAD_DIM = 64
GROUP = 4
BLOCK = 128
LRU_C = 8.0
EPS = 1e-6
HALO = 16
ADAM_LR, ADAM_B1, ADAM_B2, ADAM_EPS, ADAM_WD, ADAM_STEP = 0.001, 0.9, 0.999, 1e-08, 0.01, 10
VMEM_LIMIT = 56 * 1024 * 1024
MESH = pl.DeviceIdType.MESH
GELU_C = math.sqrt(2.0 / math.pi)


def _cp(*sem, vmem=VMEM_LIMIT):
    return pltpu.CompilerParams(dimension_semantics=tuple(sem), vmem_limit_bytes=vmem)


def _tile(dim, pref, mult=128):
    if dim <= pref:
        return dim
    t = (pref // mult) * mult
    while t >= mult:
        if dim % t == 0:
            return t
        t -= mult
    return dim


def _gelu(x):
    th = jnp.tanh(GELU_C * (x + 0.044715 * x * x * x))
    return 0.5 * x * (1.0 + th)


def _gelu_and_grad(x):
    x2 = x * x
    th = jnp.tanh(x * (GELU_C + (GELU_C * 0.044715) * x2))
    a = 0.5 + 0.5 * th
    g = x * a
    dg = a + g * (1.0 - th) * (GELU_C + (3.0 * GELU_C * 0.044715) * x2)
    return g, dg


def _dot(a, b, dims):
    return lax.dot_general(a.astype(BF16), b.astype(BF16), (dims, ((), ())), preferred_element_type=F32)


NN = ((1,), (0,))
NT = ((1,), (1,))
TN = ((0,), (0,))


def _matmul(a, b, *, dims, grid, a_spec, b_spec, o_spec, out_shape, acc_shape, res=None, res_spec=None,
            norm_gain=None, carry=None, name):
    ni, nj, nk = grid
    nres = 0 if res is None else 1
    nnorm = 0 if norm_gain is None else 1
    ncar = 0 if carry is None else len(carry["arrays"])
    nin = 2 + nres + nnorm

    def body(*refs):
        a_ref, b_ref = refs[0], refs[1]
        r_ref = refs[2] if nres else None
        car_in = refs[nin:nin + ncar]
        o_ref = refs[nin + ncar]
        car_out = refs[nin + ncar + 1 + nnorm:nin + 2 * ncar + 1 + nnorm]
        scratch = refs[nin + 2 * ncar + 1 + nnorm:]
        sems = scratch[1 + nnorm:]
        i, j, k = pl.program_id(0), pl.program_id(1), pl.program_id(2)

        if carry is not None:
            @pl.when((i == 0) & (j == 0) & (k == 0))
            def _():
                carry["start"](car_in, car_out, sems)

        def finish(acc):
            if r_ref is not None:
                acc = acc + r_ref[...].astype(F32)
            o_ref[...] = acc.astype(o_ref.dtype)

        if nnorm:
            g_ref, h_ref, h_scr = refs[2 + nres], refs[nin + ncar + 1], scratch[1]

            @pl.when(j == 0)
            def _():
                xv = a_ref[...]
                r = lax.rsqrt(jnp.mean(xv * xv, axis=-1, keepdims=True) + EPS)
                hv = (xv * r * g_ref[...]).astype(h_scr.dtype)
                h_scr[...] = hv
                h_ref[...] = hv

            finish(_dot(h_scr[...], b_ref[...], dims))
        elif nk == 1:
            finish(_dot(a_ref[...], b_ref[...], dims))
        else:
            acc_ref = scratch[0]

            @pl.when(k == 0)
            def _():
                acc_ref[...] = _dot(a_ref[...], b_ref[...], dims)

            if nk > 2:
                @pl.when((k > 0) & (k < nk - 1))
                def _():
                    acc_ref[...] += _dot(a_ref[...], b_ref[...], dims)

            @pl.when(k == nk - 1)
            def _():
                finish(acc_ref[...] + _dot(a_ref[...], b_ref[...], dims))

        if carry is not None:
            @pl.when((i == ni - 1) & (j == nj - 1) & (k == nk - 1))
            def _():
                carry["finish"](car_in, car_out, sems)

    in_specs = [a_spec, b_spec] + ([res_spec] if nres else [])
    args = (a, b) + ((res,) if nres else ())
    out_specs, out_shapes = [o_spec], [out_shape]
    scratch_shapes = [pltpu.VMEM(acc_shape if nk > 1 else (8, 128), F32)]
    sem = ("parallel", "parallel", "arbitrary")
    if nnorm:
        assert nk == 1, "the normalised operand needs whole rows in one block"
        rows, width = a_spec.block_shape
        in_specs.append(pl.BlockSpec((1, width), lambda i, j, k: (0, 0)))
        args += (norm_gain,)
        out_specs.append(pl.BlockSpec((rows, width), lambda i, j, k: (i, 0)))
        out_shapes.append(jax.ShapeDtypeStruct(a.shape, BF16))
        scratch_shapes.append(pltpu.VMEM((rows, width), BF16))
        sem = ("parallel", "arbitrary", "arbitrary")
    if carry is not None:
        in_specs += [HBM] * ncar
        args += tuple(carry["arrays"])
        out_specs += [HBM] * ncar
        out_shapes += list(carry["out_structs"])
        scratch_shapes += carry["sems"]
        sem = ("arbitrary", "arbitrary", "arbitrary")
    outs = pl.pallas_call(
        body, grid=grid, in_specs=in_specs, out_specs=out_specs, out_shape=out_shapes,
        scratch_shapes=scratch_shapes, compiler_params=_cp(*sem), name=name,
    )(*args)
    main = outs[0] if not nnorm else (outs[0], outs[1])
    return main if carry is None else (main, list(outs[1 + nnorm:]))


def mm_nn(a, b, *, out_dtype, res=None, norm_gain=None, carry=None, name):
    M, K = a.shape
    N = b.shape[1]
    wide = res is None and out_dtype == BF16 and norm_gain is None
    tm, tn, tk = _tile(M, 1024), _tile(N, 2048 if wide else 1024), _tile(K, 2048)
    return _matmul(
        a, b, dims=NN, grid=(M // tm, N // tn, K // tk),
        a_spec=pl.BlockSpec((tm, tk), lambda i, j, k: (i, k)),
        b_spec=pl.BlockSpec((tk, tn), lambda i, j, k: (k, j)),
        o_spec=pl.BlockSpec((tm, tn), lambda i, j, k: (i, j)),
        out_shape=jax.ShapeDtypeStruct((M, N), out_dtype), acc_shape=(tm, tn),
        res=res, res_spec=pl.BlockSpec((tm, tn), lambda i, j, k: (i, j)), norm_gain=norm_gain, carry=carry, name=name)


def mm_nt(a, b, *, out_dtype, res=None, carry=None, name):
    M, N = a.shape
    K = b.shape[0]
    wide = res is None and out_dtype == BF16 and a.dtype == BF16
    tm, tn, tk = _tile(M, 1024), _tile(K, 2048 if wide else 1024), _tile(N, 2048)
    return _matmul(
        a, b, dims=NT, grid=(M // tm, K // tn, N // tk),
        a_spec=pl.BlockSpec((tm, tk), lambda i, j, k: (i, k)),
        b_spec=pl.BlockSpec((tn, tk), lambda i, j, k: (j, k)),
        o_spec=pl.BlockSpec((tm, tn), lambda i, j, k: (i, j)),
        out_shape=jax.ShapeDtypeStruct((M, K), out_dtype), acc_shape=(tm, tn),
        res=res, res_spec=pl.BlockSpec((tm, tn), lambda i, j, k: (i, j)), carry=carry, name=name)


def mm_tn(a, b, *, out_dtype, col_shards=None, shard_of_block=None, carry=None, name):
    T, K = a.shape
    N = b.shape[1]
    tt = _tile(T, 2048)
    tm = _tile(K, 1024)
    if col_shards is None:
        tn = _tile(N, 1024)
        o_spec = pl.BlockSpec((tm, tn), lambda i, j, k: (i, j))
        out_shape = jax.ShapeDtypeStruct((K, N), out_dtype)
    else:
        n = N // col_shards
        tn = _tile(n, 1536)
        per = n // tn
        sob = shard_of_block if shard_of_block is not None else (lambda s: s)
        o_spec = pl.BlockSpec((None, tm, tn), lambda i, j, k: (sob(j // per), i, j % per))
        out_shape = jax.ShapeDtypeStruct((col_shards, K, n), out_dtype)
    return _matmul(
        a, b, dims=TN, grid=(K // tm, N // tn, T // tt),
        a_spec=pl.BlockSpec((tt, tm), lambda i, j, k: (k, i)),
        b_spec=pl.BlockSpec((tt, tn), lambda i, j, k: (k, j)),
        o_spec=o_spec, out_shape=out_shape, acc_shape=(tm, tn), carry=carry, name=name)


def rmsnorm_bwd(x, g, dh, dres, *, name):
    T, D = x.shape
    tm = _tile(T, 512, 16)

    def body(x_ref, g_ref, dh_ref, dres_ref, dx_ref, dxb_ref, dg_ref):
        @pl.when(pl.program_id(0) == 0)
        def _():
            dg_ref[...] = jnp.zeros_like(dg_ref)

        xv = x_ref[...]
        dh_v = dh_ref[...].astype(F32)
        r = lax.rsqrt(jnp.mean(xv * xv, axis=-1, keepdims=True) + EPS)
        u = dh_v * g_ref[...]
        dot = jnp.mean(u * xv, axis=-1, keepdims=True)
        dx = dres_ref[...] + r * u - xv * (r * r * r * dot)
        dx_ref[...] = dx
        dxb_ref[...] = dx.astype(dxb_ref.dtype)
        dg_ref[...] += jnp.sum(dh_v * xv * r, axis=0, keepdims=True)

    row = pl.BlockSpec((tm, D), lambda i: (i, 0))
    vec = pl.BlockSpec((1, D), lambda i: (0, 0))
    return pl.pallas_call(
        body, grid=(T // tm,), in_specs=[row, vec, row, row], out_specs=[row, row, vec],
        out_shape=[jax.ShapeDtypeStruct((T, D), F32), jax.ShapeDtypeStruct((T, D), BF16),
                   jax.ShapeDtypeStruct((1, D), F32)],
        compiler_params=_cp("arbitrary"), name=name)(x, g, dh, dres)


def loss_head(y, target, *, name):
    T, D = y.shape
    tm = _tile(T, 512, 16)

    def body(y_ref, t_ref, l_ref, dy_ref, dyb_ref):
        @pl.when(pl.program_id(0) == 0)
        def _():
            l_ref[...] = jnp.zeros_like(l_ref)

        e = y_ref[...] - t_ref[...]
        dy = e * (1.0 / D)
        dy_ref[...] = dy
        dyb_ref[...] = dy.astype(dyb_ref.dtype)
        l_ref[...] += jnp.sum(e * e, axis=0, keepdims=True) * (0.5 / D)

    row = pl.BlockSpec((tm, D), lambda i: (i, 0))
    vec = pl.BlockSpec((1, D), lambda i: (0, 0))
    return pl.pallas_call(
        body, grid=(T // tm,), in_specs=[row, row], out_specs=[vec, row, row],
        out_shape=[jax.ShapeDtypeStruct((1, D), F32), jax.ShapeDtypeStruct((T, D), F32),
                   jax.ShapeDtypeStruct((T, D), BF16)],
        compiler_params=_cp("arbitrary"), name=name)(y, target)


NEG = -1e30


def _kv_heads_per_step(hkv):
    return 4 if hkv % 4 == 0 else 2


def _query_blocks_per_step(nblocks, want):
    while nblocks % want:
        want //= 2
    return want


LANE = 128


def _hi_lo_dot(x, w):
    hi = x.astype(BF16)
    lo = x - hi.astype(F32)
    return _dot(hi, w, NN) + _dot(lo, w, NN)


def _head_sum_matrix():
    r = lax.broadcasted_iota(jnp.int32, (LANE, LANE), 0) // HEAD_DIM
    c = lax.broadcasted_iota(jnp.int32, (LANE, LANE), 1) // HEAD_DIM
    return jnp.where(r == c, 1.0, 0.0).astype(BF16)


def _chunk(x, c):
    return x[:, c * LANE:(c + 1) * LANE]


def _head_sums(x, e):
    return jnp.concatenate([_hi_lo_dot(_chunk(x, c), e) for c in range(x.shape[1] // LANE)], axis=1)


def _row_sums(x):
    return _hi_lo_dot(x, jnp.ones((x.shape[1], LANE), BF16))


def _headnorm(x, e):
    r = lax.rsqrt(_head_sums(x * x, e) * (1.0 / HEAD_DIM) + EPS)
    return x * r, r


def _attn_mask(n):
    qi = lax.broadcasted_iota(jnp.int32, (BLOCK, 2 * BLOCK), 0)
    kj = lax.broadcasted_iota(jnp.int32, (BLOCK, 2 * BLOCK), 1)
    rel = qi + BLOCK - kj
    m = (rel >= 0) & (rel < BLOCK) & ((kj >= BLOCK) | (n > 0))
    return jnp.concatenate([m] * GROUP, axis=0)


def _qk_operands(qhat, khat, j, lo_half):
    kb = jnp.where(lo_half == (j % 2 == 0), _chunk(khat, j // 2), 0.0)
    kd = kb + pltpu.roll(kb, HEAD_DIM, 1)
    q4 = jnp.concatenate(
        [jnp.where(lo_half == (g % 2 == 0), _chunk(qhat, 2 * j + g // 2), 0.0) for g in range(GROUP)], axis=0)
    return q4, kd


def _attn_softmax(s, sink_ref, mask4, head0):
    s = s * (HEAD_DIM ** -0.5)
    sink = jnp.concatenate([jnp.full((BLOCK, 1), sink_ref[0, head0 + g], F32) for g in range(GROUP)], axis=0)
    m = jnp.maximum(jnp.max(jnp.where(mask4, s, NEG), axis=-1, keepdims=True), sink)
    ex = jnp.where(mask4, jnp.exp(s - m), 0.0)
    es = jnp.exp(sink - m)
    inv = 1.0 / (_row_sums(ex) + es)
    return ex * jnp.concatenate([inv] * (s.shape[1] // LANE), axis=1), es * inv[:, 0:1]


def attn_fwd(qkv, q_gain, k_gain, sinks, *, carry=None, name):
    ncar = 0 if carry is None else len(carry["arrays"])
    T, W = qkv.shape
    hq = W // HEAD_DIM * GROUP // (GROUP + 2)
    dq = hq * HEAD_DIM
    PAIR = _kv_heads_per_step(hq // GROUP)
    QW, KW = PAIR * GROUP * HEAD_DIM, PAIR * HEAD_DIM
    npair = hq // (GROUP * PAIR)
    QB = _query_blocks_per_step(T // BLOCK, 4)
    nb = T // (QB * BLOCK)
    k0 = dq // KW
    v0 = k0 + npair

    def body(q_ref, kp_ref, kc_ref, vp_ref, vc_ref, qg_ref, kg_ref, sink_ref, *rest):
        car_in, o_ref, car_out, sems = rest[:ncar], rest[ncar], rest[ncar + 1:2 * ncar + 1], rest[2 * ncar + 1:]
        p, m = pl.program_id(0), pl.program_id(1)
        if carry is not None:
            @pl.when((p == 0) & (m == 0))
            def _():
                carry["start"](car_in, car_out, sems)

            @pl.when((p == npair - 1) & (m == nb - 1))
            def _():
                carry["finish"](car_in, car_out, sems)

        e = _head_sum_matrix()
        lo_half = lax.broadcasted_iota(jnp.int32, (1, LANE), 1) < HEAD_DIM
        qn, _ = _headnorm(q_ref[...], e)
        qhat_all = _tapmul(qn, qg_ref[...])
        kn, _ = _headnorm(jnp.concatenate([kp_ref[...], kc_ref[...]], axis=0), e)
        khat_all = _tapmul(kn, kg_ref[...])
        v_all = jnp.concatenate([vp_ref[...], vc_ref[...]], axis=0).astype(BF16)
        ops = [[_qk_operands(qhat_all[qb * BLOCK:(qb + 1) * BLOCK], khat_all[qb * BLOCK:(qb + 2) * BLOCK], j, lo_half)
                for j in range(PAIR)] for qb in range(QB)]
        scores = [[_dot(q4, kd, NT) for q4, kd in ops[qb]] for qb in range(QB)]
        probs = [[_attn_softmax(s, sink_ref, _attn_mask(QB * m + qb), (p * PAIR + j) * GROUP)[0]
                  for j, s in enumerate(scores[qb])] for qb in range(QB)]
        outs = []
        for qb in range(QB):
            vwin = v_all[qb * BLOCK:(qb + 2) * BLOCK]
            o4 = [_dot(probs[qb][j], _chunk(vwin, j // 2), NN) for j in range(PAIR)]
            chunks = []
            for c in range(2 * PAIR):
                j, t = c // 2, c % 2
                a = o4[j][(2 * t) * BLOCK:(2 * t + 1) * BLOCK]
                b = o4[j][(2 * t + 1) * BLOCK:(2 * t + 2) * BLOCK]
                if j % 2 == 0:
                    b = pltpu.roll(b, HEAD_DIM, 1)
                else:
                    a = pltpu.roll(a, HEAD_DIM, 1)
                chunks.append(jnp.where(lo_half, a, b))
            outs.append(jnp.concatenate(chunks, axis=1))
        o_ref[...] = jnp.concatenate(outs, axis=0).astype(o_ref.dtype)

    prev = lambda m: jnp.maximum(QB * m - 1, 0)
    car = carry if carry is not None else dict(arrays=[], out_structs=[], sems=[])
    qg8 = _rep8(jnp.tile(q_gain[0], PAIR * GROUP))
    kg8 = _rep8(jnp.tile(k_gain[0], PAIR))
    outs = pl.pallas_call(
        body, grid=(npair, nb),
        in_specs=[pl.BlockSpec((QB * BLOCK, QW), lambda p, n: (n, p)),
                  pl.BlockSpec((BLOCK, KW), lambda p, n: (prev(n), k0 + p)),
                  pl.BlockSpec((QB * BLOCK, KW), lambda p, n: (n, k0 + p)),
                  pl.BlockSpec((BLOCK, KW), lambda p, n: (prev(n), v0 + p)),
                  pl.BlockSpec((QB * BLOCK, KW), lambda p, n: (n, v0 + p)),
                  pl.BlockSpec((8, QW), lambda p, n: (0, 0)), pl.BlockSpec((8, KW), lambda p, n: (0, 0)),
                  pl.BlockSpec(memory_space=pltpu.SMEM)] + [HBM] * ncar,
        out_specs=[pl.BlockSpec((QB * BLOCK, QW), lambda p, n: (n, p))] + [HBM] * ncar,
        out_shape=[jax.ShapeDtypeStruct((T, dq), BF16)] + list(car["out_structs"]),
        scratch_shapes=list(car["sems"]),
        compiler_params=_cp(*(("parallel", "parallel") if carry is None else ("arbitrary", "arbitrary"))),
        name=name)(qkv, qkv, qkv, qkv, qkv, qg8, kg8, sinks, *car["arrays"])
    return outs[0] if carry is None else (outs[0], list(outs[1:]))


def attn_bwd(qkv, d_out, q_gain, k_gain, sinks, *, name):
    T, W = qkv.shape
    hq = W // HEAD_DIM * GROUP // (GROUP + 2)
    dq_w = hq * HEAD_DIM
    PAIR = _kv_heads_per_step(hq // GROUP)
    QW, KW = PAIR * GROUP * HEAD_DIM, PAIR * HEAD_DIM
    npair = hq // (GROUP * PAIR)
    nb = T // BLOCK
    k0 = dq_w // KW
    v0 = k0 + npair

    def body(q_ref, kp_ref, kc_ref, vp_ref, vc_ref, do_ref, qg_ref, kg_ref, sink_ref,
             dq_ref, dk_ref, dv_ref, dqg_ref, dkg_ref, dsink_ref, dk_carry, dv_carry):
        p, i = pl.program_id(0), pl.program_id(1)
        n = nb - 1 - i

        @pl.when(i == 0)
        def _():
            dk_carry[...] = jnp.zeros_like(dk_carry)
            dv_carry[...] = jnp.zeros_like(dv_carry)
            dqg_ref[...] = jnp.zeros_like(dqg_ref)
            dkg_ref[...] = jnp.zeros_like(dkg_ref)
            dsink_ref[...] = jnp.zeros_like(dsink_ref)

        e = _head_sum_matrix()
        lo_half = lax.broadcasted_iota(jnp.int32, (1, LANE), 1) < HEAD_DIM
        mask4 = _attn_mask(n)
        qg, kg = qg_ref[...], kg_ref[...]
        q = q_ref[...]
        kwin = jnp.concatenate([kp_ref[...], kc_ref[...]], axis=0)
        qn, qr = _headnorm(q, e)
        qhat = _tapmul(qn, qg)
        kn, kr = _headnorm(kwin, e)
        khat = _tapmul(kn, kg)
        vwin = jnp.concatenate([vp_ref[...], vc_ref[...]], axis=0).astype(BF16)
        do = do_ref[...].astype(F32)
        ops = [_qk_operands(qhat, khat, j, lo_half) for j in range(PAIR)]
        do4 = []
        for j in range(PAIR):
            parts = []
            for g in range(GROUP):
                dc = _chunk(do, 2 * j + g // 2)
                if g % 2 != j % 2:
                    dc = pltpu.roll(dc, HEAD_DIM, 1)
                parts.append(jnp.where(lo_half == (j % 2 == 0), dc, 0.0))
            do4.append(jnp.concatenate(parts, axis=0))
        scores = [_dot(q4, kd, NT) for q4, kd in ops]
        dps = [_dot(do4[j], _chunk(vwin, j // 2), NT) for j in range(PAIR)]
        soft = [_attn_softmax(s, sink_ref, mask4, (p * PAIR + j) * GROUP) for j, s in enumerate(scores)]
        dss, dsink_rows = [], []
        for j in range(PAIR):
            pr, psink = soft[j]
            delta = _row_sums(pr * dps[j])
            dss.append(pr * (dps[j] - jnp.concatenate([delta] * 2, axis=1)) * (HEAD_DIM ** -0.5))
            dsk = -psink * delta[:, 0:1]
            for g in range(GROUP):
                tot = jnp.sum(dsk[g * BLOCK:(g + 1) * BLOCK], axis=0, keepdims=True)
                dsink_rows.append(jnp.broadcast_to(tot, (1, LANE)))
        dq4 = [_dot(dss[j], ops[j][1], NN) for j in range(PAIR)]
        dkd = [_dot(dss[j], ops[j][0], TN) for j in range(PAIR)]
        dvc = [_dot(soft[j][0], do4[j], TN) for j in range(PAIR)]
        dqhat = jnp.concatenate(
            [jnp.where(lo_half, dq4[c // 2][(2 * (c % 2)) * BLOCK:(2 * (c % 2) + 1) * BLOCK],
                       dq4[c // 2][(2 * (c % 2) + 1) * BLOCK:(2 * (c % 2) + 2) * BLOCK]) for c in range(2 * PAIR)],
            axis=1)
        dkhat_chunks, dv_chunks = [], []
        for kc in range(PAIR // 2):
            tot_k, tot_v = None, None
            for j in (2 * kc, 2 * kc + 1):
                t = jnp.where(lo_half == (j % 2 == 0), dkd[j] + pltpu.roll(dkd[j], HEAD_DIM, 1), 0.0)
                tot_k = t if tot_k is None else tot_k + t
                tot_v = dvc[j] if tot_v is None else tot_v + dvc[j]
            dkhat_chunks.append(tot_k)
            dv_chunks.append(tot_v)
        dkhat_win = jnp.concatenate(dkhat_chunks, axis=1)
        dv_win = jnp.concatenate(dv_chunks, axis=1)
        dqg_ref[...] += _colsum8(dqhat * qn)
        dqn = _tapmul(dqhat, qg)
        dq_ref[...] = (qr * dqn - q * (qr * qr * qr * _head_sums(dqn * q, e) * (1.0 / HEAD_DIM))).astype(dq_ref.dtype)
        dkh = dkhat_win[BLOCK:] + dk_carry[...]
        dk_carry[...] = dkhat_win[:BLOCK]
        kcur, knc, krc = kwin[BLOCK:], kn[BLOCK:], kr[BLOCK:]
        dkg_ref[...] += _colsum8(dkh * knc)
        dkn = _tapmul(dkh, kg)
        dk_ref[...] = (krc * dkn
                       - kcur * (krc * krc * krc * _head_sums(dkn * kcur, e) * (1.0 / HEAD_DIM))).astype(dk_ref.dtype)
        dv_ref[...] = (dv_win[BLOCK:] + dv_carry[...]).astype(dv_ref.dtype)
        dv_carry[...] = dv_win[:BLOCK]
        dsink_ref[...] += jnp.concatenate(dsink_rows, axis=0)

    rev = lambda i: nb - 1 - i
    prev = lambda i: jnp.maximum(nb - 2 - i, 0)
    qg8 = _rep8(jnp.tile(q_gain[0], PAIR * GROUP))
    kg8 = _rep8(jnp.tile(k_gain[0], PAIR))
    dq, dk, dv, dqg, dkg, dsink = pl.pallas_call(
        body, grid=(npair, nb),
        in_specs=[pl.BlockSpec((BLOCK, QW), lambda p, i: (rev(i), p)),
                  pl.BlockSpec((BLOCK, KW), lambda p, i: (prev(i), k0 + p)),
                  pl.BlockSpec((BLOCK, KW), lambda p, i: (rev(i), k0 + p)),
                  pl.BlockSpec((BLOCK, KW), lambda p, i: (prev(i), v0 + p)),
                  pl.BlockSpec((BLOCK, KW), lambda p, i: (rev(i), v0 + p)),
                  pl.BlockSpec((BLOCK, QW), lambda p, i: (rev(i), p)),
                  pl.BlockSpec((8, QW), lambda p, i: (0, 0)), pl.BlockSpec((8, KW), lambda p, i: (0, 0)),
                  pl.BlockSpec(memory_space=pltpu.SMEM)],
        out_specs=[pl.BlockSpec((BLOCK, QW), lambda p, i: (rev(i), p)),
                   pl.BlockSpec((BLOCK, KW), lambda p, i: (rev(i), p)),
                   pl.BlockSpec((BLOCK, KW), lambda p, i: (rev(i), p)),
                   pl.BlockSpec((None, 8, QW), lambda p, i: (p, 0, 0)),
                   pl.BlockSpec((None, 8, KW), lambda p, i: (p, 0, 0)),
                   pl.BlockSpec((None, PAIR * GROUP, LANE), lambda p, i: (p, 0, 0))],
        out_shape=[jax.ShapeDtypeStruct((T, dq_w), BF16),
                   jax.ShapeDtypeStruct((T, npair * KW), BF16),
                   jax.ShapeDtypeStruct((T, npair * KW), BF16),
                   jax.ShapeDtypeStruct((npair, 8, QW), F32),
                   jax.ShapeDtypeStruct((npair, 8, KW), F32),
                   jax.ShapeDtypeStruct((npair, PAIR * GROUP, LANE), F32)],
        scratch_shapes=[pltpu.VMEM((BLOCK, KW), F32), pltpu.VMEM((BLOCK, KW), F32)],
        compiler_params=_cp("parallel", "arbitrary"), name=name,
    )(qkv, qkv, qkv, qkv, qkv, d_out, qg8, kg8, sinks)
    dqg = jnp.sum(dqg.reshape(-1, HEAD_DIM), axis=0, keepdims=True)
    dkg = jnp.sum(dkg.reshape(-1, HEAD_DIM), axis=0, keepdims=True)
    return dq, dk, dv, dqg, dkg, dsink[:, :, 0].reshape(-1)


def _softplus_neg(lam):
    return jnp.maximum(-lam, 0.0) + jnp.log1p(jnp.exp(-jnp.abs(lam)))


def _causal_conv(x, prev8, cw_ref, cb8):
    K = cw_ref.shape[0]
    acc = _tapmul(x, cw_ref[K - 1])
    for s in range(1, K):
        acc = acc + _tapmul(_shift_down(x, s, prev8), cw_ref[K - 1 - s])
    return (_rows8(acc) + cb8[None]).reshape(x.shape)


def _bcast_row(x, row):
    return jnp.broadcast_to(x[row:row + 1, :], x.shape)


def _rows8(x):
    return x.reshape(x.shape[0] // 8, 8, x.shape[1])


def _tapmul(x, w8):
    return (_rows8(x) * w8[None]).reshape(x.shape)


def _colsum8(x):
    return jnp.sum(_rows8(x), axis=0)


def _shift_down(x, s, prev8):
    R = x.shape[0]
    return _shift_up(jnp.concatenate([prev8, x[:R - 8]], axis=0), 8 - s, x[R - 8:])


def _shift_up(x, s, next8):
    R = x.shape[0]
    r = pltpu.roll(x, R - s, 0)
    rowid = lax.broadcasted_iota(jnp.int32, (8, 1), 0)
    tail = jnp.where(rowid < 8 - s, r[R - 8:], pltpu.roll(next8, 8 - s, 0))
    return jnp.concatenate([r[:R - 8], tail], axis=0)


def _rep8(v):
    return jnp.broadcast_to(v[..., None, :], v.shape[:-1] + (8, v.shape[-1]))


def rec_fwd(z, cw, cb, wa, ba, wi, bi, lam, *, name):
    T, C2 = z.shape
    C = C2 // 2
    nblk, bd, _ = wa.shape
    tt = _tile(T, 128, HALO)
    ng = tt // 8

    def body(x_ref, y_ref, halo_ref, cw_ref, cb_ref, wa_ref, ba_ref, wi_ref, bi_ref, lam_ref,
             xb_ref, r_ref, i_ref, a_ref, h_ref, hp_ref, hg_ref, carry, u_scr):
        step = pl.program_id(0)

        @pl.when(step == 0)
        def _():
            carry[...] = jnp.zeros_like(carry)

        xb = _causal_conv(x_ref[...], jnp.where(step > 0, halo_ref[HALO - 8:, :], 0.0), cw_ref, cb_ref[...])
        xb_ref[...] = xb
        pa, pi = [], []
        for b in range(nblk):
            xs = xb[:, b * bd:(b + 1) * bd]
            pa.append(_dot(xs, wa_ref[b], NN))
            pi.append(_dot(xs, wi_ref[b], NN))
        r = jax.nn.sigmoid(jnp.concatenate(pa, axis=1) + ba_ref[...])
        ig = jax.nn.sigmoid(jnp.concatenate(pi, axis=1) + bi_ref[...])
        r_ref[...] = r
        i_ref[...] = ig
        nl = LRU_C * r * _softplus_neg(lam_ref[...])
        a_ref[...] = jnp.exp(-nl)
        th = jnp.tanh(nl)
        u_scr[...] = jnp.sqrt(2.0 * th / (1.0 + th)) * (ig * xb)

        rowid = lax.broadcasted_iota(jnp.int32, (8, C), 0)

        def group(gi, hc):
            r0 = pl.multiple_of(gi * 8, 8)
            a8 = a_ref[pl.ds(r0, 8), :]
            u8 = u_scr[pl.ds(r0, 8), :]
            for d in (1, 2, 4):
                a_sh = jnp.where(rowid >= d, pltpu.roll(a8, d, 0), 1.0)
                u_sh = jnp.where(rowid >= d, pltpu.roll(u8, d, 0), 0.0)
                u8 = a8 * u_sh + u8
                a8 = a8 * a_sh
            h8 = u8 + a8 * hc
            h_ref[pl.ds(r0, 8), :] = h8
            hp_ref[pl.ds(r0, 8), :] = jnp.where(rowid >= 1, pltpu.roll(h8, 1, 0), hc)
            return _bcast_row(h8, 7)

        carry[...] = lax.fori_loop(0, ng, group, carry[...], unroll=4)
        hg_ref[...] = (h_ref[...] * _gelu(y_ref[...])).astype(hg_ref.dtype)

    row = lambda c: pl.BlockSpec((tt, C), lambda i, c=c: (i, c))
    vec = pl.BlockSpec((1, C), lambda i: (0, 0))
    full = lambda shp: pl.BlockSpec(shp, lambda i, n=len(shp): (0,) * n)
    per = tt // HALO
    outs = pl.pallas_call(
        body, grid=(T // tt,),
        in_specs=[row(0), row(1), pl.BlockSpec((HALO, C), lambda i: (jnp.maximum(i * per - 1, 0), 0)),
                  full(cw.shape[:1] + (8, C)), full((8, C)), full(wa.shape), vec, full(wi.shape), vec, vec],
        out_specs=[row(0)] * 7,
        out_shape=[jax.ShapeDtypeStruct((T, C), F32)] * 6 + [jax.ShapeDtypeStruct((T, C), BF16)],
        scratch_shapes=[pltpu.VMEM((8, C), F32), pltpu.VMEM((tt, C), F32)],
        compiler_params=_cp("arbitrary"), name=name,
    )(z, z, z, _rep8(cw), _rep8(cb[0]), wa, ba, wi, bi, lam)
    return outs


def rec_bwd_scan(dhg, h, a, z, *, name):
    T, C = h.shape
    tt = _tile(T, 256, HALO)
    ng = tt // 8
    nb = T // tt

    def body(dhg_ref, h_ref, a_ref, y_ref, dy_ref, yb_ref, ycarry, acarry, g_scr):
        step = pl.program_id(0)

        @pl.when(step == 0)
        def _():
            ycarry[...] = jnp.zeros_like(ycarry)
            acarry[...] = jnp.zeros_like(acarry)

        gate, dgate = _gelu_and_grad(y_ref[...])
        dhg_v = dhg_ref[...].astype(F32)
        dy_ref[...] = (dhg_v * h_ref[...] * dgate).astype(dy_ref.dtype)
        g_scr[...] = dhg_v * gate
        rowid = lax.broadcasted_iota(jnp.int32, (8, C), 0)

        def group(j, c):
            yc, ac = c
            r0 = pl.multiple_of((ng - 1 - j) * 8, 8)
            a8 = a_ref[pl.ds(r0, 8), :]
            y8 = g_scr[pl.ds(r0, 8), :]
            b8 = jnp.where(rowid < 7, pltpu.roll(a8, 7, 0), ac)
            for d in (1, 2, 4):
                y_sh = jnp.where(rowid < 8 - d, pltpu.roll(y8, 8 - d, 0), 0.0)
                b_sh = jnp.where(rowid < 8 - d, pltpu.roll(b8, 8 - d, 0), 1.0)
                y8 = y8 + b8 * y_sh
                b8 = b8 * b_sh
            y8 = y8 + b8 * yc
            yb_ref[pl.ds(r0, 8), :] = y8
            return _bcast_row(y8, 0), _bcast_row(a8, 0)

        yc, ac = lax.fori_loop(0, ng, group, (ycarry[...], acarry[...]), unroll=4)
        ycarry[...] = yc
        acarry[...] = ac

    rev = lambda c: pl.BlockSpec((tt, C), lambda i, c=c: (nb - 1 - i, c))
    return pl.pallas_call(
        body, grid=(nb,), in_specs=[rev(0), rev(0), rev(0), rev(1)], out_specs=[rev(0), rev(0)],
        out_shape=[jax.ShapeDtypeStruct((T, C), BF16), jax.ShapeDtypeStruct((T, C), F32)],
        scratch_shapes=[pltpu.VMEM((8, C), F32), pltpu.VMEM((8, C), F32), pltpu.VMEM((tt, C), F32)],
        compiler_params=_cp("arbitrary"), name=name,
    )(dhg, h, a, z)


def rec_bwd_gates(ybar, hprev, a, r, ig, xb, lam, wa, wi, *, name):
    T, C = xb.shape
    nblk, bd, _ = wa.shape
    tt = _tile(T, 256, 8)
    nb = T // tt

    def body(y_ref, hp_ref, a_ref, r_ref, i_ref, xb_ref, lam_ref, wa_ref, wi_ref,
             dxb_ref, dwa_ref, dwi_ref, dba_ref, dbi_ref, dlam_ref):
        step = pl.program_id(0)

        @pl.when(step == 0)
        def _():
            for ref in (dwa_ref, dwi_ref, dba_ref, dbi_ref, dlam_ref):
                ref[...] = jnp.zeros_like(ref)

        y, av, rv, iv, xv = y_ref[...], a_ref[...], r_ref[...], i_ref[...], xb_ref[...]
        sp = _softplus_neg(lam_ref[...])
        th = jnp.tanh(LRU_C * rv * sp)
        s = jnp.sqrt(2.0 * th / (1.0 + th))
        d_nl = -(y * hp_ref[...] * av) + (y * iv * xv) * (av * av) / s
        dlam_ref[...] += jnp.sum(d_nl * rv, axis=0, keepdims=True) * LRU_C
        dr = d_nl * (LRU_C * sp)
        di = y * s * xv
        dpa = dr * rv * (1.0 - rv)
        dpi = di * iv * (1.0 - iv)
        dba_ref[...] += jnp.sum(dpa, axis=0, keepdims=True)
        dbi_ref[...] += jnp.sum(dpi, axis=0, keepdims=True)
        parts = []
        for b in range(nblk):
            sl = slice(b * bd, (b + 1) * bd)
            xs, da_b, di_b = xv[:, sl], dpa[:, sl], dpi[:, sl]
            dwa_ref[b] += _dot(xs, da_b, TN)
            dwi_ref[b] += _dot(xs, di_b, TN)
            parts.append(_dot(da_b, wa_ref[b], NT) + _dot(di_b, wi_ref[b], NT))
        dxb_ref[...] = y * s * iv + jnp.concatenate(parts, axis=1)

        @pl.when(step == nb - 1)
        def _():
            dlam_ref[...] = dlam_ref[...] * (-jax.nn.sigmoid(-lam_ref[...]))

    row = pl.BlockSpec((tt, C), lambda i: (i, 0))
    vec = pl.BlockSpec((1, C), lambda i: (0, 0))
    wsp = pl.BlockSpec(wa.shape, lambda i: (0, 0, 0))
    return pl.pallas_call(
        body, grid=(nb,), in_specs=[row] * 6 + [vec, wsp, wsp],
        out_specs=[row, wsp, wsp, vec, vec, vec],
        out_shape=[jax.ShapeDtypeStruct((T, C), F32), jax.ShapeDtypeStruct(wa.shape, F32),
                   jax.ShapeDtypeStruct(wa.shape, F32)] + [jax.ShapeDtypeStruct((1, C), F32)] * 3,
        compiler_params=_cp("arbitrary"), name=name,
    )(ybar, hprev, a, r, ig, xb, lam, wa, wi)


def conv_bwd(d, x0, cw, *, name):
    T, C = d.shape
    K = cw.shape[0]
    tt = _tile(T, 256, HALO)
    per = tt // HALO
    nb = T // tt

    def body(d_ref, dn_ref, x_ref, cw_ref, dx_ref, dcw_ref, dcb_ref, dcw_acc, dcb_acc):
        step = pl.program_id(1)

        @pl.when(step == 0)
        def _():
            dcw_acc[...] = jnp.zeros_like(dcw_acc)
            dcb_acc[...] = jnp.zeros_like(dcb_acc)

        dv = d_ref[...].astype(F32)
        next8 = jnp.where(step < nb - 1, dn_ref[...].astype(F32)[0:8], 0.0)
        xt = x_ref[...].astype(F32)
        acc = _tapmul(dv, cw_ref[K - 1])
        dcw_acc[K - 1] += _colsum8(dv * xt)
        dcb_acc[...] += _colsum8(dv)
        for sh in range(1, K):
            dsh = _shift_up(dv, sh, next8)
            acc = acc + _tapmul(dsh, cw_ref[K - 1 - sh])
            dcw_acc[K - 1 - sh] += _colsum8(dsh * xt)
        dx_ref[...] = acc.astype(dx_ref.dtype)

        @pl.when(step == nb - 1)
        def _():
            dcw_ref[...] = jnp.sum(dcw_acc[...], axis=1)
            dcb_ref[...] = jnp.sum(dcb_acc[...], axis=0, keepdims=True)

    tc = C
    row = pl.BlockSpec((tt, tc), lambda j, i: (i, j))
    return pl.pallas_call(
        body, grid=(C // tc, nb),
        in_specs=[row, pl.BlockSpec((HALO, tc), lambda j, i: (jnp.minimum((i + 1) * per, T // HALO - 1), j)),
                  row, pl.BlockSpec((K, 8, tc), lambda j, i: (0, 0, j))],
        out_specs=[row, pl.BlockSpec((K, tc), lambda j, i: (0, j)), pl.BlockSpec((1, tc), lambda j, i: (0, j))],
        out_shape=[jax.ShapeDtypeStruct((T, C), BF16), jax.ShapeDtypeStruct((K, C), F32),
                   jax.ShapeDtypeStruct((1, C), F32)],
        scratch_shapes=[pltpu.VMEM((K, 8, tc), F32), pltpu.VMEM((8, tc), F32)],
        compiler_params=_cp("parallel", "arbitrary"), name=name,
    )(d, d, x0, _rep8(cw))


def ffn_act_fwd(u0, cw, cb, *, n, name):
    T, W = u0.shape
    G = W // (2 * n)
    tt = _tile(T, 256, HALO)
    per = tt // HALO

    K = cw.shape[0]

    def body(u_ref, up_ref, cw_ref, cb_ref, a_ref, uo_ref):
        step = pl.program_id(1)
        prev8 = jnp.where(step > 0, up_ref[...].astype(F32)[HALO - 8:], 0.0)
        u = _causal_conv(u_ref[...].astype(F32), prev8, cw_ref, cb_ref[...])
        uo_ref[...] = u.astype(uo_ref.dtype)
        a_ref[...] = (_gelu(u[:, :n]) * u[:, n:]).astype(a_ref.dtype)

    return pl.pallas_call(
        body, grid=(G, T // tt),
        in_specs=[pl.BlockSpec((tt, 2 * n), lambda j, i: (i, j)),
                  pl.BlockSpec((HALO, 2 * n), lambda j, i: (jnp.maximum(i * per - 1, 0), j)),
                  pl.BlockSpec((K, 8, 2 * n), lambda j, i: (0, 0, j)),
                  pl.BlockSpec((8, 2 * n), lambda j, i: (0, j))],
        out_specs=[pl.BlockSpec((tt, n), lambda j, i: (i, j)), pl.BlockSpec((tt, 2 * n), lambda j, i: (i, j))],
        out_shape=[jax.ShapeDtypeStruct((T, G * n), BF16), jax.ShapeDtypeStruct((T, W), BF16)],
        compiler_params=_cp("parallel", "parallel"), name=name)(u0, u0, _rep8(cw), _rep8(cb[0]))


def ffn_down_act_bwd(dx, w_down, u0, u, cw, *, n, carry=None, name):
    T, W = u0.shape
    D = dx.shape[1]
    G = W // (2 * n)
    K = cw.shape[0]
    tt = _tile(T, 256, HALO)
    tc = _tile(n, 512)
    nb = T // tt
    car = carry if carry is not None else dict(arrays=[], out_structs=[], sems=[])
    ncar = len(car["arrays"])

    def body(dx_ref, w_ref, x_ref, u_ref, cw_ref, *rest):
        car_in, (du_ref, dcw_ref, dcb_ref) = rest[:ncar], rest[ncar:ncar + 3]
        car_out, (later8, dcw_acc, dcb_acc) = rest[ncar + 3:2 * ncar + 3], rest[2 * ncar + 3:2 * ncar + 6]
        sems = rest[2 * ncar + 6:]
        group, step = pl.program_id(0), pl.program_id(1)
        if carry is not None:
            @pl.when((group == 0) & (step == 0))
            def _():
                carry["start"](car_in, car_out, sems)

            @pl.when((group == G - 1) & (step == nb - 1))
            def _():
                carry["finish"](car_in, car_out, sems)

        @pl.when(step == 0)
        def _():
            later8[...] = jnp.zeros_like(later8)
            dcw_acc[...] = jnp.zeros_like(dcw_acc)
            dcb_acc[...] = jnp.zeros_like(dcb_acc)

        dact = _dot(dx_ref[...], w_ref[...], NT)
        for q in range(n // tc):
            gs, vs = slice(q * tc, (q + 1) * tc), slice(n + q * tc, n + (q + 1) * tc)
            gl, dgl = _gelu_and_grad(u_ref[:, gs].astype(F32))
            daf = dact[:, gs]
            for cols, d in ((gs, daf * u_ref[:, vs].astype(F32) * dgl), (vs, daf * gl)):
                next8 = later8[:, cols]
                xt = x_ref[:, cols].astype(F32)
                acc = _tapmul(d, cw_ref[K - 1, :, cols])
                dcw_acc[K - 1, :, cols] += _colsum8(d * xt)
                dcb_acc[:, cols] += _colsum8(d)
                for sh in range(1, K):
                    dsh = _shift_up(d, sh, next8)
                    acc = acc + _tapmul(dsh, cw_ref[K - 1 - sh, :, cols])
                    dcw_acc[K - 1 - sh, :, cols] += _colsum8(dsh * xt)
                du_ref[:, cols] = acc.astype(du_ref.dtype)
                later8[:, cols] = d[0:8]

        @pl.when(step == nb - 1)
        def _():
            dcw_ref[...] = jnp.sum(dcw_acc[...], axis=1)
            dcb_ref[...] = jnp.sum(dcb_acc[...], axis=0, keepdims=True)

    rev = lambda i: nb - 1 - i
    outs = pl.pallas_call(
        body, grid=(G, nb),
        in_specs=[pl.BlockSpec((tt, D), lambda j, i: (rev(i), 0)),
                  pl.BlockSpec((n, D), lambda j, i: (j, 0)),
                  pl.BlockSpec((tt, 2 * n), lambda j, i: (rev(i), j)),
                  pl.BlockSpec((tt, 2 * n), lambda j, i: (rev(i), j)),
                  pl.BlockSpec((K, 8, 2 * n), lambda j, i: (0, 0, j))] + [HBM] * ncar,
        out_specs=[pl.BlockSpec((tt, 2 * n), lambda j, i: (rev(i), j)),
                   pl.BlockSpec((K, 2 * n), lambda j, i: (0, j)),
                   pl.BlockSpec((1, 2 * n), lambda j, i: (0, j))] + [HBM] * ncar,
        out_shape=[jax.ShapeDtypeStruct((T, W), BF16), jax.ShapeDtypeStruct((K, W), F32),
                   jax.ShapeDtypeStruct((1, W), F32)] + list(car["out_structs"]),
        scratch_shapes=[pltpu.VMEM((8, 2 * n), F32), pltpu.VMEM((K, 8, 2 * n), F32), pltpu.VMEM((8, 2 * n), F32)]
        + list(car["sems"]),
        compiler_params=_cp(*(("parallel", "arbitrary") if carry is None else ("arbitrary", "arbitrary"))), name=name,
    )(dx, w_down, u0, u, _rep8(cw), *car["arrays"])
    return tuple(outs[:3]) if carry is None else (tuple(outs[:3]), list(outs[3:]))


def _ffn_shard_of_block(q):
    return (q % 2) * (N_DEV // 2) + q // 2


def _ffn_block_of_shard(s):
    return (s % (N_DEV // 2)) * 2 + s // (N_DEV // 2)


def _group_cols(v):
    lead = v.shape[:-1]
    n = v.shape[-1] // N_DEV
    return jnp.swapaxes(v.reshape(lead + (2, N_DEV // 2, n)), -3, -2).reshape(v.shape)


def _ungroup_cols(v):
    lead = v.shape[:-1]
    n = v.shape[-1] // N_DEV
    return jnp.swapaxes(v.reshape(lead + (N_DEV // 2, 2, n)), -3, -2).reshape(v.shape)


def local_step(x, target, w, plan=None):
    depth = w["mix_norm"].shape[0]
    n_up = w["ffn_conv_w"].shape[-1] // N_DEV
    g = {k: [None] * (w[k].shape[0] if hasattr(w[k], "shape") else len(w[k])) for k in w}

    def run(fn, stage, l, slot, *args, **kw):
        carry = None if plan is None else plan.carry(stage, l, slot, g)
        if carry is None:
            return fn(*args, **kw)
        out, extra = fn(*args, carry=carry, **kw)
        plan.done(stage, l, slot, extra)
        return out

    saved = []
    for l in range(depth):
        j = l // 2
        gain = w["mix_norm"][l:l + 1]
        if l % 2 == 0:
            qkv, h = run(mm_nn, "fwd", l, "in", x, w["attn_w_qkv"][j], out_dtype=F32, norm_gain=gain,
                         name="qkv_proj")
            ao = run(attn_fwd, "fwd", l, "attn", qkv, w["attn_q_gain"][j:j + 1], w["attn_k_gain"][j:j + 1],
                     w["attn_sinks"][j:j + 1], name="attn_fwd")
            x1 = run(mm_nn, "fwd", l, "out", ao, w["attn_w_o"][j], out_dtype=F32, res=x, name="attn_out_proj")
            mix = (qkv, ao)
        else:
            z, h = run(mm_nn, "fwd", l, "in", x, w["rec_w_in"][j], out_dtype=F32, norm_gain=gain,
                       name="rec_in_proj")
            xb, r, ig, a, hs, hprev, hg = rec_fwd(
                z, w["rec_conv_w"][j], w["rec_conv_b"][j:j + 1], w["rec_w_a"][j], w["rec_b_a"][j:j + 1],
                w["rec_w_i"][j], w["rec_b_i"][j:j + 1], w["rec_lambda"][j:j + 1], name="rec_fwd")
            x1 = run(mm_nn, "fwd", l, "out", hg, w["rec_w_out"][j], out_dtype=F32, res=x, name="rec_out_proj")
            mix = (z, xb, r, ig, a, hs, hprev, hg)
        u0, h2 = run(mm_nn, "fwd", l, "up", x1, w["ffn_w_up"][l], out_dtype=BF16,
                     norm_gain=w["ffn_norm"][l:l + 1], name="ffn_up_proj")
        act, u = ffn_act_fwd(u0, w["ffn_conv_w"][l], w["ffn_conv_b"][l:l + 1], n=n_up, name="ffn_act_fwd")
        x2 = run(mm_nn, "fwd", l, "down", act, w["ffn_w_down"][l], out_dtype=F32, res=x1, name="ffn_down_proj")
        saved.append((x, h, mix, x1, h2, u0, u, act))
        x = x2

    loss_vec, dx, dxb = loss_head(x, target, name="loss_head")

    for l in reversed(range(depth)):
        j = l // 2
        x0, h, mix, x1, h2, u0, u, act = saved[l]
        g["ffn_w_down"][l] = mm_tn(act, dxb, out_dtype=BF16, name="ffn_down_dw")
        du0, dcw, dcb = run(ffn_down_act_bwd, "bwd", l, "act_bwd", dxb, w["ffn_w_down"][l], u0, u,
                            w["ffn_conv_w"][l], n=n_up, name="ffn_down_act_bwd")
        g["ffn_conv_w"][l], g["ffn_conv_b"][l] = _ungroup_cols(dcw), _ungroup_cols(dcb)[0]
        g["ffn_w_up"][l] = run(mm_tn, "bwd", l, "up_dw", h2, du0, out_dtype=BF16, col_shards=N_DEV, shard_of_block=_ffn_shard_of_block,
                                 name="ffn_up_dw")
        dh2 = run(mm_nt, "bwd", l, "up_dx", du0, w["ffn_w_up"][l], out_dtype=BF16, name="ffn_up_dx")
        dx1, dx1b, dgf = rmsnorm_bwd(x1, w["ffn_norm"][l:l + 1], dh2, dx, name="ffn_norm_bwd")
        g["ffn_norm"][l] = dgf[0]
        if l % 2 == 0:
            qkv, ao = mix
            dao = mm_nt(dx1b, w["attn_w_o"][j], out_dtype=BF16, name="attn_out_dx")
            g["attn_w_o"][j] = mm_tn(ao, dx1b, out_dtype=BF16, name="attn_out_dw")
            dq, dk, dv, dqg, dkg, dsk = attn_bwd(qkv, dao, w["attn_q_gain"][j:j + 1], w["attn_k_gain"][j:j + 1],
                                                 w["attn_sinks"][j:j + 1], name="attn_bwd")
            g["attn_q_gain"][j], g["attn_k_gain"][j], g["attn_sinks"][j] = dqg[0], dkg[0], dsk
            dqkv = jnp.concatenate([dq, dk, dv], axis=1)
            g["attn_w_qkv"][j] = run(mm_tn, "bwd", l, "in_dw", h, dqkv, out_dtype=BF16, col_shards=N_DEV, name="qkv_dw")
            dh = run(mm_nt, "bwd", l, "in_dx", dqkv, w["attn_w_qkv"][j], out_dtype=BF16, name="qkv_dx")
        else:
            z, xb, r, ig, a, hs, hprev, hg = mix
            dhg = mm_nt(dx1b, w["rec_w_out"][j], out_dtype=BF16, name="rec_out_dx")
            g["rec_w_out"][j] = mm_tn(hg, dx1b, out_dtype=BF16, name="rec_out_dw")
            dyb, ybar = rec_bwd_scan(dhg, hs, a, z, name="rec_bwd_scan")
            dxb, dwa, dwi, dba, dbi, dlam = rec_bwd_gates(
                ybar, hprev, a, r, ig, xb, w["rec_lambda"][j:j + 1], w["rec_w_a"][j], w["rec_w_i"][j],
                name="rec_bwd_gates")
            dxb0, dcw, dcb = conv_bwd(dxb, z, w["rec_conv_w"][j], name="rec_conv_bwd")
            g["rec_w_a"][j], g["rec_w_i"][j] = dwa, dwi
            g["rec_b_a"][j], g["rec_b_i"][j], g["rec_lambda"][j] = dba[0], dbi[0], dlam[0]
            g["rec_conv_w"][j], g["rec_conv_b"][j] = dcw, dcb[0]
            dz = jnp.concatenate([dxb0, dyb], axis=1)
            g["rec_w_in"][j] = run(mm_tn, "bwd", l, "in_dw", h, dz, out_dtype=BF16, col_shards=N_DEV, name="rec_in_dw")
            dh = run(mm_nt, "bwd", l, "in_dx", dz, w["rec_w_in"][j], out_dtype=BF16, name="rec_in_dx")
        dx, dxb, dgm = rmsnorm_bwd(x0, w["mix_norm"][l:l + 1], dh, dx1, name="mix_norm_bwd")
        g["mix_norm"][l] = dgm[0]
    return loss_vec, dx, g


HBM = pl.BlockSpec(memory_space=pltpu.HBM)
N_PEER = N_DEV - 1


def _here():
    return lax.axis_index("x"), lax.axis_index("y"), lax.axis_index("c")


def _sid(dev):
    return 4 * dev[0] + 2 * dev[1] + dev[2]


def _exchange_sems(n):
    return [pltpu.SemaphoreType.DMA((n * N_PEER,)), pltpu.SemaphoreType.DMA((n * N_PEER,)),
            pltpu.SemaphoreType.DMA((n,))]


def gather_exchange(shards, out_structs, windows):
    n = len(shards)

    def parts(outs, sems):
        send_sems, recv_sems, _ = sems
        x, y, c = _here()
        me, sib = (x, y, c), (x, y, 1 - c)
        chips = [(1 - x, y), (x, 1 - y), (1 - x, 1 - y)]

        def copy(i, k, block, to, src=None):
            dst = windows[i](outs[i], _sid(block))
            return pltpu.make_async_remote_copy(
                src_ref=dst if src is None else src, dst_ref=dst,
                send_sem=send_sems.at[i * N_PEER + k], recv_sem=recv_sems.at[i * N_PEER + k],
                device_id=to, device_id_type=MESH)

        return me, sib, chips, c, copy

    def own_copies(ins, outs, sems):
        me, sib, chips, c, copy = parts(outs, sems)
        local = [pltpu.make_async_copy(ins[i], windows[i](outs[i], _sid(me)), sems[2].at[i]) for i in range(n)]
        first = []
        for i in range(n):
            first.append(copy(i, 0, me, sib, src=ins[i]))
            first += [copy(i, 1 + j, me, (*chip, c), src=ins[i]) for j, chip in enumerate(chips)]
        return local, first

    def start(ins, outs, sems):
        local, first = own_copies(ins, outs, sems)
        for cp in local + first:
            cp.start()

    def finish(ins, outs, sems):
        me, sib, chips, c, copy = parts(outs, sems)
        local, first = own_copies(ins, outs, sems)
        passed = []
        for i in range(n):
            for j, chip in enumerate(chips):
                copy(i, 1 + j, (*chip, c), me).wait_recv()
                fwd = copy(i, 4 + j, (*chip, c), sib)
                fwd.start()
                passed.append(fwd)
        for i in range(n):
            copy(i, 0, sib, me).wait_recv()
            for j, chip in enumerate(chips):
                copy(i, 4 + j, (*chip, 1 - c), me).wait_recv()
        for cp in first + passed:
            cp.wait_send()
        for cp in local:
            cp.wait()

    return dict(arrays=list(shards), out_structs=list(out_structs), sems=_exchange_sems(n), start=start,
                finish=finish)


def run_exchange(ex, *, name):
    n = len(ex["arrays"])

    def body(*refs):
        ins, outs, sems = refs[:n], refs[n:2 * n], refs[2 * n:]
        ex["start"](ins, outs, sems)
        ex["finish"](ins, outs, sems)

    return pl.pallas_call(
        body, in_specs=[HBM] * n, out_specs=[HBM] * n, out_shape=ex["out_structs"], scratch_shapes=ex["sems"],
        name=name)(*ex["arrays"])


def scatter_exchange(grads):
    n = len(grads)

    def parts(ins, outs, sems):
        send_sems, recv_sems, local_sems = sems
        x, y, c = _here()
        me = (x, y, c)
        peers = []
        for k in range(1, N_DEV):
            kx, ky, kc = (k >> 2) & 1, (k >> 1) & 1, k & 1
            peers.append((1 - x if kx else x, 1 - y if ky else y, 1 - c if kc else c))

        def copy(i, k):
            return pltpu.make_async_remote_copy(
                src_ref=ins[i].at[_sid(peers[k])], dst_ref=outs[i].at[_sid(me)],
                send_sem=send_sems.at[i * N_PEER + k], recv_sem=recv_sems.at[i * N_PEER + k],
                device_id=peers[k], device_id_type=MESH)

        def arrival(i, k):
            return pltpu.make_async_remote_copy(
                src_ref=ins[i].at[_sid(me)], dst_ref=outs[i].at[_sid(peers[k])],
                send_sem=send_sems.at[i * N_PEER + k], recv_sem=recv_sems.at[i * N_PEER + k],
                device_id=peers[k], device_id_type=MESH)

        local = [pltpu.make_async_copy(ins[i].at[_sid(me)], outs[i].at[_sid(me)], local_sems.at[i]) for i in range(n)]
        sends = [copy(i, k) for i in range(n) for k in range(N_PEER)]
        return local, sends, arrival

    def start(ins, outs, sems):
        local, sends, _ = parts(ins, outs, sems)
        for cp in local + sends:
            cp.start()

    def finish(ins, outs, sems):
        local, sends, arrival = parts(ins, outs, sems)
        for i in range(n):
            for k in range(N_PEER):
                arrival(i, k).wait_recv()
        for cp in sends:
            cp.wait_send()
        for cp in local:
            cp.wait()

    return dict(arrays=list(grads), out_structs=[jax.ShapeDtypeStruct(g.shape, g.dtype) for g in grads],
                sems=_exchange_sems(n), start=start, finish=finish)


def adamw_family(contribs, w, m, v, *, name):
    L, R, C = w.shape
    S = contribs[0].shape[0]
    tr = _tile(R, max(8, (1 << 20) // (C * S)), 8)
    nr = R // tr
    c1 = 1.0 / (1.0 - ADAM_B1 ** ADAM_STEP)
    c2 = 1.0 / (1.0 - ADAM_B2 ** ADAM_STEP)

    def body(*refs):
        c_refs = refs[:L]
        w_ref, m_ref, v_ref, g_ref, d_ref, nm_ref, nv_ref = refs[L:]
        layer = pl.program_id(0)
        for l in range(L):
            @pl.when(layer == l)
            def _(l=l):
                g = c_refs[l][0].astype(F32)
                for s in range(1, S):
                    g = g + c_refs[l][s].astype(F32)
                mm = ADAM_B1 * m_ref[...] + (1.0 - ADAM_B1) * g
                vv = ADAM_B2 * v_ref[...] + (1.0 - ADAM_B2) * (g * g)
                g_ref[...] = g
                nm_ref[...] = mm
                nv_ref[...] = vv
                d_ref[...] = -ADAM_LR * ((mm * c1) / (jnp.sqrt(vv * c2) + ADAM_EPS) + ADAM_WD * w_ref[...])

    def cspec(l):
        return pl.BlockSpec((S, tr, C), lambda ll, i, l=l: (0, jnp.where(ll == l, i, 0), 0))

    lay = pl.BlockSpec((None, tr, C), lambda ll, i: (ll, i, 0))
    return pl.pallas_call(
        body, grid=(L, nr), in_specs=[cspec(l) for l in range(L)] + [lay] * 3, out_specs=[lay] * 4,
        out_shape=[jax.ShapeDtypeStruct((L, R, C), F32)] * 4,
        compiler_params=_cp("arbitrary", "arbitrary"), name=name)(*contribs, w, m, v)


def sum_slots(a, *, name):
    S, R, C = a.shape
    tr = _tile(R, 256, 8)

    def body(a_ref, o_ref):
        t = a_ref[0]
        for s in range(1, S):
            t = t + a_ref[s]
        o_ref[...] = t

    return pl.pallas_call(
        body, grid=(R // tr,), in_specs=[pl.BlockSpec((S, tr, C), lambda i: (0, i, 0))],
        out_specs=pl.BlockSpec((tr, C), lambda i: (i, 0)), out_shape=jax.ShapeDtypeStruct((R, C), F32),
        compiler_params=_cp("parallel"), name=name)(a)


LANES = 128


def _pack(arrs):
    flat = jnp.concatenate([a.reshape(-1).astype(F32) for a in arrs])
    rows = -(-flat.shape[0] // LANES)
    rows = -(-rows // 256) * 256
    return jnp.pad(flat, (0, rows * LANES - flat.shape[0])).reshape(rows, LANES)


def _unpack(buf, shapes):
    flat = buf.reshape(-1)
    out, off = [], 0
    for shp in shapes:
        size = int(np.prod(shp))
        out.append(flat[off:off + size].reshape(shp))
        off += size
    return out


def _gather_last(g):
    t = jnp.moveaxis(g, 0, -2)
    return t.reshape(t.shape[:-2] + (t.shape[-2] * t.shape[-1],))


def _own_last(full, s):
    n = full.shape[-1] // N_DEV
    t = full.reshape(full.shape[:-1] + (N_DEV, n))
    return lax.dynamic_index_in_dim(t, s, axis=t.ndim - 2, keepdims=False)


BIG = ["attn_w_qkv", "attn_w_o", "rec_w_in", "rec_w_out", "ffn_w_up", "ffn_w_down", "rec_w_a", "rec_w_i"]
SMALL_REPLICATED = ["mix_norm", "ffn_norm", "attn_q_gain", "attn_k_gain", "attn_sinks", "ffn_conv_b"]
SMALL_SHARDED = ["rec_conv_w", "rec_conv_b", "rec_b_a", "rec_b_i", "rec_lambda", "ffn_conv_w"]
SMALL = SMALL_REPLICATED + SMALL_SHARDED
WEIGHTS = ["mix_norm", "ffn_norm", "attn_w_qkv", "attn_q_gain", "attn_k_gain", "attn_sinks", "attn_w_o", "rec_w_in",
           "rec_conv_w", "rec_conv_b", "rec_w_a", "rec_b_a", "rec_w_i", "rec_b_i", "rec_lambda", "rec_w_out",
           "ffn_w_up", "ffn_conv_w", "ffn_conv_b", "ffn_w_down"]


def _col_window(n, block_of_shard=None):
    def win(ref, s):
        q = s if block_of_shard is None else block_of_shard(s)
        return ref.at[:, pl.ds(pl.multiple_of(q * n, 128), n)]
    return win


def _row_window(r):
    return lambda ref, s: ref.at[pl.ds(pl.multiple_of(s * r, 16), r), :]


def _gate_window(r):
    return lambda ref, s: ref.at[:, pl.ds(pl.multiple_of(s * r, 16), r), :]


def _slot_window(ref, s):
    return ref.at[s]


def _idx(name, layer):
    return layer if name.startswith("ffn") else layer // 2


def _slot_names(layer, slot):
    attn = layer % 2 == 0
    return {"in": ["attn_w_qkv"] if attn else ["rec_w_in"],
            "out": ["attn_w_o"] if attn else ["rec_w_out", "rec_w_a", "rec_w_i"],
            "up": ["ffn_w_up"], "down": ["ffn_w_down"]}[slot]


def _gather_for(p, items):
    shards, structs, wins = [], [], []
    for nme, layer in items:
        sh = p[nme][_idx(nme, layer)].astype(BF16)
        if nme in ("attn_w_qkv", "rec_w_in", "ffn_w_up"):
            K, n = sh.shape
            structs.append(jax.ShapeDtypeStruct((K, n * N_DEV), BF16))
            wins.append(_col_window(n, _ffn_block_of_shard if nme == "ffn_w_up" else None))
        elif nme in ("rec_w_a", "rec_w_i"):
            nblk, r, bd = sh.shape
            structs.append(jax.ShapeDtypeStruct((nblk, r * N_DEV, bd), BF16))
            wins.append(_gate_window(r))
        else:
            r, N = sh.shape
            structs.append(jax.ShapeDtypeStruct((r * N_DEV, N), BF16))
            wins.append(_row_window(r))
        shards.append(sh)
    return gather_exchange(shards, structs, wins)


def _send_layout(name, t):
    if name in ("rec_w_a", "rec_w_i"):
        nblk, bd, _ = t.shape
        return jnp.transpose(t.reshape(nblk, N_DEV, bd // N_DEV, bd), (1, 0, 2, 3)).astype(BF16)
    if name in ("attn_w_o", "rec_w_out", "ffn_w_down"):
        return t.reshape((N_DEV, t.shape[0] // N_DEV) + t.shape[1:])
    return t


class _Plan:
    BWD_SLOT = {"up_dw": "down", "in_dw": "out", "in_dx": "in"}

    def __init__(self, p, w, contribs, depth):
        self.p, self.w, self.contribs, self.depth = p, w, contribs, depth
        self.pending = None

    def carry(self, stage, l, slot, g):
        if stage == "fwd":
            items = []
            if l == 0 and slot == "in":
                items += [(k, 0) for k in _slot_names(0, "out")]
            if l == 0 and slot == "attn":
                items += [(k, 0) for s in ("up", "down") for k in _slot_names(0, s)]
            if slot != "attn" and l + 1 < self.depth:
                items += [(k, l + 1) for k in _slot_names(l + 1, slot)]
            if not items:
                return None
            self.pending = items
            return _gather_for(self.p, items)
        if slot == "act_bwd":
            items = [("ffn_w_up", l + 1)] if l + 1 < self.depth else []
        elif slot == "up_dx":
            items = [("ffn_w_up", 0)] if l == 0 else []
        else:
            items = [(k, l) for k in _slot_names(l, self.BWD_SLOT[slot])]
        if not items:
            return None
        self.pending = items
        return scatter_exchange([_send_layout(k, g[k][_idx(k, layer)]) for k, layer in items])

    def done(self, stage, l, slot, outs):
        dst = self.w if stage == "fwd" else self.contribs
        for (k, layer), o in zip(self.pending, outs):
            dst[k][_idx(k, layer)] = o


def _train_step(p, x, target, mom, vel):
    depth = p["mix_norm"].shape[0]
    s_me = _sid(_here())

    w = {k: [None] * p[k].shape[0] for k in BIG}
    first = [(k, 0) for k in _slot_names(0, "in")]
    for (k, _), t in zip(first, run_exchange(_gather_for(p, first), name="all_gather_first")):
        w[k][0] = t

    local_small = [p[k] for k in SMALL_SHARDED]
    packed = _pack(local_small)
    gathered, = run_exchange(
        gather_exchange([packed], [jax.ShapeDtypeStruct((N_DEV,) + packed.shape, F32)], [_slot_window]),
        name="all_gather_small")
    per_dev = [_unpack(gathered[s], [a.shape for a in local_small]) for s in range(N_DEV)]
    for i, k in enumerate(SMALL_SHARDED):
        w[k] = _gather_last(jnp.stack([per_dev[s][i] for s in range(N_DEV)]))
    for k in SMALL_REPLICATED:
        w[k] = p[k]
    nrec = w["rec_b_a"].shape[0]
    w["rec_b_a"] = w["rec_b_a"].reshape(nrec, -1)
    w["rec_b_i"] = w["rec_b_i"].reshape(nrec, -1)
    w["ffn_conv_w"] = _group_cols(w["ffn_conv_w"])
    w["ffn_conv_b"] = _group_cols(w["ffn_conv_b"])

    contribs = {k: [None] * len(w[k]) for k in BIG}
    loss_vec, dx, g = local_step(x[0], target[0], w, _Plan(p, w, contribs, depth))
    loss = lax.psum(jnp.sum(loss_vec), ("x", "y", "c"))

    out = {}
    for k in BIG:
        shp = p[k].shape
        L = shp[0]
        C = shp[-1]
        R = int(np.prod(shp[1:-1]))
        cs = [c.reshape(N_DEV, R, C) for c in contribs[k]]
        res = adamw_family(cs, p[k].reshape(L, R, C), mom[k].reshape(L, R, C), vel[k].reshape(L, R, C),
                           name="adamw_" + k)
        out[k] = [t.reshape(shp) for t in res]

    gsmall = [jnp.stack(g[k]) for k in SMALL]
    gp = _pack(gsmall)
    gall, = run_exchange(
        gather_exchange([gp], [jax.ShapeDtypeStruct((N_DEV,) + gp.shape, F32)], [_slot_window]),
        name="all_gather_small_grads")
    gsum = _unpack(sum_slots(gall, name="sum_small_grads"), [a.shape for a in gsmall])
    glocal = []
    for k, t in zip(SMALL, gsum):
        if k in SMALL_SHARDED:
            t = _own_last(t.reshape(p[k].shape[:-1] + (p[k].shape[-1] * N_DEV,)), s_me)
        glocal.append(t.reshape(p[k].shape))
    wp, mp, vp, gpk = (_pack([d[k] for k in SMALL]) for d in (p, mom, vel, dict(zip(SMALL, glocal))))
    res = adamw_family([gpk[None]], wp[None], mp[None], vp[None], name="adamw_small")
    shapes = [p[k].shape for k in SMALL]
    unp = [_unpack(t[0], shapes) for t in res]
    for i, k in enumerate(SMALL):
        out[k] = [glocal[i], unp[1][i], unp[2][i], unp[3][i]]

    return (loss, dx[None]) + tuple(out[k][q] for q in range(4) for k in WEIGHTS)


def kernel(x, mix_norm, ffn_norm, attn_w_qkv, attn_q_gain, attn_k_gain, attn_sinks, attn_w_o, rec_w_in, rec_conv_w, rec_conv_b, rec_w_a, rec_b_a, rec_w_i, rec_b_i, rec_lambda, rec_w_out, ffn_w_up, ffn_conv_w, ffn_conv_b, ffn_w_down, loss_target, m_mix_norm, m_ffn_norm, m_attn_w_qkv, m_attn_q_gain, m_attn_k_gain, m_attn_sinks, m_attn_w_o, m_rec_w_in, m_rec_conv_w, m_rec_conv_b, m_rec_w_a, m_rec_b_a, m_rec_w_i, m_rec_b_i, m_rec_lambda, m_rec_w_out, m_ffn_w_up, m_ffn_conv_w, m_ffn_conv_b, m_ffn_w_down, v_mix_norm, v_ffn_norm, v_attn_w_qkv, v_attn_q_gain, v_attn_k_gain, v_attn_sinks, v_attn_w_o, v_rec_w_in, v_rec_conv_w, v_rec_conv_b, v_rec_w_a, v_rec_b_a, v_rec_w_i, v_rec_b_i, v_rec_lambda, v_rec_w_out, v_ffn_w_up, v_ffn_conv_w, v_ffn_conv_b, v_ffn_w_down):
    p = dict(zip(WEIGHTS, (mix_norm, ffn_norm, attn_w_qkv, attn_q_gain, attn_k_gain, attn_sinks, attn_w_o, rec_w_in,
                           rec_conv_w, rec_conv_b, rec_w_a, rec_b_a, rec_w_i, rec_b_i, rec_lambda, rec_w_out,
                           ffn_w_up, ffn_conv_w, ffn_conv_b, ffn_w_down)))
    mom = dict(zip(WEIGHTS, (m_mix_norm, m_ffn_norm, m_attn_w_qkv, m_attn_q_gain, m_attn_k_gain, m_attn_sinks,
                             m_attn_w_o, m_rec_w_in, m_rec_conv_w, m_rec_conv_b, m_rec_w_a, m_rec_b_a, m_rec_w_i,
                             m_rec_b_i, m_rec_lambda, m_rec_w_out, m_ffn_w_up, m_ffn_conv_w, m_ffn_conv_b,
                             m_ffn_w_down)))
    vel = dict(zip(WEIGHTS, (v_mix_norm, v_ffn_norm, v_attn_w_qkv, v_attn_q_gain, v_attn_k_gain, v_attn_sinks,
                             v_attn_w_o, v_rec_w_in, v_rec_conv_w, v_rec_conv_b, v_rec_w_a, v_rec_b_a, v_rec_w_i,
                             v_rec_b_i, v_rec_lambda, v_rec_w_out, v_ffn_w_up, v_ffn_conv_w, v_ffn_conv_b,
                             v_ffn_w_down)))
    return _train_step(p, x, loss_target, mom, vel)
```

```python
import functools
import math

import jax
import jax.numpy as jnp
import numpy as np
from jax import lax
from jax.experimental import pallas as pl
from jax.experimental.pallas import tpu as pltpu

F32 = jnp.float32
BF16 = jnp.bfloat16

N_DEV = 8
HEAD_DIM = 64
GROUP = 4
BLOCK = 128
LRU_C = 8.0
EPS = 1e-6
HALO = 16
ADAM_LR, ADAM_B1, ADAM_B2, ADAM_EPS, ADAM_WD, ADAM_STEP = 0.001, 0.9, 0.999, 1e-08, 0.01, 10
VMEM_LIMIT = 56 * 1024 * 1024
MESH = pl.DeviceIdType.MESH
GELU_C = math.sqrt(2.0 / math.pi)


def _cp(*sem, vmem=VMEM_LIMIT):
    return pltpu.CompilerParams(dimension_semantics=tuple(sem), vmem_limit_bytes=vmem)


def _tile(dim, pref, mult=128):
    if dim <= pref:
        return dim
    t = (pref // mult) * mult
    while t >= mult:
        if dim % t == 0:
            return t
        t -= mult
    return dim


def _gelu(x):
    th = jnp.tanh(GELU_C * (x + 0.044715 * x * x * x))
    return 0.5 * x * (1.0 + th)


def _gelu_and_grad(x):
    x2 = x * x
    th = jnp.tanh(x * (GELU_C + (GELU_C * 0.044715) * x2))
    a = 0.5 + 0.5 * th
    g = x * a
    dg = a + g * (1.0 - th) * (GELU_C + (3.0 * GELU_C * 0.044715) * x2)
    return g, dg


def _dot(a, b, dims):
    return lax.dot_general(a.astype(BF16), b.astype(BF16), (dims, ((), ())), preferred_element_type=F32)


NN = ((1,), (0,))
NT = ((1,), (1,))
TN = ((0,), (0,))


def _matmul(a, b, *, dims, grid, a_spec, b_spec, o_spec, out_shape, acc_shape, res=None, res_spec=None,
            carry=None, name):
    ni, nj, nk = grid
    nres = 0 if res is None else 1
    ncar = 0 if carry is None else len(carry["arrays"])

    def body(*refs):
        a_ref, b_ref = refs[0], refs[1]
        r_ref = refs[2] if nres else None
        car_in = refs[2 + nres:2 + nres + ncar]
        o_ref = refs[2 + nres + ncar]
        car_out = refs[3 + nres + ncar:3 + nres + 2 * ncar]
        scratch = refs[3 + nres + 2 * ncar:]
        i, j, k = pl.program_id(0), pl.program_id(1), pl.program_id(2)

        if carry is not None:
            @pl.when((i == 0) & (j == 0) & (k == 0))
            def _():
                carry["start"](car_in, car_out, scratch[1:])

        def finish(acc):
            if r_ref is not None:
                acc = acc + r_ref[...].astype(F32)
            o_ref[...] = acc.astype(o_ref.dtype)

        if nk == 1:
            finish(_dot(a_ref[...], b_ref[...], dims))
        else:
            acc_ref = scratch[0]

            @pl.when(k == 0)
            def _():
                acc_ref[...] = _dot(a_ref[...], b_ref[...], dims)

            if nk > 2:
                @pl.when((k > 0) & (k < nk - 1))
                def _():
                    acc_ref[...] += _dot(a_ref[...], b_ref[...], dims)

            @pl.when(k == nk - 1)
            def _():
                finish(acc_ref[...] + _dot(a_ref[...], b_ref[...], dims))

        if carry is not None:
            @pl.when((i == ni - 1) & (j == nj - 1) & (k == nk - 1))
            def _():
                carry["finish"](car_in, car_out, scratch[1:])

    in_specs = [a_spec, b_spec] + ([res_spec] if nres else [])
    args = (a, b) + ((res,) if nres else ())
    out_specs, out_shapes = [o_spec], [out_shape]
    scratch_shapes = [pltpu.VMEM(acc_shape if nk > 1 else (8, 128), F32)]
    sem = ("parallel", "parallel", "arbitrary")
    if carry is not None:
        in_specs += [HBM] * ncar
        args += tuple(carry["arrays"])
        out_specs += [HBM] * ncar
        out_shapes += list(carry["out_structs"])
        scratch_shapes += carry["sems"]
        sem = ("arbitrary", "arbitrary", "arbitrary")
    outs = pl.pallas_call(
        body, grid=grid, in_specs=in_specs, out_specs=out_specs, out_shape=out_shapes,
        scratch_shapes=scratch_shapes, compiler_params=_cp(*sem), name=name,
    )(*args)
    return outs[0] if carry is None else (outs[0], list(outs[1:]))


def mm_nn(a, b, *, out_dtype, res=None, carry=None, name):
    M, K = a.shape
    N = b.shape[1]
    wide = res is None and out_dtype == BF16
    tm, tn, tk = _tile(M, 1024), _tile(N, 2048 if wide else 1024), _tile(K, 3072)
    return _matmul(
        a, b, dims=NN, grid=(M // tm, N // tn, K // tk),
        a_spec=pl.BlockSpec((tm, tk), lambda i, j, k: (i, k)),
        b_spec=pl.BlockSpec((tk, tn), lambda i, j, k: (k, j)),
        o_spec=pl.BlockSpec((tm, tn), lambda i, j, k: (i, j)),
        out_shape=jax.ShapeDtypeStruct((M, N), out_dtype), acc_shape=(tm, tn),
        res=res, res_spec=pl.BlockSpec((tm, tn), lambda i, j, k: (i, j)), carry=carry, name=name)


def mm_nt(a, b, *, out_dtype, res=None, carry=None, name):
    M, N = a.shape
    K = b.shape[0]
    wide = res is None and out_dtype == BF16 and a.dtype == BF16
    tm, tn, tk = _tile(M, 1024), _tile(K, 2048 if wide else 1024), _tile(N, 2048)
    return _matmul(
        a, b, dims=NT, grid=(M // tm, K // tn, N // tk),
        a_spec=pl.BlockSpec((tm, tk), lambda i, j, k: (i, k)),
        b_spec=pl.BlockSpec((tn, tk), lambda i, j, k: (j, k)),
        o_spec=pl.BlockSpec((tm, tn), lambda i, j, k: (i, j)),
        out_shape=jax.ShapeDtypeStruct((M, K), out_dtype), acc_shape=(tm, tn),
        res=res, res_spec=pl.BlockSpec((tm, tn), lambda i, j, k: (i, j)), carry=carry, name=name)


def mm_tn(a, b, *, out_dtype, col_shards=None, shard_of_block=None, carry=None, name):
    T, K = a.shape
    N = b.shape[1]
    tt = _tile(T, 2048)
    tm = _tile(K, 1024)
    if col_shards is None:
        tn = _tile(N, 1024)
        o_spec = pl.BlockSpec((tm, tn), lambda i, j, k: (i, j))
        out_shape = jax.ShapeDtypeStruct((K, N), out_dtype)
    else:
        n = N // col_shards
        tn = _tile(n, 1536)
        per = n // tn
        sob = shard_of_block if shard_of_block is not None else (lambda s: s)
        o_spec = pl.BlockSpec((None, tm, tn), lambda i, j, k: (sob(j // per), i, j % per))
        out_shape = jax.ShapeDtypeStruct((col_shards, K, n), out_dtype)
    return _matmul(
        a, b, dims=TN, grid=(K // tm, N // tn, T // tt),
        a_spec=pl.BlockSpec((tt, tm), lambda i, j, k: (k, i)),
        b_spec=pl.BlockSpec((tt, tn), lambda i, j, k: (k, j)),
        o_spec=o_spec, out_shape=out_shape, acc_shape=(tm, tn), carry=carry, name=name)


def rmsnorm_fwd(x, g, *, name):
    T, D = x.shape
    tm = _tile(T, 512, 8)

    def body(x_ref, g_ref, o_ref):
        xv = x_ref[...]
        r = lax.rsqrt(jnp.mean(xv * xv, axis=-1, keepdims=True) + EPS)
        o_ref[...] = (xv * r * g_ref[...]).astype(o_ref.dtype)

    return pl.pallas_call(
        body, grid=(T // tm,),
        in_specs=[pl.BlockSpec((tm, D), lambda i: (i, 0)), pl.BlockSpec((1, D), lambda i: (0, 0))],
        out_specs=pl.BlockSpec((tm, D), lambda i: (i, 0)),
        out_shape=jax.ShapeDtypeStruct((T, D), BF16), compiler_params=_cp("parallel"), name=name)(x, g)


def rmsnorm_bwd(x, g, dh, dres, *, name):
    T, D = x.shape
    tm = _tile(T, 512, 16)

    def body(x_ref, g_ref, dh_ref, dres_ref, dx_ref, dxb_ref, dg_ref):
        @pl.when(pl.program_id(0) == 0)
        def _():
            dg_ref[...] = jnp.zeros_like(dg_ref)

        xv = x_ref[...]
        dh_v = dh_ref[...].astype(F32)
        r = lax.rsqrt(jnp.mean(xv * xv, axis=-1, keepdims=True) + EPS)
        u = dh_v * g_ref[...]
        dot = jnp.mean(u * xv, axis=-1, keepdims=True)
        dx = dres_ref[...] + r * u - xv * (r * r * r * dot)
        dx_ref[...] = dx
        dxb_ref[...] = dx.astype(dxb_ref.dtype)
        dg_ref[...] += jnp.sum(dh_v * xv * r, axis=0, keepdims=True)

    row = pl.BlockSpec((tm, D), lambda i: (i, 0))
    vec = pl.BlockSpec((1, D), lambda i: (0, 0))
    return pl.pallas_call(
        body, grid=(T // tm,), in_specs=[row, vec, row, row], out_specs=[row, row, vec],
        out_shape=[jax.ShapeDtypeStruct((T, D), F32), jax.ShapeDtypeStruct((T, D), BF16),
                   jax.ShapeDtypeStruct((1, D), F32)],
        compiler_params=_cp("arbitrary"), name=name)(x, g, dh, dres)


def loss_head(y, target, *, name):
    T, D = y.shape
    tm = _tile(T, 512, 16)

    def body(y_ref, t_ref, l_ref, dy_ref, dyb_ref):
        @pl.when(pl.program_id(0) == 0)
        def _():
            l_ref[...] = jnp.zeros_like(l_ref)

        e = y_ref[...] - t_ref[...]
        dy = e * (1.0 / D)
        dy_ref[...] = dy
        dyb_ref[...] = dy.astype(dyb_ref.dtype)
        l_ref[...] += jnp.sum(e * e, axis=0, keepdims=True) * (0.5 / D)

    row = pl.BlockSpec((tm, D), lambda i: (i, 0))
    vec = pl.BlockSpec((1, D), lambda i: (0, 0))
    return pl.pallas_call(
        body, grid=(T // tm,), in_specs=[row, row], out_specs=[vec, row, row],
        out_shape=[jax.ShapeDtypeStruct((1, D), F32), jax.ShapeDtypeStruct((T, D), F32),
                   jax.ShapeDtypeStruct((T, D), BF16)],
        compiler_params=_cp("arbitrary"), name=name)(y, target)


NEG = -1e30


def _kv_heads_per_step(hkv):
    return 4 if hkv % 4 == 0 else 2


def _query_blocks_per_step(nblocks, want):
    while nblocks % want:
        want //= 2
    return want


LANE = 128


def _hi_lo_dot(x, w):
    hi = x.astype(BF16)
    lo = x - hi.astype(F32)
    return _dot(hi, w, NN) + _dot(lo, w, NN)


def _head_sum_matrix():
    r = lax.broadcasted_iota(jnp.int32, (LANE, LANE), 0) // HEAD_DIM
    c = lax.broadcasted_iota(jnp.int32, (LANE, LANE), 1) // HEAD_DIM
    return jnp.where(r == c, 1.0, 0.0).astype(BF16)


def _chunk(x, c):
    return x[:, c * LANE:(c + 1) * LANE]


def _head_sums(x, e):
    return jnp.concatenate([_hi_lo_dot(_chunk(x, c), e) for c in range(x.shape[1] // LANE)], axis=1)


def _row_sums(x):
    return _hi_lo_dot(x, jnp.ones((x.shape[1], LANE), BF16))


def _headnorm(x, e):
    r = lax.rsqrt(_head_sums(x * x, e) * (1.0 / HEAD_DIM) + EPS)
    return x * r, r


def _attn_mask(n):
    qi = lax.broadcasted_iota(jnp.int32, (BLOCK, 2 * BLOCK), 0)
    kj = lax.broadcasted_iota(jnp.int32, (BLOCK, 2 * BLOCK), 1)
    rel = qi + BLOCK - kj
    m = (rel >= 0) & (rel < BLOCK) & ((kj >= BLOCK) | (n > 0))
    return jnp.concatenate([m] * GROUP, axis=0)


def _qk_operands(qhat, khat, j, lo_half):
    kb = jnp.where(lo_half == (j % 2 == 0), _chunk(khat, j // 2), 0.0)
    kd = kb + pltpu.roll(kb, HEAD_DIM, 1)
    q4 = jnp.concatenate(
        [jnp.where(lo_half == (g % 2 == 0), _chunk(qhat, 2 * j + g // 2), 0.0) for g in range(GROUP)], axis=0)
    return q4, kd


def _attn_softmax(s, sink_ref, mask4, head0):
    s = s * (HEAD_DIM ** -0.5)
    sink = jnp.concatenate([jnp.full((BLOCK, 1), sink_ref[0, head0 + g], F32) for g in range(GROUP)], axis=0)
    m = jnp.maximum(jnp.max(jnp.where(mask4, s, NEG), axis=-1, keepdims=True), sink)
    ex = jnp.where(mask4, jnp.exp(s - m), 0.0)
    es = jnp.exp(sink - m)
    inv = 1.0 / (_row_sums(ex) + es)
    return ex * jnp.concatenate([inv] * (s.shape[1] // LANE), axis=1), es * inv[:, 0:1]


def attn_fwd(qkv, q_gain, k_gain, sinks, *, carry=None, name):
    ncar = 0 if carry is None else len(carry["arrays"])
    T, W = qkv.shape
    hq = W // HEAD_DIM * GROUP // (GROUP + 2)
    dq = hq * HEAD_DIM
    PAIR = _kv_heads_per_step(hq // GROUP)
    QW, KW = PAIR * GROUP * HEAD_DIM, PAIR * HEAD_DIM
    npair = hq // (GROUP * PAIR)
    QB = _query_blocks_per_step(T // BLOCK, 8)
    nb = T // (QB * BLOCK)
    k0 = dq // KW
    v0 = k0 + npair

    def body(q_ref, kp_ref, kc_ref, vp_ref, vc_ref, qg_ref, kg_ref, sink_ref, *rest):
        car_in, o_ref, car_out, sems = rest[:ncar], rest[ncar], rest[ncar + 1:2 * ncar + 1], rest[2 * ncar + 1:]
        p, m = pl.program_id(0), pl.program_id(1)
        if carry is not None:
            @pl.when((p == 0) & (m == 0))
            def _():
                carry["start"](car_in, car_out, sems)

            @pl.when((p == npair - 1) & (m == nb - 1))
            def _():
                carry["finish"](car_in, car_out, sems)

        e = _head_sum_matrix()
        lo_half = lax.broadcasted_iota(jnp.int32, (1, LANE), 1) < HEAD_DIM
        qn, _ = _headnorm(q_ref[...], e)
        qhat_all = _tapmul(qn, qg_ref[...])
        kn, _ = _headnorm(jnp.concatenate([kp_ref[...], kc_ref[...]], axis=0), e)
        khat_all = _tapmul(kn, kg_ref[...])
        v_all = jnp.concatenate([vp_ref[...], vc_ref[...]], axis=0).astype(BF16)
        ops = [[_qk_operands(qhat_all[qb * BLOCK:(qb + 1) * BLOCK], khat_all[qb * BLOCK:(qb + 2) * BLOCK], j, lo_half)
                for j in range(PAIR)] for qb in range(QB)]
        scores = [[_dot(q4, kd, NT) for q4, kd in ops[qb]] for qb in range(QB)]
        probs = [[_attn_softmax(s, sink_ref, _attn_mask(QB * m + qb), (p * PAIR + j) * GROUP)[0]
                  for j, s in enumerate(scores[qb])] for qb in range(QB)]
        outs = []
        for qb in range(QB):
            vwin = v_all[qb * BLOCK:(qb + 2) * BLOCK]
            o4 = [_dot(probs[qb][j], _chunk(vwin, j // 2), NN) for j in range(PAIR)]
            chunks = []
            for c in range(2 * PAIR):
                j, t = c // 2, c % 2
                a = o4[j][(2 * t) * BLOCK:(2 * t + 1) * BLOCK]
                b = o4[j][(2 * t + 1) * BLOCK:(2 * t + 2) * BLOCK]
                if j % 2 == 0:
                    b = pltpu.roll(b, HEAD_DIM, 1)
                else:
                    a = pltpu.roll(a, HEAD_DIM, 1)
                chunks.append(jnp.where(lo_half, a, b))
            outs.append(jnp.concatenate(chunks, axis=1))
        o_ref[...] = jnp.concatenate(outs, axis=0).astype(o_ref.dtype)

    prev = lambda m: jnp.maximum(QB * m - 1, 0)
    car = carry if carry is not None else dict(arrays=[], out_structs=[], sems=[])
    qg8 = _rep8(jnp.tile(q_gain[0], PAIR * GROUP))
    kg8 = _rep8(jnp.tile(k_gain[0], PAIR))
    outs = pl.pallas_call(
        body, grid=(npair, nb),
        in_specs=[pl.BlockSpec((QB * BLOCK, QW), lambda p, n: (n, p)),
                  pl.BlockSpec((BLOCK, KW), lambda p, n: (prev(n), k0 + p)),
                  pl.BlockSpec((QB * BLOCK, KW), lambda p, n: (n, k0 + p)),
                  pl.BlockSpec((BLOCK, KW), lambda p, n: (prev(n), v0 + p)),
                  pl.BlockSpec((QB * BLOCK, KW), lambda p, n: (n, v0 + p)),
                  pl.BlockSpec((8, QW), lambda p, n: (0, 0)), pl.BlockSpec((8, KW), lambda p, n: (0, 0)),
                  pl.BlockSpec(memory_space=pltpu.SMEM)] + [HBM] * ncar,
        out_specs=[pl.BlockSpec((QB * BLOCK, QW), lambda p, n: (n, p))] + [HBM] * ncar,
        out_shape=[jax.ShapeDtypeStruct((T, dq), BF16)] + list(car["out_structs"]),
        scratch_shapes=list(car["sems"]),
        compiler_params=_cp(*(("parallel", "parallel") if carry is None else ("arbitrary", "arbitrary"))),
        name=name)(qkv, qkv, qkv, qkv, qkv, qg8, kg8, sinks, *car["arrays"])
    return outs[0] if carry is None else (outs[0], list(outs[1:]))


def attn_bwd(qkv, d_out, q_gain, k_gain, sinks, *, name):
    T, W = qkv.shape
    hq = W // HEAD_DIM * GROUP // (GROUP + 2)
    dq_w = hq * HEAD_DIM
    PAIR = _kv_heads_per_step(hq // GROUP)
    QW, KW = PAIR * GROUP * HEAD_DIM, PAIR * HEAD_DIM
    npair = hq // (GROUP * PAIR)
    nb = T // BLOCK
    k0 = dq_w // KW
    v0 = k0 + npair

    def body(q_ref, kp_ref, kc_ref, vp_ref, vc_ref, do_ref, qg_ref, kg_ref, sink_ref,
             dq_ref, dk_ref, dv_ref, dqg_ref, dkg_ref, dsink_ref, dk_carry, dv_carry):
        p, i = pl.program_id(0), pl.program_id(1)
        n = nb - 1 - i

        @pl.when(i == 0)
        def _():
            dk_carry[...] = jnp.zeros_like(dk_carry)
            dv_carry[...] = jnp.zeros_like(dv_carry)
            dqg_ref[...] = jnp.zeros_like(dqg_ref)
            dkg_ref[...] = jnp.zeros_like(dkg_ref)
            dsink_ref[...] = jnp.zeros_like(dsink_ref)

        e = _head_sum_matrix()
        lo_half = lax.broadcasted_iota(jnp.int32, (1, LANE), 1) < HEAD_DIM
        mask4 = _attn_mask(n)
        qg, kg = qg_ref[...], kg_ref[...]
        q = q_ref[...]
        kwin = jnp.concatenate([kp_ref[...], kc_ref[...]], axis=0)
        qn, qr = _headnorm(q, e)
        qhat = _tapmul(qn, qg)
        kn, kr = _headnorm(kwin, e)
        khat = _tapmul(kn, kg)
        vwin = jnp.concatenate([vp_ref[...], vc_ref[...]], axis=0).astype(BF16)
        do = do_ref[...].astype(F32)
        ops = [_qk_operands(qhat, khat, j, lo_half) for j in range(PAIR)]
        do4 = []
        for j in range(PAIR):
            parts = []
            for g in range(GROUP):
                dc = _chunk(do, 2 * j + g // 2)
                if g % 2 != j % 2:
                    dc = pltpu.roll(dc, HEAD_DIM, 1)
                parts.append(jnp.where(lo_half == (j % 2 == 0), dc, 0.0))
            do4.append(jnp.concatenate(parts, axis=0))
        scores = [_dot(q4, kd, NT) for q4, kd in ops]
        dps = [_dot(do4[j], _chunk(vwin, j // 2), NT) for j in range(PAIR)]
        soft = [_attn_softmax(s, sink_ref, mask4, (p * PAIR + j) * GROUP) for j, s in enumerate(scores)]
        dss, dsink_rows = [], []
        for j in range(PAIR):
            pr, psink = soft[j]
            delta = _row_sums(pr * dps[j])
            dss.append(pr * (dps[j] - jnp.concatenate([delta] * 2, axis=1)) * (HEAD_DIM ** -0.5))
            dsk = -psink * delta[:, 0:1]
            for g in range(GROUP):
                tot = jnp.sum(dsk[g * BLOCK:(g + 1) * BLOCK], axis=0, keepdims=True)
                dsink_rows.append(jnp.broadcast_to(tot, (1, LANE)))
        dq4 = [_dot(dss[j], ops[j][1], NN) for j in range(PAIR)]
        dkd = [_dot(dss[j], ops[j][0], TN) for j in range(PAIR)]
        dvc = [_dot(soft[j][0], do4[j], TN) for j in range(PAIR)]
        dqhat = jnp.concatenate(
            [jnp.where(lo_half, dq4[c // 2][(2 * (c % 2)) * BLOCK:(2 * (c % 2) + 1) * BLOCK],
                       dq4[c // 2][(2 * (c % 2) + 1) * BLOCK:(2 * (c % 2) + 2) * BLOCK]) for c in range(2 * PAIR)],
            axis=1)
        dkhat_chunks, dv_chunks = [], []
        for kc in range(PAIR // 2):
            tot_k, tot_v = None, None
            for j in (2 * kc, 2 * kc + 1):
                t = jnp.where(lo_half == (j % 2 == 0), dkd[j] + pltpu.roll(dkd[j], HEAD_DIM, 1), 0.0)
                tot_k = t if tot_k is None else tot_k + t
                tot_v = dvc[j] if tot_v is None else tot_v + dvc[j]
            dkhat_chunks.append(tot_k)
            dv_chunks.append(tot_v)
        dkhat_win = jnp.concatenate(dkhat_chunks, axis=1)
        dv_win = jnp.concatenate(dv_chunks, axis=1)
        dqg_ref[...] += _colsum8(dqhat * qn)
        dqn = _tapmul(dqhat, qg)
        dq_ref[...] = (qr * dqn - q * (qr * qr * qr * _head_sums(dqn * q, e) * (1.0 / HEAD_DIM))).astype(dq_ref.dtype)
        dkh = dkhat_win[BLOCK:] + dk_carry[...]
        dk_carry[...] = dkhat_win[:BLOCK]
        kcur, knc, krc = kwin[BLOCK:], kn[BLOCK:], kr[BLOCK:]
        dkg_ref[...] += _colsum8(dkh * knc)
        dkn = _tapmul(dkh, kg)
        dk_ref[...] = (krc * dkn
                       - kcur * (krc * krc * krc * _head_sums(dkn * kcur, e) * (1.0 / HEAD_DIM))).astype(dk_ref.dtype)
        dv_ref[...] = (dv_win[BLOCK:] + dv_carry[...]).astype(dv_ref.dtype)
        dv_carry[...] = dv_win[:BLOCK]
        dsink_ref[...] += jnp.concatenate(dsink_rows, axis=0)

    rev = lambda i: nb - 1 - i
    prev = lambda i: jnp.maximum(nb - 2 - i, 0)
    qg8 = _rep8(jnp.tile(q_gain[0], PAIR * GROUP))
    kg8 = _rep8(jnp.tile(k_gain[0], PAIR))
    dq, dk, dv, dqg, dkg, dsink = pl.pallas_call(
        body, grid=(npair, nb),
        in_specs=[pl.BlockSpec((BLOCK, QW), lambda p, i: (rev(i), p)),
                  pl.BlockSpec((BLOCK, KW), lambda p, i: (prev(i), k0 + p)),
                  pl.BlockSpec((BLOCK, KW), lambda p, i: (rev(i), k0 + p)),
                  pl.BlockSpec((BLOCK, KW), lambda p, i: (prev(i), v0 + p)),
                  pl.BlockSpec((BLOCK, KW), lambda p, i: (rev(i), v0 + p)),
                  pl.BlockSpec((BLOCK, QW), lambda p, i: (rev(i), p)),
                  pl.BlockSpec((8, QW), lambda p, i: (0, 0)), pl.BlockSpec((8, KW), lambda p, i: (0, 0)),
                  pl.BlockSpec(memory_space=pltpu.SMEM)],
        out_specs=[pl.BlockSpec((BLOCK, QW), lambda p, i: (rev(i), p)),
                   pl.BlockSpec((BLOCK, KW), lambda p, i: (rev(i), p)),
                   pl.BlockSpec((BLOCK, KW), lambda p, i: (rev(i), p)),
                   pl.BlockSpec((None, 8, QW), lambda p, i: (p, 0, 0)),
                   pl.BlockSpec((None, 8, KW), lambda p, i: (p, 0, 0)),
                   pl.BlockSpec((None, PAIR * GROUP, LANE), lambda p, i: (p, 0, 0))],
        out_shape=[jax.ShapeDtypeStruct((T, dq_w), BF16),
                   jax.ShapeDtypeStruct((T, npair * KW), BF16),
                   jax.ShapeDtypeStruct((T, npair * KW), BF16),
                   jax.ShapeDtypeStruct((npair, 8, QW), F32),
                   jax.ShapeDtypeStruct((npair, 8, KW), F32),
                   jax.ShapeDtypeStruct((npair, PAIR * GROUP, LANE), F32)],
        scratch_shapes=[pltpu.VMEM((BLOCK, KW), F32), pltpu.VMEM((BLOCK, KW), F32)],
        compiler_params=_cp("parallel", "arbitrary"), name=name,
    )(qkv, qkv, qkv, qkv, qkv, d_out, qg8, kg8, sinks)
    dqg = jnp.sum(dqg.reshape(-1, HEAD_DIM), axis=0, keepdims=True)
    dkg = jnp.sum(dkg.reshape(-1, HEAD_DIM), axis=0, keepdims=True)
    return dq, dk, dv, dqg, dkg, dsink[:, :, 0].reshape(-1)


def _softplus_neg(lam):
    return jnp.maximum(-lam, 0.0) + jnp.log1p(jnp.exp(-jnp.abs(lam)))


def _causal_conv(x, prev8, cw_ref, cb8):
    K = cw_ref.shape[0]
    acc = _tapmul(x, cw_ref[K - 1])
    for s in range(1, K):
        acc = acc + _tapmul(_shift_down(x, s, prev8), cw_ref[K - 1 - s])
    return (_rows8(acc) + cb8[None]).reshape(x.shape)


def _bcast_row(x, row):
    return jnp.broadcast_to(x[row:row + 1, :], x.shape)


def _rows8(x):
    return x.reshape(x.shape[0] // 8, 8, x.shape[1])


def _tapmul(x, w8):
    return (_rows8(x) * w8[None]).reshape(x.shape)


def _colsum8(x):
    return jnp.sum(_rows8(x), axis=0)


def _shift_down(x, s, prev8):
    R = x.shape[0]
    return _shift_up(jnp.concatenate([prev8, x[:R - 8]], axis=0), 8 - s, x[R - 8:])


def _shift_up(x, s, next8):
    R = x.shape[0]
    r = pltpu.roll(x, R - s, 0)
    rowid = lax.broadcasted_iota(jnp.int32, (8, 1), 0)
    tail = jnp.where(rowid < 8 - s, r[R - 8:], pltpu.roll(next8, 8 - s, 0))
    return jnp.concatenate([r[:R - 8], tail], axis=0)


def _rep8(v):
    return jnp.broadcast_to(v[..., None, :], v.shape[:-1] + (8, v.shape[-1]))


def rec_fwd(z, cw, cb, wa, ba, wi, bi, lam, *, name):
    T, C2 = z.shape
    C = C2 // 2
    nblk, bd, _ = wa.shape
    tt = _tile(T, 128, HALO)
    ng = tt // 8

    def body(x_ref, y_ref, halo_ref, cw_ref, cb_ref, wa_ref, ba_ref, wi_ref, bi_ref, lam_ref,
             xb_ref, r_ref, i_ref, a_ref, h_ref, hp_ref, hg_ref, carry, u_scr):
        step = pl.program_id(0)

        @pl.when(step == 0)
        def _():
            carry[...] = jnp.zeros_like(carry)

        xb = _causal_conv(x_ref[...], jnp.where(step > 0, halo_ref[HALO - 8:, :], 0.0), cw_ref, cb_ref[...])
        xb_ref[...] = xb
        pa, pi = [], []
        for b in range(nblk):
            xs = xb[:, b * bd:(b + 1) * bd]
            pa.append(_dot(xs, wa_ref[b], NN))
            pi.append(_dot(xs, wi_ref[b], NN))
        r = jax.nn.sigmoid(jnp.concatenate(pa, axis=1) + ba_ref[...])
        ig = jax.nn.sigmoid(jnp.concatenate(pi, axis=1) + bi_ref[...])
        r_ref[...] = r
        i_ref[...] = ig
        nl = LRU_C * r * _softplus_neg(lam_ref[...])
        a_ref[...] = jnp.exp(-nl)
        th = jnp.tanh(nl)
        u_scr[...] = jnp.sqrt(2.0 * th / (1.0 + th)) * (ig * xb)

        rowid = lax.broadcasted_iota(jnp.int32, (8, C), 0)

        def group(gi, hc):
            r0 = pl.multiple_of(gi * 8, 8)
            a8 = a_ref[pl.ds(r0, 8), :]
            u8 = u_scr[pl.ds(r0, 8), :]
            for d in (1, 2, 4):
                a_sh = jnp.where(rowid >= d, pltpu.roll(a8, d, 0), 1.0)
                u_sh = jnp.where(rowid >= d, pltpu.roll(u8, d, 0), 0.0)
                u8 = a8 * u_sh + u8
                a8 = a8 * a_sh
            h8 = u8 + a8 * hc
            h_ref[pl.ds(r0, 8), :] = h8
            hp_ref[pl.ds(r0, 8), :] = jnp.where(rowid >= 1, pltpu.roll(h8, 1, 0), hc)
            return _bcast_row(h8, 7)

        carry[...] = lax.fori_loop(0, ng, group, carry[...])
        hg_ref[...] = (h_ref[...] * _gelu(y_ref[...])).astype(hg_ref.dtype)

    row = lambda c: pl.BlockSpec((tt, C), lambda i, c=c: (i, c))
    vec = pl.BlockSpec((1, C), lambda i: (0, 0))
    full = lambda shp: pl.BlockSpec(shp, lambda i, n=len(shp): (0,) * n)
    per = tt // HALO
    outs = pl.pallas_call(
        body, grid=(T // tt,),
        in_specs=[row(0), row(1), pl.BlockSpec((HALO, C), lambda i: (jnp.maximum(i * per - 1, 0), 0)),
                  full(cw.shape[:1] + (8, C)), full((8, C)), full(wa.shape), vec, full(wi.shape), vec, vec],
        out_specs=[row(0)] * 7,
        out_shape=[jax.ShapeDtypeStruct((T, C), F32)] * 6 + [jax.ShapeDtypeStruct((T, C), BF16)],
        scratch_shapes=[pltpu.VMEM((8, C), F32), pltpu.VMEM((tt, C), F32)],
        compiler_params=_cp("arbitrary"), name=name,
    )(z, z, z, _rep8(cw), _rep8(cb[0]), wa, ba, wi, bi, lam)
    return outs


def rec_bwd_scan(dhg, h, a, z, *, name):
    T, C = h.shape
    tt = _tile(T, 256, HALO)
    ng = tt // 8
    nb = T // tt

    def body(dhg_ref, h_ref, a_ref, y_ref, dy_ref, yb_ref, ycarry, acarry, g_scr):
        step = pl.program_id(0)

        @pl.when(step == 0)
        def _():
            ycarry[...] = jnp.zeros_like(ycarry)
            acarry[...] = jnp.zeros_like(acarry)

        gate, dgate = _gelu_and_grad(y_ref[...])
        dhg_v = dhg_ref[...].astype(F32)
        dy_ref[...] = (dhg_v * h_ref[...] * dgate).astype(dy_ref.dtype)
        g_scr[...] = dhg_v * gate
        rowid = lax.broadcasted_iota(jnp.int32, (8, C), 0)

        def group(j, c):
            yc, ac = c
            r0 = pl.multiple_of((ng - 1 - j) * 8, 8)
            a8 = a_ref[pl.ds(r0, 8), :]
            y8 = g_scr[pl.ds(r0, 8), :]
            b8 = jnp.where(rowid < 7, pltpu.roll(a8, 7, 0), ac)
            for d in (1, 2, 4):
                y_sh = jnp.where(rowid < 8 - d, pltpu.roll(y8, 8 - d, 0), 0.0)
                b_sh = jnp.where(rowid < 8 - d, pltpu.roll(b8, 8 - d, 0), 1.0)
                y8 = y8 + b8 * y_sh
                b8 = b8 * b_sh
            y8 = y8 + b8 * yc
            yb_ref[pl.ds(r0, 8), :] = y8
            return _bcast_row(y8, 0), _bcast_row(a8, 0)

        yc, ac = lax.fori_loop(0, ng, group, (ycarry[...], acarry[...]))
        ycarry[...] = yc
        acarry[...] = ac

    rev = lambda c: pl.BlockSpec((tt, C), lambda i, c=c: (nb - 1 - i, c))
    return pl.pallas_call(
        body, grid=(nb,), in_specs=[rev(0), rev(0), rev(0), rev(1)], out_specs=[rev(0), rev(0)],
        out_shape=[jax.ShapeDtypeStruct((T, C), BF16), jax.ShapeDtypeStruct((T, C), F32)],
        scratch_shapes=[pltpu.VMEM((8, C), F32), pltpu.VMEM((8, C), F32), pltpu.VMEM((tt, C), F32)],
        compiler_params=_cp("arbitrary"), name=name,
    )(dhg, h, a, z)


def rec_bwd_gates(ybar, hprev, a, r, ig, xb, lam, wa, wi, *, name):
    T, C = xb.shape
    nblk, bd, _ = wa.shape
    tt = _tile(T, 256, 8)
    nb = T // tt

    def body(y_ref, hp_ref, a_ref, r_ref, i_ref, xb_ref, lam_ref, wa_ref, wi_ref,
             dxb_ref, dwa_ref, dwi_ref, dba_ref, dbi_ref, dlam_ref):
        step = pl.program_id(0)

        @pl.when(step == 0)
        def _():
            for ref in (dwa_ref, dwi_ref, dba_ref, dbi_ref, dlam_ref):
                ref[...] = jnp.zeros_like(ref)

        y, av, rv, iv, xv = y_ref[...], a_ref[...], r_ref[...], i_ref[...], xb_ref[...]
        sp = _softplus_neg(lam_ref[...])
        th = jnp.tanh(LRU_C * rv * sp)
        s = jnp.sqrt(2.0 * th / (1.0 + th))
        d_nl = -(y * hp_ref[...] * av) + (y * iv * xv) * (av * av) / s
        dlam_ref[...] += jnp.sum(d_nl * rv, axis=0, keepdims=True) * LRU_C
        dr = d_nl * (LRU_C * sp)
        di = y * s * xv
        dpa = dr * rv * (1.0 - rv)
        dpi = di * iv * (1.0 - iv)
        dba_ref[...] += jnp.sum(dpa, axis=0, keepdims=True)
        dbi_ref[...] += jnp.sum(dpi, axis=0, keepdims=True)
        parts = []
        for b in range(nblk):
            sl = slice(b * bd, (b + 1) * bd)
            xs, da_b, di_b = xv[:, sl], dpa[:, sl], dpi[:, sl]
            dwa_ref[b] += _dot(xs, da_b, TN)
            dwi_ref[b] += _dot(xs, di_b, TN)
            parts.append(_dot(da_b, wa_ref[b], NT) + _dot(di_b, wi_ref[b], NT))
        dxb_ref[...] = y * s * iv + jnp.concatenate(parts, axis=1)

        @pl.when(step == nb - 1)
        def _():
            dlam_ref[...] = dlam_ref[...] * (-jax.nn.sigmoid(-lam_ref[...]))

    row = pl.BlockSpec((tt, C), lambda i: (i, 0))
    vec = pl.BlockSpec((1, C), lambda i: (0, 0))
    wsp = pl.BlockSpec(wa.shape, lambda i: (0, 0, 0))
    return pl.pallas_call(
        body, grid=(nb,), in_specs=[row] * 6 + [vec, wsp, wsp],
        out_specs=[row, wsp, wsp, vec, vec, vec],
        out_shape=[jax.ShapeDtypeStruct((T, C), F32), jax.ShapeDtypeStruct(wa.shape, F32),
                   jax.ShapeDtypeStruct(wa.shape, F32)] + [jax.ShapeDtypeStruct((1, C), F32)] * 3,
        compiler_params=_cp("arbitrary"), name=name,
    )(ybar, hprev, a, r, ig, xb, lam, wa, wi)


def conv_bwd(d, x0, cw, *, name):
    T, C = d.shape
    K = cw.shape[0]
    tt = _tile(T, 256, HALO)
    per = tt // HALO
    nb = T // tt

    def body(d_ref, dn_ref, x_ref, cw_ref, dx_ref, dcw_ref, dcb_ref, dcw_acc, dcb_acc):
        step = pl.program_id(1)

        @pl.when(step == 0)
        def _():
            dcw_acc[...] = jnp.zeros_like(dcw_acc)
            dcb_acc[...] = jnp.zeros_like(dcb_acc)

        dv = d_ref[...].astype(F32)
        next8 = jnp.where(step < nb - 1, dn_ref[...].astype(F32)[0:8], 0.0)
        xt = x_ref[...].astype(F32)
        acc = _tapmul(dv, cw_ref[K - 1])
        dcw_acc[K - 1] += _colsum8(dv * xt)
        dcb_acc[...] += _colsum8(dv)
        for sh in range(1, K):
            dsh = _shift_up(dv, sh, next8)
            acc = acc + _tapmul(dsh, cw_ref[K - 1 - sh])
            dcw_acc[K - 1 - sh] += _colsum8(dsh * xt)
        dx_ref[...] = acc.astype(dx_ref.dtype)

        @pl.when(step == nb - 1)
        def _():
            dcw_ref[...] = jnp.sum(dcw_acc[...], axis=1)
            dcb_ref[...] = jnp.sum(dcb_acc[...], axis=0, keepdims=True)

    tc = C
    row = pl.BlockSpec((tt, tc), lambda j, i: (i, j))
    return pl.pallas_call(
        body, grid=(C // tc, nb),
        in_specs=[row, pl.BlockSpec((HALO, tc), lambda j, i: (jnp.minimum((i + 1) * per, T // HALO - 1), j)),
                  row, pl.BlockSpec((K, 8, tc), lambda j, i: (0, 0, j))],
        out_specs=[row, pl.BlockSpec((K, tc), lambda j, i: (0, j)), pl.BlockSpec((1, tc), lambda j, i: (0, j))],
        out_shape=[jax.ShapeDtypeStruct((T, C), BF16), jax.ShapeDtypeStruct((K, C), F32),
                   jax.ShapeDtypeStruct((1, C), F32)],
        scratch_shapes=[pltpu.VMEM((K, 8, tc), F32), pltpu.VMEM((8, tc), F32)],
        compiler_params=_cp("parallel", "arbitrary"), name=name,
    )(d, d, x0, _rep8(cw))


def ffn_act_fwd(u0, cw, cb, *, n, name):
    T, W = u0.shape
    G = W // (2 * n)
    tt = _tile(T, 256, HALO)
    per = tt // HALO

    K = cw.shape[0]

    def body(u_ref, up_ref, cw_ref, cb_ref, a_ref, uo_ref):
        step = pl.program_id(1)
        prev8 = jnp.where(step > 0, up_ref[...].astype(F32)[HALO - 8:], 0.0)
        u = _causal_conv(u_ref[...].astype(F32), prev8, cw_ref, cb_ref[...])
        uo_ref[...] = u.astype(uo_ref.dtype)
        a_ref[...] = (_gelu(u[:, :n]) * u[:, n:]).astype(a_ref.dtype)

    return pl.pallas_call(
        body, grid=(G, T // tt),
        in_specs=[pl.BlockSpec((tt, 2 * n), lambda j, i: (i, j)),
                  pl.BlockSpec((HALO, 2 * n), lambda j, i: (jnp.maximum(i * per - 1, 0), j)),
                  pl.BlockSpec((K, 8, 2 * n), lambda j, i: (0, 0, j)),
                  pl.BlockSpec((8, 2 * n), lambda j, i: (0, j))],
        out_specs=[pl.BlockSpec((tt, n), lambda j, i: (i, j)), pl.BlockSpec((tt, 2 * n), lambda j, i: (i, j))],
        out_shape=[jax.ShapeDtypeStruct((T, G * n), BF16), jax.ShapeDtypeStruct((T, W), BF16)],
        compiler_params=_cp("parallel", "parallel"), name=name)(u0, u0, _rep8(cw), _rep8(cb[0]))


def ffn_down_act_bwd(dx, w_down, u0, u, cw, *, n, carry=None, name):
    T, W = u0.shape
    D = dx.shape[1]
    G = W // (2 * n)
    K = cw.shape[0]
    tt = _tile(T, 256, HALO)
    tc = _tile(n, 512)
    nb = T // tt
    car = carry if carry is not None else dict(arrays=[], out_structs=[], sems=[])
    ncar = len(car["arrays"])

    def body(dx_ref, w_ref, x_ref, u_ref, cw_ref, *rest):
        car_in, (du_ref, dcw_ref, dcb_ref) = rest[:ncar], rest[ncar:ncar + 3]
        car_out, (later8, dcw_acc, dcb_acc) = rest[ncar + 3:2 * ncar + 3], rest[2 * ncar + 3:2 * ncar + 6]
        sems = rest[2 * ncar + 6:]
        group, step = pl.program_id(0), pl.program_id(1)
        if carry is not None:
            @pl.when((group == 0) & (step == 0))
            def _():
                carry["start"](car_in, car_out, sems)

            @pl.when((group == G - 1) & (step == nb - 1))
            def _():
                carry["finish"](car_in, car_out, sems)

        @pl.when(step == 0)
        def _():
            later8[...] = jnp.zeros_like(later8)
            dcw_acc[...] = jnp.zeros_like(dcw_acc)
            dcb_acc[...] = jnp.zeros_like(dcb_acc)

        dact = _dot(dx_ref[...], w_ref[...], NT)
        for q in range(n // tc):
            gs, vs = slice(q * tc, (q + 1) * tc), slice(n + q * tc, n + (q + 1) * tc)
            gl, dgl = _gelu_and_grad(u_ref[:, gs].astype(F32))
            daf = dact[:, gs]
            for cols, d in ((gs, daf * u_ref[:, vs].astype(F32) * dgl), (vs, daf * gl)):
                next8 = later8[:, cols]
                xt = x_ref[:, cols].astype(F32)
                acc = _tapmul(d, cw_ref[K - 1, :, cols])
                dcw_acc[K - 1, :, cols] += _colsum8(d * xt)
                dcb_acc[:, cols] += _colsum8(d)
                for sh in range(1, K):
                    dsh = _shift_up(d, sh, next8)
                    acc = acc + _tapmul(dsh, cw_ref[K - 1 - sh, :, cols])
                    dcw_acc[K - 1 - sh, :, cols] += _colsum8(dsh * xt)
                du_ref[:, cols] = acc.astype(du_ref.dtype)
                later8[:, cols] = d[0:8]

        @pl.when(step == nb - 1)
        def _():
            dcw_ref[...] = jnp.sum(dcw_acc[...], axis=1)
            dcb_ref[...] = jnp.sum(dcb_acc[...], axis=0, keepdims=True)

    rev = lambda i: nb - 1 - i
    outs = pl.pallas_call(
        body, grid=(G, nb),
        in_specs=[pl.BlockSpec((tt, D), lambda j, i: (rev(i), 0)),
                  pl.BlockSpec((n, D), lambda j, i: (j, 0)),
                  pl.BlockSpec((tt, 2 * n), lambda j, i: (rev(i), j)),
                  pl.BlockSpec((tt, 2 * n), lambda j, i: (rev(i), j)),
                  pl.BlockSpec((K, 8, 2 * n), lambda j, i: (0, 0, j))] + [HBM] * ncar,
        out_specs=[pl.BlockSpec((tt, 2 * n), lambda j, i: (rev(i), j)),
                   pl.BlockSpec((K, 2 * n), lambda j, i: (0, j)),
                   pl.BlockSpec((1, 2 * n), lambda j, i: (0, j))] + [HBM] * ncar,
        out_shape=[jax.ShapeDtypeStruct((T, W), BF16), jax.ShapeDtypeStruct((K, W), F32),
                   jax.ShapeDtypeStruct((1, W), F32)] + list(car["out_structs"]),
        scratch_shapes=[pltpu.VMEM((8, 2 * n), F32), pltpu.VMEM((K, 8, 2 * n), F32), pltpu.VMEM((8, 2 * n), F32)]
        + list(car["sems"]),
        compiler_params=_cp(*(("parallel", "arbitrary") if carry is None else ("arbitrary", "arbitrary"))), name=name,
    )(dx, w_down, u0, u, _rep8(cw), *car["arrays"])
    return tuple(outs[:3]) if carry is None else (tuple(outs[:3]), list(outs[3:]))


def _ffn_shard_of_block(q):
    return (q % 2) * (N_DEV // 2) + q // 2


def _ffn_block_of_shard(s):
    return (s % (N_DEV // 2)) * 2 + s // (N_DEV // 2)


def _group_cols(v):
    lead = v.shape[:-1]
    n = v.shape[-1] // N_DEV
    return jnp.swapaxes(v.reshape(lead + (2, N_DEV // 2, n)), -3, -2).reshape(v.shape)


def _ungroup_cols(v):
    lead = v.shape[:-1]
    n = v.shape[-1] // N_DEV
    return jnp.swapaxes(v.reshape(lead + (N_DEV // 2, 2, n)), -3, -2).reshape(v.shape)


def local_step(x, target, w, plan=None):
    depth = w["mix_norm"].shape[0]
    n_up = w["ffn_conv_w"].shape[-1] // N_DEV
    g = {k: [None] * (w[k].shape[0] if hasattr(w[k], "shape") else len(w[k])) for k in w}

    def run(fn, stage, l, slot, *args, **kw):
        carry = None if plan is None else plan.carry(stage, l, slot, g)
        if carry is None:
            return fn(*args, **kw)
        out, extra = fn(*args, carry=carry, **kw)
        plan.done(stage, l, slot, extra)
        return out

    saved = []
    for l in range(depth):
        j = l // 2
        h = rmsnorm_fwd(x, w["mix_norm"][l:l + 1], name="mix_norm_fwd")
        if l % 2 == 0:
            qkv = run(mm_nn, "fwd", l, "in", h, w["attn_w_qkv"][j], out_dtype=F32, name="qkv_proj")
            ao = run(attn_fwd, "fwd", l, "attn", qkv, w["attn_q_gain"][j:j + 1], w["attn_k_gain"][j:j + 1],
                     w["attn_sinks"][j:j + 1], name="attn_fwd")
            x1 = run(mm_nn, "fwd", l, "out", ao, w["attn_w_o"][j], out_dtype=F32, res=x, name="attn_out_proj")
            mix = (qkv, ao)
        else:
            z = run(mm_nn, "fwd", l, "in", h, w["rec_w_in"][j], out_dtype=F32, name="rec_in_proj")
            xb, r, ig, a, hs, hprev, hg = rec_fwd(
                z, w["rec_conv_w"][j], w["rec_conv_b"][j:j + 1], w["rec_w_a"][j], w["rec_b_a"][j:j + 1],
                w["rec_w_i"][j], w["rec_b_i"][j:j + 1], w["rec_lambda"][j:j + 1], name="rec_fwd")
            x1 = run(mm_nn, "fwd", l, "out", hg, w["rec_w_out"][j], out_dtype=F32, res=x, name="rec_out_proj")
            mix = (z, xb, r, ig, a, hs, hprev, hg)
        h2 = rmsnorm_fwd(x1, w["ffn_norm"][l:l + 1], name="ffn_norm_fwd")
        u0 = run(mm_nn, "fwd", l, "up", h2, w["ffn_w_up"][l], out_dtype=BF16, name="ffn_up_proj")
        act, u = ffn_act_fwd(u0, w["ffn_conv_w"][l], w["ffn_conv_b"][l:l + 1], n=n_up, name="ffn_act_fwd")
        x2 = run(mm_nn, "fwd", l, "down", act, w["ffn_w_down"][l], out_dtype=F32, res=x1, name="ffn_down_proj")
        saved.append((x, h, mix, x1, h2, u0, u, act))
        x = x2

    loss_vec, dx, dxb = loss_head(x, target, name="loss_head")

    for l in reversed(range(depth)):
        j = l // 2
        x0, h, mix, x1, h2, u0, u, act = saved[l]
        g["ffn_w_down"][l] = mm_tn(act, dxb, out_dtype=BF16, name="ffn_down_dw")
        du0, dcw, dcb = run(ffn_down_act_bwd, "bwd", l, "act_bwd", dxb, w["ffn_w_down"][l], u0, u,
                            w["ffn_conv_w"][l], n=n_up, name="ffn_down_act_bwd")
        g["ffn_conv_w"][l], g["ffn_conv_b"][l] = _ungroup_cols(dcw), _ungroup_cols(dcb)[0]
        g["ffn_w_up"][l] = run(mm_tn, "bwd", l, "up_dw", h2, du0, out_dtype=BF16, col_shards=N_DEV, shard_of_block=_ffn_shard_of_block,
                                 name="ffn_up_dw")
        dh2 = run(mm_nt, "bwd", l, "up_dx", du0, w["ffn_w_up"][l], out_dtype=BF16, name="ffn_up_dx")
        dx1, dx1b, dgf = rmsnorm_bwd(x1, w["ffn_norm"][l:l + 1], dh2, dx, name="ffn_norm_bwd")
        g["ffn_norm"][l] = dgf[0]
        if l % 2 == 0:
            qkv, ao = mix
            dao = mm_nt(dx1b, w["attn_w_o"][j], out_dtype=BF16, name="attn_out_dx")
            g["attn_w_o"][j] = mm_tn(ao, dx1b, out_dtype=BF16, name="attn_out_dw")
            dq, dk, dv, dqg, dkg, dsk = attn_bwd(qkv, dao, w["attn_q_gain"][j:j + 1], w["attn_k_gain"][j:j + 1],
                                                 w["attn_sinks"][j:j + 1], name="attn_bwd")
            g["attn_q_gain"][j], g["attn_k_gain"][j], g["attn_sinks"][j] = dqg[0], dkg[0], dsk
            dqkv = jnp.concatenate([dq, dk, dv], axis=1)
            g["attn_w_qkv"][j] = run(mm_tn, "bwd", l, "in_dw", h, dqkv, out_dtype=BF16, col_shards=N_DEV, name="qkv_dw")
            dh = run(mm_nt, "bwd", l, "in_dx", dqkv, w["attn_w_qkv"][j], out_dtype=BF16, name="qkv_dx")
        else:
            z, xb, r, ig, a, hs, hprev, hg = mix
            dhg = mm_nt(dx1b, w["rec_w_out"][j], out_dtype=BF16, name="rec_out_dx")
            g["rec_w_out"][j] = mm_tn(hg, dx1b, out_dtype=BF16, name="rec_out_dw")
            dyb, ybar = rec_bwd_scan(dhg, hs, a, z, name="rec_bwd_scan")
            dxb, dwa, dwi, dba, dbi, dlam = rec_bwd_gates(
                ybar, hprev, a, r, ig, xb, w["rec_lambda"][j:j + 1], w["rec_w_a"][j], w["rec_w_i"][j],
                name="rec_bwd_gates")
            dxb0, dcw, dcb = conv_bwd(dxb, z, w["rec_conv_w"][j], name="rec_conv_bwd")
            g["rec_w_a"][j], g["rec_w_i"][j] = dwa, dwi
            g["rec_b_a"][j], g["rec_b_i"][j], g["rec_lambda"][j] = dba[0], dbi[0], dlam[0]
            g["rec_conv_w"][j], g["rec_conv_b"][j] = dcw, dcb[0]
            dz = jnp.concatenate([dxb0, dyb], axis=1)
            g["rec_w_in"][j] = run(mm_tn, "bwd", l, "in_dw", h, dz, out_dtype=BF16, col_shards=N_DEV, name="rec_in_dw")
            dh = run(mm_nt, "bwd", l, "in_dx", dz, w["rec_w_in"][j], out_dtype=BF16, name="rec_in_dx")
        dx, dxb, dgm = rmsnorm_bwd(x0, w["mix_norm"][l:l + 1], dh, dx1, name="mix_norm_bwd")
        g["mix_norm"][l] = dgm[0]
    return loss_vec, dx, g


HBM = pl.BlockSpec(memory_space=pltpu.HBM)
N_PEER = N_DEV - 1


def _here():
    return lax.axis_index("x"), lax.axis_index("y"), lax.axis_index("c")


def _sid(dev):
    return 4 * dev[0] + 2 * dev[1] + dev[2]


def _exchange_sems(n):
    return [pltpu.SemaphoreType.DMA((n * N_PEER,)), pltpu.SemaphoreType.DMA((n * N_PEER,)),
            pltpu.SemaphoreType.DMA((n,))]


def gather_exchange(shards, out_structs, windows):
    n = len(shards)

    def parts(outs, sems):
        send_sems, recv_sems, _ = sems
        x, y, c = _here()
        me, sib = (x, y, c), (x, y, 1 - c)
        chips = [(1 - x, y), (x, 1 - y), (1 - x, 1 - y)]

        def copy(i, k, block, to, src=None):
            dst = windows[i](outs[i], _sid(block))
            return pltpu.make_async_remote_copy(
                src_ref=dst if src is None else src, dst_ref=dst,
                send_sem=send_sems.at[i * N_PEER + k], recv_sem=recv_sems.at[i * N_PEER + k],
                device_id=to, device_id_type=MESH)

        return me, sib, chips, c, copy

    def own_copies(ins, outs, sems):
        me, sib, chips, c, copy = parts(outs, sems)
        local = [pltpu.make_async_copy(ins[i], windows[i](outs[i], _sid(me)), sems[2].at[i]) for i in range(n)]
        first = []
        for i in range(n):
            first.append(copy(i, 0, me, sib, src=ins[i]))
            first += [copy(i, 1 + j, me, (*chip, c), src=ins[i]) for j, chip in enumerate(chips)]
        return local, first

    def start(ins, outs, sems):
        local, first = own_copies(ins, outs, sems)
        for cp in local + first:
            cp.start()

    def finish(ins, outs, sems):
        me, sib, chips, c, copy = parts(outs, sems)
        local, first = own_copies(ins, outs, sems)
        passed = []
        for i in range(n):
            for j, chip in enumerate(chips):
                copy(i, 1 + j, (*chip, c), me).wait_recv()
                fwd = copy(i, 4 + j, (*chip, c), sib)
                fwd.start()
                passed.append(fwd)
        for i in range(n):
            copy(i, 0, sib, me).wait_recv()
            for j, chip in enumerate(chips):
                copy(i, 4 + j, (*chip, 1 - c), me).wait_recv()
        for cp in first + passed:
            cp.wait_send()
        for cp in local:
            cp.wait()

    return dict(arrays=list(shards), out_structs=list(out_structs), sems=_exchange_sems(n), start=start,
                finish=finish)


def run_exchange(ex, *, name):
    n = len(ex["arrays"])

    def body(*refs):
        ins, outs, sems = refs[:n], refs[n:2 * n], refs[2 * n:]
        ex["start"](ins, outs, sems)
        ex["finish"](ins, outs, sems)

    return pl.pallas_call(
        body, in_specs=[HBM] * n, out_specs=[HBM] * n, out_shape=ex["out_structs"], scratch_shapes=ex["sems"],
        name=name)(*ex["arrays"])


def scatter_exchange(grads):
    n = len(grads)

    def parts(ins, outs, sems):
        send_sems, recv_sems, local_sems = sems
        x, y, c = _here()
        me = (x, y, c)
        peers = []
        for k in range(1, N_DEV):
            kx, ky, kc = (k >> 2) & 1, (k >> 1) & 1, k & 1
            peers.append((1 - x if kx else x, 1 - y if ky else y, 1 - c if kc else c))

        def copy(i, k):
            return pltpu.make_async_remote_copy(
                src_ref=ins[i].at[_sid(peers[k])], dst_ref=outs[i].at[_sid(me)],
                send_sem=send_sems.at[i * N_PEER + k], recv_sem=recv_sems.at[i * N_PEER + k],
                device_id=peers[k], device_id_type=MESH)

        def arrival(i, k):
            return pltpu.make_async_remote_copy(
                src_ref=ins[i].at[_sid(me)], dst_ref=outs[i].at[_sid(peers[k])],
                send_sem=send_sems.at[i * N_PEER + k], recv_sem=recv_sems.at[i * N_PEER + k],
                device_id=peers[k], device_id_type=MESH)

        local = [pltpu.make_async_copy(ins[i].at[_sid(me)], outs[i].at[_sid(me)], local_sems.at[i]) for i in range(n)]
        sends = [copy(i, k) for i in range(n) for k in range(N_PEER)]
        return local, sends, arrival

    def start(ins, outs, sems):
        local, sends, _ = parts(ins, outs, sems)
        for cp in local + sends:
            cp.start()

    def finish(ins, outs, sems):
        local, sends, arrival = parts(ins, outs, sems)
        for i in range(n):
            for k in range(N_PEER):
                arrival(i, k).wait_recv()
        for cp in sends:
            cp.wait_send()
        for cp in local:
            cp.wait()

    return dict(arrays=list(grads), out_structs=[jax.ShapeDtypeStruct(g.shape, g.dtype) for g in grads],
                sems=_exchange_sems(n), start=start, finish=finish)


def adamw_family(contribs, w, m, v, *, name):
    L, R, C = w.shape
    S = contribs[0].shape[0]
    tr = _tile(R, max(8, (1 << 20) // (C * S)), 8)
    nr = R // tr
    c1 = 1.0 / (1.0 - ADAM_B1 ** ADAM_STEP)
    c2 = 1.0 / (1.0 - ADAM_B2 ** ADAM_STEP)

    def body(*refs):
        c_refs = refs[:L]
        w_ref, m_ref, v_ref, g_ref, d_ref, nm_ref, nv_ref = refs[L:]
        layer = pl.program_id(0)
        for l in range(L):
            @pl.when(layer == l)
            def _(l=l):
                g = c_refs[l][0].astype(F32)
                for s in range(1, S):
                    g = g + c_refs[l][s].astype(F32)
                mm = ADAM_B1 * m_ref[...] + (1.0 - ADAM_B1) * g
                vv = ADAM_B2 * v_ref[...] + (1.0 - ADAM_B2) * (g * g)
                g_ref[...] = g
                nm_ref[...] = mm
                nv_ref[...] = vv
                d_ref[...] = -ADAM_LR * ((mm * c1) / (jnp.sqrt(vv * c2) + ADAM_EPS) + ADAM_WD * w_ref[...])

    def cspec(l):
        return pl.BlockSpec((S, tr, C), lambda ll, i, l=l: (0, jnp.where(ll == l, i, 0), 0))

    lay = pl.BlockSpec((None, tr, C), lambda ll, i: (ll, i, 0))
    return pl.pallas_call(
        body, grid=(L, nr), in_specs=[cspec(l) for l in range(L)] + [lay] * 3, out_specs=[lay] * 4,
        out_shape=[jax.ShapeDtypeStruct((L, R, C), F32)] * 4,
        compiler_params=_cp("arbitrary", "arbitrary"), name=name)(*contribs, w, m, v)


def sum_slots(a, *, name):
    S, R, C = a.shape
    tr = _tile(R, 256, 8)

    def body(a_ref, o_ref):
        t = a_ref[0]
        for s in range(1, S):
            t = t + a_ref[s]
        o_ref[...] = t

    return pl.pallas_call(
        body, grid=(R // tr,), in_specs=[pl.BlockSpec((S, tr, C), lambda i: (0, i, 0))],
        out_specs=pl.BlockSpec((tr, C), lambda i: (i, 0)), out_shape=jax.ShapeDtypeStruct((R, C), F32),
        compiler_params=_cp("parallel"), name=name)(a)


LANES = 128


def _pack(arrs):
    flat = jnp.concatenate([a.reshape(-1).astype(F32) for a in arrs])
    rows = -(-flat.shape[0] // LANES)
    rows = -(-rows // 256) * 256
    return jnp.pad(flat, (0, rows * LANES - flat.shape[0])).reshape(rows, LANES)


def _unpack(buf, shapes):
    flat = buf.reshape(-1)
    out, off = [], 0
    for shp in shapes:
        size = int(np.prod(shp))
        out.append(flat[off:off + size].reshape(shp))
        off += size
    return out


def _gather_last(g):
    t = jnp.moveaxis(g, 0, -2)
    return t.reshape(t.shape[:-2] + (t.shape[-2] * t.shape[-1],))


def _own_last(full, s):
    n = full.shape[-1] // N_DEV
    t = full.reshape(full.shape[:-1] + (N_DEV, n))
    return lax.dynamic_index_in_dim(t, s, axis=t.ndim - 2, keepdims=False)


BIG = ["attn_w_qkv", "attn_w_o", "rec_w_in", "rec_w_out", "ffn_w_up", "ffn_w_down", "rec_w_a", "rec_w_i"]
SMALL_REPLICATED = ["mix_norm", "ffn_norm", "attn_q_gain", "attn_k_gain", "attn_sinks", "ffn_conv_b"]
SMALL_SHARDED = ["rec_conv_w", "rec_conv_b", "rec_b_a", "rec_b_i", "rec_lambda", "ffn_conv_w"]
SMALL = SMALL_REPLICATED + SMALL_SHARDED
WEIGHTS = ["mix_norm", "ffn_norm", "attn_w_qkv", "attn_q_gain", "attn_k_gain", "attn_sinks", "attn_w_o", "rec_w_in",
           "rec_conv_w", "rec_conv_b", "rec_w_a", "rec_b_a", "rec_w_i", "rec_b_i", "rec_lambda", "rec_w_out",
           "ffn_w_up", "ffn_conv_w", "ffn_conv_b", "ffn_w_down"]


def _col_window(n, block_of_shard=None):
    def win(ref, s):
        q = s if block_of_shard is None else block_of_shard(s)
        return ref.at[:, pl.ds(pl.multiple_of(q * n, 128), n)]
    return win


def _row_window(r):
    return lambda ref, s: ref.at[pl.ds(pl.multiple_of(s * r, 16), r), :]


def _gate_window(r):
    return lambda ref, s: ref.at[:, pl.ds(pl.multiple_of(s * r, 16), r), :]


def _slot_window(ref, s):
    return ref.at[s]


def _idx(name, layer):
    return layer if name.startswith("ffn") else layer // 2


def _slot_names(layer, slot):
    attn = layer % 2 == 0
    return {"in": ["attn_w_qkv"] if attn else ["rec_w_in"],
            "out": ["attn_w_o"] if attn else ["rec_w_out", "rec_w_a", "rec_w_i"],
            "up": ["ffn_w_up"], "down": ["ffn_w_down"]}[slot]


def _gather_for(p, items):
    shards, structs, wins = [], [], []
    for nme, layer in items:
        sh = p[nme][_idx(nme, layer)].astype(BF16)
        if nme in ("attn_w_qkv", "rec_w_in", "ffn_w_up"):
            K, n = sh.shape
            structs.append(jax.ShapeDtypeStruct((K, n * N_DEV), BF16))
            wins.append(_col_window(n, _ffn_block_of_shard if nme == "ffn_w_up" else None))
        elif nme in ("rec_w_a", "rec_w_i"):
            nblk, r, bd = sh.shape
            structs.append(jax.ShapeDtypeStruct((nblk, r * N_DEV, bd), BF16))
            wins.append(_gate_window(r))
        else:
            r, N = sh.shape
            structs.append(jax.ShapeDtypeStruct((r * N_DEV, N), BF16))
            wins.append(_row_window(r))
        shards.append(sh)
    return gather_exchange(shards, structs, wins)


def _send_layout(name, t):
    if name in ("rec_w_a", "rec_w_i"):
        nblk, bd, _ = t.shape
        return jnp.transpose(t.reshape(nblk, N_DEV, bd // N_DEV, bd), (1, 0, 2, 3)).astype(BF16)
    if name in ("attn_w_o", "rec_w_out", "ffn_w_down"):
        return t.reshape((N_DEV, t.shape[0] // N_DEV) + t.shape[1:])
    return t


class _Plan:
    BWD_SLOT = {"up_dw": "down", "in_dw": "out", "in_dx": "in"}

    def __init__(self, p, w, contribs, depth):
        self.p, self.w, self.contribs, self.depth = p, w, contribs, depth
        self.pending = None

    def carry(self, stage, l, slot, g):
        if stage == "fwd":
            items = []
            if l == 0 and slot == "in":
                items += [(k, 0) for k in _slot_names(0, "out")]
            if l == 0 and slot == "attn":
                items += [(k, 0) for s in ("up", "down") for k in _slot_names(0, s)]
            if slot != "attn" and l + 1 < self.depth:
                items += [(k, l + 1) for k in _slot_names(l + 1, slot)]
            if not items:
                return None
            self.pending = items
            return _gather_for(self.p, items)
        if slot == "act_bwd":
            items = [("ffn_w_up", l + 1)] if l + 1 < self.depth else []
        elif slot == "up_dx":
            items = [("ffn_w_up", 0)] if l == 0 else []
        else:
            items = [(k, l) for k in _slot_names(l, self.BWD_SLOT[slot])]
        if not items:
            return None
        self.pending = items
        return scatter_exchange([_send_layout(k, g[k][_idx(k, layer)]) for k, layer in items])

    def done(self, stage, l, slot, outs):
        dst = self.w if stage == "fwd" else self.contribs
        for (k, layer), o in zip(self.pending, outs):
            dst[k][_idx(k, layer)] = o


def _train_step(p, x, target, mom, vel):
    depth = p["mix_norm"].shape[0]
    s_me = _sid(_here())

    w = {k: [None] * p[k].shape[0] for k in BIG}
    first = [(k, 0) for k in _slot_names(0, "in")]
    for (k, _), t in zip(first, run_exchange(_gather_for(p, first), name="all_gather_first")):
        w[k][0] = t

    local_small = [p[k] for k in SMALL_SHARDED]
    packed = _pack(local_small)
    gathered, = run_exchange(
        gather_exchange([packed], [jax.ShapeDtypeStruct((N_DEV,) + packed.shape, F32)], [_slot_window]),
        name="all_gather_small")
    per_dev = [_unpack(gathered[s], [a.shape for a in local_small]) for s in range(N_DEV)]
    for i, k in enumerate(SMALL_SHARDED):
        w[k] = _gather_last(jnp.stack([per_dev[s][i] for s in range(N_DEV)]))
    for k in SMALL_REPLICATED:
        w[k] = p[k]
    nrec = w["rec_b_a"].shape[0]
    w["rec_b_a"] = w["rec_b_a"].reshape(nrec, -1)
    w["rec_b_i"] = w["rec_b_i"].reshape(nrec, -1)
    w["ffn_conv_w"] = _group_cols(w["ffn_conv_w"])
    w["ffn_conv_b"] = _group_cols(w["ffn_conv_b"])

    contribs = {k: [None] * len(w[k]) for k in BIG}
    loss_vec, dx, g = local_step(x[0], target[0], w, _Plan(p, w, contribs, depth))
    loss = lax.psum(jnp.sum(loss_vec), ("x", "y", "c"))

    out = {}
    for k in BIG:
        shp = p[k].shape
        L = shp[0]
        C = shp[-1]
        R = int(np.prod(shp[1:-1]))
        cs = [c.reshape(N_DEV, R, C) for c in contribs[k]]
        res = adamw_family(cs, p[k].reshape(L, R, C), mom[k].reshape(L, R, C), vel[k].reshape(L, R, C),
                           name="adamw_" + k)
        out[k] = [t.reshape(shp) for t in res]

    gsmall = [jnp.stack(g[k]) for k in SMALL]
    gp = _pack(gsmall)
    gall, = run_exchange(
        gather_exchange([gp], [jax.ShapeDtypeStruct((N_DEV,) + gp.shape, F32)], [_slot_window]),
        name="all_gather_small_grads")
    gsum = _unpack(sum_slots(gall, name="sum_small_grads"), [a.shape for a in gsmall])
    glocal = []
    for k, t in zip(SMALL, gsum):
        if k in SMALL_SHARDED:
            t = _own_last(t.reshape(p[k].shape[:-1] + (p[k].shape[-1] * N_DEV,)), s_me)
        glocal.append(t.reshape(p[k].shape))
    wp, mp, vp, gpk = (_pack([d[k] for k in SMALL]) for d in (p, mom, vel, dict(zip(SMALL, glocal))))
    res = adamw_family([gpk[None]], wp[None], mp[None], vp[None], name="adamw_small")
    shapes = [p[k].shape for k in SMALL]
    unp = [_unpack(t[0], shapes) for t in res]
    for i, k in enumerate(SMALL):
        out[k] = [glocal[i], unp[1][i], unp[2][i], unp[3][i]]

    return (loss, dx[None]) + tuple(out[k][q] for q in range(4) for k in WEIGHTS)


def kernel(x, mix_norm, ffn_norm, attn_w_qkv, attn_q_gain, attn_k_gain, attn_sinks, attn_w_o, rec_w_in, rec_conv_w, rec_conv_b, rec_w_a, rec_b_a, rec_w_i, rec_b_i, rec_lambda, rec_w_out, ffn_w_up, ffn_conv_w, ffn_conv_b, ffn_w_down, loss_target, m_mix_norm, m_ffn_norm, m_attn_w_qkv, m_attn_q_gain, m_attn_k_gain, m_attn_sinks, m_attn_w_o, m_rec_w_in, m_rec_conv_w, m_rec_conv_b, m_rec_w_a, m_rec_b_a, m_rec_w_i, m_rec_b_i, m_rec_lambda, m_rec_w_out, m_ffn_w_up, m_ffn_conv_w, m_ffn_conv_b, m_ffn_w_down, v_mix_norm, v_ffn_norm, v_attn_w_qkv, v_attn_q_gain, v_attn_k_gain, v_attn_sinks, v_attn_w_o, v_rec_w_in, v_rec_conv_w, v_rec_conv_b, v_rec_w_a, v_rec_b_a, v_rec_w_i, v_rec_b_i, v_rec_lambda, v_rec_w_out, v_ffn_w_up, v_ffn_conv_w, v_ffn_conv_b, v_ffn_w_down):
    p = dict(zip(WEIGHTS, (mix_norm, ffn_norm, attn_w_qkv, attn_q_gain, attn_k_gain, attn_sinks, attn_w_o, rec_w_in,
                           rec_conv_w, rec_conv_b, rec_w_a, rec_b_a, rec_w_i, rec_b_i, rec_lambda, rec_w_out,
                           ffn_w_up, ffn_conv_w, ffn_conv_b, ffn_w_down)))
    mom = dict(zip(WEIGHTS, (m_mix_norm, m_ffn_norm, m_attn_w_qkv, m_attn_q_gain, m_attn_k_gain, m_attn_sinks,
                             m_attn_w_o, m_rec_w_in, m_rec_conv_w, m_rec_conv_b, m_rec_w_a, m_rec_b_a, m_rec_w_i,
                             m_rec_b_i, m_rec_lambda, m_rec_w_out, m_ffn_w_up, m_ffn_conv_w, m_ffn_conv_b,
                             m_ffn_w_down)))
    vel = dict(zip(WEIGHTS, (v_mix_norm, v_ffn_norm, v_attn_w_qkv, v_attn_q_gain, v_attn_k_gain, v_attn_sinks,
                             v_attn_w_o, v_rec_w_in, v_rec_conv_w, v_rec_conv_b, v_rec_w_a, v_rec_b_a, v_rec_w_i,
                             v_rec_b_i, v_rec_lambda, v_rec_w_out, v_ffn_w_up, v_ffn_conv_w, v_ffn_conv_b,
                             v_ffn_w_down)))
    return _train_step(p, x, loss_target, mom, vel)
```

```python
import functools
import math

import jax
import jax.numpy as jnp
import numpy as np
from jax import lax
from jax.experimental import pallas as pl
from jax.experimental.pallas import tpu as pltpu

F32 = jnp.float32
BF16 = jnp.bfloat16

N_DEV = 8
HEAD_DIM = 64
GROUP = 4
BLOCK = 128
LRU_C = 8.0
EPS = 1e-6
HALO = 16
ADAM_LR, ADAM_B1, ADAM_B2, ADAM_EPS, ADAM_WD, ADAM_STEP = 0.001, 0.9, 0.999, 1e-08, 0.01, 10
VMEM_LIMIT = 56 * 1024 * 1024
MESH = pl.DeviceIdType.MESH
GELU_C = math.sqrt(2.0 / math.pi)


def _cp(*sem, vmem=VMEM_LIMIT):
    return pltpu.CompilerParams(dimension_semantics=tuple(sem), vmem_limit_bytes=vmem)


def _tile(dim, pref, mult=128):
    if dim <= pref:
        return dim
    t = (pref // mult) * mult
    while t >= mult:
        if dim % t == 0:
            return t
        t -= mult
    return dim


def _gelu(x):
    th = jnp.tanh(GELU_C * (x + 0.044715 * x * x * x))
    return 0.5 * x * (1.0 + th)


def _gelu_and_grad(x):
    x2 = x * x
    th = jnp.tanh(x * (GELU_C + (GELU_C * 0.044715) * x2))
    a = 0.5 + 0.5 * th
    g = x * a
    dg = a + g * (1.0 - th) * (GELU_C + (3.0 * GELU_C * 0.044715) * x2)
    return g, dg


def _dot(a, b, dims):
    return lax.dot_general(a.astype(BF16), b.astype(BF16), (dims, ((), ())), preferred_element_type=F32)


NN = ((1,), (0,))
NT = ((1,), (1,))
TN = ((0,), (0,))


def _matmul(a, b, *, dims, grid, a_spec, b_spec, o_spec, out_shape, acc_shape, res=None, res_spec=None,
            carry=None, name):
    ni, nj, nk = grid
    nres = 0 if res is None else 1
    ncar = 0 if carry is None else len(carry["arrays"])

    def body(*refs):
        a_ref, b_ref = refs[0], refs[1]
        r_ref = refs[2] if nres else None
        car_in = refs[2 + nres:2 + nres + ncar]
        o_ref = refs[2 + nres + ncar]
        car_out = refs[3 + nres + ncar:3 + nres + 2 * ncar]
        scratch = refs[3 + nres + 2 * ncar:]
        i, j, k = pl.program_id(0), pl.program_id(1), pl.program_id(2)

        if carry is not None:
            @pl.when((i == 0) & (j == 0) & (k == 0))
            def _():
                carry["start"](car_in, car_out, scratch[1:])

        def finish(acc):
            if r_ref is not None:
                acc = acc + r_ref[...].astype(F32)
            o_ref[...] = acc.astype(o_ref.dtype)

        if nk == 1:
            finish(_dot(a_ref[...], b_ref[...], dims))
        else:
            acc_ref = scratch[0]

            @pl.when(k == 0)
            def _():
                acc_ref[...] = _dot(a_ref[...], b_ref[...], dims)

            if nk > 2:
                @pl.when((k > 0) & (k < nk - 1))
                def _():
                    acc_ref[...] += _dot(a_ref[...], b_ref[...], dims)

            @pl.when(k == nk - 1)
            def _():
                finish(acc_ref[...] + _dot(a_ref[...], b_ref[...], dims))

        if carry is not None:
            @pl.when((i == ni - 1) & (j == nj - 1) & (k == nk - 1))
            def _():
                carry["finish"](car_in, car_out, scratch[1:])

    in_specs = [a_spec, b_spec] + ([res_spec] if nres else [])
    args = (a, b) + ((res,) if nres else ())
    out_specs, out_shapes = [o_spec], [out_shape]
    scratch_shapes = [pltpu.VMEM(acc_shape if nk > 1 else (8, 128), F32)]
    sem = ("parallel", "parallel", "arbitrary")
    if carry is not None:
        in_specs += [HBM] * ncar
        args += tuple(carry["arrays"])
        out_specs += [HBM] * ncar
        out_shapes += list(carry["out_structs"])
        scratch_shapes += carry["sems"]
        sem = ("arbitrary", "arbitrary", "arbitrary")
    outs = pl.pallas_call(
        body, grid=grid, in_specs=in_specs, out_specs=out_specs, out_shape=out_shapes,
        scratch_shapes=scratch_shapes, compiler_params=_cp(*sem), name=name,
    )(*args)
    return outs[0] if carry is None else (outs[0], list(outs[1:]))


def mm_nn(a, b, *, out_dtype, res=None, carry=None, name):
    M, K = a.shape
    N = b.shape[1]
    wide = res is None and out_dtype == BF16
    tm, tn, tk = _tile(M, 1024), _tile(N, 2048 if wide else 1024), _tile(K, 3072)
    return _matmul(
        a, b, dims=NN, grid=(M // tm, N // tn, K // tk),
        a_spec=pl.BlockSpec((tm, tk), lambda i, j, k: (i, k)),
        b_spec=pl.BlockSpec((tk, tn), lambda i, j, k: (k, j)),
        o_spec=pl.BlockSpec((tm, tn), lambda i, j, k: (i, j)),
        out_shape=jax.ShapeDtypeStruct((M, N), out_dtype), acc_shape=(tm, tn),
        res=res, res_spec=pl.BlockSpec((tm, tn), lambda i, j, k: (i, j)), carry=carry, name=name)


def mm_nt(a, b, *, out_dtype, res=None, carry=None, name):
    M, N = a.shape
    K = b.shape[0]
    wide = res is None and out_dtype == BF16 and a.dtype == BF16
    tm, tn, tk = _tile(M, 1024), _tile(K, 2048 if wide else 1024), _tile(N, 2048)
    return _matmul(
        a, b, dims=NT, grid=(M // tm, K // tn, N // tk),
        a_spec=pl.BlockSpec((tm, tk), lambda i, j, k: (i, k)),
        b_spec=pl.BlockSpec((tn, tk), lambda i, j, k: (j, k)),
        o_spec=pl.BlockSpec((tm, tn), lambda i, j, k: (i, j)),
        out_shape=jax.ShapeDtypeStruct((M, K), out_dtype), acc_shape=(tm, tn),
        res=res, res_spec=pl.BlockSpec((tm, tn), lambda i, j, k: (i, j)), carry=carry, name=name)


def mm_tn(a, b, *, out_dtype, col_shards=None, shard_of_block=None, carry=None, name):
    T, K = a.shape
    N = b.shape[1]
    tm = _tile(K, 1024)
    if col_shards is None:
        tn = _tile(N, 1024)
        o_spec = pl.BlockSpec((tm, tn), lambda i, j, k: (i, j))
        out_shape = jax.ShapeDtypeStruct((K, N), out_dtype)
    else:
        n = N // col_shards
        tn = _tile(n, 1536)
        per = n // tn
        sob = shard_of_block if shard_of_block is not None else (lambda s: s)
        o_spec = pl.BlockSpec((None, tm, tn), lambda i, j, k: (sob(j // per), i, j % per))
        out_shape = jax.ShapeDtypeStruct((col_shards, K, n), out_dtype)
    tt = _tile(T, 4096 if tn <= 1024 else 2048)
    return _matmul(
        a, b, dims=TN, grid=(K // tm, N // tn, T // tt),
        a_spec=pl.BlockSpec((tt, tm), lambda i, j, k: (k, i)),
        b_spec=pl.BlockSpec((tt, tn), lambda i, j, k: (k, j)),
        o_spec=o_spec, out_shape=out_shape, acc_shape=(tm, tn), carry=carry, name=name)


def rmsnorm_fwd(x, g, *, name):
    T, D = x.shape
    tm = _tile(T, 512, 8)

    def body(x_ref, g_ref, o_ref):
        xv = x_ref[...]
        r = lax.rsqrt(jnp.mean(xv * xv, axis=-1, keepdims=True) + EPS)
        o_ref[...] = (xv * r * g_ref[...]).astype(o_ref.dtype)

    return pl.pallas_call(
        body, grid=(T // tm,),
        in_specs=[pl.BlockSpec((tm, D), lambda i: (i, 0)), pl.BlockSpec((1, D), lambda i: (0, 0))],
        out_specs=pl.BlockSpec((tm, D), lambda i: (i, 0)),
        out_shape=jax.ShapeDtypeStruct((T, D), BF16), compiler_params=_cp("parallel"), name=name)(x, g)


def rmsnorm_bwd(x, g, dh, dres, *, name):
    T, D = x.shape
    tm = _tile(T, 512, 16)

    def body(x_ref, g_ref, dh_ref, dres_ref, dx_ref, dxb_ref, dg_ref):
        @pl.when(pl.program_id(0) == 0)
        def _():
            dg_ref[...] = jnp.zeros_like(dg_ref)

        xv = x_ref[...]
        dh_v = dh_ref[...].astype(F32)
        r = lax.rsqrt(jnp.mean(xv * xv, axis=-1, keepdims=True) + EPS)
        u = dh_v * g_ref[...]
        dot = jnp.mean(u * xv, axis=-1, keepdims=True)
        dx = dres_ref[...] + r * u - xv * (r * r * r * dot)
        dx_ref[...] = dx
        dxb_ref[...] = dx.astype(dxb_ref.dtype)
        dg_ref[...] += jnp.sum(dh_v * xv * r, axis=0, keepdims=True)

    row = pl.BlockSpec((tm, D), lambda i: (i, 0))
    vec = pl.BlockSpec((1, D), lambda i: (0, 0))
    return pl.pallas_call(
        body, grid=(T // tm,), in_specs=[row, vec, row, row], out_specs=[row, row, vec],
        out_shape=[jax.ShapeDtypeStruct((T, D), F32), jax.ShapeDtypeStruct((T, D), BF16),
                   jax.ShapeDtypeStruct((1, D), F32)],
        compiler_params=_cp("arbitrary"), name=name)(x, g, dh, dres)


def loss_head(y, target, *, name):
    T, D = y.shape
    tm = _tile(T, 512, 16)

    def body(y_ref, t_ref, l_ref, dy_ref, dyb_ref):
        @pl.when(pl.program_id(0) == 0)
        def _():
            l_ref[...] = jnp.zeros_like(l_ref)

        e = y_ref[...] - t_ref[...]
        dy = e * (1.0 / D)
        dy_ref[...] = dy
        dyb_ref[...] = dy.astype(dyb_ref.dtype)
        l_ref[...] += jnp.sum(e * e, axis=0, keepdims=True) * (0.5 / D)

    row = pl.BlockSpec((tm, D), lambda i: (i, 0))
    vec = pl.BlockSpec((1, D), lambda i: (0, 0))
    return pl.pallas_call(
        body, grid=(T // tm,), in_specs=[row, row], out_specs=[vec, row, row],
        out_shape=[jax.ShapeDtypeStruct((1, D), F32), jax.ShapeDtypeStruct((T, D), F32),
                   jax.ShapeDtypeStruct((T, D), BF16)],
        compiler_params=_cp("arbitrary"), name=name)(y, target)


NEG = -1e30


def _kv_heads_per_step(hkv):
    return 4 if hkv % 4 == 0 else 2


def _query_blocks_per_step(nblocks, want):
    while nblocks % want:
        want //= 2
    return want


LANE = 128


def _hi_lo_dot(x, w):
    hi = x.astype(BF16)
    lo = x - hi.astype(F32)
    return _dot(hi, w, NN) + _dot(lo, w, NN)


def _head_sum_matrix():
    r = lax.broadcasted_iota(jnp.int32, (LANE, LANE), 0) // HEAD_DIM
    c = lax.broadcasted_iota(jnp.int32, (LANE, LANE), 1) // HEAD_DIM
    return jnp.where(r == c, 1.0, 0.0).astype(BF16)


def _chunk(x, c):
    return x[:, c * LANE:(c + 1) * LANE]


def _head_sums(x, e):
    return jnp.concatenate([_hi_lo_dot(_chunk(x, c), e) for c in range(x.shape[1] // LANE)], axis=1)


def _row_sums(x):
    return _hi_lo_dot(x, jnp.ones((x.shape[1], LANE), BF16))


def _headnorm(x, e):
    r = lax.rsqrt(_head_sums(x * x, e) * (1.0 / HEAD_DIM) + EPS)
    return x * r, r


def _attn_mask(n):
    qi = lax.broadcasted_iota(jnp.int32, (BLOCK, 2 * BLOCK), 0)
    kj = lax.broadcasted_iota(jnp.int32, (BLOCK, 2 * BLOCK), 1)
    rel = qi + BLOCK - kj
    m = (rel >= 0) & (rel < BLOCK) & ((kj >= BLOCK) | (n > 0))
    return jnp.concatenate([m] * GROUP, axis=0)


def _qk_operands(qhat, khat, j, lo_half):
    kb = jnp.where(lo_half == (j % 2 == 0), _chunk(khat, j // 2), 0.0)
    kd = kb + pltpu.roll(kb, HEAD_DIM, 1)
    q4 = jnp.concatenate(
        [jnp.where(lo_half == (g % 2 == 0), _chunk(qhat, 2 * j + g // 2), 0.0) for g in range(GROUP)], axis=0)
    return q4, kd


def _attn_softmax(s, sink_ref, mask4, head0):
    s = s * (HEAD_DIM ** -0.5)
    sink = jnp.concatenate([jnp.full((BLOCK, 1), sink_ref[0, head0 + g], F32) for g in range(GROUP)], axis=0)
    m = jnp.maximum(jnp.max(jnp.where(mask4, s, NEG), axis=-1, keepdims=True), sink)
    ex = jnp.where(mask4, jnp.exp(s - m), 0.0)
    es = jnp.exp(sink - m)
    inv = 1.0 / (_row_sums(ex) + es)
    return ex * jnp.concatenate([inv] * (s.shape[1] // LANE), axis=1), es * inv[:, 0:1]


def attn_fwd(qkv, q_gain, k_gain, sinks, *, carry=None, name):
    ncar = 0 if carry is None else len(carry["arrays"])
    T, W = qkv.shape
    hq = W // HEAD_DIM * GROUP // (GROUP + 2)
    dq = hq * HEAD_DIM
    PAIR = _kv_heads_per_step(hq // GROUP)
    QW, KW = PAIR * GROUP * HEAD_DIM, PAIR * HEAD_DIM
    npair = hq // (GROUP * PAIR)
    QB = _query_blocks_per_step(T // BLOCK, 8)
    nb = T // (QB * BLOCK)
    k0 = dq // KW
    v0 = k0 + npair

    def body(q_ref, kp_ref, kc_ref, vp_ref, vc_ref, qg_ref, kg_ref, sink_ref, *rest):
        car_in, o_ref, car_out, sems = rest[:ncar], rest[ncar], rest[ncar + 1:2 * ncar + 1], rest[2 * ncar + 1:]
        p, m = pl.program_id(0), pl.program_id(1)
        if carry is not None:
            @pl.when((p == 0) & (m == 0))
            def _():
                carry["start"](car_in, car_out, sems)

            @pl.when((p == npair - 1) & (m == nb - 1))
            def _():
                carry["finish"](car_in, car_out, sems)

        e = _head_sum_matrix()
        lo_half = lax.broadcasted_iota(jnp.int32, (1, LANE), 1) < HEAD_DIM
        qn, _ = _headnorm(q_ref[...], e)
        qhat_all = _tapmul(qn, qg_ref[...])
        kn, _ = _headnorm(jnp.concatenate([kp_ref[...], kc_ref[...]], axis=0), e)
        khat_all = _tapmul(kn, kg_ref[...])
        v_all = jnp.concatenate([vp_ref[...], vc_ref[...]], axis=0).astype(BF16)
        ops = [[_qk_operands(qhat_all[qb * BLOCK:(qb + 1) * BLOCK], khat_all[qb * BLOCK:(qb + 2) * BLOCK], j, lo_half)
                for j in range(PAIR)] for qb in range(QB)]
        scores = [[_dot(q4, kd, NT) for q4, kd in ops[qb]] for qb in range(QB)]
        probs = [[_attn_softmax(s, sink_ref, _attn_mask(QB * m + qb), (p * PAIR + j) * GROUP)[0]
                  for j, s in enumerate(scores[qb])] for qb in range(QB)]
        outs = []
        for qb in range(QB):
            vwin = v_all[qb * BLOCK:(qb + 2) * BLOCK]
            o4 = [_dot(probs[qb][j], _chunk(vwin, j // 2), NN) for j in range(PAIR)]
            chunks = []
            for c in range(2 * PAIR):
                j, t = c // 2, c % 2
                a = o4[j][(2 * t) * BLOCK:(2 * t + 1) * BLOCK]
                b = o4[j][(2 * t + 1) * BLOCK:(2 * t + 2) * BLOCK]
                if j % 2 == 0:
                    b = pltpu.roll(b, HEAD_DIM, 1)
                else:
                    a = pltpu.roll(a, HEAD_DIM, 1)
                chunks.append(jnp.where(lo_half, a, b))
            outs.append(jnp.concatenate(chunks, axis=1))
        o_ref[...] = jnp.concatenate(outs, axis=0).astype(o_ref.dtype)

    prev = lambda m: jnp.maximum(QB * m - 1, 0)
    car = carry if carry is not None else dict(arrays=[], out_structs=[], sems=[])
    qg8 = _rep8(jnp.tile(q_gain[0], PAIR * GROUP))
    kg8 = _rep8(jnp.tile(k_gain[0], PAIR))
    outs = pl.pallas_call(
        body, grid=(npair, nb),
        in_specs=[pl.BlockSpec((QB * BLOCK, QW), lambda p, n: (n, p)),
                  pl.BlockSpec((BLOCK, KW), lambda p, n: (prev(n), k0 + p)),
                  pl.BlockSpec((QB * BLOCK, KW), lambda p, n: (n, k0 + p)),
                  pl.BlockSpec((BLOCK, KW), lambda p, n: (prev(n), v0 + p)),
                  pl.BlockSpec((QB * BLOCK, KW), lambda p, n: (n, v0 + p)),
                  pl.BlockSpec((8, QW), lambda p, n: (0, 0)), pl.BlockSpec((8, KW), lambda p, n: (0, 0)),
                  pl.BlockSpec(memory_space=pltpu.SMEM)] + [HBM] * ncar,
        out_specs=[pl.BlockSpec((QB * BLOCK, QW), lambda p, n: (n, p))] + [HBM] * ncar,
        out_shape=[jax.ShapeDtypeStruct((T, dq), BF16)] + list(car["out_structs"]),
        scratch_shapes=list(car["sems"]),
        compiler_params=_cp(*(("parallel", "parallel") if carry is None else ("arbitrary", "arbitrary"))),
        name=name)(qkv, qkv, qkv, qkv, qkv, qg8, kg8, sinks, *car["arrays"])
    return outs[0] if carry is None else (outs[0], list(outs[1:]))


def attn_bwd(qkv, d_out, q_gain, k_gain, sinks, *, name):
    T, W = qkv.shape
    hq = W // HEAD_DIM * GROUP // (GROUP + 2)
    dq_w = hq * HEAD_DIM
    PAIR = _kv_heads_per_step(hq // GROUP)
    QW, KW = PAIR * GROUP * HEAD_DIM, PAIR * HEAD_DIM
    npair = hq // (GROUP * PAIR)
    nb = T // BLOCK
    k0 = dq_w // KW
    v0 = k0 + npair

    def body(q_ref, kp_ref, kc_ref, vp_ref, vc_ref, do_ref, qg_ref, kg_ref, sink_ref,
             dq_ref, dk_ref, dv_ref, dqg_ref, dkg_ref, dsink_ref, dk_carry, dv_carry):
        p, i = pl.program_id(0), pl.program_id(1)
        n = nb - 1 - i

        @pl.when(i == 0)
        def _():
            dk_carry[...] = jnp.zeros_like(dk_carry)
            dv_carry[...] = jnp.zeros_like(dv_carry)
            dqg_ref[...] = jnp.zeros_like(dqg_ref)
            dkg_ref[...] = jnp.zeros_like(dkg_ref)
            dsink_ref[...] = jnp.zeros_like(dsink_ref)

        e = _head_sum_matrix()
        lo_half = lax.broadcasted_iota(jnp.int32, (1, LANE), 1) < HEAD_DIM
        mask4 = _attn_mask(n)
        qg, kg = qg_ref[...], kg_ref[...]
        q = q_ref[...]
        kwin = jnp.concatenate([kp_ref[...], kc_ref[...]], axis=0)
        qn, qr = _headnorm(q, e)
        qhat = _tapmul(qn, qg)
        kn, kr = _headnorm(kwin, e)
        khat = _tapmul(kn, kg)
        vwin = jnp.concatenate([vp_ref[...], vc_ref[...]], axis=0).astype(BF16)
        do = do_ref[...].astype(F32)
        ops = [_qk_operands(qhat, khat, j, lo_half) for j in range(PAIR)]
        do4 = []
        for j in range(PAIR):
            parts = []
            for g in range(GROUP):
                dc = _chunk(do, 2 * j + g // 2)
                if g % 2 != j % 2:
                    dc = pltpu.roll(dc, HEAD_DIM, 1)
                parts.append(jnp.where(lo_half == (j % 2 == 0), dc, 0.0))
            do4.append(jnp.concatenate(parts, axis=0))
        scores = [_dot(q4, kd, NT) for q4, kd in ops]
        dps = [_dot(do4[j], _chunk(vwin, j // 2), NT) for j in range(PAIR)]
        soft = [_attn_softmax(s, sink_ref, mask4, (p * PAIR + j) * GROUP) for j, s in enumerate(scores)]
        dss, dsink_rows = [], []
        for j in range(PAIR):
            pr, psink = soft[j]
            delta = _row_sums(pr * dps[j])
            dss.append(pr * (dps[j] - jnp.concatenate([delta] * 2, axis=1)) * (HEAD_DIM ** -0.5))
            dsk = -psink * delta[:, 0:1]
            for g in range(GROUP):
                tot = jnp.sum(dsk[g * BLOCK:(g + 1) * BLOCK], axis=0, keepdims=True)
                dsink_rows.append(jnp.broadcast_to(tot, (1, LANE)))
        dq4 = [_dot(dss[j], ops[j][1], NN) for j in range(PAIR)]
        dkd = [_dot(dss[j], ops[j][0], TN) for j in range(PAIR)]
        dvc = [_dot(soft[j][0], do4[j], TN) for j in range(PAIR)]
        dqhat = jnp.concatenate(
            [jnp.where(lo_half, dq4[c // 2][(2 * (c % 2)) * BLOCK:(2 * (c % 2) + 1) * BLOCK],
                       dq4[c // 2][(2 * (c % 2) + 1) * BLOCK:(2 * (c % 2) + 2) * BLOCK]) for c in range(2 * PAIR)],
            axis=1)
        dkhat_chunks, dv_chunks = [], []
        for kc in range(PAIR // 2):
            tot_k, tot_v = None, None
            for j in (2 * kc, 2 * kc + 1):
                t = jnp.where(lo_half == (j % 2 == 0), dkd[j] + pltpu.roll(dkd[j], HEAD_DIM, 1), 0.0)
                tot_k = t if tot_k is None else tot_k + t
                tot_v = dvc[j] if tot_v is None else tot_v + dvc[j]
            dkhat_chunks.append(tot_k)
            dv_chunks.append(tot_v)
        dkhat_win = jnp.concatenate(dkhat_chunks, axis=1)
        dv_win = jnp.concatenate(dv_chunks, axis=1)
        dqg_ref[...] += _colsum8(dqhat * qn)
        dqn = _tapmul(dqhat, qg)
        dq_ref[...] = (qr * dqn - q * (qr * qr * qr * _head_sums(dqn * q, e) * (1.0 / HEAD_DIM))).astype(dq_ref.dtype)
        dkh = dkhat_win[BLOCK:] + dk_carry[...]
        dk_carry[...] = dkhat_win[:BLOCK]
        kcur, knc, krc = kwin[BLOCK:], kn[BLOCK:], kr[BLOCK:]
        dkg_ref[...] += _colsum8(dkh * knc)
        dkn = _tapmul(dkh, kg)
        dk_ref[...] = (krc * dkn
                       - kcur * (krc * krc * krc * _head_sums(dkn * kcur, e) * (1.0 / HEAD_DIM))).astype(dk_ref.dtype)
        dv_ref[...] = (dv_win[BLOCK:] + dv_carry[...]).astype(dv_ref.dtype)
        dv_carry[...] = dv_win[:BLOCK]
        dsink_ref[...] += jnp.concatenate(dsink_rows, axis=0)

    rev = lambda i: nb - 1 - i
    prev = lambda i: jnp.maximum(nb - 2 - i, 0)
    qg8 = _rep8(jnp.tile(q_gain[0], PAIR * GROUP))
    kg8 = _rep8(jnp.tile(k_gain[0], PAIR))
    dq, dk, dv, dqg, dkg, dsink = pl.pallas_call(
        body, grid=(npair, nb),
        in_specs=[pl.BlockSpec((BLOCK, QW), lambda p, i: (rev(i), p)),
                  pl.BlockSpec((BLOCK, KW), lambda p, i: (prev(i), k0 + p)),
                  pl.BlockSpec((BLOCK, KW), lambda p, i: (rev(i), k0 + p)),
                  pl.BlockSpec((BLOCK, KW), lambda p, i: (prev(i), v0 + p)),
                  pl.BlockSpec((BLOCK, KW), lambda p, i: (rev(i), v0 + p)),
                  pl.BlockSpec((BLOCK, QW), lambda p, i: (rev(i), p)),
                  pl.BlockSpec((8, QW), lambda p, i: (0, 0)), pl.BlockSpec((8, KW), lambda p, i: (0, 0)),
                  pl.BlockSpec(memory_space=pltpu.SMEM)],
        out_specs=[pl.BlockSpec((BLOCK, QW), lambda p, i: (rev(i), p)),
                   pl.BlockSpec((BLOCK, KW), lambda p, i: (rev(i), p)),
                   pl.BlockSpec((BLOCK, KW), lambda p, i: (rev(i), p)),
                   pl.BlockSpec((None, 8, QW), lambda p, i: (p, 0, 0)),
                   pl.BlockSpec((None, 8, KW), lambda p, i: (p, 0, 0)),
                   pl.BlockSpec((None, PAIR * GROUP, LANE), lambda p, i: (p, 0, 0))],
        out_shape=[jax.ShapeDtypeStruct((T, dq_w), BF16),
                   jax.ShapeDtypeStruct((T, npair * KW), BF16),
                   jax.ShapeDtypeStruct((T, npair * KW), BF16),
                   jax.ShapeDtypeStruct((npair, 8, QW), F32),
                   jax.ShapeDtypeStruct((npair, 8, KW), F32),
                   jax.ShapeDtypeStruct((npair, PAIR * GROUP, LANE), F32)],
        scratch_shapes=[pltpu.VMEM((BLOCK, KW), F32), pltpu.VMEM((BLOCK, KW), F32)],
        compiler_params=_cp("parallel", "arbitrary"), name=name,
    )(qkv, qkv, qkv, qkv, qkv, d_out, qg8, kg8, sinks)
    dqg = jnp.sum(dqg.reshape(-1, HEAD_DIM), axis=0, keepdims=True)
    dkg = jnp.sum(dkg.reshape(-1, HEAD_DIM), axis=0, keepdims=True)
    return dq, dk, dv, dqg, dkg, dsink[:, :, 0].reshape(-1)


def _softplus_neg(lam):
    return jnp.maximum(-lam, 0.0) + jnp.log1p(jnp.exp(-jnp.abs(lam)))


def _causal_conv(x, prev8, cw_ref, cb8):
    K = cw_ref.shape[0]
    acc = _tapmul(x, cw_ref[K - 1])
    for s in range(1, K):
        acc = acc + _tapmul(_shift_down(x, s, prev8), cw_ref[K - 1 - s])
    return (_rows8(acc) + cb8[None]).reshape(x.shape)


def _bcast_row(x, row):
    return jnp.broadcast_to(x[row:row + 1, :], x.shape)


def _rows8(x):
    return x.reshape(x.shape[0] // 8, 8, x.shape[1])


def _tapmul(x, w8):
    return (_rows8(x) * w8[None]).reshape(x.shape)


def _colsum8(x):
    return jnp.sum(_rows8(x), axis=0)


def _shift_down(x, s, prev8):
    R = x.shape[0]
    return _shift_up(jnp.concatenate([prev8, x[:R - 8]], axis=0), 8 - s, x[R - 8:])


def _shift_up(x, s, next8):
    R = x.shape[0]
    r = pltpu.roll(x, R - s, 0)
    rowid = lax.broadcasted_iota(jnp.int32, (8, 1), 0)
    tail = jnp.where(rowid < 8 - s, r[R - 8:], pltpu.roll(next8, 8 - s, 0))
    return jnp.concatenate([r[:R - 8], tail], axis=0)


def _rep8(v):
    return jnp.broadcast_to(v[..., None, :], v.shape[:-1] + (8, v.shape[-1]))


def rec_fwd(z, cw, cb, wa, ba, wi, bi, lam, *, name):
    T, C2 = z.shape
    C = C2 // 2
    nblk, bd, _ = wa.shape
    tt = _tile(T, 128, HALO)
    ng = tt // 8

    def body(x_ref, y_ref, halo_ref, cw_ref, cb_ref, wa_ref, ba_ref, wi_ref, bi_ref, lam_ref,
             xb_ref, r_ref, i_ref, a_ref, h_ref, hp_ref, hg_ref, carry, u_scr):
        step = pl.program_id(0)

        @pl.when(step == 0)
        def _():
            carry[...] = jnp.zeros_like(carry)

        xb = _causal_conv(x_ref[...], jnp.where(step > 0, halo_ref[HALO - 8:, :], 0.0), cw_ref, cb_ref[...])
        xb_ref[...] = xb
        pa, pi = [], []
        for b in range(nblk):
            xs = xb[:, b * bd:(b + 1) * bd]
            pa.append(_dot(xs, wa_ref[b], NN))
            pi.append(_dot(xs, wi_ref[b], NN))
        r = jax.nn.sigmoid(jnp.concatenate(pa, axis=1) + ba_ref[...])
        ig = jax.nn.sigmoid(jnp.concatenate(pi, axis=1) + bi_ref[...])
        r_ref[...] = r
        i_ref[...] = ig
        nl = LRU_C * r * _softplus_neg(lam_ref[...])
        a_ref[...] = jnp.exp(-nl)
        th = jnp.tanh(nl)
        u_scr[...] = jnp.sqrt(2.0 * th / (1.0 + th)) * (ig * xb)

        rowid = lax.broadcasted_iota(jnp.int32, (8, C), 0)

        def group(gi, hc):
            r0 = pl.multiple_of(gi * 8, 8)
            a8 = a_ref[pl.ds(r0, 8), :]
            u8 = u_scr[pl.ds(r0, 8), :]
            for d in (1, 2, 4):
                a_sh = jnp.where(rowid >= d, pltpu.roll(a8, d, 0), 1.0)
                u_sh = jnp.where(rowid >= d, pltpu.roll(u8, d, 0), 0.0)
                u8 = a8 * u_sh + u8
                a8 = a8 * a_sh
            h8 = u8 + a8 * hc
            h_ref[pl.ds(r0, 8), :] = h8
            hp_ref[pl.ds(r0, 8), :] = jnp.where(rowid >= 1, pltpu.roll(h8, 1, 0), hc)
            return _bcast_row(h8, 7)

        carry[...] = lax.fori_loop(0, ng, group, carry[...])
        hg_ref[...] = (h_ref[...] * _gelu(y_ref[...])).astype(hg_ref.dtype)

    row = lambda c: pl.BlockSpec((tt, C), lambda i, c=c: (i, c))
    vec = pl.BlockSpec((1, C), lambda i: (0, 0))
    full = lambda shp: pl.BlockSpec(shp, lambda i, n=len(shp): (0,) * n)
    per = tt // HALO
    outs = pl.pallas_call(
        body, grid=(T // tt,),
        in_specs=[row(0), row(1), pl.BlockSpec((HALO, C), lambda i: (jnp.maximum(i * per - 1, 0), 0)),
                  full(cw.shape[:1] + (8, C)), full((8, C)), full(wa.shape), vec, full(wi.shape), vec, vec],
        out_specs=[row(0)] * 7,
        out_shape=[jax.ShapeDtypeStruct((T, C), F32)] * 6 + [jax.ShapeDtypeStruct((T, C), BF16)],
        scratch_shapes=[pltpu.VMEM((8, C), F32), pltpu.VMEM((tt, C), F32)],
        compiler_params=_cp("arbitrary"), name=name,
    )(z, z, z, _rep8(cw), _rep8(cb[0]), wa, ba, wi, bi, lam)
    return outs


def rec_bwd_scan(dhg, h, a, z, *, name):
    T, C = h.shape
    tt = _tile(T, 256, HALO)
    ng = tt // 8
    nb = T // tt

    def body(dhg_ref, h_ref, a_ref, y_ref, dy_ref, yb_ref, ycarry, acarry, g_scr):
        step = pl.program_id(0)

        @pl.when(step == 0)
        def _():
            ycarry[...] = jnp.zeros_like(ycarry)
            acarry[...] = jnp.zeros_like(acarry)

        gate, dgate = _gelu_and_grad(y_ref[...])
        dhg_v = dhg_ref[...].astype(F32)
        dy_ref[...] = (dhg_v * h_ref[...] * dgate).astype(dy_ref.dtype)
        g_scr[...] = dhg_v * gate
        rowid = lax.broadcasted_iota(jnp.int32, (8, C), 0)

        def group(j, c):
            yc, ac = c
            r0 = pl.multiple_of((ng - 1 - j) * 8, 8)
            a8 = a_ref[pl.ds(r0, 8), :]
            y8 = g_scr[pl.ds(r0, 8), :]
            b8 = jnp.where(rowid < 7, pltpu.roll(a8, 7, 0), ac)
            for d in (1, 2, 4):
                y_sh = jnp.where(rowid < 8 - d, pltpu.roll(y8, 8 - d, 0), 0.0)
                b_sh = jnp.where(rowid < 8 - d, pltpu.roll(b8, 8 - d, 0), 1.0)
                y8 = y8 + b8 * y_sh
                b8 = b8 * b_sh
            y8 = y8 + b8 * yc
            yb_ref[pl.ds(r0, 8), :] = y8
            return _bcast_row(y8, 0), _bcast_row(a8, 0)

        yc, ac = lax.fori_loop(0, ng, group, (ycarry[...], acarry[...]))
        ycarry[...] = yc
        acarry[...] = ac

    rev = lambda c: pl.BlockSpec((tt, C), lambda i, c=c: (nb - 1 - i, c))
    return pl.pallas_call(
        body, grid=(nb,), in_specs=[rev(0), rev(0), rev(0), rev(1)], out_specs=[rev(0), rev(0)],
        out_shape=[jax.ShapeDtypeStruct((T, C), BF16), jax.ShapeDtypeStruct((T, C), F32)],
        scratch_shapes=[pltpu.VMEM((8, C), F32), pltpu.VMEM((8, C), F32), pltpu.VMEM((tt, C), F32)],
        compiler_params=_cp("arbitrary"), name=name,
    )(dhg, h, a, z)


def rec_bwd_gates(ybar, hprev, a, r, ig, xb, lam, wa, wi, *, name):
    T, C = xb.shape
    nblk, bd, _ = wa.shape
    tt = _tile(T, 256, 8)
    nb = T // tt

    def body(y_ref, hp_ref, a_ref, r_ref, i_ref, xb_ref, lam_ref, wa_ref, wi_ref,
             dxb_ref, dwa_ref, dwi_ref, dba_ref, dbi_ref, dlam_ref):
        step = pl.program_id(0)

        @pl.when(step == 0)
        def _():
            for ref in (dwa_ref, dwi_ref, dba_ref, dbi_ref, dlam_ref):
                ref[...] = jnp.zeros_like(ref)

        y, av, rv, iv, xv = y_ref[...], a_ref[...], r_ref[...], i_ref[...], xb_ref[...]
        sp = _softplus_neg(lam_ref[...])
        th = jnp.tanh(LRU_C * rv * sp)
        s = jnp.sqrt(2.0 * th / (1.0 + th))
        d_nl = -(y * hp_ref[...] * av) + (y * iv * xv) * (av * av) / s
        dlam_ref[...] += jnp.sum(d_nl * rv, axis=0, keepdims=True) * LRU_C
        dr = d_nl * (LRU_C * sp)
        di = y * s * xv
        dpa = dr * rv * (1.0 - rv)
        dpi = di * iv * (1.0 - iv)
        dba_ref[...] += jnp.sum(dpa, axis=0, keepdims=True)
        dbi_ref[...] += jnp.sum(dpi, axis=0, keepdims=True)
        parts = []
        for b in range(nblk):
            sl = slice(b * bd, (b + 1) * bd)
            xs, da_b, di_b = xv[:, sl], dpa[:, sl], dpi[:, sl]
            dwa_ref[b] += _dot(xs, da_b, TN)
            dwi_ref[b] += _dot(xs, di_b, TN)
            parts.append(_dot(da_b, wa_ref[b], NT) + _dot(di_b, wi_ref[b], NT))
        dxb_ref[...] = y * s * iv + jnp.concatenate(parts, axis=1)

        @pl.when(step == nb - 1)
        def _():
            dlam_ref[...] = dlam_ref[...] * (-jax.nn.sigmoid(-lam_ref[...]))

    row = pl.BlockSpec((tt, C), lambda i: (i, 0))
    vec = pl.BlockSpec((1, C), lambda i: (0, 0))
    wsp = pl.BlockSpec(wa.shape, lambda i: (0, 0, 0))
    return pl.pallas_call(
        body, grid=(nb,), in_specs=[row] * 6 + [vec, wsp, wsp],
        out_specs=[row, wsp, wsp, vec, vec, vec],
        out_shape=[jax.ShapeDtypeStruct((T, C), F32), jax.ShapeDtypeStruct(wa.shape, F32),
                   jax.ShapeDtypeStruct(wa.shape, F32)] + [jax.ShapeDtypeStruct((1, C), F32)] * 3,
        compiler_params=_cp("arbitrary"), name=name,
    )(ybar, hprev, a, r, ig, xb, lam, wa, wi)


def conv_bwd(d, x0, cw, *, name):
    T, C = d.shape
    K = cw.shape[0]
    tt = _tile(T, 256, HALO)
    per = tt // HALO
    nb = T // tt

    def body(d_ref, dn_ref, x_ref, cw_ref, dx_ref, dcw_ref, dcb_ref, dcw_acc, dcb_acc):
        step = pl.program_id(1)

        @pl.when(step == 0)
        def _():
            dcw_acc[...] = jnp.zeros_like(dcw_acc)
            dcb_acc[...] = jnp.zeros_like(dcb_acc)

        dv = d_ref[...].astype(F32)
        next8 = jnp.where(step < nb - 1, dn_ref[...].astype(F32)[0:8], 0.0)
        xt = x_ref[...].astype(F32)
        acc = _tapmul(dv, cw_ref[K - 1])
        dcw_acc[K - 1] += _colsum8(dv * xt)
        dcb_acc[...] += _colsum8(dv)
        for sh in range(1, K):
            dsh = _shift_up(dv, sh, next8)
            acc = acc + _tapmul(dsh, cw_ref[K - 1 - sh])
            dcw_acc[K - 1 - sh] += _colsum8(dsh * xt)
        dx_ref[...] = acc.astype(dx_ref.dtype)

        @pl.when(step == nb - 1)
        def _():
            dcw_ref[...] = jnp.sum(dcw_acc[...], axis=1)
            dcb_ref[...] = jnp.sum(dcb_acc[...], axis=0, keepdims=True)

    tc = C
    row = pl.BlockSpec((tt, tc), lambda j, i: (i, j))
    return pl.pallas_call(
        body, grid=(C // tc, nb),
        in_specs=[row, pl.BlockSpec((HALO, tc), lambda j, i: (jnp.minimum((i + 1) * per, T // HALO - 1), j)),
                  row, pl.BlockSpec((K, 8, tc), lambda j, i: (0, 0, j))],
        out_specs=[row, pl.BlockSpec((K, tc), lambda j, i: (0, j)), pl.BlockSpec((1, tc), lambda j, i: (0, j))],
        out_shape=[jax.ShapeDtypeStruct((T, C), BF16), jax.ShapeDtypeStruct((K, C), F32),
                   jax.ShapeDtypeStruct((1, C), F32)],
        scratch_shapes=[pltpu.VMEM((K, 8, tc), F32), pltpu.VMEM((8, tc), F32)],
        compiler_params=_cp("parallel", "arbitrary"), name=name,
    )(d, d, x0, _rep8(cw))


def ffn_act_fwd(u0, cw, cb, *, n, name):
    T, W = u0.shape
    G = W // (2 * n)
    tt = _tile(T, 256, HALO)
    per = tt // HALO

    K = cw.shape[0]

    def body(u_ref, up_ref, cw_ref, cb_ref, a_ref, uo_ref):
        step = pl.program_id(1)
        prev8 = jnp.where(step > 0, up_ref[...].astype(F32)[HALO - 8:], 0.0)
        u = _causal_conv(u_ref[...].astype(F32), prev8, cw_ref, cb_ref[...])
        uo_ref[...] = u.astype(uo_ref.dtype)
        a_ref[...] = (_gelu(u[:, :n]) * u[:, n:]).astype(a_ref.dtype)

    return pl.pallas_call(
        body, grid=(G, T // tt),
        in_specs=[pl.BlockSpec((tt, 2 * n), lambda j, i: (i, j)),
                  pl.BlockSpec((HALO, 2 * n), lambda j, i: (jnp.maximum(i * per - 1, 0), j)),
                  pl.BlockSpec((K, 8, 2 * n), lambda j, i: (0, 0, j)),
                  pl.BlockSpec((8, 2 * n), lambda j, i: (0, j))],
        out_specs=[pl.BlockSpec((tt, n), lambda j, i: (i, j)), pl.BlockSpec((tt, 2 * n), lambda j, i: (i, j))],
        out_shape=[jax.ShapeDtypeStruct((T, G * n), BF16), jax.ShapeDtypeStruct((T, W), BF16)],
        compiler_params=_cp("parallel", "parallel"), name=name)(u0, u0, _rep8(cw), _rep8(cb[0]))


def ffn_down_act_bwd(dx, w_down, u0, u, cw, *, n, carry=None, name):
    T, W = u0.shape
    D = dx.shape[1]
    G = W // (2 * n)
    K = cw.shape[0]
    tt = _tile(T, 256, HALO)
    tc = _tile(n, 512)
    nb = T // tt
    car = carry if carry is not None else dict(arrays=[], out_structs=[], sems=[])
    ncar = len(car["arrays"])

    def body(dx_ref, w_ref, x_ref, u_ref, cw_ref, *rest):
        car_in, (du_ref, dcw_ref, dcb_ref) = rest[:ncar], rest[ncar:ncar + 3]
        car_out, (later8, dcw_acc, dcb_acc) = rest[ncar + 3:2 * ncar + 3], rest[2 * ncar + 3:2 * ncar + 6]
        sems = rest[2 * ncar + 6:]
        group, step = pl.program_id(0), pl.program_id(1)
        if carry is not None:
            @pl.when((group == 0) & (step == 0))
            def _():
                carry["start"](car_in, car_out, sems)

            @pl.when((group == G - 1) & (step == nb - 1))
            def _():
                carry["finish"](car_in, car_out, sems)

        @pl.when(step == 0)
        def _():
            later8[...] = jnp.zeros_like(later8)
            dcw_acc[...] = jnp.zeros_like(dcw_acc)
            dcb_acc[...] = jnp.zeros_like(dcb_acc)

        dact = _dot(dx_ref[...], w_ref[...], NT)
        for q in range(n // tc):
            gs, vs = slice(q * tc, (q + 1) * tc), slice(n + q * tc, n + (q + 1) * tc)
            gl, dgl = _gelu_and_grad(u_ref[:, gs].astype(F32))
            daf = dact[:, gs]
            for cols, d in ((gs, daf * u_ref[:, vs].astype(F32) * dgl), (vs, daf * gl)):
                next8 = later8[:, cols]
                xt = x_ref[:, cols].astype(F32)
                acc = _tapmul(d, cw_ref[K - 1, :, cols])
                dcw_acc[K - 1, :, cols] += _colsum8(d * xt)
                dcb_acc[:, cols] += _colsum8(d)
                for sh in range(1, K):
                    dsh = _shift_up(d, sh, next8)
                    acc = acc + _tapmul(dsh, cw_ref[K - 1 - sh, :, cols])
                    dcw_acc[K - 1 - sh, :, cols] += _colsum8(dsh * xt)
                du_ref[:, cols] = acc.astype(du_ref.dtype)
                later8[:, cols] = d[0:8]

        @pl.when(step == nb - 1)
        def _():
            dcw_ref[...] = jnp.sum(dcw_acc[...], axis=1)
            dcb_ref[...] = jnp.sum(dcb_acc[...], axis=0, keepdims=True)

    rev = lambda i: nb - 1 - i
    outs = pl.pallas_call(
        body, grid=(G, nb),
        in_specs=[pl.BlockSpec((tt, D), lambda j, i: (rev(i), 0)),
                  pl.BlockSpec((n, D), lambda j, i: (j, 0)),
                  pl.BlockSpec((tt, 2 * n), lambda j, i: (rev(i), j)),
                  pl.BlockSpec((tt, 2 * n), lambda j, i: (rev(i), j)),
                  pl.BlockSpec((K, 8, 2 * n), lambda j, i: (0, 0, j))] + [HBM] * ncar,
        out_specs=[pl.BlockSpec((tt, 2 * n), lambda j, i: (rev(i), j)),
                   pl.BlockSpec((K, 2 * n), lambda j, i: (0, j)),
                   pl.BlockSpec((1, 2 * n), lambda j, i: (0, j))] + [HBM] * ncar,
        out_shape=[jax.ShapeDtypeStruct((T, W), BF16), jax.ShapeDtypeStruct((K, W), F32),
                   jax.ShapeDtypeStruct((1, W), F32)] + list(car["out_structs"]),
        scratch_shapes=[pltpu.VMEM((8, 2 * n), F32), pltpu.VMEM((K, 8, 2 * n), F32), pltpu.VMEM((8, 2 * n), F32)]
        + list(car["sems"]),
        compiler_params=_cp(*(("parallel", "arbitrary") if carry is None else ("arbitrary", "arbitrary"))), name=name,
    )(dx, w_down, u0, u, _rep8(cw), *car["arrays"])
    return tuple(outs[:3]) if carry is None else (tuple(outs[:3]), list(outs[3:]))


def _ffn_shard_of_block(q):
    return (q % 2) * (N_DEV // 2) + q // 2


def _ffn_block_of_shard(s):
    return (s % (N_DEV // 2)) * 2 + s // (N_DEV // 2)


def _group_cols(v):
    lead = v.shape[:-1]
    n = v.shape[-1] // N_DEV
    return jnp.swapaxes(v.reshape(lead + (2, N_DEV // 2, n)), -3, -2).reshape(v.shape)


def _ungroup_cols(v):
    lead = v.shape[:-1]
    n = v.shape[-1] // N_DEV
    return jnp.swapaxes(v.reshape(lead + (N_DEV // 2, 2, n)), -3, -2).reshape(v.shape)


def local_step(x, target, w, plan=None):
    depth = w["mix_norm"].shape[0]
    n_up = w["ffn_conv_w"].shape[-1] // N_DEV
    g = {k: [None] * (w[k].shape[0] if hasattr(w[k], "shape") else len(w[k])) for k in w}

    def run(fn, stage, l, slot, *args, **kw):
        carry = None if plan is None else plan.carry(stage, l, slot, g)
        if carry is None:
            return fn(*args, **kw)
        out, extra = fn(*args, carry=carry, **kw)
        plan.done(stage, l, slot, extra)
        return out

    saved = []
    for l in range(depth):
        j = l // 2
        h = rmsnorm_fwd(x, w["mix_norm"][l:l + 1], name="mix_norm_fwd")
        if l % 2 == 0:
            qkv = run(mm_nn, "fwd", l, "in", h, w["attn_w_qkv"][j], out_dtype=F32, name="qkv_proj")
            ao = run(attn_fwd, "fwd", l, "attn", qkv, w["attn_q_gain"][j:j + 1], w["attn_k_gain"][j:j + 1],
                     w["attn_sinks"][j:j + 1], name="attn_fwd")
            x1 = run(mm_nn, "fwd", l, "out", ao, w["attn_w_o"][j], out_dtype=F32, res=x, name="attn_out_proj")
            mix = (qkv, ao)
        else:
            z = run(mm_nn, "fwd", l, "in", h, w["rec_w_in"][j], out_dtype=F32, name="rec_in_proj")
            xb, r, ig, a, hs, hprev, hg = rec_fwd(
                z, w["rec_conv_w"][j], w["rec_conv_b"][j:j + 1], w["rec_w_a"][j], w["rec_b_a"][j:j + 1],
                w["rec_w_i"][j], w["rec_b_i"][j:j + 1], w["rec_lambda"][j:j + 1], name="rec_fwd")
            x1 = run(mm_nn, "fwd", l, "out", hg, w["rec_w_out"][j], out_dtype=F32, res=x, name="rec_out_proj")
            mix = (z, xb, r, ig, a, hs, hprev, hg)
        h2 = rmsnorm_fwd(x1, w["ffn_norm"][l:l + 1], name="ffn_norm_fwd")
        u0 = run(mm_nn, "fwd", l, "up", h2, w["ffn_w_up"][l], out_dtype=BF16, name="ffn_up_proj")
        act, u = ffn_act_fwd(u0, w["ffn_conv_w"][l], w["ffn_conv_b"][l:l + 1], n=n_up, name="ffn_act_fwd")
        x2 = run(mm_nn, "fwd", l, "down", act, w["ffn_w_down"][l], out_dtype=F32, res=x1, name="ffn_down_proj")
        saved.append((x, h, mix, x1, h2, u0, u, act))
        x = x2

    loss_vec, dx, dxb = loss_head(x, target, name="loss_head")

    for l in reversed(range(depth)):
        j = l // 2
        x0, h, mix, x1, h2, u0, u, act = saved[l]
        g["ffn_w_down"][l] = mm_tn(act, dxb, out_dtype=BF16, name="ffn_down_dw")
        du0, dcw, dcb = run(ffn_down_act_bwd, "bwd", l, "act_bwd", dxb, w["ffn_w_down"][l], u0, u,
                            w["ffn_conv_w"][l], n=n_up, name="ffn_down_act_bwd")
        g["ffn_conv_w"][l], g["ffn_conv_b"][l] = _ungroup_cols(dcw), _ungroup_cols(dcb)[0]
        g["ffn_w_up"][l] = run(mm_tn, "bwd", l, "up_dw", h2, du0, out_dtype=BF16, col_shards=N_DEV, shard_of_block=_ffn_shard_of_block,
                                 name="ffn_up_dw")
        dh2 = run(mm_nt, "bwd", l, "up_dx", du0, w["ffn_w_up"][l], out_dtype=BF16, name="ffn_up_dx")
        dx1, dx1b, dgf = rmsnorm_bwd(x1, w["ffn_norm"][l:l + 1], dh2, dx, name="ffn_norm_bwd")
        g["ffn_norm"][l] = dgf[0]
        if l % 2 == 0:
            qkv, ao = mix
            dao = mm_nt(dx1b, w["attn_w_o"][j], out_dtype=BF16, name="attn_out_dx")
            g["attn_w_o"][j] = mm_tn(ao, dx1b, out_dtype=BF16, name="attn_out_dw")
            dq, dk, dv, dqg, dkg, dsk = attn_bwd(qkv, dao, w["attn_q_gain"][j:j + 1], w["attn_k_gain"][j:j + 1],
                                                 w["attn_sinks"][j:j + 1], name="attn_bwd")
            g["attn_q_gain"][j], g["attn_k_gain"][j], g["attn_sinks"][j] = dqg[0], dkg[0], dsk
            dqkv = jnp.concatenate([dq, dk, dv], axis=1)
            g["attn_w_qkv"][j] = run(mm_tn, "bwd", l, "in_dw", h, dqkv, out_dtype=BF16, col_shards=N_DEV, name="qkv_dw")
            dh = run(mm_nt, "bwd", l, "in_dx", dqkv, w["attn_w_qkv"][j], out_dtype=BF16, name="qkv_dx")
        else:
            z, xb, r, ig, a, hs, hprev, hg = mix
            dhg = mm_nt(dx1b, w["rec_w_out"][j], out_dtype=BF16, name="rec_out_dx")
            g["rec_w_out"][j] = mm_tn(hg, dx1b, out_dtype=BF16, name="rec_out_dw")
            dyb, ybar = rec_bwd_scan(dhg, hs, a, z, name="rec_bwd_scan")
            dxb, dwa, dwi, dba, dbi, dlam = rec_bwd_gates(
                ybar, hprev, a, r, ig, xb, w["rec_lambda"][j:j + 1], w["rec_w_a"][j], w["rec_w_i"][j],
                name="rec_bwd_gates")
            dxb0, dcw, dcb = conv_bwd(dxb, z, w["rec_conv_w"][j], name="rec_conv_bwd")
            g["rec_w_a"][j], g["rec_w_i"][j] = dwa, dwi
            g["rec_b_a"][j], g["rec_b_i"][j], g["rec_lambda"][j] = dba[0], dbi[0], dlam[0]
            g["rec_conv_w"][j], g["rec_conv_b"][j] = dcw, dcb[0]
            dz = jnp.concatenate([dxb0, dyb], axis=1)
            g["rec_w_in"][j] = run(mm_tn, "bwd", l, "in_dw", h, dz, out_dtype=BF16, col_shards=N_DEV, name="rec_in_dw")
            dh = run(mm_nt, "bwd", l, "in_dx", dz, w["rec_w_in"][j], out_dtype=BF16, name="rec_in_dx")
        dx, dxb, dgm = rmsnorm_bwd(x0, w["mix_norm"][l:l + 1], dh, dx1, name="mix_norm_bwd")
        g["mix_norm"][l] = dgm[0]
    return loss_vec, dx, g


HBM = pl.BlockSpec(memory_space=pltpu.HBM)
N_PEER = N_DEV - 1


def _here():
    return lax.axis_index("x"), lax.axis_index("y"), lax.axis_index("c")


def _sid(dev):
    return 4 * dev[0] + 2 * dev[1] + dev[2]


def _exchange_sems(n):
    return [pltpu.SemaphoreType.DMA((n * N_PEER,)), pltpu.SemaphoreType.DMA((n * N_PEER,)),
            pltpu.SemaphoreType.DMA((n,))]


def gather_exchange(shards, out_structs, windows):
    n = len(shards)

    def parts(outs, sems):
        send_sems, recv_sems, _ = sems
        x, y, c = _here()
        me, sib = (x, y, c), (x, y, 1 - c)
        chips = [(1 - x, y), (x, 1 - y), (1 - x, 1 - y)]

        def copy(i, k, block, to, src=None):
            dst = windows[i](outs[i], _sid(block))
            return pltpu.make_async_remote_copy(
                src_ref=dst if src is None else src, dst_ref=dst,
                send_sem=send_sems.at[i * N_PEER + k], recv_sem=recv_sems.at[i * N_PEER + k],
                device_id=to, device_id_type=MESH)

        return me, sib, chips, c, copy

    def own_copies(ins, outs, sems):
        me, sib, chips, c, copy = parts(outs, sems)
        local = [pltpu.make_async_copy(ins[i], windows[i](outs[i], _sid(me)), sems[2].at[i]) for i in range(n)]
        first = []
        for i in range(n):
            first.append(copy(i, 0, me, sib, src=ins[i]))
            first += [copy(i, 1 + j, me, (*chip, c), src=ins[i]) for j, chip in enumerate(chips)]
        return local, first

    def start(ins, outs, sems):
        local, first = own_copies(ins, outs, sems)
        for cp in local + first:
            cp.start()

    def finish(ins, outs, sems):
        me, sib, chips, c, copy = parts(outs, sems)
        local, first = own_copies(ins, outs, sems)
        passed = []
        for i in range(n):
            for j, chip in enumerate(chips):
                copy(i, 1 + j, (*chip, c), me).wait_recv()
                fwd = copy(i, 4 + j, (*chip, c), sib)
                fwd.start()
                passed.append(fwd)
        for i in range(n):
            copy(i, 0, sib, me).wait_recv()
            for j, chip in enumerate(chips):
                copy(i, 4 + j, (*chip, 1 - c), me).wait_recv()
        for cp in first + passed:
            cp.wait_send()
        for cp in local:
            cp.wait()

    return dict(arrays=list(shards), out_structs=list(out_structs), sems=_exchange_sems(n), start=start,
                finish=finish)


def run_exchange(ex, *, name):
    n = len(ex["arrays"])

    def body(*refs):
        ins, outs, sems = refs[:n], refs[n:2 * n], refs[2 * n:]
        ex["start"](ins, outs, sems)
        ex["finish"](ins, outs, sems)

    return pl.pallas_call(
        body, in_specs=[HBM] * n, out_specs=[HBM] * n, out_shape=ex["out_structs"], scratch_shapes=ex["sems"],
        name=name)(*ex["arrays"])


def scatter_exchange(grads):
    n = len(grads)

    def parts(ins, outs, sems):
        send_sems, recv_sems, local_sems = sems
        x, y, c = _here()
        me = (x, y, c)
        peers = []
        for k in range(1, N_DEV):
            kx, ky, kc = (k >> 2) & 1, (k >> 1) & 1, k & 1
            peers.append((1 - x if kx else x, 1 - y if ky else y, 1 - c if kc else c))

        def copy(i, k):
            return pltpu.make_async_remote_copy(
                src_ref=ins[i].at[_sid(peers[k])], dst_ref=outs[i].at[_sid(me)],
                send_sem=send_sems.at[i * N_PEER + k], recv_sem=recv_sems.at[i * N_PEER + k],
                device_id=peers[k], device_id_type=MESH)

        def arrival(i, k):
            return pltpu.make_async_remote_copy(
                src_ref=ins[i].at[_sid(me)], dst_ref=outs[i].at[_sid(peers[k])],
                send_sem=send_sems.at[i * N_PEER + k], recv_sem=recv_sems.at[i * N_PEER + k],
                device_id=peers[k], device_id_type=MESH)

        local = [pltpu.make_async_copy(ins[i].at[_sid(me)], outs[i].at[_sid(me)], local_sems.at[i]) for i in range(n)]
        sends = [copy(i, k) for i in range(n) for k in range(N_PEER)]
        return local, sends, arrival

    def start(ins, outs, sems):
        local, sends, _ = parts(ins, outs, sems)
        for cp in local + sends:
            cp.start()

    def finish(ins, outs, sems):
        local, sends, arrival = parts(ins, outs, sems)
        for i in range(n):
            for k in range(N_PEER):
                arrival(i, k).wait_recv()
        for cp in sends:
            cp.wait_send()
        for cp in local:
            cp.wait()

    return dict(arrays=list(grads), out_structs=[jax.ShapeDtypeStruct(g.shape, g.dtype) for g in grads],
                sems=_exchange_sems(n), start=start, finish=finish)


def adamw_family(contribs, w, m, v, *, name):
    L, R, C = w.shape
    S = contribs[0].shape[0]
    tr = _tile(R, max(8, (1 << 20) // (C * S)), 8)
    nr = R // tr
    c1 = 1.0 / (1.0 - ADAM_B1 ** ADAM_STEP)
    c2 = 1.0 / (1.0 - ADAM_B2 ** ADAM_STEP)

    def body(*refs):
        c_refs = refs[:L]
        w_ref, m_ref, v_ref, g_ref, d_ref, nm_ref, nv_ref = refs[L:]
        layer = pl.program_id(0)
        for l in range(L):
            @pl.when(layer == l)
            def _(l=l):
                g = c_refs[l][0].astype(F32)
                for s in range(1, S):
                    g = g + c_refs[l][s].astype(F32)
                mm = ADAM_B1 * m_ref[...] + (1.0 - ADAM_B1) * g
                vv = ADAM_B2 * v_ref[...] + (1.0 - ADAM_B2) * (g * g)
                g_ref[...] = g
                nm_ref[...] = mm
                nv_ref[...] = vv
                d_ref[...] = -ADAM_LR * ((mm * c1) / (jnp.sqrt(vv * c2) + ADAM_EPS) + ADAM_WD * w_ref[...])

    def cspec(l):
        return pl.BlockSpec((S, tr, C), lambda ll, i, l=l: (0, jnp.where(ll == l, i, 0), 0))

    lay = pl.BlockSpec((None, tr, C), lambda ll, i: (ll, i, 0))
    return pl.pallas_call(
        body, grid=(L, nr), in_specs=[cspec(l) for l in range(L)] + [lay] * 3, out_specs=[lay] * 4,
        out_shape=[jax.ShapeDtypeStruct((L, R, C), F32)] * 4,
        compiler_params=_cp("arbitrary", "arbitrary"), name=name)(*contribs, w, m, v)


def sum_slots(a, *, name):
    S, R, C = a.shape
    tr = _tile(R, 256, 8)

    def body(a_ref, o_ref):
        t = a_ref[0]
        for s in range(1, S):
            t = t + a_ref[s]
        o_ref[...] = t

    return pl.pallas_call(
        body, grid=(R // tr,), in_specs=[pl.BlockSpec((S, tr, C), lambda i: (0, i, 0))],
        out_specs=pl.BlockSpec((tr, C), lambda i: (i, 0)), out_shape=jax.ShapeDtypeStruct((R, C), F32),
        compiler_params=_cp("parallel"), name=name)(a)


LANES = 128


def _pack(arrs):
    flat = jnp.concatenate([a.reshape(-1).astype(F32) for a in arrs])
    rows = -(-flat.shape[0] // LANES)
    rows = -(-rows // 256) * 256
    return jnp.pad(flat, (0, rows * LANES - flat.shape[0])).reshape(rows, LANES)


def _unpack(buf, shapes):
    flat = buf.reshape(-1)
    out, off = [], 0
    for shp in shapes:
        size = int(np.prod(shp))
        out.append(flat[off:off + size].reshape(shp))
        off += size
    return out


def _gather_last(g):
    t = jnp.moveaxis(g, 0, -2)
    return t.reshape(t.shape[:-2] + (t.shape[-2] * t.shape[-1],))


def _own_last(full, s):
    n = full.shape[-1] // N_DEV
    t = full.reshape(full.shape[:-1] + (N_DEV, n))
    return lax.dynamic_index_in_dim(t, s, axis=t.ndim - 2, keepdims=False)


BIG = ["attn_w_qkv", "attn_w_o", "rec_w_in", "rec_w_out", "ffn_w_up", "ffn_w_down", "rec_w_a", "rec_w_i"]
SMALL_REPLICATED = ["mix_norm", "ffn_norm", "attn_q_gain", "attn_k_gain", "attn_sinks", "ffn_conv_b"]
SMALL_SHARDED = ["rec_conv_w", "rec_conv_b", "rec_b_a", "rec_b_i", "rec_lambda", "ffn_conv_w"]
SMALL = SMALL_REPLICATED + SMALL_SHARDED
WEIGHTS = ["mix_norm", "ffn_norm", "attn_w_qkv", "attn_q_gain", "attn_k_gain", "attn_sinks", "attn_w_o", "rec_w_in",
           "rec_conv_w", "rec_conv_b", "rec_w_a", "rec_b_a", "rec_w_i", "rec_b_i", "rec_lambda", "rec_w_out",
           "ffn_w_up", "ffn_conv_w", "ffn_conv_b", "ffn_w_down"]


def _col_window(n, block_of_shard=None):
    def win(ref, s):
        q = s if block_of_shard is None else block_of_shard(s)
        return ref.at[:, pl.ds(pl.multiple_of(q * n, 128), n)]
    return win


def _row_window(r):
    return lambda ref, s: ref.at[pl.ds(pl.multiple_of(s * r, 16), r), :]


def _gate_window(r):
    return lambda ref, s: ref.at[:, pl.ds(pl.multiple_of(s * r, 16), r), :]


def _slot_window(ref, s):
    return ref.at[s]


def _idx(name, layer):
    return layer if name.startswith("ffn") else layer // 2


def _slot_names(layer, slot):
    attn = layer % 2 == 0
    return {"in": ["attn_w_qkv"] if attn else ["rec_w_in"],
            "out": ["attn_w_o"] if attn else ["rec_w_out", "rec_w_a", "rec_w_i"],
            "up": ["ffn_w_up"], "down": ["ffn_w_down"]}[slot]


def _gather_for(p, items):
    shards, structs, wins = [], [], []
    for nme, layer in items:
        sh = p[nme][_idx(nme, layer)].astype(BF16)
        if nme in ("attn_w_qkv", "rec_w_in", "ffn_w_up"):
            K, n = sh.shape
            structs.append(jax.ShapeDtypeStruct((K, n * N_DEV), BF16))
            wins.append(_col_window(n, _ffn_block_of_shard if nme == "ffn_w_up" else None))
        elif nme in ("rec_w_a", "rec_w_i"):
            nblk, r, bd = sh.shape
            structs.append(jax.ShapeDtypeStruct((nblk, r * N_DEV, bd), BF16))
            wins.append(_gate_window(r))
        else:
            r, N = sh.shape
            structs.append(jax.ShapeDtypeStruct((r * N_DEV, N), BF16))
            wins.append(_row_window(r))
        shards.append(sh)
    return gather_exchange(shards, structs, wins)


def _send_layout(name, t):
    if name in ("rec_w_a", "rec_w_i"):
        nblk, bd, _ = t.shape
        return jnp.transpose(t.reshape(nblk, N_DEV, bd // N_DEV, bd), (1, 0, 2, 3)).astype(BF16)
    if name in ("attn_w_o", "rec_w_out", "ffn_w_down"):
        return t.reshape((N_DEV, t.shape[0] // N_DEV) + t.shape[1:])
    return t


class _Plan:
    BWD_SLOT = {"up_dw": "down", "in_dw": "out", "in_dx": "in"}

    def __init__(self, p, w, contribs, depth):
        self.p, self.w, self.contribs, self.depth = p, w, contribs, depth
        self.pending = None

    def carry(self, stage, l, slot, g):
        if stage == "fwd":
            items = []
            if l == 0 and slot == "in":
                items += [(k, 0) for k in _slot_names(0, "out")]
            if l == 0 and slot == "attn":
                items += [(k, 0) for s in ("up", "down") for k in _slot_names(0, s)]
            if slot != "attn" and l + 1 < self.depth:
                items += [(k, l + 1) for k in _slot_names(l + 1, slot)]
            if not items:
                return None
            self.pending = items
            return _gather_for(self.p, items)
        if slot == "act_bwd":
            items = [("ffn_w_up", l + 1)] if l + 1 < self.depth else []
        elif slot == "up_dx":
            items = [("ffn_w_up", 0)] if l == 0 else []
        else:
            items = [(k, l) for k in _slot_names(l, self.BWD_SLOT[slot])]
        if not items:
            return None
        self.pending = items
        return scatter_exchange([_send_layout(k, g[k][_idx(k, layer)]) for k, layer in items])

    def done(self, stage, l, slot, outs):
        dst = self.w if stage == "fwd" else self.contribs
        for (k, layer), o in zip(self.pending, outs):
            dst[k][_idx(k, layer)] = o


def _train_step(p, x, target, mom, vel):
    depth = p["mix_norm"].shape[0]
    s_me = _sid(_here())

    w = {k: [None] * p[k].shape[0] for k in BIG}
    first = [(k, 0) for k in _slot_names(0, "in")]
    for (k, _), t in zip(first, run_exchange(_gather_for(p, first), name="all_gather_first")):
        w[k][0] = t

    local_small = [p[k] for k in SMALL_SHARDED]
    packed = _pack(local_small)
    gathered, = run_exchange(
        gather_exchange([packed], [jax.ShapeDtypeStruct((N_DEV,) + packed.shape, F32)], [_slot_window]),
        name="all_gather_small")
    per_dev = [_unpack(gathered[s], [a.shape for a in local_small]) for s in range(N_DEV)]
    for i, k in enumerate(SMALL_SHARDED):
        w[k] = _gather_last(jnp.stack([per_dev[s][i] for s in range(N_DEV)]))
    for k in SMALL_REPLICATED:
        w[k] = p[k]
    nrec = w["rec_b_a"].shape[0]
    w["rec_b_a"] = w["rec_b_a"].reshape(nrec, -1)
    w["rec_b_i"] = w["rec_b_i"].reshape(nrec, -1)
    w["ffn_conv_w"] = _group_cols(w["ffn_conv_w"])
    w["ffn_conv_b"] = _group_cols(w["ffn_conv_b"])

    contribs = {k: [None] * len(w[k]) for k in BIG}
    loss_vec, dx, g = local_step(x[0], target[0], w, _Plan(p, w, contribs, depth))
    loss = lax.psum(jnp.sum(loss_vec), ("x", "y", "c"))

    out = {}
    for k in BIG:
        shp = p[k].shape
        L = shp[0]
        C = shp[-1]
        R = int(np.prod(shp[1:-1]))
        cs = [c.reshape(N_DEV, R, C) for c in contribs[k]]
        res = adamw_family(cs, p[k].reshape(L, R, C), mom[k].reshape(L, R, C), vel[k].reshape(L, R, C),
                           name="adamw_" + k)
        out[k] = [t.reshape(shp) for t in res]

    gsmall = [jnp.stack(g[k]) for k in SMALL]
    gp = _pack(gsmall)
    gall, = run_exchange(
        gather_exchange([gp], [jax.ShapeDtypeStruct((N_DEV,) + gp.shape, F32)], [_slot_window]),
        name="all_gather_small_grads")
    gsum = _unpack(sum_slots(gall, name="sum_small_grads"), [a.shape for a in gsmall])
    glocal = []
    for k, t in zip(SMALL, gsum):
        if k in SMALL_SHARDED:
            t = _own_last(t.reshape(p[k].shape[:-1] + (p[k].shape[-1] * N_DEV,)), s_me)
        glocal.append(t.reshape(p[k].shape))
    wp, mp, vp, gpk = (_pack([d[k] for k in SMALL]) for d in (p, mom, vel, dict(zip(SMALL, glocal))))
    res = adamw_family([gpk[None]], wp[None], mp[None], vp[None], name="adamw_small")
    shapes = [p[k].shape for k in SMALL]
    unp = [_unpack(t[0], shapes) for t in res]
    for i, k in enumerate(SMALL):
        out[k] = [glocal[i], unp[1][i], unp[2][i], unp[3][i]]

    return (loss, dx[None]) + tuple(out[k][q] for q in range(4) for k in WEIGHTS)


def kernel(x, mix_norm, ffn_norm, attn_w_qkv, attn_q_gain, attn_k_gain, attn_sinks, attn_w_o, rec_w_in, rec_conv_w, rec_conv_b, rec_w_a, rec_b_a, rec_w_i, rec_b_i, rec_lambda, rec_w_out, ffn_w_up, ffn_conv_w, ffn_conv_b, ffn_w_down, loss_target, m_mix_norm, m_ffn_norm, m_attn_w_qkv, m_attn_q_gain, m_attn_k_gain, m_attn_sinks, m_attn_w_o, m_rec_w_in, m_rec_conv_w, m_rec_conv_b, m_rec_w_a, m_rec_b_a, m_rec_w_i, m_rec_b_i, m_rec_lambda, m_rec_w_out, m_ffn_w_up, m_ffn_conv_w, m_ffn_conv_b, m_ffn_w_down, v_mix_norm, v_ffn_norm, v_attn_w_qkv, v_attn_q_gain, v_attn_k_gain, v_attn_sinks, v_attn_w_o, v_rec_w_in, v_rec_conv_w, v_rec_conv_b, v_rec_w_a, v_rec_b_a, v_rec_w_i, v_rec_b_i, v_rec_lambda, v_rec_w_out, v_ffn_w_up, v_ffn_conv_w, v_ffn_conv_b, v_ffn_w_down):
    p = dict(zip(WEIGHTS, (mix_norm, ffn_norm, attn_w_qkv, attn_q_gain, attn_k_gain, attn_sinks, attn_w_o, rec_w_in,
                           rec_conv_w, rec_conv_b, rec_w_a, rec_b_a, rec_w_i, rec_b_i, rec_lambda, rec_w_out,
                           ffn_w_up, ffn_conv_w, ffn_conv_b, ffn_w_down)))
    mom = dict(zip(WEIGHTS, (m_mix_norm, m_ffn_norm, m_attn_w_qkv, m_attn_q_gain, m_attn_k_gain, m_attn_sinks,
                             m_attn_w_o, m_rec_w_in, m_rec_conv_w, m_rec_conv_b, m_rec_w_a, m_rec_b_a, m_rec_w_i,
                             m_rec_b_i, m_rec_lambda, m_rec_w_out, m_ffn_w_up, m_ffn_conv_w, m_ffn_conv_b,
                             m_ffn_w_down)))
    vel = dict(zip(WEIGHTS, (v_mix_norm, v_ffn_norm, v_attn_w_qkv, v_attn_q_gain, v_attn_k_gain, v_attn_sinks,
                             v_attn_w_o, v_rec_w_in, v_rec_conv_w, v_rec_conv_b, v_rec_w_a, v_rec_b_a, v_rec_w_i,
                             v_rec_b_i, v_rec_lambda, v_rec_w_out, v_ffn_w_up, v_ffn_conv_w, v_ffn_conv_b,
                             v_ffn_w_down)))
    return _train_step(p, x, loss_target, mom, vel)
```
